```python
import functools
import jax, jax.numpy as jnp
from jax import lax
import numpy as np

D_MODEL = 1024
BATCH = 4
SEQ = 4096
DEPTH = 2
DEC_BATCH = 32
DEC_SEQ = 1
PAST_LEN = 8192
PAGE_SIZE = 128

N_EVEN = (DEPTH + 1) // 2
N_ODD = DEPTH // 2
A_WIDTH = D_MODEL // 2
DH_A = 128
H_A = A_WIDTH // DH_A
DK_B = 128
DV_B = 128
H_B = (D_MODEL // 2) // DK_B
PROJ_SPLITS = (A_WIDTH, 2 * A_WIDTH, 3 * A_WIDTH, 3 * A_WIDTH + H_B * DK_B,
               3 * A_WIDTH + 2 * H_B * DK_B, 3 * A_WIDTH + 2 * H_B * DK_B + H_B * DV_B)
N_PROJ = 3 * A_WIDTH + 2 * H_B * DK_B + 2 * H_B * DV_B
DILATED = ((128, 1), (512, 4), (2048, 16))
W_MAX = 2048
Q_BLOCK = 128
NUM_BUCKETS = 32
MAX_DISTANCE = 2048
HGRN_CHUNK = 64
POOL_WINDOWS = (2, 4, 8, 16)
POOL_GROUP = D_MODEL // 4
POOL_KEEP = 15
N_GROUPS = 4
EXPERTS_PER_GROUP = 8
N_EXPERTS = 32
TOP_K = 2
D_EXPERT = 512
MOE_BLOCK = 128
MOE_BLOCK_SMALL = 8
EPS = 1e-6

kernel_name = 'hybrid_dilated_hgrn2_pool_hmoe_step'


def rmsnorm(x, g):
    xf = x.astype(jnp.float32)
    y = xf * lax.rsqrt(jnp.mean(xf * xf, -1, keepdims=True) + EPS) * g.astype(jnp.float32)
    return y.astype(x.dtype)


def t5_bucket(dist):
    max_exact = NUM_BUCKETS // 2
    d = np.asarray(dist)
    large = max_exact + np.floor(np.log(np.maximum(d, 1) / max_exact)
                                 / np.log(MAX_DISTANCE / max_exact) * (NUM_BUCKETS - max_exact)).astype(np.int32)
    large = np.minimum(large, NUM_BUCKETS - 1)
    return np.where(d < max_exact, d, large).astype(np.int32)


def dilated_attention(q, k, v, k_prev, v_prev, pos0, rel_bias):
    b_sz, t_len, n_h, d_h = q.shape
    f32 = jnp.float32
    qb = Q_BLOCK if t_len % Q_BLOCK == 0 else t_len
    n_blk = t_len // qb
    k_ext = jnp.concatenate([k_prev.astype(k.dtype), k], axis=1)
    v_ext = jnp.concatenate([v_prev.astype(v.dtype), v], axis=1)
    scale = d_h ** -0.5
    branches = []
    for w, d in DILATED:
        nk = w // d + 1
        dist = (d * np.arange(nk)).astype(np.int32)
        idx = (W_MAX + np.arange(qb)[:, None] - dist[None, :]).astype(np.int32)
        bias = rel_bias.astype(f32)[t5_bucket(dist)].T
        branches.append((dist, idx, bias))

    def block(bi):
        start = bi * qb
        qf = lax.dynamic_slice_in_dim(q, start, qb, 1).astype(f32) * scale
        ks = lax.dynamic_slice_in_dim(k_ext, start, W_MAX + qb, 1)
        vs = lax.dynamic_slice_in_dim(v_ext, start, W_MAX + qb, 1)
        qpos = pos0 + start + jnp.arange(qb, dtype=jnp.int32)
        ms, ss, us = [], [], []
        for dist, idx, bias in branches:
            kg = ks[:, idx].astype(f32)
            vg = vs[:, idx].astype(f32)
            logit = jnp.einsum('bqhd,bqjhd->bhqj', qf, kg) + bias[None, :, None, :]
            valid = (qpos[:, None] - dist[None, :]) >= 0
            logit = jnp.where(valid[None, None], logit, -jnp.inf)
            m = jnp.max(logit, -1)
            p = jnp.exp(logit - m[..., None])
            ms.append(m)
            ss.append(jnp.sum(p, -1))
            us.append(jnp.einsum('bhqj,bqjhd->bhqd', p, vg))
        m_all = jnp.stack(ms)
        s_all = jnp.stack(ss)
        u_all = jnp.stack(us)
        c = jnp.exp(m_all - jnp.max(m_all, 0, keepdims=True))
        den = jnp.sum(c * s_all, 0)
        num = jnp.sum(c[..., None] * u_all, 0)
        return (num / den[..., None]).transpose(0, 2, 1, 3)

    out = lax.map(block, jnp.arange(n_blk, dtype=jnp.int32))
    return out.transpose(1, 0, 2, 3, 4).reshape(b_sz, t_len, n_h, d_h)


def hgrn2(q, fpre, i, g, s0, lb, gnorm):
    b, t, _ = q.shape
    f32 = jnp.float32
    qf = jax.nn.silu(q.astype(f32)).reshape(b, t, H_B, DK_B)
    lbh = lb.reshape(H_B, DK_B)
    f = lbh + (1.0 - lbh) * jax.nn.sigmoid(fpre.astype(f32).reshape(b, t, H_B, DK_B))
    logf = jnp.log(f)
    kf = 1.0 - f
    vf = i.astype(f32).reshape(b, t, H_B, DV_B)
    c = HGRN_CHUNK if t % HGRN_CHUNK == 0 else t
    n_c = t // c

    def chunks(a):
        return a.reshape(b, n_c, c, H_B, a.shape[-1]).swapaxes(0, 1)

    tril = jnp.tril(jnp.ones((c, c), dtype=bool))[None, :, :, None, None]

    def step(s, inp):
        qc, kc, vc, lfc = inp
        cb = jnp.cumsum(lfc, axis=1)
        inter = jnp.einsum('bthk,bhkv->bthv', qc * jnp.exp(cb), s)
        decay = jnp.exp(jnp.where(tril, cb[:, :, None] - cb[:, None, :], -jnp.inf))
        scores = jnp.einsum('btshk,bshk->btsh', qc[:, :, None] * decay, kc)
        intra = jnp.einsum('btsh,bshv->bthv', scores, vc)
        cl = cb[:, -1]
        s_new = jnp.exp(cl)[..., None] * s + jnp.einsum('bshk,bshv->bhkv', kc * jnp.exp(cl[:, None] - cb), vc)
        return s_new, inter + intra

    s_fin, o = lax.scan(step, s0.astype(f32), (chunks(qf), chunks(kf), chunks(vf), chunks(logf)))
    o = o.swapaxes(0, 1).reshape(b, t, H_B, DV_B)
    o = o * lax.rsqrt(jnp.mean(o * o, -1, keepdims=True) + EPS) * gnorm.astype(f32)
    o = o.reshape(b, t, H_B * DV_B) * jax.nn.silu(g.astype(f32))
    return o, s_fin


def pool_mixer(h, prev, pos0, w_groups, scale):
    b, t, d = h.shape
    f32 = jnp.float32
    hf = h.astype(f32)
    ext = jnp.concatenate([prev.astype(f32), hf], axis=1)
    csum = jnp.concatenate([jnp.zeros((b, 1, d), f32), jnp.cumsum(ext, axis=1)], axis=1)
    pos = pos0 + jnp.arange(t, dtype=jnp.int32)
    outs = []
    for gi, w in enumerate(POOL_WINDOWS):
        sl = slice(gi * POOL_GROUP, (gi + 1) * POOL_GROUP)
        c_g = csum[:, :, sl]
        win_sum = c_g[:, POOL_KEEP + 1:POOL_KEEP + 1 + t] - c_g[:, POOL_KEEP + 1 - w:POOL_KEEP + 1 - w + t]
        cnt = jnp.minimum(w, pos + 1).astype(f32)
        mixed = win_sum / cnt[None, :, None] - hf[:, :, sl]
        outs.append(jnp.einsum('btc,cd->btd', mixed, w_groups[gi].astype(f32)))
    return jnp.concatenate(outs, -1) * scale.astype(f32)


def grouped_experts(xt, e, gate, w1, w3, w2):
    n, d = xt.shape
    a = e.shape[0]
    tok = jnp.repeat(jnp.arange(n, dtype=jnp.int32), TOP_K)
    blk = MOE_BLOCK if a >= MOE_BLOCK * N_EXPERTS else MOE_BLOCK_SMALL
    n_blk = (a + N_EXPERTS * (blk - 1) + blk - 1) // blk
    order = jnp.argsort(e)
    se = e[order]
    counts = jax.ops.segment_sum(jnp.ones((a,), jnp.int32), e, num_segments=N_EXPERTS)
    pcounts = (counts + blk - 1) // blk * blk
    start = jnp.cumsum(counts) - counts
    pend = jnp.cumsum(pcounts)
    pstart = pend - pcounts
    dest = pstart[se] + jnp.arange(a, dtype=jnp.int32) - start[se]
    tok_buf = jnp.full((n_blk * blk,), n, jnp.int32).at[dest].set(tok[order])
    gate_buf = jnp.zeros((n_blk * blk,), jnp.float32).at[dest].set(gate[order].astype(jnp.float32))
    blk_exp = jnp.minimum(jnp.searchsorted(pend, jnp.arange(n_blk, dtype=jnp.int32) * blk, side='right'),
                          N_EXPERTS - 1).astype(jnp.int32)
    x_pad = jnp.concatenate([xt, jnp.zeros((1, d), xt.dtype)], 0)

    def expert_block(args):
        tb, eb = args
        xb = x_pad[tb]
        return jnp.dot(jax.nn.silu(jnp.dot(xb, w1[eb])) * jnp.dot(xb, w3[eb]), w2[eb])

    yb = lax.map(expert_block, (tok_buf.reshape(n_blk, blk), blk_exp))
    y = jnp.zeros((n + 1, d), jnp.float32).at[tok_buf].add(yb.reshape(-1, d).astype(jnp.float32) * gate_buf[:, None])
    return y[:n]


def hier_moe(h, wg_group, bg_group, wg_exp, bg_exp, w1, w3, w2):
    b, t, d = h.shape
    n = b * t
    xt = h.reshape(n, d)
    glog = jnp.dot(xt, wg_group).astype(jnp.float32) + bg_group.astype(jnp.float32)
    pg, gsel = lax.top_k(jax.nn.softmax(glog, -1), 1)
    elog = (jnp.dot(xt, wg_exp).astype(jnp.float32) + bg_exp.astype(jnp.float32)).reshape(n, N_GROUPS, EXPERTS_PER_GROUP)
    elog = jnp.take_along_axis(elog, gsel[:, :, None], axis=1)[:, 0]
    pe, esel = lax.top_k(jax.nn.softmax(elog, -1), TOP_K)
    gate = pe / jnp.sum(pe, -1, keepdims=True) * pg
    eid = (gsel * EXPERTS_PER_GROUP + esel).astype(jnp.int32)
    y = grouped_experts(xt, eid.reshape(-1), gate.reshape(-1), w1, w3, w2)
    return y.reshape(b, t, d).astype(h.dtype)


def trunk(x, win_k_prev, win_v_prev, hgrn_prev, pool_prev, pos0, rel_bias, norm_mix, norm_ffn, norm_final,
          w_in, w_out, hgrn_lb, hgrn_gnorm, pool_w, pool_scale, moe_wg_group, moe_bg_group,
          moe_wg_exp, moe_bg_exp, moe_w1, moe_w3, moe_w2):
    b, t, _ = x.shape
    lb_all = jnp.cumsum(jax.nn.softmax(hgrn_lb.astype(jnp.float32), axis=0), axis=0)
    new_k, new_v, new_s, new_p = [], [], [], []
    for layer in range(DEPTH):
        h = rmsnorm(x, norm_mix[layer])
        if layer % 2 == 0:
            e = layer // 2
            proj = jnp.dot(h, w_in[e])
            qa, ka, va, qb, fb, ib, gb = jnp.split(proj, PROJ_SPLITS, axis=-1)
            ka = ka.reshape(b, t, H_A, DH_A)
            va = va.reshape(b, t, H_A, DH_A)
            oa = dilated_attention(qa.reshape(b, t, H_A, DH_A), ka, va, win_k_prev[e], win_v_prev[e], pos0, rel_bias)
            ob, s_fin = hgrn2(qb, fb, ib, gb, hgrn_prev[e], lb_all[e], hgrn_gnorm[e])
            mix = jnp.dot(jnp.concatenate([oa.reshape(b, t, A_WIDTH).astype(h.dtype), ob.astype(h.dtype)], -1), w_out[e])
            keep = min(W_MAX, t)
            new_k.append(ka[:, t - keep:])
            new_v.append(va[:, t - keep:])
            new_s.append(s_fin)
        else:
            o = layer // 2
            mix = pool_mixer(h, pool_prev[o], pos0, pool_w[o], pool_scale[o])
            keep = min(POOL_KEEP, t)
            new_p.append(h[:, t - keep:])
        x = x + mix.astype(x.dtype)
        h = rmsnorm(x, norm_ffn[layer])
        x = x + hier_moe(h, moe_wg_group[layer], moe_bg_group[layer], moe_wg_exp[layer], moe_bg_exp[layer],
                         moe_w1[layer], moe_w3[layer], moe_w2[layer])
    y = rmsnorm(x, norm_final)
    return y, jnp.stack(new_k), jnp.stack(new_v), jnp.stack(new_s), jnp.stack(new_p)


def setup_inputs(seed: int = 0) -> dict:
    key = jax.random.key(seed)
    ks = jax.random.split(key, 23)
    f32 = jnp.float32
    l_buf = min(W_MAX, PAST_LEN)

    def nrm(k, shape, scale):
        return jax.random.normal(k, shape, f32) * scale

    return {
        'x_prompt': nrm(ks[0], (BATCH, SEQ, D_MODEL), 1.0),
        'x_sample': nrm(ks[1], (DEC_BATCH, DEC_SEQ, D_MODEL), 1.0),
        'cache_win_k': nrm(ks[2], (N_EVEN, DEC_BATCH, l_buf, H_A, DH_A), 1.0),
        'cache_win_v': nrm(ks[3], (N_EVEN, DEC_BATCH, l_buf, H_A, DH_A), 1.0),
        'state_hgrn': nrm(ks[4], (N_EVEN, DEC_BATCH, H_B, DK_B, DV_B), 0.3),
        'state_pool': nrm(ks[5], (N_ODD, DEC_BATCH, POOL_KEEP, D_MODEL), 1.0),
        'rel_bias': nrm(ks[6], (NUM_BUCKETS, H_A), 0.5),
        'norm_mix': 1.0 + nrm(ks[7], (DEPTH, D_MODEL), 0.05),
        'norm_ffn': 1.0 + nrm(ks[8], (DEPTH, D_MODEL), 0.05),
        'norm_final': 1.0 + nrm(ks[9], (D_MODEL,), 0.05),
        'w_in': nrm(ks[10], (N_EVEN, D_MODEL, N_PROJ), D_MODEL ** -0.5),
        'w_out': nrm(ks[11], (N_EVEN, A_WIDTH + H_B * DV_B, D_MODEL), (A_WIDTH + H_B * DV_B) ** -0.5),
        'hgrn_lb': nrm(ks[12], (N_EVEN + 1, H_B * DK_B), 0.5),
        'hgrn_gnorm': 1.0 + nrm(ks[13], (N_EVEN, DV_B), 0.05),
        'pool_w': nrm(ks[14], (N_ODD, len(POOL_WINDOWS), POOL_GROUP, POOL_GROUP), POOL_GROUP ** -0.5),
        'pool_scale': 1.0 + nrm(ks[15], (N_ODD, D_MODEL), 0.05),
        'moe_wg_group': nrm(ks[16], (DEPTH, D_MODEL, N_GROUPS), D_MODEL ** -0.5),
        'moe_bg_group': nrm(ks[17], (DEPTH, N_GROUPS), 0.01),
        'moe_wg_exp': nrm(ks[18], (DEPTH, D_MODEL, N_EXPERTS), D_MODEL ** -0.5),
        'moe_bg_exp': nrm(ks[19], (DEPTH, N_EXPERTS), 0.01),
        'moe_w1': nrm(ks[20], (DEPTH, N_EXPERTS, D_MODEL, D_EXPERT), D_MODEL ** -0.5),
        'moe_w3': nrm(ks[21], (DEPTH, N_EXPERTS, D_MODEL, D_EXPERT), D_MODEL ** -0.5),
        'moe_w2': nrm(ks[22], (DEPTH, N_EXPERTS, D_EXPERT, D_MODEL), D_EXPERT ** -0.5),
    }


def reference(x_prompt, x_sample, cache_win_k, cache_win_v, state_hgrn, state_pool, rel_bias, norm_mix,
              norm_ffn, norm_final, w_in, w_out, hgrn_lb, hgrn_gnorm, pool_w, pool_scale, moe_wg_group,
              moe_bg_group, moe_wg_exp, moe_bg_exp, moe_w1, moe_w3, moe_w2):
    run = functools.partial(trunk, rel_bias=rel_bias, norm_mix=norm_mix, norm_ffn=norm_ffn, norm_final=norm_final,
                            w_in=w_in, w_out=w_out, hgrn_lb=hgrn_lb, hgrn_gnorm=hgrn_gnorm, pool_w=pool_w,
                            pool_scale=pool_scale, moe_wg_group=moe_wg_group, moe_bg_group=moe_bg_group,
                            moe_wg_exp=moe_wg_exp, moe_bg_exp=moe_bg_exp, moe_w1=moe_w1, moe_w3=moe_w3,
                            moe_w2=moe_w2)
    b = x_prompt.shape[0]
    zk = jnp.zeros((N_EVEN, b, W_MAX, H_A, DH_A), x_prompt.dtype)
    zs = jnp.zeros((N_EVEN, b, H_B, DK_B, DV_B), jnp.float32)
    zp = jnp.zeros((N_ODD, b, POOL_KEEP, D_MODEL), x_prompt.dtype)
    y_prompt, k_p, v_p, s_p, p_p = run(x_prompt, zk, zk, zs, zp, 0)
    l_buf = cache_win_k.shape[2]
    pad = ((0, 0), (0, 0), (W_MAX - l_buf, 0), (0, 0), (0, 0))
    y_sample, k_s, v_s, s_s, p_s = run(x_sample, jnp.pad(cache_win_k, pad), jnp.pad(cache_win_v, pad),
                                       state_hgrn, state_pool, PAST_LEN)
    return (y_prompt, y_sample, k_p, v_p, s_p, p_p, k_s, v_s, s_s, p_s)
```

```python
import functools

import numpy as np
import jax
import jax.numpy as jnp
from jax import lax
from jax.experimental import pallas as pl
from jax.experimental.pallas import tpu as pltpu

F32 = jnp.float32
BF16 = jnp.bfloat16
I32 = jnp.int32

D_MODEL = 1024
BATCH = 4
SEQ = 4096
DEC_BATCH = 32
PAST_LEN = 8192
W_MAX = 2048
N_HEADS = 4
D_HEAD = 128
A_WIDTH = N_HEADS * D_HEAD
N_PROJ = 7 * A_WIDTH
DILATED = ((128, 1), (512, 4), (2048, 16))
NUM_BUCKETS = 32
MAX_DISTANCE = 2048
POOL_WINDOWS = (2, 4, 8, 16)
POOL_GROUP = 256
POOL_KEEP = 15
N_GROUPS = 4
EXPERTS_PER_GROUP = 8
N_EXPERTS = 32
D_EXPERT = 512
EPS = 1e-6
NEG = -1e30

LANES = 128
SUBLANES = 8
ROW_VREGS = D_MODEL // LANES
TR = 256
NP = BATCH * SEQ
NV = NP + DEC_BATCH
NT = NP + TR
N_TILES = NT // TR
SAMPLE_TILE = NP // TR
TILES_PER_SEQ = SEQ // TR
TM = 256
N_SLOTS_BLK = (2 * NV + N_EXPERTS * (TM - 1) + TM - 1) // TM
N_SLOTS = N_SLOTS_BLK * TM
QB = 128
CH = 128
N_LEVELS = 7
POOL_HALO = 32
VMEM_LIMIT = 56 * 1024 * 1024


def _cparams(sem=None, vmem=VMEM_LIMIT):
    kw = dict(vmem_limit_bytes=vmem)
    if sem is not None:
        kw["dimension_semantics"] = sem
    return pltpu.CompilerParams(**kw)


def _rms(x, g):
    return x * lax.rsqrt(jnp.mean(x * x, -1, keepdims=True) + EPS) * g


def _sigmoid(x):
    return 1.0 / (1.0 + jnp.exp(-x))


def _silu(x):
    return x * _sigmoid(x)


def _inproj_kernel(x_ref, g_ref, w_ref, *out_refs):
    h = _rms(x_ref[...], g_ref[...])
    p = jnp.dot(h.astype(BF16), w_ref[...], preferred_element_type=F32)
    for i, o_ref in enumerate(out_refs):
        o_ref[...] = p[:, i * A_WIDTH:(i + 1) * A_WIDTH]


def _inproj(x_all, g, w_bf16):
    out_sds = [jax.ShapeDtypeStruct((NT, A_WIDTH), F32)] * 7
    return pl.pallas_call(
        _inproj_kernel,
        grid=(N_TILES,),
        in_specs=[pl.BlockSpec((TR, D_MODEL), lambda i: (i, 0)),
                  pl.BlockSpec((1, D_MODEL), lambda i: (0, 0)),
                  pl.BlockSpec((D_MODEL, N_PROJ), lambda i: (0, 0))],
        out_specs=[pl.BlockSpec((TR, A_WIDTH), lambda i: (i, 0))] * 7,
        out_shape=out_sds,
        compiler_params=_cparams(("arbitrary",)),
        name="inproj",
    )(x_all, g, w_bf16)


def _t5_bucket(dist):
    max_exact = NUM_BUCKETS // 2
    d = np.asarray(dist)
    large = max_exact + np.floor(np.log(np.maximum(d, 1) / max_exact)
                                 / np.log(MAX_DISTANCE / max_exact) * (NUM_BUCKETS - max_exact)).astype(np.int32)
    large = np.minimum(large, NUM_BUCKETS - 1)
    return np.where(d < max_exact, d, large).astype(np.int32)


def _attn_bias_tables(rel_bias):
    i = np.arange(QB)[:, None]
    c = np.arange(2 * QB)[None, :]
    tabs = []
    for w, d in DILATED:
        nk = w // d
        per_var = []
        for off in (0, QB):
            j = off + i - c
            ok = (j >= 0) & (j <= nk)
            bucket = _t5_bucket(d * np.clip(j, 0, nk))
            b = rel_bias.astype(F32)[bucket]
            per_var.append(jnp.where(ok[:, :, None], b, NEG))
        tabs.append(jnp.stack(per_var))
    t = jnp.stack(tabs)
    return jnp.transpose(t, (4, 0, 1, 2, 3))


def _attn_kernel(q_ref, k_ref, v_ref, bias_ref, o_ref,
                 qd, kd, vd, ud, md, sd, u_acc, m_acc, s_acc):
    scale = D_HEAD ** -0.5

    def block_stats(bi, t, nb):
        has_prev = jnp.minimum(t % nb, 1)
        q0 = pl.multiple_of(t * QB, QB)
        k0 = pl.multiple_of((t - has_prev) * QB, QB)
        qb = qd[pl.ds(q0, QB), :]
        kb = kd[pl.ds(k0, 2 * QB), :]
        vb = vd[pl.ds(k0, 2 * QB), :]
        s = lax.dot_general(qb, kb, (((1,), (1,)), ((), ())), preferred_element_type=F32)
        s = s + bias_ref[bi, has_prev]
        mb = jnp.max(s, -1, keepdims=True)
        p = jnp.exp(s - mb)
        sb = jnp.sum(p, -1, keepdims=True)
        u = jnp.dot(p.astype(BF16), vb, preferred_element_type=F32)
        return q0, mb, sb, u

    for bi in (2, 1, 0):
        d = DILATED[bi][1]
        cl = SEQ // d
        nb = cl // QB
        for r in range(d):
            src = pl.ds(r, cl, stride=d) if d > 1 else pl.ds(0, cl)
            dst = pl.ds(r * cl, cl)
            qd[dst, :] = (q_ref[src, :] * scale).astype(BF16)
            kd[dst, :] = k_ref[src, :].astype(BF16)
            vd[dst, :] = v_ref[src, :].astype(BF16)

        if d > 1:
            def body(t, carry, bi=bi, nb=nb):
                q0, mb, sb, u = block_stats(bi, t, nb)
                ud[pl.ds(q0, QB), :] = u
                md[pl.ds(q0, QB), :] = jnp.broadcast_to(mb, (QB, D_HEAD))
                sd[pl.ds(q0, QB), :] = jnp.broadcast_to(sb, (QB, D_HEAD))
                return carry
            lax.fori_loop(0, SEQ // QB, body, 0)
            for r in range(d):
                pos = pl.ds(r, cl, stride=d)
                cls = pl.ds(r * cl, cl)
                if bi == 2:
                    u_acc[pos, :] = ud[cls, :]
                    m_acc[pos, :] = md[cls, :]
                    s_acc[pos, :] = sd[cls, :]
                else:
                    m_old = m_acc[pos, :]
                    m_blk = md[cls, :]
                    m_new = jnp.maximum(m_old, m_blk)
                    a = jnp.exp(m_old - m_new)
                    b = jnp.exp(m_blk - m_new)
                    u_acc[pos, :] = a * u_acc[pos, :] + b * ud[cls, :]
                    s_acc[pos, :] = a * s_acc[pos, :] + b * sd[cls, :]
                    m_acc[pos, :] = m_new
        else:
            def body(t, carry, bi=bi, nb=nb):
                q0, mb, sb, u = block_stats(bi, t, nb)
                rows = pl.ds(q0, QB)
                m_old = m_acc[rows, :]
                m_new = jnp.maximum(m_old, mb)
                a = jnp.exp(m_old - m_new)
                b = jnp.exp(mb - m_new)
                num = a * u_acc[rows, :] + b * u
                den = a * s_acc[rows, :] + b * sb
                o_ref[rows, :] = (num / den).astype(o_ref.dtype)
                return carry
            lax.fori_loop(0, SEQ // QB, body, 0)


def _attention_prompt(q, k, v, bias_tabs):
    blk = pl.BlockSpec((SEQ, D_HEAD), lambda b, h: (b, h))
    return pl.pallas_call(
        _attn_kernel,
        grid=(BATCH, N_HEADS),
        in_specs=[blk, blk, blk,
                  pl.BlockSpec((None, 3, 2, QB, 2 * QB), lambda b, h: (h, 0, 0, 0, 0))],
        out_specs=blk,
        out_shape=jax.ShapeDtypeStruct((NT, A_WIDTH), BF16),
        scratch_shapes=[pltpu.VMEM((SEQ, D_HEAD), BF16)] * 3
                       + [pltpu.VMEM((SEQ, D_HEAD), F32)] * 6,
        compiler_params=_cparams(("arbitrary", "arbitrary")),
        name="attn_prompt",
    )(q, k, v, bias_tabs)


def _hgrn_tables():
    t = np.arange(CH)
    u = np.arange(CH)
    sums = np.zeros((2 + N_LEVELS, CH, CH), np.float32)
    sums[0] = (u[None, :] <= t[:, None])
    sums[1] = (u[None, :] > t[:, None])
    role_q = np.zeros((N_LEVELS, CH, 1), np.float32)
    role_k = np.zeros((N_LEVELS, CH, 1), np.float32)
    pair = np.zeros((N_LEVELS, CH, CH), np.float32)
    for l in range(N_LEVELS):
        h = CH >> (l + 1)
        is_q = (t // h) % 2 == 1
        half_start = (t // h) * h
        half_end = half_start + h
        sel_q = (u[None, :] >= half_start[:, None]) & (u[None, :] <= t[:, None])
        sel_k = (u[None, :] > t[:, None]) & (u[None, :] < half_end[:, None])
        sums[2 + l] = np.where(is_q[:, None], sel_q, sel_k)
        role_q[l, :, 0] = is_q
        role_k[l, :, 0] = ~is_q
        same = (t[:, None] // (2 * h)) == (t[None, :] // (2 * h))
        pair[l] = same & is_q[:, None] & (~is_q)[None, :]
    return (jnp.asarray(sums.reshape((2 + N_LEVELS) * CH, CH), BF16), jnp.asarray(role_q),
            jnp.asarray(role_k), jnp.asarray(pair), jnp.asarray(np.eye(CH, dtype=np.float32)))


def _hgrn_kernel(q_ref, f_ref, i_ref, g_ref, lb_ref, gn_ref, sums_ref, rq_ref, rk_ref, pair_ref, eye_ref,
                 o_ref, s_ref):
    lb = lb_ref[...]
    gn = gn_ref[...]
    nt_dims = (((1,), (1,)), ((), ()))
    tn_dims = (((0,), (0,)), ((), ()))

    def chunk(c, st):
        rows = pl.ds(pl.multiple_of(c * CH, CH), CH)
        q = _silu(q_ref[rows, :])
        f = lb + (1.0 - lb) * _sigmoid(f_ref[rows, :])
        lf = jnp.log(f)
        k = 1.0 - f
        v = i_ref[rows, :]
        v16 = v.astype(BF16)
        lf_hi = lf.astype(BF16)
        lf_lo = (lf - lf_hi.astype(F32)).astype(BF16)
        sums = sums_ref[...]
        ex = jnp.exp(jnp.dot(sums, lf_hi, preferred_element_type=F32)
                     + jnp.dot(sums, lf_lo, preferred_element_type=F32))
        e_cb = ex[0:CH]
        e_rem = ex[CH:2 * CH]
        inter = lax.dot_general((q * e_cb).astype(BF16), st.astype(BF16), nt_dims,
                                preferred_element_type=F32)
        sc = eye_ref[...] * jnp.sum(q * k, -1, keepdims=True)
        for l in range(N_LEVELS):
            e_l = ex[(2 + l) * CH:(3 + l) * CH]
            ql = (q * e_l * rq_ref[l]).astype(BF16)
            kl = (k * e_l * rk_ref[l]).astype(BF16)
            sc = sc + pair_ref[l] * lax.dot_general(ql, kl, nt_dims, preferred_element_type=F32)
        o = inter + jnp.dot(sc.astype(BF16), v16, preferred_element_type=F32)
        st_new = st * e_cb[CH - 1:CH, :] + lax.dot_general(v16, (k * e_rem).astype(BF16), tn_dims,
                                                           preferred_element_type=F32)
        o = _rms(o, gn) * _silu(g_ref[rows, :])
        o_ref[rows, :] = o.astype(o_ref.dtype)
        return st_new

    st = lax.fori_loop(0, SEQ // CH, chunk, jnp.zeros((D_HEAD, D_HEAD), F32))
    s_ref[...] = st.T


def _hgrn_prompt(hq, hf, hi, hg, lb, gnorm):
    sums, rq, rk, pair, eye = _hgrn_tables()
    blk = pl.BlockSpec((SEQ, D_HEAD), lambda b, h: (b, h))

    def full(a):
        return pl.BlockSpec(a.shape, lambda b, h, n=a.ndim: (0,) * n)

    return pl.pallas_call(
        _hgrn_kernel,
        grid=(BATCH, N_HEADS),
        in_specs=[blk, blk, blk, blk,
                  pl.BlockSpec((1, D_HEAD), lambda b, h: (0, h)),
                  pl.BlockSpec((1, D_HEAD), lambda b, h: (0, 0)),
                  full(sums), full(rq), full(rk), full(pair), full(eye)],
        out_specs=[blk, pl.BlockSpec((None, None, D_HEAD, D_HEAD), lambda b, h: (b, h, 0, 0))],
        out_shape=[jax.ShapeDtypeStruct((NT, A_WIDTH), BF16),
                   jax.ShapeDtypeStruct((BATCH, N_HEADS, D_HEAD, D_HEAD), F32)],
        compiler_params=_cparams(("arbitrary", "arbitrary")),
        name="hgrn_prompt",
    )(hq, hf, hi, hg, lb, gnorm, sums, rq, rk, pair, eye)


def _sample_bias_tables(rel_bias):
    tabs = []
    for w, d in DILATED:
        j = np.concatenate([QB - np.arange(QB), [0]])
        tabs.append(rel_bias.astype(F32)[_t5_bucket(d * j)].T)
    return jnp.stack(tabs)[..., None]


def _bf16_round(x):
    return x.astype(BF16).astype(F32)


def _col(row, eye):
    return jnp.sum(eye * row, -1, keepdims=True)


def _sample_kernel(q_ref, k_ref, v_ref, hq_ref, hf_ref, hi_ref, hg_ref, kc_ref, vc_ref, s0_ref,
                   bias_ref, lb_ref, gn_ref, eye_ref, oa_ref, ob_ref, s_ref, oa_acc, ob_acc):
    b = pl.program_id(0)
    row = pl.ds(b, 1)
    scale = D_HEAD ** -0.5
    q = _bf16_round(q_ref[row, :] * scale)
    kn = _bf16_round(k_ref[row, :])
    vn = _bf16_round(v_ref[row, :])
    eye = eye_ref[...]

    stats = []
    for bi, (w, d) in enumerate(DILATED):
        per_head = []
        for h in range(N_HEADS):
            hs = slice(h * D_HEAD, (h + 1) * D_HEAD)
            rows = pl.ds((W_MAX - QB * d) * N_HEADS + h, QB, stride=d * N_HEADS)
            kb = _bf16_round(kc_ref[rows, :])
            vb = _bf16_round(vc_ref[rows, :])
            s = jnp.sum(kb * q[:, hs], -1, keepdims=True) + bias_ref[bi, h, 0:QB]
            s0 = jnp.sum(q[:, hs] * kn[:, hs], -1, keepdims=True) + bias_ref[bi, h, QB:QB + 1]
            m = jnp.maximum(jnp.max(s, 0, keepdims=True), s0)
            p = jnp.exp(s - m)
            p0 = jnp.exp(s0 - m)
            ssum = jnp.sum(p, 0, keepdims=True) + p0
            u = jnp.sum(_bf16_round(p) * vb, 0, keepdims=True) + _bf16_round(p0) * vn[:, hs]
            per_head.append((m, ssum, u))
        stats.append(per_head)
    outs = []
    for h in range(N_HEADS):
        m_all = functools.reduce(jnp.maximum, [stats[bi][h][0] for bi in range(3)])
        num = 0.0
        den = 0.0
        for bi in range(3):
            m, ssum, u = stats[bi][h]
            c = jnp.exp(m - m_all)
            num = num + c * u
            den = den + c * ssum
        outs.append(num / den)
    oa_acc[row, :] = jnp.concatenate(outs, -1)

    qh = _silu(hq_ref[row, :])
    lb = lb_ref[...]
    f = lb + (1.0 - lb) * _sigmoid(hf_ref[row, :])
    vi = hi_ref[row, :]
    gate = _silu(hg_ref[row, :])
    gn = gn_ref[...]
    obs = []
    for h in range(N_HEADS):
        hs = slice(h * D_HEAD, (h + 1) * D_HEAD)
        f_col = _col(f[:, hs], eye)
        q_col = _col(qh[:, hs], eye)
        s_old = s0_ref[h]
        s_ref[h] = f_col * s_old + (1.0 - f_col) * vi[:, hs]
        inter = jnp.sum(_bf16_round(q_col * f_col) * _bf16_round(s_old), 0, keepdims=True)
        qk = jnp.sum(qh[:, hs] * (1.0 - f[:, hs]), -1, keepdims=True)
        o = inter + qk * vi[:, hs]
        obs.append(_rms(o, gn) * gate[:, hs])
    ob_acc[row, :] = jnp.concatenate(obs, -1)

    @pl.when(b == DEC_BATCH - 1)
    def _():
        pad = jnp.zeros((TR - DEC_BATCH, A_WIDTH), F32)
        oa_ref[...] = jnp.concatenate([oa_acc[...], pad], 0).astype(oa_ref.dtype)
        ob_ref[...] = jnp.concatenate([ob_acc[...], pad], 0).astype(ob_ref.dtype)


def _sample_mixers(oa, ob, q, k, v, hq, hf, hi, hg, cache_k, cache_v, state, bias_s, lb, gnorm):
    tile = pl.BlockSpec((TR, A_WIDTH), lambda b: (SAMPLE_TILE, 0))
    cache = pl.BlockSpec((None, W_MAX * N_HEADS, D_HEAD), lambda b: (b, 0, 0))
    eye = jnp.eye(D_HEAD, dtype=F32)

    def kern(oa_in, ob_in, *rest):
        del oa_in, ob_in
        _sample_kernel(*rest)

    return pl.pallas_call(
        kern,
        grid=(DEC_BATCH,),
        in_specs=[pl.BlockSpec(memory_space=pl.ANY), pl.BlockSpec(memory_space=pl.ANY),
                  tile, tile, tile, tile, tile, tile, tile, cache, cache,
                  pl.BlockSpec((None, N_HEADS, D_HEAD, D_HEAD), lambda b: (b, 0, 0, 0)),
                  pl.BlockSpec(bias_s.shape, lambda b: (0, 0, 0, 0)),
                  pl.BlockSpec((1, A_WIDTH), lambda b: (0, 0)),
                  pl.BlockSpec((1, D_HEAD), lambda b: (0, 0)),
                  pl.BlockSpec((D_HEAD, D_HEAD), lambda b: (0, 0))],
        out_specs=[tile, tile,
                   pl.BlockSpec((None, N_HEADS, D_HEAD, D_HEAD), lambda b: (b, 0, 0, 0))],
        out_shape=[jax.ShapeDtypeStruct((NT, A_WIDTH), BF16), jax.ShapeDtypeStruct((NT, A_WIDTH), BF16),
                   jax.ShapeDtypeStruct((DEC_BATCH, N_HEADS, D_HEAD, D_HEAD), F32)],
        scratch_shapes=[pltpu.VMEM((DEC_BATCH, A_WIDTH), F32)] * 2,
        input_output_aliases={0: 0, 1: 1},
        compiler_params=_cparams(("arbitrary",)),
        name="sample_mixers",
    )(oa, ob, q, k, v, hq, hf, hi, hg, cache_k, cache_v, state, bias_s, lb, gnorm, eye)


def _store_rows_as_tiles(ref, val):
    n = val.shape[0]
    for j in range(ROW_VREGS):
        ref[pl.ds(j, n, stride=ROW_VREGS), :] = val[:, j * LANES:(j + 1) * LANES]


def _load_rows_from_tiles(ref, n):
    return jnp.concatenate([ref[pl.ds(j, n, stride=ROW_VREGS), :] for j in range(ROW_VREGS)], -1)


def _route(h2, wr_ref, br_ref, tri_ref, cnt_ref, meta_ref, gate_ref, cnt_out_ref):
    i = pl.program_id(0)

    @pl.when(i == 0)
    def _():
        cnt_ref[...] = jnp.zeros_like(cnt_ref)

    logits = jnp.dot(h2.astype(BF16), wr_ref[...], preferred_element_type=F32) + br_ref[...]
    lane = lax.broadcasted_iota(I32, (TR, LANES), 1).astype(F32)
    big = float(1 << 20)
    is_g = lane < N_GROUPS
    gl = jnp.where(is_g, logits, NEG)
    gmax = jnp.max(gl, -1, keepdims=True)
    gsel = jnp.min(jnp.where(gl == gmax, lane, big), -1, keepdims=True)
    pg = 1.0 / jnp.sum(jnp.where(is_g, jnp.exp(gl - gmax), 0.0), -1, keepdims=True)
    lo = N_GROUPS + EXPERTS_PER_GROUP * gsel
    in_grp = jnp.logical_and(lane >= lo, lane < lo + EXPERTS_PER_GROUP)
    el = jnp.where(in_grp, logits, NEG)
    m1 = jnp.max(el, -1, keepdims=True)
    i1 = jnp.min(jnp.where(el == m1, lane, big), -1, keepdims=True)
    el2 = jnp.where(lane == i1, NEG, el)
    m2 = jnp.max(el2, -1, keepdims=True)
    i2 = jnp.min(jnp.where(el2 == m2, lane, big), -1, keepdims=True)
    r = jnp.exp(m2 - m1)
    g1 = pg / (1.0 + r)
    g2 = pg * r / (1.0 + r)
    e1 = i1 - N_GROUPS
    e2 = i2 - N_GROUPS

    tok = i * TR + lax.broadcasted_iota(I32, (TR, 1), 0)
    valid = tok < NV
    oh1 = jnp.logical_and(lane == e1, valid)
    oh2 = jnp.logical_and(lane == e2, valid)
    oh = jnp.where(jnp.logical_or(oh1, oh2), 1.0, 0.0)
    before = jnp.dot(tri_ref[...], oh.astype(BF16), preferred_element_type=F32) + cnt_ref[...]
    rank1 = jnp.sum(jnp.where(oh1, before, 0.0), -1, keepdims=True)
    rank2 = jnp.sum(jnp.where(oh2, before, 0.0), -1, keepdims=True)
    cnt_ref[...] = cnt_ref[...] + jnp.sum(oh, 0, keepdims=True)

    meta = jnp.where(lane == 0, e1, jnp.where(lane == 1, e2, jnp.where(lane == 2, rank1,
                     jnp.where(lane == 3, rank2, 0.0))))
    gates = jnp.where(lane == 0, g1, jnp.where(lane == 1, g2, 0.0))
    gates = jnp.where(valid, gates, 0.0)
    meta_ref[...] = meta[:, 0:8].astype(I32)
    gate_ref[...] = gates[:, 0:8]
    cnt_out_ref[...] = cnt_ref[...]


def _router_operands(wg_group, bg_group, wg_exp, bg_exp):
    wr = jnp.zeros((D_MODEL, LANES), F32)
    wr = wr.at[:, 0:N_GROUPS].set(wg_group.astype(F32)).at[:, N_GROUPS:N_GROUPS + N_EXPERTS].set(wg_exp.astype(F32))
    br = jnp.zeros((1, LANES), F32)
    br = br.at[0, 0:N_GROUPS].set(bg_group.astype(F32)).at[0, N_GROUPS:N_GROUPS + N_EXPERTS].set(bg_exp.astype(F32))
    tri = jnp.asarray(np.tril(np.ones((TR, TR), np.float32), -1), BF16)
    return wr.astype(BF16), br, tri


_ROUTE_OUT_SPECS = [pl.BlockSpec((TR, 8), lambda i: (i, 0)),
                    pl.BlockSpec((TR, 8), lambda i: (i, 0)),
                    pl.BlockSpec((1, LANES), lambda i: (0, 0))]
_ROUTE_OUT_SHAPES = [jax.ShapeDtypeStruct((NT, 8), I32), jax.ShapeDtypeStruct((NT, 8), F32),
                     jax.ShapeDtypeStruct((1, LANES), F32)]


def _const_spec(a):
    return pl.BlockSpec(a.shape, lambda i, n=a.ndim: (0,) * n)


def _outproj_kernel(x_ref, oa_ref, ob_ref, w_ref, g_ref, wr_ref, br_ref, tri_ref,
                    x1_ref, h2_ref, meta_ref, gate_ref, cnt_out_ref, cnt_ref):
    mixed = jnp.concatenate([oa_ref[...], ob_ref[...]], -1)
    x1 = x_ref[...] + jnp.dot(mixed, w_ref[...], preferred_element_type=F32)
    x1_ref[...] = x1
    h2 = _rms(x1, g_ref[...])
    _store_rows_as_tiles(h2_ref, h2)
    _route(h2, wr_ref, br_ref, tri_ref, cnt_ref, meta_ref, gate_ref, cnt_out_ref)


def _outproj_route(x_all, oa, ob, w_out_bf16, g_ffn, wr, br, tri):
    row = pl.BlockSpec((TR, D_MODEL), lambda i: (i, 0))
    half = pl.BlockSpec((TR, A_WIDTH), lambda i: (i, 0))
    return pl.pallas_call(
        _outproj_kernel,
        grid=(N_TILES,),
        in_specs=[row, half, half, _const_spec(w_out_bf16), _const_spec(g_ffn),
                  _const_spec(wr), _const_spec(br), _const_spec(tri)],
        out_specs=[row, pl.BlockSpec((TR * ROW_VREGS, LANES), lambda i: (i, 0))] + _ROUTE_OUT_SPECS,
        out_shape=[jax.ShapeDtypeStruct((NT, D_MODEL), F32),
                   jax.ShapeDtypeStruct((NT * ROW_VREGS, LANES), F32)] + _ROUTE_OUT_SHAPES,
        scratch_shapes=[pltpu.VMEM((1, LANES), F32)],
        compiler_params=_cparams(("arbitrary",)),
        name="outproj_route",
    )(x_all, oa, ob, w_out_bf16, g_ffn, wr, br, tri)


def _plan(meta, counts):
    cnt = counts[0, :N_EXPERTS].astype(I32)
    nblk = (cnt + TM - 1) // TM
    blk_end = jnp.cumsum(nblk)
    pstart = (blk_end - nblk) * TM
    n_used = blk_end[-1]
    blk = jnp.arange(N_SLOTS_BLK, dtype=I32)
    blk_exp = jnp.searchsorted(blk_end, jnp.minimum(blk, n_used - 1), side="right").astype(I32)
    blk_exp = jnp.minimum(blk_exp, N_EXPERTS - 1)
    eid = meta[:, 0:2]
    rank = meta[:, 2:4]
    tok = jnp.arange(NT, dtype=I32)[:, None]
    dest = jnp.where(tok < NV, pstart[eid] + rank, 0).astype(I32).reshape(-1)
    pad_lo = pstart + cnt
    pad_hi = pstart + nblk * TM
    return dest, blk_exp, n_used.reshape(1).astype(I32), pad_lo.astype(I32), pad_hi.astype(I32)


def _row(ref, r):
    return ref.at[pl.ds(pl.multiple_of(r * ROW_VREGS, ROW_VREGS), ROW_VREGS), :]


def _dispatch_kernel(dest_ref, plo_ref, phi_ref, h_ref, zero_ref, xs_ref, sems, zsem):
    i = pl.program_id(0)
    n_i = jnp.minimum(NV - i * TR, TR)

    def issue(t, carry):
        tok = i * TR + t
        for kk in range(2):
            pltpu.make_async_copy(_row(h_ref, tok), _row(xs_ref, dest_ref[2 * tok + kk]),
                                  sems.at[i % 2]).start()
        return carry
    lax.fori_loop(0, n_i, issue, 0)

    def drain(step):
        n = jnp.minimum(NV - step * TR, TR)

        def wait_one(t, carry):
            for kk in range(2):
                pltpu.make_async_copy(_row(h_ref, 0), _row(xs_ref, 0), sems.at[step % 2]).wait()
            return carry
        lax.fori_loop(0, n, wait_one, 0)

    @pl.when(i > 0)
    def _():
        drain(i - 1)

    @pl.when(i == pl.num_programs(0) - 1)
    def _():
        drain(i)
        def per_expert(e, carry):
            def fill(p, c2):
                pltpu.make_async_copy(zero_ref, _row(xs_ref, p), zsem).start()
                return c2
            lax.fori_loop(plo_ref[e], phi_ref[e], fill, 0)

            def fill_wait(p, c2):
                pltpu.make_async_copy(zero_ref, _row(xs_ref, 0), zsem).wait()
                return c2
            lax.fori_loop(plo_ref[e], phi_ref[e], fill_wait, 0)
            return carry
        lax.fori_loop(0, N_EXPERTS, per_expert, 0)


def _dispatch(h2_tiles, dest, pad_lo, pad_hi):
    zero = jnp.zeros((ROW_VREGS, LANES), F32)
    n_steps = (NV + TR - 1) // TR
    return pl.pallas_call(
        _dispatch_kernel,
        grid_spec=pltpu.PrefetchScalarGridSpec(
            num_scalar_prefetch=3, grid=(n_steps,),
            in_specs=[pl.BlockSpec(memory_space=pl.ANY), pl.BlockSpec(memory_space=pl.ANY)],
            out_specs=pl.BlockSpec(memory_space=pl.ANY),
            scratch_shapes=[pltpu.SemaphoreType.DMA((2,)), pltpu.SemaphoreType.DMA(())]),
        out_shape=jax.ShapeDtypeStruct((N_SLOTS * ROW_VREGS, LANES), F32),
        compiler_params=_cparams(("arbitrary",)),
        name="dispatch",
    )(dest, pad_lo, pad_hi, h2_tiles, zero)


def _gather_back_kernel(dest_ref, y_ref, out_ref, sems):
    i = pl.program_id(0)

    def issue(t, carry):
        a = i * (2 * TR) + t
        pltpu.make_async_copy(_row(y_ref, dest_ref[a]), _row(out_ref, a), sems.at[i % 2]).start()
        return carry
    lax.fori_loop(0, 2 * TR, issue, 0)

    def drain(step):
        def wait_one(t, carry):
            pltpu.make_async_copy(_row(y_ref, 0), _row(out_ref, 0), sems.at[step % 2]).wait()
            return carry
        lax.fori_loop(0, 2 * TR, wait_one, 0)

    @pl.when(i > 0)
    def _():
        drain(i - 1)

    @pl.when(i == pl.num_programs(0) - 1)
    def _():
        drain(i)


def _gather_back(y_sorted, dest):
    return pl.pallas_call(
        _gather_back_kernel,
        grid_spec=pltpu.PrefetchScalarGridSpec(
            num_scalar_prefetch=1, grid=(N_TILES,),
            in_specs=[pl.BlockSpec(memory_space=pl.ANY)],
            out_specs=pl.BlockSpec(memory_space=pl.ANY),
            scratch_shapes=[pltpu.SemaphoreType.DMA((2,))]),
        out_shape=jax.ShapeDtypeStruct((NT * 2 * ROW_VREGS, LANES), F32),
        compiler_params=_cparams(("arbitrary",)),
        name="gather_back",
    )(dest, y_sorted)


def _ffn_kernel(be_ref, nused_ref, xs_ref, w1_ref, w3_ref, w2_ref, y_ref, w1b, w3b, w2b):
    nb = pl.program_id(0)

    @pl.when(nb < nused_ref[0])
    def _():
        prev = be_ref[jnp.maximum(nb - 1, 0)]
        fresh = jnp.logical_or(nb == 0, be_ref[nb] != prev)

        @pl.when(fresh)
        def _():
            w1b[...] = w1_ref[...].astype(BF16)
            w3b[...] = w3_ref[...].astype(BF16)
            w2b[...] = w2_ref[...].astype(BF16)

        x = _load_rows_from_tiles(xs_ref, TM).astype(BF16)
        a = jnp.dot(x, w1b[...], preferred_element_type=F32)
        b = jnp.dot(x, w3b[...], preferred_element_type=F32)
        mid = (_silu(a) * b).astype(BF16)
        y = jnp.dot(mid, w2b[...], preferred_element_type=F32)
        _store_rows_as_tiles(y_ref, y)


def _experts(xs, blk_exp, n_used, w1, w3, w2):
    rows = pl.BlockSpec((TM * ROW_VREGS, LANES), lambda nb, be, nu: (nb, 0))
    return pl.pallas_call(
        _ffn_kernel,
        grid_spec=pltpu.PrefetchScalarGridSpec(
            num_scalar_prefetch=2, grid=(N_SLOTS_BLK,),
            in_specs=[rows,
                      pl.BlockSpec((None, D_MODEL, D_EXPERT), lambda nb, be, nu: (be[nb], 0, 0)),
                      pl.BlockSpec((None, D_MODEL, D_EXPERT), lambda nb, be, nu: (be[nb], 0, 0)),
                      pl.BlockSpec((None, D_EXPERT, D_MODEL), lambda nb, be, nu: (be[nb], 0, 0))],
            out_specs=rows,
            scratch_shapes=[pltpu.VMEM((D_MODEL, D_EXPERT), BF16), pltpu.VMEM((D_MODEL, D_EXPERT), BF16),
                            pltpu.VMEM((D_EXPERT, D_MODEL), BF16)]),
        out_shape=jax.ShapeDtypeStruct((N_SLOTS * ROW_VREGS, LANES), F32),
        compiler_params=_cparams(("arbitrary",)),
        name="experts",
    )(blk_exp, n_used, xs, w1, w3, w2)


def _moe(h2_tiles, meta, counts, w1, w3, w2):
    dest, blk_exp, n_used, pad_lo, pad_hi = _plan(meta, counts)
    xs = _dispatch(h2_tiles, dest, pad_lo, pad_hi)
    ys = _experts(xs, blk_exp, n_used, w1, w3, w2)
    return _gather_back(ys, dest)


def _combine(x, y2_ref, gate_ref):
    g = gate_ref[...]
    for kk in range(2):
        y = jnp.concatenate([y2_ref[pl.ds(kk * ROW_VREGS + j, TR, stride=2 * ROW_VREGS), :]
                             for j in range(ROW_VREGS)], -1)
        x = x + g[:, kk:kk + 1] * y
    return x


def _pool_kernel(x1_ref, y2_ref, gate0_ref, gmix_ref, sp_ref, pw_ref, ps_ref, gffn_ref, wr_ref, br_ref, tri_ref,
                 x3_ref, h2_ref, hkeep_ref, hs_ref, meta_ref, gate_ref, cnt_out_ref,
                 cnt_ref, ext_ref, e1_ref, e2_ref, e3_ref, e4_ref, mixed_ref):
    i = pl.program_id(0)
    x2 = _combine(x1_ref[...], y2_ref, gate0_ref)
    h = _rms(x2, gmix_ref[...])
    H = POOL_HALO
    lvl_refs = (e1_ref, e2_ref, e3_ref, e4_ref)

    @pl.when(i < SAMPLE_TILE)
    def _():
        hkeep_ref[...] = h[TR - 16:TR]
        @pl.when(i % TILES_PER_SEQ == 0)
        def _():
            ext_ref[0:H, :] = jnp.zeros((H, D_MODEL), F32)
        ext_ref[H:H + TR, :] = h
        src = ext_ref
        for lv in range(4):
            sh = 1 << lv
            lo = 8 * (lv + 1)
            c0 = POOL_GROUP * lv
            dst = lvl_refs[lv]
            dst[lo:H + TR, c0:] = src[lo:H + TR, c0:] + src[lo - sh:H + TR - sh, c0:]
            src = dst
        pos = (i % TILES_PER_SEQ) * TR + lax.broadcasted_iota(I32, (TR, 1), 0)
        for gi, w in enumerate(POOL_WINDOWS):
            cs = slice(gi * POOL_GROUP, (gi + 1) * POOL_GROUP)
            inv = 1.0 / jnp.minimum(w, pos + 1).astype(F32)
            mixed_ref[:, cs] = lvl_refs[gi][H:H + TR, cs] * inv - h[:, cs]
        ext_ref[0:H, :] = h[TR - H:TR]

    @pl.when(i == SAMPLE_TILE)
    def _():
        hs = h[0:DEC_BATCH]
        hs_ref[...] = hs
        mixed_ref[...] = jnp.zeros_like(mixed_ref)
        for gi, w in enumerate(POOL_WINDOWS):
            cs = slice(gi * POOL_GROUP, (gi + 1) * POOL_GROUP)
            win = hs[:, cs]
            for dlt in range(1, w):
                win = win + sp_ref[POOL_KEEP - dlt][:, cs]
            mixed_ref[0:DEC_BATCH, cs] = win * (1.0 / w) - hs[:, cs]

    mixed = mixed_ref[...].astype(BF16)
    outs = [jnp.dot(mixed[:, gi * POOL_GROUP:(gi + 1) * POOL_GROUP], pw_ref[gi], preferred_element_type=F32)
            for gi in range(len(POOL_WINDOWS))]
    x3 = x2 + jnp.concatenate(outs, -1) * ps_ref[...]
    x3_ref[...] = x3
    h2 = _rms(x3, gffn_ref[...])
    _store_rows_as_tiles(h2_ref, h2)
    _route(h2, wr_ref, br_ref, tri_ref, cnt_ref, meta_ref, gate_ref, cnt_out_ref)


def _pool_route(x1, y2, gates0, g_mix, sp_t, pw_bf16, p_scale, g_ffn, wr, br, tri):
    row = pl.BlockSpec((TR, D_MODEL), lambda i: (i, 0))
    ext = pltpu.VMEM((POOL_HALO + TR, D_MODEL), F32)
    return pl.pallas_call(
        _pool_kernel,
        grid=(N_TILES,),
        in_specs=[row, pl.BlockSpec((2 * TR * ROW_VREGS, LANES), lambda i: (i, 0)),
                  pl.BlockSpec((TR, 8), lambda i: (i, 0)),
                  _const_spec(g_mix), _const_spec(sp_t), _const_spec(pw_bf16), _const_spec(p_scale),
                  _const_spec(g_ffn), _const_spec(wr), _const_spec(br), _const_spec(tri)],
        out_specs=[row, pl.BlockSpec((TR * ROW_VREGS, LANES), lambda i: (i, 0)),
                   pl.BlockSpec((None, 16, D_MODEL), lambda i: (jnp.minimum(i // TILES_PER_SEQ, BATCH - 1), 0, 0)),
                   pl.BlockSpec((DEC_BATCH, D_MODEL), lambda i: (0, 0))] + _ROUTE_OUT_SPECS,
        out_shape=[jax.ShapeDtypeStruct((NT, D_MODEL), F32),
                   jax.ShapeDtypeStruct((NT * ROW_VREGS, LANES), F32),
                   jax.ShapeDtypeStruct((BATCH, 16, D_MODEL), F32),
                   jax.ShapeDtypeStruct((DEC_BATCH, D_MODEL), F32)] + _ROUTE_OUT_SHAPES,
        scratch_shapes=[pltpu.VMEM((1, LANES), F32), ext, ext, ext, ext, ext,
                        pltpu.VMEM((TR, D_MODEL), F32)],
        compiler_params=_cparams(("arbitrary",)),
        name="pool_route",
    )(x1, y2, gates0, g_mix, sp_t, pw_bf16, p_scale, g_ffn, wr, br, tri)


def _final_kernel(x_ref, y2_ref, gate_ref, g_ref, o_ref):
    o_ref[...] = _rms(_combine(x_ref[...], y2_ref, gate_ref), g_ref[...])


def _final(x3, y2, gates, g_final):
    row = pl.BlockSpec((TR, D_MODEL), lambda i: (i, 0))
    return pl.pallas_call(
        _final_kernel,
        grid=(N_TILES,),
        in_specs=[row, pl.BlockSpec((2 * TR * ROW_VREGS, LANES), lambda i: (i, 0)),
                  pl.BlockSpec((TR, 8), lambda i: (i, 0)), _const_spec(g_final)],
        out_specs=row,
        out_shape=jax.ShapeDtypeStruct((NT, D_MODEL), F32),
        compiler_params=_cparams(("arbitrary",)),
        name="final_norm",
    )(x3, y2, gates, g_final)


def kernel(x_prompt, x_sample, cache_win_k, cache_win_v, state_hgrn, state_pool, rel_bias, norm_mix, norm_ffn,
           norm_final, w_in, w_out, hgrn_lb, hgrn_gnorm, pool_w, pool_scale, moe_wg_group, moe_bg_group,
           moe_wg_exp, moe_bg_exp, moe_w1, moe_w3, moe_w2):
    x_all = jnp.concatenate([x_prompt.reshape(NP, D_MODEL), x_sample.reshape(DEC_BATCH, D_MODEL),
                             jnp.zeros((NT - NV, D_MODEL), F32)], 0)
    lb = jnp.cumsum(jax.nn.softmax(hgrn_lb.astype(F32), axis=0), axis=0)[0:1]
    gnorm = hgrn_gnorm[0:1].astype(F32)

    q, k, v, hq, hf, hi, hg = _inproj(x_all, norm_mix[0:1], w_in[0].astype(BF16))
    oa = _attention_prompt(q, k, v, _attn_bias_tables(rel_bias))
    ob, s_prompt = _hgrn_prompt(hq, hf, hi, hg, lb, gnorm)
    oa, ob, s_sample = _sample_mixers(
        oa, ob, q, k, v, hq, hf, hi, hg,
        cache_win_k[0].reshape(DEC_BATCH, W_MAX * N_HEADS, D_HEAD),
        cache_win_v[0].reshape(DEC_BATCH, W_MAX * N_HEADS, D_HEAD),
        state_hgrn[0], _sample_bias_tables(rel_bias), lb, gnorm)
    wr, br, tri = _router_operands(moe_wg_group[0], moe_bg_group[0], moe_wg_exp[0], moe_bg_exp[0])
    x1, h2, meta, gates0, counts = _outproj_route(x_all, oa, ob, w_out[0].astype(BF16), norm_ffn[0:1], wr, br, tri)
    y2 = _moe(h2, meta, counts, moe_w1[0], moe_w3[0], moe_w2[0])

    wr, br, tri = _router_operands(moe_wg_group[1], moe_bg_group[1], moe_wg_exp[1], moe_bg_exp[1])
    sp_t = jnp.transpose(state_pool[0], (1, 0, 2))
    x3, h2, hkeep, hsample, meta, gates1, counts = _pool_route(
        x1, y2, gates0, norm_mix[1:2], sp_t, pool_w[0].astype(BF16), pool_scale[0:1], norm_ffn[1:2], wr, br, tri)
    y2 = _moe(h2, meta, counts, moe_w1[1], moe_w3[1], moe_w2[1])
    y = _final(x3, y2, gates1, norm_final.reshape(1, D_MODEL))

    keep = W_MAX
    k_p = k[:NP].reshape(BATCH, SEQ, N_HEADS, D_HEAD)[:, SEQ - keep:][None]
    v_p = v[:NP].reshape(BATCH, SEQ, N_HEADS, D_HEAD)[:, SEQ - keep:][None]
    k_s = k[NP:NV].reshape(1, DEC_BATCH, 1, N_HEADS, D_HEAD)
    v_s = v[NP:NV].reshape(1, DEC_BATCH, 1, N_HEADS, D_HEAD)
    return (y[:NP].reshape(BATCH, SEQ, D_MODEL), y[NP:NV].reshape(DEC_BATCH, 1, D_MODEL),
            k_p, v_p, s_prompt[None], hkeep[:, 16 - POOL_KEEP:][None],
            k_s, v_s, s_sample[None], hsample.reshape(1, DEC_BATCH, 1, D_MODEL))
```

```python
import functools

import numpy as np
import jax
import jax.numpy as jnp
from jax import lax
from jax.experimental import pallas as pl
from jax.experimental.pallas import tpu as pltpu

F32 = jnp.float32
BF16 = jnp.bfloat16
I32 = jnp.int32

D_MODEL = 1024
BATCH = 4
SEQ = 4096
DEC_BATCH = 32
PAST_LEN = 8192
W_MAX = 2048
N_HEADS = 4
D_HEAD = 128
A_WIDTH = N_HEADS * D_HEAD
N_PROJ = 7 * A_WIDTH
DILATED = ((128, 1), (512, 4), (2048, 16))
NUM_BUCKETS = 32
MAX_DISTANCE = 2048
POOL_WINDOWS = (2, 4, 8, 16)
POOL_GROUP = 256
POOL_KEEP = 15
N_GROUPS = 4
EXPERTS_PER_GROUP = 8
N_EXPERTS = 32
D_EXPERT = 512
EPS = 1e-6
NEG = -1e30

LANES = 128
SUBLANES = 8
ROW_VREGS = D_MODEL // LANES
TR = 256
NP = BATCH * SEQ
NV = NP + DEC_BATCH
NT = NP + TR
N_TILES = NT // TR
SAMPLE_TILE = NP // TR
TILES_PER_SEQ = SEQ // TR
TM = 256
N_SLOTS_BLK = (2 * NV + N_EXPERTS * (TM - 1) + TM - 1) // TM
N_SLOTS = N_SLOTS_BLK * TM
QB = 128
CH = 128
N_LEVELS = 7
POOL_HALO = 32
VMEM_LIMIT = 56 * 1024 * 1024


def _cparams(sem=None, vmem=VMEM_LIMIT):
    kw = dict(vmem_limit_bytes=vmem)
    if sem is not None:
        kw["dimension_semantics"] = sem
    return pltpu.CompilerParams(**kw)


def _rms(x, g):
    return x * lax.rsqrt(jnp.mean(x * x, -1, keepdims=True) + EPS) * g


def _sigmoid(x):
    return 1.0 / (1.0 + jnp.exp(-x))


def _silu(x):
    return x * _sigmoid(x)


def _token_tile(i, xp_ref, xs_ref):
    return jnp.where(i == SAMPLE_TILE, xs_ref[...], xp_ref[...])


_PROMPT_ROWS = pl.BlockSpec((TR, D_MODEL), lambda i: (jnp.minimum(i, SAMPLE_TILE - 1), 0))
_SAMPLE_ROWS = pl.BlockSpec((TR, D_MODEL), lambda i: (0, 0))
WIN_TILES = W_MAX // TR


def _window_block(i):
    seq = jnp.minimum(i // TILES_PER_SEQ, BATCH - 1)
    j = jnp.clip(i % TILES_PER_SEQ - (TILES_PER_SEQ - WIN_TILES), 0, WIN_TILES - 1)
    return jnp.where(i >= SAMPLE_TILE, BATCH * WIN_TILES - 1, seq * WIN_TILES + j)


def _inproj_kernel(xp_ref, xs_ref, g_ref, w_ref, *out_refs):
    i = pl.program_id(0)
    h = _rms(_token_tile(i, xp_ref, xs_ref), g_ref[...])
    p = jnp.dot(h.astype(BF16), w_ref[...], preferred_element_type=F32)
    for n, o_ref in enumerate(out_refs[:7]):
        o_ref[...] = p[:, n * A_WIDTH:(n + 1) * A_WIDTH]

    @pl.when(jnp.logical_and(i < SAMPLE_TILE, i % TILES_PER_SEQ >= TILES_PER_SEQ - WIN_TILES))
    def _():
        for n, o_ref in ((1, out_refs[7]), (2, out_refs[8])):
            for h_i in range(N_HEADS):
                c0 = n * A_WIDTH + h_i * D_HEAD
                o_ref[pl.ds(h_i, TR, stride=N_HEADS), :] = p[:, c0:c0 + D_HEAD]


def _inproj(xp, xs_pad, g, w_bf16):
    out_sds = ([jax.ShapeDtypeStruct((NT, A_WIDTH), F32)] * 7
               + [jax.ShapeDtypeStruct((BATCH * W_MAX * N_HEADS, D_HEAD), F32)] * 2)
    win = pl.BlockSpec((TR * N_HEADS, D_HEAD), lambda i: (_window_block(i), 0))
    return pl.pallas_call(
        _inproj_kernel,
        grid=(N_TILES,),
        in_specs=[_PROMPT_ROWS, _SAMPLE_ROWS,
                  pl.BlockSpec((1, D_MODEL), lambda i: (0, 0)),
                  pl.BlockSpec((D_MODEL, N_PROJ), lambda i: (0, 0))],
        out_specs=[pl.BlockSpec((TR, A_WIDTH), lambda i: (i, 0))] * 7 + [win, win],
        out_shape=out_sds,
        compiler_params=_cparams(("arbitrary",)),
        name="inproj",
    )(xp, xs_pad, g, w_bf16)


def _t5_bucket(dist):
    max_exact = NUM_BUCKETS // 2
    d = np.asarray(dist)
    large = max_exact + np.floor(np.log(np.maximum(d, 1) / max_exact)
                                 / np.log(MAX_DISTANCE / max_exact) * (NUM_BUCKETS - max_exact)).astype(np.int32)
    large = np.minimum(large, NUM_BUCKETS - 1)
    return np.where(d < max_exact, d, large).astype(np.int32)


def _attn_bias_tables(rel_bias):
    period = 3 * QB
    m = np.arange(period)
    u = np.where(m < 2 * QB, m, m - period)
    pick = np.zeros((len(DILATED), 2, period, NUM_BUCKETS), np.float32)
    mask = np.zeros((len(DILATED), 2, period, 1), np.float32)
    for bi, (w, d) in enumerate(DILATED):
        nk = w // d
        for vi, off in enumerate((0, QB)):
            j = off - u
            ok = (j >= 0) & (j <= nk)
            pick[bi, vi, m[ok], _t5_bucket(d * j[ok])] = 1.0
            mask[bi, vi, ~ok, 0] = NEG
    vec = jnp.einsum("bvmk,kh->hbvm", pick, rel_bias.astype(F32), precision=lax.Precision.HIGHEST)
    vec = vec + jnp.transpose(mask, (3, 0, 1, 2))
    tiled = jnp.tile(vec, (1, 1, 1, QB))[..., :QB * (period - 1)]
    return tiled.reshape(N_HEADS, len(DILATED), 2, QB, period - 1)[..., :2 * QB]


def _attn_kernel(q_ref, k_ref, v_ref, bias_ref, o_ref,
                 qd, kd, vd, ud, md, sd, u_acc, m_acc, s_acc):
    scale = D_HEAD ** -0.5

    def block_stats(bi, t, nb):
        has_prev = jnp.minimum(t % nb, 1)
        q0 = pl.multiple_of(t * QB, QB)
        k0 = pl.multiple_of((t - has_prev) * QB, QB)
        qb = qd[pl.ds(q0, QB), :]
        kb = kd[pl.ds(k0, 2 * QB), :]
        vb = vd[pl.ds(k0, 2 * QB), :]
        s = lax.dot_general(qb, kb, (((1,), (1,)), ((), ())), preferred_element_type=F32)
        s = s + bias_ref[bi, has_prev]
        mb = jnp.max(s, -1, keepdims=True)
        p = jnp.exp(s - mb)
        sb = jnp.sum(p, -1, keepdims=True)
        u = jnp.dot(p.astype(BF16), vb, preferred_element_type=F32)
        return q0, mb, sb, u

    for bi in (2, 1, 0):
        d = DILATED[bi][1]
        cl = SEQ // d
        nb = cl // QB
        for r in range(d):
            src = pl.ds(r, cl, stride=d) if d > 1 else pl.ds(0, cl)
            dst = pl.ds(r * cl, cl)
            qd[dst, :] = (q_ref[src, :] * scale).astype(BF16)
            kd[dst, :] = k_ref[src, :].astype(BF16)
            vd[dst, :] = v_ref[src, :].astype(BF16)

        if d > 1:
            def body(t, carry, bi=bi, nb=nb):
                q0, mb, sb, u = block_stats(bi, t, nb)
                ud[pl.ds(q0, QB), :] = u
                md[pl.ds(q0, QB), :] = jnp.broadcast_to(mb, (QB, D_HEAD))
                sd[pl.ds(q0, QB), :] = jnp.broadcast_to(sb, (QB, D_HEAD))
                return carry
            lax.fori_loop(0, SEQ // QB, body, 0)
            for r in range(d):
                pos = pl.ds(r, cl, stride=d)
                cls = pl.ds(r * cl, cl)
                if bi == 2:
                    u_acc[pos, :] = ud[cls, :]
                    m_acc[pos, :] = md[cls, :]
                    s_acc[pos, :] = sd[cls, :]
                else:
                    m_old = m_acc[pos, :]
                    m_blk = md[cls, :]
                    m_new = jnp.maximum(m_old, m_blk)
                    a = jnp.exp(m_old - m_new)
                    b = jnp.exp(m_blk - m_new)
                    u_acc[pos, :] = a * u_acc[pos, :] + b * ud[cls, :]
                    s_acc[pos, :] = a * s_acc[pos, :] + b * sd[cls, :]
                    m_acc[pos, :] = m_new
        else:
            def body(t, carry, bi=bi, nb=nb):
                q0, mb, sb, u = block_stats(bi, t, nb)
                rows = pl.ds(q0, QB)
                m_old = m_acc[rows, :]
                m_new = jnp.maximum(m_old, mb)
                a = jnp.exp(m_old - m_new)
                b = jnp.exp(mb - m_new)
                num = a * u_acc[rows, :] + b * u
                den = a * s_acc[rows, :] + b * sb
                o_ref[rows, :] = (num / den).astype(o_ref.dtype)
                return carry
            lax.fori_loop(0, SEQ // QB, body, 0)


def _attention_prompt(q, k, v, bias_tabs):
    blk = pl.BlockSpec((SEQ, D_HEAD), lambda b, h: (b, h))
    return pl.pallas_call(
        _attn_kernel,
        grid=(BATCH, N_HEADS),
        in_specs=[blk, blk, blk,
                  pl.BlockSpec((None, 3, 2, QB, 2 * QB), lambda b, h: (h, 0, 0, 0, 0))],
        out_specs=blk,
        out_shape=jax.ShapeDtypeStruct((NP, A_WIDTH), BF16),
        scratch_shapes=[pltpu.VMEM((SEQ, D_HEAD), BF16)] * 3
                       + [pltpu.VMEM((SEQ, D_HEAD), F32)] * 6,
        compiler_params=_cparams(("arbitrary", "arbitrary")),
        name="attn_prompt",
    )(q, k, v, bias_tabs)


def _hgrn_tables():
    t = np.arange(CH)
    u = np.arange(CH)
    sums = np.zeros((2 + N_LEVELS, CH, CH), np.float32)
    sums[0] = (u[None, :] <= t[:, None])
    sums[1] = (u[None, :] > t[:, None])
    role_q = np.zeros((N_LEVELS, CH, 1), np.float32)
    role_k = np.zeros((N_LEVELS, CH, 1), np.float32)
    pair = np.zeros((N_LEVELS, CH, CH), np.float32)
    for l in range(N_LEVELS):
        h = CH >> (l + 1)
        is_q = (t // h) % 2 == 1
        half_start = (t // h) * h
        half_end = half_start + h
        sel_q = (u[None, :] >= half_start[:, None]) & (u[None, :] <= t[:, None])
        sel_k = (u[None, :] > t[:, None]) & (u[None, :] < half_end[:, None])
        sums[2 + l] = np.where(is_q[:, None], sel_q, sel_k)
        role_q[l, :, 0] = is_q
        role_k[l, :, 0] = ~is_q
        same = (t[:, None] // (2 * h)) == (t[None, :] // (2 * h))
        pair[l] = same & is_q[:, None] & (~is_q)[None, :]
    return (jnp.asarray(sums.reshape((2 + N_LEVELS) * CH, CH), BF16), jnp.asarray(role_q),
            jnp.asarray(role_k), jnp.asarray(pair), jnp.asarray(np.eye(CH, dtype=np.float32)))


def _hgrn_kernel(q_ref, f_ref, i_ref, g_ref, lb_ref, gn_ref, sums_ref, rq_ref, rk_ref, pair_ref, eye_ref,
                 o_ref, s_ref):
    lb = lb_ref[...]
    gn = gn_ref[...]
    nt_dims = (((1,), (1,)), ((), ()))
    tn_dims = (((0,), (0,)), ((), ()))

    def chunk(c, st):
        rows = pl.ds(pl.multiple_of(c * CH, CH), CH)
        q = _silu(q_ref[rows, :])
        f = lb + (1.0 - lb) * _sigmoid(f_ref[rows, :])
        lf = jnp.log(f)
        k = 1.0 - f
        v = i_ref[rows, :]
        v16 = v.astype(BF16)
        lf_hi = lf.astype(BF16)
        lf_lo = (lf - lf_hi.astype(F32)).astype(BF16)
        sums = sums_ref[...]
        ex = jnp.exp(jnp.dot(sums, lf_hi, preferred_element_type=F32)
                     + jnp.dot(sums, lf_lo, preferred_element_type=F32))
        e_cb = ex[0:CH]
        e_rem = ex[CH:2 * CH]
        inter = lax.dot_general((q * e_cb).astype(BF16), st.astype(BF16), nt_dims,
                                preferred_element_type=F32)
        sc = eye_ref[...] * jnp.sum(q * k, -1, keepdims=True)
        for l in range(N_LEVELS):
            e_l = ex[(2 + l) * CH:(3 + l) * CH]
            ql = (q * e_l * rq_ref[l]).astype(BF16)
            kl = (k * e_l * rk_ref[l]).astype(BF16)
            sc = sc + pair_ref[l] * lax.dot_general(ql, kl, nt_dims, preferred_element_type=F32)
        o = inter + jnp.dot(sc.astype(BF16), v16, preferred_element_type=F32)
        st_new = st * e_cb[CH - 1:CH, :] + lax.dot_general(v16, (k * e_rem).astype(BF16), tn_dims,
                                                           preferred_element_type=F32)
        o = _rms(o, gn) * _silu(g_ref[rows, :])
        o_ref[rows, :] = o.astype(o_ref.dtype)
        return st_new

    st = lax.fori_loop(0, SEQ // CH, chunk, jnp.zeros((D_HEAD, D_HEAD), F32))
    s_ref[...] = st.T


def _hgrn_prompt(hq, hf, hi, hg, lb, gnorm):
    sums, rq, rk, pair, eye = _hgrn_tables()
    blk = pl.BlockSpec((SEQ, D_HEAD), lambda b, h: (b, h))

    def full(a):
        return pl.BlockSpec(a.shape, lambda b, h, n=a.ndim: (0,) * n)

    return pl.pallas_call(
        _hgrn_kernel,
        grid=(BATCH, N_HEADS),
        in_specs=[blk, blk, blk, blk,
                  pl.BlockSpec((1, D_HEAD), lambda b, h: (0, h)),
                  pl.BlockSpec((1, D_HEAD), lambda b, h: (0, 0)),
                  full(sums), full(rq), full(rk), full(pair), full(eye)],
        out_specs=[blk, pl.BlockSpec((None, None, D_HEAD, D_HEAD), lambda b, h: (b, h, 0, 0))],
        out_shape=[jax.ShapeDtypeStruct((NP, A_WIDTH), BF16),
                   jax.ShapeDtypeStruct((BATCH, N_HEADS, D_HEAD, D_HEAD), F32)],
        compiler_params=_cparams(("arbitrary", "arbitrary")),
        name="hgrn_prompt",
    )(hq, hf, hi, hg, lb, gnorm, sums, rq, rk, pair, eye)


def _sample_bias_tables(rel_bias):
    j = np.concatenate([QB - np.arange(QB), [0]])
    pick = np.zeros((len(DILATED), QB + 1, NUM_BUCKETS), np.float32)
    for bi, (w, d) in enumerate(DILATED):
        pick[bi, np.arange(QB + 1), _t5_bucket(d * j)] = 1.0
    return jnp.einsum("bjk,kh->bhj", pick, rel_bias.astype(F32), precision=lax.Precision.HIGHEST)[..., None]


def _bf16_round(x):
    return x.astype(BF16).astype(F32)


def _col(row, eye):
    return jnp.sum(eye * row, -1, keepdims=True)


def _sample_kernel(q_ref, k_ref, v_ref, hq_ref, hf_ref, hi_ref, hg_ref, kc_ref, vc_ref, s0_ref,
                   bias_ref, lb_ref, gn_ref, eye_ref, oa_ref, ob_ref, s_ref, oa_acc, ob_acc):
    b = pl.program_id(0)
    row = pl.ds(b, 1)
    scale = D_HEAD ** -0.5
    q = _bf16_round(q_ref[row, :] * scale)
    kn = _bf16_round(k_ref[row, :])
    vn = _bf16_round(v_ref[row, :])
    eye = eye_ref[...]

    stats = []
    for bi, (w, d) in enumerate(DILATED):
        per_head = []
        for h in range(N_HEADS):
            hs = slice(h * D_HEAD, (h + 1) * D_HEAD)
            rows = pl.ds((W_MAX - QB * d) * N_HEADS + h, QB, stride=d * N_HEADS)
            kb = _bf16_round(kc_ref[rows, :])
            vb = _bf16_round(vc_ref[rows, :])
            s = jnp.sum(kb * q[:, hs], -1, keepdims=True) + bias_ref[bi, h, 0:QB]
            s0 = jnp.sum(q[:, hs] * kn[:, hs], -1, keepdims=True) + bias_ref[bi, h, QB:QB + 1]
            m = jnp.maximum(jnp.max(s, 0, keepdims=True), s0)
            p = jnp.exp(s - m)
            p0 = jnp.exp(s0 - m)
            ssum = jnp.sum(p, 0, keepdims=True) + p0
            u = jnp.sum(_bf16_round(p) * vb, 0, keepdims=True) + _bf16_round(p0) * vn[:, hs]
            per_head.append((m, ssum, u))
        stats.append(per_head)
    outs = []
    for h in range(N_HEADS):
        m_all = functools.reduce(jnp.maximum, [stats[bi][h][0] for bi in range(3)])
        num = 0.0
        den = 0.0
        for bi in range(3):
            m, ssum, u = stats[bi][h]
            c = jnp.exp(m - m_all)
            num = num + c * u
            den = den + c * ssum
        outs.append(num / den)
    oa_acc[row, :] = jnp.concatenate(outs, -1)

    qh = _silu(hq_ref[row, :])
    lb = lb_ref[...]
    f = lb + (1.0 - lb) * _sigmoid(hf_ref[row, :])
    vi = hi_ref[row, :]
    gate = _silu(hg_ref[row, :])
    gn = gn_ref[...]
    obs = []
    for h in range(N_HEADS):
        hs = slice(h * D_HEAD, (h + 1) * D_HEAD)
        f_col = _col(f[:, hs], eye)
        q_col = _col(qh[:, hs], eye)
        s_old = s0_ref[h]
        s_ref[h] = f_col * s_old + (1.0 - f_col) * vi[:, hs]
        inter = jnp.sum(_bf16_round(q_col * f_col) * _bf16_round(s_old), 0, keepdims=True)
        qk = jnp.sum(qh[:, hs] * (1.0 - f[:, hs]), -1, keepdims=True)
        o = inter + qk * vi[:, hs]
        obs.append(_rms(o, gn) * gate[:, hs])
    ob_acc[row, :] = jnp.concatenate(obs, -1)

    @pl.when(b == DEC_BATCH - 1)
    def _():
        pad = jnp.zeros((TR - DEC_BATCH, A_WIDTH), F32)
        oa_ref[...] = jnp.concatenate([oa_acc[...], pad], 0).astype(oa_ref.dtype)
        ob_ref[...] = jnp.concatenate([ob_acc[...], pad], 0).astype(ob_ref.dtype)


def _sample_mixers(q, k, v, hq, hf, hi, hg, cache_k, cache_v, state, bias_s, lb, gnorm):
    tile = pl.BlockSpec((TR, A_WIDTH), lambda b: (SAMPLE_TILE, 0))
    out_tile = pl.BlockSpec((TR, A_WIDTH), lambda b: (0, 0))
    cache = pl.BlockSpec((None, W_MAX * N_HEADS, D_HEAD), lambda b: (b, 0, 0))
    eye = jnp.eye(D_HEAD, dtype=F32)

    return pl.pallas_call(
        _sample_kernel,
        grid=(DEC_BATCH,),
        in_specs=[tile, tile, tile, tile, tile, tile, tile, cache, cache,
                  pl.BlockSpec((None, N_HEADS, D_HEAD, D_HEAD), lambda b: (b, 0, 0, 0)),
                  pl.BlockSpec(bias_s.shape, lambda b: (0, 0, 0, 0)),
                  pl.BlockSpec((1, A_WIDTH), lambda b: (0, 0)),
                  pl.BlockSpec((1, D_HEAD), lambda b: (0, 0)),
                  pl.BlockSpec((D_HEAD, D_HEAD), lambda b: (0, 0))],
        out_specs=[out_tile, out_tile,
                   pl.BlockSpec((None, N_HEADS, D_HEAD, D_HEAD), lambda b: (b, 0, 0, 0))],
        out_shape=[jax.ShapeDtypeStruct((TR, A_WIDTH), BF16), jax.ShapeDtypeStruct((TR, A_WIDTH), BF16),
                   jax.ShapeDtypeStruct((DEC_BATCH, N_HEADS, D_HEAD, D_HEAD), F32)],
        scratch_shapes=[pltpu.VMEM((DEC_BATCH, A_WIDTH), F32)] * 2,
        compiler_params=_cparams(("arbitrary",)),
        name="sample_mixers",
    )(q, k, v, hq, hf, hi, hg, cache_k, cache_v, state, bias_s, lb, gnorm, eye)


def _store_rows_as_tiles(ref, val):
    n = val.shape[0]
    for j in range(ROW_VREGS):
        ref[pl.ds(j, n, stride=ROW_VREGS), :] = val[:, j * LANES:(j + 1) * LANES]


def _load_rows_from_tiles(ref, n):
    return jnp.concatenate([ref[pl.ds(j, n, stride=ROW_VREGS), :] for j in range(ROW_VREGS)], -1)


def _route(h2, wr_ref, br_ref, tri_ref, cnt_ref, meta_ref, gate_ref, cnt_out_ref):
    i = pl.program_id(0)

    @pl.when(i == 0)
    def _():
        cnt_ref[...] = jnp.zeros_like(cnt_ref)

    logits = jnp.dot(h2.astype(BF16), wr_ref[...], preferred_element_type=F32) + br_ref[...]
    lane = lax.broadcasted_iota(I32, (TR, LANES), 1).astype(F32)
    big = float(1 << 20)
    is_g = lane < N_GROUPS
    gl = jnp.where(is_g, logits, NEG)
    gmax = jnp.max(gl, -1, keepdims=True)
    gsel = jnp.min(jnp.where(gl == gmax, lane, big), -1, keepdims=True)
    pg = 1.0 / jnp.sum(jnp.where(is_g, jnp.exp(gl - gmax), 0.0), -1, keepdims=True)
    lo = N_GROUPS + EXPERTS_PER_GROUP * gsel
    in_grp = jnp.logical_and(lane >= lo, lane < lo + EXPERTS_PER_GROUP)
    el = jnp.where(in_grp, logits, NEG)
    m1 = jnp.max(el, -1, keepdims=True)
    i1 = jnp.min(jnp.where(el == m1, lane, big), -1, keepdims=True)
    el2 = jnp.where(lane == i1, NEG, el)
    m2 = jnp.max(el2, -1, keepdims=True)
    i2 = jnp.min(jnp.where(el2 == m2, lane, big), -1, keepdims=True)
    r = jnp.exp(m2 - m1)
    g1 = pg / (1.0 + r)
    g2 = pg * r / (1.0 + r)
    e1 = i1 - N_GROUPS
    e2 = i2 - N_GROUPS

    tok = i * TR + lax.broadcasted_iota(I32, (TR, 1), 0)
    valid = tok < NV
    oh1 = jnp.logical_and(lane == e1, valid)
    oh2 = jnp.logical_and(lane == e2, valid)
    oh = jnp.where(jnp.logical_or(oh1, oh2), 1.0, 0.0)
    before = jnp.dot(tri_ref[...], oh.astype(BF16), preferred_element_type=F32) + cnt_ref[...]
    rank1 = jnp.sum(jnp.where(oh1, before, 0.0), -1, keepdims=True)
    rank2 = jnp.sum(jnp.where(oh2, before, 0.0), -1, keepdims=True)
    cnt_ref[...] = cnt_ref[...] + jnp.sum(oh, 0, keepdims=True)

    meta = jnp.where(lane == 0, e1, jnp.where(lane == 1, e2, jnp.where(lane == 2, rank1,
                     jnp.where(lane == 3, rank2, 0.0))))
    gates = jnp.where(lane == 0, g1, jnp.where(lane == 1, g2, 0.0))
    gates = jnp.where(valid, gates, 0.0)
    meta_ref[...] = meta[:, 0:8].astype(I32)
    gate_ref[...] = gates[:, 0:8]
    cnt_out_ref[...] = cnt_ref[...]


def _router_operands(wg_group, bg_group, wg_exp, bg_exp):
    wr = jnp.zeros((D_MODEL, LANES), F32)
    wr = wr.at[:, 0:N_GROUPS].set(wg_group.astype(F32)).at[:, N_GROUPS:N_GROUPS + N_EXPERTS].set(wg_exp.astype(F32))
    br = jnp.zeros((1, LANES), F32)
    br = br.at[0, 0:N_GROUPS].set(bg_group.astype(F32)).at[0, N_GROUPS:N_GROUPS + N_EXPERTS].set(bg_exp.astype(F32))
    tri = jnp.asarray(np.tril(np.ones((TR, TR), np.float32), -1), BF16)
    return wr.astype(BF16), br, tri


_ROUTE_OUT_SPECS = [pl.BlockSpec((TR, 8), lambda i: (i, 0)),
                    pl.BlockSpec((TR, 8), lambda i: (i, 0)),
                    pl.BlockSpec((1, LANES), lambda i: (0, 0))]
_ROUTE_OUT_SHAPES = [jax.ShapeDtypeStruct((NT, 8), I32), jax.ShapeDtypeStruct((NT, 8), F32),
                     jax.ShapeDtypeStruct((1, LANES), F32)]


def _const_spec(a):
    return pl.BlockSpec(a.shape, lambda i, n=a.ndim: (0,) * n)


def _outproj_kernel(xp_ref, xs_ref, oa_ref, oas_ref, ob_ref, obs_ref, w_ref, g_ref, wr_ref, br_ref, tri_ref,
                    x1_ref, h2_ref, meta_ref, gate_ref, cnt_out_ref, cnt_ref):
    i = pl.program_id(0)
    x = _token_tile(i, xp_ref, xs_ref)
    mixed = jnp.concatenate([_token_tile(i, oa_ref, oas_ref), _token_tile(i, ob_ref, obs_ref)], -1)
    x1 = x + jnp.dot(mixed, w_ref[...], preferred_element_type=F32)
    x1_ref[...] = x1
    h2 = _rms(x1, g_ref[...])
    _store_rows_as_tiles(h2_ref, h2)
    _route(h2, wr_ref, br_ref, tri_ref, cnt_ref, meta_ref, gate_ref, cnt_out_ref)


def _outproj_route(xp, xs_pad, oa, oa_s, ob, ob_s, w_out_bf16, g_ffn, wr, br, tri):
    row = pl.BlockSpec((TR, D_MODEL), lambda i: (i, 0))
    half = pl.BlockSpec((TR, A_WIDTH), lambda i: (jnp.minimum(i, SAMPLE_TILE - 1), 0))
    half_s = pl.BlockSpec((TR, A_WIDTH), lambda i: (0, 0))
    return pl.pallas_call(
        _outproj_kernel,
        grid=(N_TILES,),
        in_specs=[_PROMPT_ROWS, _SAMPLE_ROWS, half, half_s, half, half_s,
                  _const_spec(w_out_bf16), _const_spec(g_ffn),
                  _const_spec(wr), _const_spec(br), _const_spec(tri)],
        out_specs=[row, pl.BlockSpec((TR * ROW_VREGS, LANES), lambda i: (i, 0))] + _ROUTE_OUT_SPECS,
        out_shape=[jax.ShapeDtypeStruct((NT, D_MODEL), F32),
                   jax.ShapeDtypeStruct((NT * ROW_VREGS, LANES), F32)] + _ROUTE_OUT_SHAPES,
        scratch_shapes=[pltpu.VMEM((1, LANES), F32)],
        compiler_params=_cparams(("arbitrary",)),
        name="outproj_route",
    )(xp, xs_pad, oa, oa_s, ob, ob_s, w_out_bf16, g_ffn, wr, br, tri)


def _plan(meta, counts):
    cnt = counts[0, :N_EXPERTS].astype(I32)
    nblk = (cnt + TM - 1) // TM
    blk_end = jnp.cumsum(nblk)
    pstart = (blk_end - nblk) * TM
    n_used = blk_end[-1]
    blk = jnp.minimum(jnp.arange(N_SLOTS_BLK, dtype=I32), n_used - 1)
    blk_exp = jnp.sum((blk[:, None] >= blk_end[None, :]).astype(I32), -1)
    blk_exp = jnp.minimum(blk_exp, N_EXPERTS - 1)
    experts = jnp.arange(N_EXPERTS, dtype=I32)
    start_of = jnp.sum(jnp.where(meta[:, 0:2, None] == experts, pstart, 0), -1)
    tok = jnp.arange(NT, dtype=I32)[:, None]
    dest = jnp.where(tok < NV, start_of + meta[:, 2:4], 0).astype(I32).reshape(-1)
    pad_lo = pstart + cnt
    pad_hi = pstart + nblk * TM
    return dest, blk_exp, n_used.reshape(1).astype(I32), pad_lo.astype(I32), pad_hi.astype(I32)


def _row(ref, r):
    return ref.at[pl.ds(pl.multiple_of(r * ROW_VREGS, ROW_VREGS), ROW_VREGS), :]


def _dispatch_kernel(dest_ref, plo_ref, phi_ref, nused_ref, h_ref, xs_ref, zero_ref, sem, zsem):
    i = pl.program_id(0)
    last = pl.num_programs(0) - 1

    def scatter(n_tok):
        def issue(t, carry):
            tok = i * TR + t
            for kk in range(2):
                pltpu.make_async_copy(_row(h_ref, t), _row(xs_ref, dest_ref[2 * tok + kk]), sem).start()
            return carry
        lax.fori_loop(0, n_tok, issue, 0)
        n_rows = 2 * n_tok * ROW_VREGS
        pltpu.make_async_copy(h_ref.at[pl.ds(0, n_rows // 2), :], xs_ref.at[pl.ds(0, n_rows // 2), :], sem).wait()
        pltpu.make_async_copy(h_ref.at[pl.ds(0, n_rows // 2), :], xs_ref.at[pl.ds(0, n_rows // 2), :], sem).wait()

    @pl.when(i < last)
    def _():
        scatter(TR)

    @pl.when(i == last)
    def _():
        scatter(NV - (N_TILES - 1) * TR)
        zero_ref[...] = jnp.zeros_like(zero_ref)

        zero_row = zero_ref.at[pl.ds(0, ROW_VREGS), :]

        def per_expert(e, carry):
            def fill(p, c2):
                pltpu.make_async_copy(zero_row, _row(xs_ref, p), zsem).start()
                return c2
            lax.fori_loop(plo_ref[e], phi_ref[e], fill, 0)

            def fill_wait(p, c2):
                pltpu.make_async_copy(zero_row, _row(xs_ref, 0), zsem).wait()
                return c2
            lax.fori_loop(plo_ref[e], phi_ref[e], fill_wait, 0)
            return carry
        lax.fori_loop(0, N_EXPERTS, per_expert, 0)

        def block_of(nb):
            return xs_ref.at[pl.ds(pl.multiple_of(nb * (TM * ROW_VREGS), TM * ROW_VREGS), TM * ROW_VREGS), :]

        def fill_blk(nb, carry):
            pltpu.make_async_copy(zero_ref, block_of(nb), zsem).start()
            return carry
        lax.fori_loop(nused_ref[0], N_SLOTS_BLK, fill_blk, 0)

        def fill_blk_wait(nb, carry):
            pltpu.make_async_copy(zero_ref, block_of(0), zsem).wait()
            return carry
        lax.fori_loop(nused_ref[0], N_SLOTS_BLK, fill_blk_wait, 0)


def _dispatch(h2_tiles, dest, pad_lo, pad_hi, n_used):
    return pl.pallas_call(
        _dispatch_kernel,
        grid_spec=pltpu.PrefetchScalarGridSpec(
            num_scalar_prefetch=4, grid=(N_TILES,),
            in_specs=[pl.BlockSpec((TR * ROW_VREGS, LANES), lambda i, *_: (i, 0))],
            out_specs=pl.BlockSpec(memory_space=pl.ANY),
            scratch_shapes=[pltpu.VMEM((TM * ROW_VREGS, LANES), F32),
                            pltpu.SemaphoreType.DMA(()), pltpu.SemaphoreType.DMA(())]),
        out_shape=jax.ShapeDtypeStruct((N_SLOTS * ROW_VREGS, LANES), F32),
        compiler_params=_cparams(("arbitrary",)),
        name="dispatch",
    )(dest, pad_lo, pad_hi, n_used, h2_tiles)


class _ExpertRows:
    scratch = [pltpu.VMEM((2, 2 * TR * ROW_VREGS, LANES), F32), pltpu.SemaphoreType.DMA((2,))]

    def __init__(self, dest_ref, ys_ref, buf, sems):
        self.dest_ref, self.ys_ref, self.buf, self.sems = dest_ref, ys_ref, buf, sems

    def _copy(self, src_row, slot, dst_row):
        dst = self.buf.at[slot, pl.ds(pl.multiple_of(dst_row * ROW_VREGS, ROW_VREGS), ROW_VREGS), :]
        return pltpu.make_async_copy(_row(self.ys_ref, src_row), dst, self.sems.at[slot])

    def start(self, tile):
        slot = tile % 2

        def issue(t, carry):
            for kk in range(2):
                self._copy(self.dest_ref[2 * (tile * TR + t) + kk], slot, kk * TR + t).start()
            return carry
        lax.fori_loop(0, TR, issue, 0)

    def wait(self, tile):
        slot = tile % 2
        pltpu.make_async_copy(self.ys_ref.at[pl.ds(0, 2 * TR * ROW_VREGS), :], self.buf.at[slot],
                              self.sems.at[slot]).wait()

    def combine(self, tile, x, gate_ref):
        rows = self.buf.at[tile % 2]
        g = gate_ref[...]
        for kk in range(2):
            y = jnp.concatenate([rows[pl.ds(kk * TR * ROW_VREGS + j, TR, stride=ROW_VREGS), :]
                                 for j in range(ROW_VREGS)], -1)
            x = x + g[:, kk:kk + 1] * y
        return x

    def fetch_combine(self, x, gate_ref):
        i = pl.program_id(0)

        @pl.when(i == 0)
        def _():
            self.start(i)

        @pl.when(i + 1 < pl.num_programs(0))
        def _():
            self.start(i + 1)

        self.wait(i)
        return self.combine(i, x, gate_ref)


def _ffn_kernel(be_ref, nused_ref, xs_ref, w1_ref, w3_ref, w2_ref, y_ref, w1b, w3b, w2b):
    nb = pl.program_id(0)

    @pl.when(nb < nused_ref[0])
    def _():
        prev = be_ref[jnp.maximum(nb - 1, 0)]
        fresh = jnp.logical_or(nb == 0, be_ref[nb] != prev)

        @pl.when(fresh)
        def _():
            w1b[...] = w1_ref[...].astype(BF16)
            w3b[...] = w3_ref[...].astype(BF16)
            w2b[...] = w2_ref[...].astype(BF16)

        x = _load_rows_from_tiles(xs_ref, TM).astype(BF16)
        a = jnp.dot(x, w1b[...], preferred_element_type=F32)
        b = jnp.dot(x, w3b[...], preferred_element_type=F32)
        mid = (_silu(a) * b).astype(BF16)
        y = jnp.dot(mid, w2b[...], preferred_element_type=F32)
        _store_rows_as_tiles(y_ref, y)

    @pl.when(nb >= nused_ref[0])
    def _():
        y_ref[...] = jnp.zeros_like(y_ref)


def _experts(xs, blk_exp, n_used, layer, w1, w3, w2):
    rows = pl.BlockSpec((TM * ROW_VREGS, LANES), lambda nb, be, nu: (nb, 0))

    def wspec(a, b):
        return pl.BlockSpec((None, None, a, b), lambda nb, be, nu: (layer, be[nb], 0, 0))

    return pl.pallas_call(
        _ffn_kernel,
        grid_spec=pltpu.PrefetchScalarGridSpec(
            num_scalar_prefetch=2, grid=(N_SLOTS_BLK,),
            in_specs=[rows, wspec(D_MODEL, D_EXPERT), wspec(D_MODEL, D_EXPERT), wspec(D_EXPERT, D_MODEL)],
            out_specs=rows,
            scratch_shapes=[pltpu.VMEM((D_MODEL, D_EXPERT), BF16), pltpu.VMEM((D_MODEL, D_EXPERT), BF16),
                            pltpu.VMEM((D_EXPERT, D_MODEL), BF16)]),
        out_shape=jax.ShapeDtypeStruct((N_SLOTS * ROW_VREGS, LANES), F32),
        compiler_params=_cparams(("arbitrary",)),
        name="experts",
    )(blk_exp, n_used, xs, w1, w3, w2)


def _moe(h2_tiles, meta, counts, layer, w1, w3, w2):
    dest, blk_exp, n_used, pad_lo, pad_hi = _plan(meta, counts)
    xs = _dispatch(h2_tiles, dest, pad_lo, pad_hi, n_used)
    return _experts(xs, blk_exp, n_used, layer, w1, w3, w2), dest


def _pool_kernel(dest_ref, x1_ref, ys_ref, gate0_ref, gmix_ref, sp_ref, pw_ref, ps_ref, gffn_ref,
                 wr_ref, br_ref, tri_ref,
                 x3_ref, h2_ref, hkeep_ref, hs_ref, meta_ref, gate_ref, cnt_out_ref,
                 cnt_ref, ext_ref, e1_ref, e2_ref, e3_ref, e4_ref, mixed_ref, ybuf, ysems):
    i = pl.program_id(0)
    x2 = _ExpertRows(dest_ref, ys_ref, ybuf, ysems).fetch_combine(x1_ref[...], gate0_ref)
    h = _rms(x2, gmix_ref[...])
    H = POOL_HALO
    lvl_refs = (e1_ref, e2_ref, e3_ref, e4_ref)

    @pl.when(i < SAMPLE_TILE)
    def _():
        hkeep_ref[...] = h[TR - 16:TR]
        @pl.when(i % TILES_PER_SEQ == 0)
        def _():
            ext_ref[0:H, :] = jnp.zeros((H, D_MODEL), F32)
        ext_ref[H:H + TR, :] = h
        src = ext_ref
        for lv in range(4):
            sh = 1 << lv
            lo = 8 * (lv + 1)
            c0 = POOL_GROUP * lv
            dst = lvl_refs[lv]
            dst[lo:H + TR, c0:] = src[lo:H + TR, c0:] + src[lo - sh:H + TR - sh, c0:]
            src = dst
        pos = (i % TILES_PER_SEQ) * TR + lax.broadcasted_iota(I32, (TR, 1), 0)
        for gi, w in enumerate(POOL_WINDOWS):
            cs = slice(gi * POOL_GROUP, (gi + 1) * POOL_GROUP)
            inv = 1.0 / jnp.minimum(w, pos + 1).astype(F32)
            mixed_ref[:, cs] = lvl_refs[gi][H:H + TR, cs] * inv - h[:, cs]
        ext_ref[0:H, :] = h[TR - H:TR]

    @pl.when(i == SAMPLE_TILE)
    def _():
        hs = h[0:DEC_BATCH]
        hs_ref[...] = hs
        mixed_ref[...] = jnp.zeros_like(mixed_ref)
        for gi, w in enumerate(POOL_WINDOWS):
            cs = slice(gi * POOL_GROUP, (gi + 1) * POOL_GROUP)
            win = hs[:, cs]
            for dlt in range(1, w):
                win = win + sp_ref[POOL_KEEP - dlt][:, cs]
            mixed_ref[0:DEC_BATCH, cs] = win * (1.0 / w) - hs[:, cs]

    mixed = mixed_ref[...].astype(BF16)
    outs = [jnp.dot(mixed[:, gi * POOL_GROUP:(gi + 1) * POOL_GROUP], pw_ref[gi], preferred_element_type=F32)
            for gi in range(len(POOL_WINDOWS))]
    x3 = x2 + jnp.concatenate(outs, -1) * ps_ref[...]
    x3_ref[...] = x3
    h2 = _rms(x3, gffn_ref[...])
    _store_rows_as_tiles(h2_ref, h2)
    _route(h2, wr_ref, br_ref, tri_ref, cnt_ref, meta_ref, gate_ref, cnt_out_ref)


def _const_spec_p(a):
    return pl.BlockSpec(a.shape, lambda i, *_, n=a.ndim: (0,) * n)


def _pool_route(dest, x1, ys, gates0, g_mix, sp_t, pw_bf16, p_scale, g_ffn, wr, br, tri):
    def rows(shape):
        return pl.BlockSpec(shape, lambda i, *_: (i, 0))

    ext = pltpu.VMEM((POOL_HALO + TR, D_MODEL), F32)
    consts = [g_mix, sp_t, pw_bf16, p_scale, g_ffn, wr, br, tri]
    return pl.pallas_call(
        _pool_kernel,
        grid_spec=pltpu.PrefetchScalarGridSpec(
            num_scalar_prefetch=1, grid=(N_TILES,),
            in_specs=[rows((TR, D_MODEL)), pl.BlockSpec(memory_space=pl.ANY), rows((TR, 8))]
                     + [_const_spec_p(a) for a in consts],
            out_specs=[rows((TR, D_MODEL)), rows((TR * ROW_VREGS, LANES)),
                       pl.BlockSpec((None, 16, D_MODEL),
                                    lambda i, *_: (jnp.minimum(i // TILES_PER_SEQ, BATCH - 1), 0, 0)),
                       pl.BlockSpec((DEC_BATCH, D_MODEL), lambda i, *_: (0, 0)),
                       rows((TR, 8)), rows((TR, 8)), pl.BlockSpec((1, LANES), lambda i, *_: (0, 0))],
            scratch_shapes=[pltpu.VMEM((1, LANES), F32), ext, ext, ext, ext, ext,
                            pltpu.VMEM((TR, D_MODEL), F32)] + _ExpertRows.scratch),
        out_shape=[jax.ShapeDtypeStruct((NT, D_MODEL), F32),
                   jax.ShapeDtypeStruct((NT * ROW_VREGS, LANES), F32),
                   jax.ShapeDtypeStruct((BATCH, 16, D_MODEL), F32),
                   jax.ShapeDtypeStruct((DEC_BATCH, D_MODEL), F32)] + _ROUTE_OUT_SHAPES,
        compiler_params=_cparams(("arbitrary",)),
        name="pool_route",
    )(dest, x1, ys, gates0, *consts)


def _final_kernel(dest_ref, x_ref, ys_ref, gate_ref, g_ref, yp_ref, ysm_ref, ybuf, ysems):
    i = pl.program_id(0)
    y = _rms(_ExpertRows(dest_ref, ys_ref, ybuf, ysems).fetch_combine(x_ref[...], gate_ref), g_ref[...])

    @pl.when(i < SAMPLE_TILE)
    def _():
        yp_ref[...] = y

    @pl.when(i == SAMPLE_TILE)
    def _():
        ysm_ref[...] = y[0:DEC_BATCH]


def _final(dest, x3, ys, gates, g_final):
    return pl.pallas_call(
        _final_kernel,
        grid_spec=pltpu.PrefetchScalarGridSpec(
            num_scalar_prefetch=1, grid=(N_TILES,),
            in_specs=[pl.BlockSpec((TR, D_MODEL), lambda i, *_: (i, 0)), pl.BlockSpec(memory_space=pl.ANY),
                      pl.BlockSpec((TR, 8), lambda i, *_: (i, 0)), _const_spec_p(g_final)],
            out_specs=[pl.BlockSpec((TR, D_MODEL), lambda i, *_: (jnp.minimum(i, SAMPLE_TILE - 1), 0)),
                       pl.BlockSpec((DEC_BATCH, D_MODEL), lambda i, *_: (0, 0))],
            scratch_shapes=_ExpertRows.scratch),
        out_shape=[jax.ShapeDtypeStruct((NP, D_MODEL), F32), jax.ShapeDtypeStruct((DEC_BATCH, D_MODEL), F32)],
        compiler_params=_cparams(("arbitrary",)),
        name="final_norm",
    )(dest, x3, ys, gates, g_final)


def kernel(x_prompt, x_sample, cache_win_k, cache_win_v, state_hgrn, state_pool, rel_bias, norm_mix, norm_ffn,
           norm_final, w_in, w_out, hgrn_lb, hgrn_gnorm, pool_w, pool_scale, moe_wg_group, moe_bg_group,
           moe_wg_exp, moe_bg_exp, moe_w1, moe_w3, moe_w2):
    xp = x_prompt.reshape(NP, D_MODEL)
    xs_pad = jnp.pad(x_sample.reshape(DEC_BATCH, D_MODEL), ((0, TR - DEC_BATCH), (0, 0)))
    lb = jnp.cumsum(jax.nn.softmax(hgrn_lb.astype(F32), axis=0), axis=0)[0:1]
    gnorm = hgrn_gnorm[0:1].astype(F32)

    q, k, v, hq, hf, hi, hg, k_win, v_win = _inproj(xp, xs_pad, norm_mix[0:1], w_in[0].astype(BF16))
    oa = _attention_prompt(q, k, v, _attn_bias_tables(rel_bias))
    ob, s_prompt = _hgrn_prompt(hq, hf, hi, hg, lb, gnorm)
    oa_s, ob_s, s_sample = _sample_mixers(
        q, k, v, hq, hf, hi, hg,
        cache_win_k[0].reshape(DEC_BATCH, W_MAX * N_HEADS, D_HEAD),
        cache_win_v[0].reshape(DEC_BATCH, W_MAX * N_HEADS, D_HEAD),
        state_hgrn[0], _sample_bias_tables(rel_bias), lb, gnorm)
    wr, br, tri = _router_operands(moe_wg_group[0], moe_bg_group[0], moe_wg_exp[0], moe_bg_exp[0])
    x1, h2, meta, gates0, counts = _outproj_route(xp, xs_pad, oa, oa_s, ob, ob_s, w_out[0].astype(BF16),
                                                  norm_ffn[0:1], wr, br, tri)
    ys, dest = _moe(h2, meta, counts, 0, moe_w1, moe_w3, moe_w2)

    wr, br, tri = _router_operands(moe_wg_group[1], moe_bg_group[1], moe_wg_exp[1], moe_bg_exp[1])
    sp_t = jnp.transpose(state_pool[0], (1, 0, 2))
    x3, h2, hkeep, hsample, meta, gates1, counts = _pool_route(
        dest, x1, ys, gates0, norm_mix[1:2], sp_t, pool_w[0].astype(BF16), pool_scale[0:1], norm_ffn[1:2],
        wr, br, tri)
    ys, dest = _moe(h2, meta, counts, 1, moe_w1, moe_w3, moe_w2)
    y_prompt, y_sample = _final(dest, x3, ys, gates1, norm_final.reshape(1, D_MODEL))

    k_s = k[NP:NV].reshape(1, DEC_BATCH, 1, N_HEADS, D_HEAD)
    v_s = v[NP:NV].reshape(1, DEC_BATCH, 1, N_HEADS, D_HEAD)
    return (y_prompt.reshape(BATCH, SEQ, D_MODEL), y_sample.reshape(DEC_BATCH, 1, D_MODEL),
            k_win.reshape(1, BATCH, W_MAX, N_HEADS, D_HEAD), v_win.reshape(1, BATCH, W_MAX, N_HEADS, D_HEAD),
            s_prompt[None], hkeep[:, 16 - POOL_KEEP:][None],
            k_s, v_s, s_sample[None], hsample.reshape(1, DEC_BATCH, 1, D_MODEL))
```

```python
import functools

import numpy as np
import jax
import jax.numpy as jnp
from jax import lax
from jax.experimental import pallas as pl
from jax.experimental.pallas import tpu as pltpu

F32 = jnp.float32
BF16 = jnp.bfloat16
I32 = jnp.int32

D_MODEL = 1024
BATCH = 4
SEQ = 4096
DEC_BATCH = 32
PAST_LEN = 8192
W_MAX = 2048
N_HEADS = 4
D_HEAD = 128
A_WIDTH = N_HEADS * D_HEAD
N_PROJ = 7 * A_WIDTH
DILATED = ((128, 1), (512, 4), (2048, 16))
NUM_BUCKETS = 32
MAX_DISTANCE = 2048
POOL_WINDOWS = (2, 4, 8, 16)
POOL_GROUP = 256
POOL_KEEP = 15
N_GROUPS = 4
EXPERTS_PER_GROUP = 8
N_EXPERTS = 32
D_EXPERT = 512
EPS = 1e-6
NEG = -1e30

LANES = 128
SUBLANES = 8
ROW_VREGS = D_MODEL // LANES
TR = 256
NP = BATCH * SEQ
NV = NP + DEC_BATCH
NT = NP + TR
N_TILES = NT // TR
SAMPLE_TILE = NP // TR
TILES_PER_SEQ = SEQ // TR
TM = 256
N_SLOTS_BLK = (2 * NV + N_EXPERTS * (TM - 1) + TM - 1) // TM
N_SLOTS = N_SLOTS_BLK * TM
QB = 128
ATTN_UNROLL = 8
CH = 128
N_LEVELS = 7
POOL_HALO = 32
VMEM_LIMIT = 56 * 1024 * 1024


def _cparams(sem=None, vmem=VMEM_LIMIT):
    kw = dict(vmem_limit_bytes=vmem)
    if sem is not None:
        kw["dimension_semantics"] = sem
    return pltpu.CompilerParams(**kw)


def _rms(x, g):
    return x * lax.rsqrt(jnp.mean(x * x, -1, keepdims=True) + EPS) * g


def _sigmoid(x):
    return 1.0 / (1.0 + jnp.exp(-x))


def _silu(x):
    return x * _sigmoid(x)


def _token_tile(i, xp_ref, xs_ref):
    return jnp.where(i == SAMPLE_TILE, xs_ref[...], xp_ref[...])


_PROMPT_ROWS = pl.BlockSpec((TR, D_MODEL), lambda i: (jnp.minimum(i, SAMPLE_TILE - 1), 0))
_SAMPLE_ROWS = pl.BlockSpec((TR, D_MODEL), lambda i: (0, 0))
WIN_TILES = W_MAX // TR


def _window_block(i):
    seq = jnp.minimum(i // TILES_PER_SEQ, BATCH - 1)
    j = jnp.clip(i % TILES_PER_SEQ - (TILES_PER_SEQ - WIN_TILES), 0, WIN_TILES - 1)
    return jnp.where(i >= SAMPLE_TILE, BATCH * WIN_TILES - 1, seq * WIN_TILES + j)


def _inproj_kernel(xp_ref, xs_ref, g_ref, w_ref, *out_refs):
    i = pl.program_id(0)
    h = _rms(_token_tile(i, xp_ref, xs_ref), g_ref[...])
    p = jnp.dot(h.astype(BF16), w_ref[...], preferred_element_type=F32)
    for n, o_ref in enumerate(out_refs[:7]):
        o_ref[...] = p[:, n * A_WIDTH:(n + 1) * A_WIDTH]

    @pl.when(jnp.logical_and(i < SAMPLE_TILE, i % TILES_PER_SEQ >= TILES_PER_SEQ - WIN_TILES))
    def _():
        for n, o_ref in ((1, out_refs[7]), (2, out_refs[8])):
            for h_i in range(N_HEADS):
                c0 = n * A_WIDTH + h_i * D_HEAD
                o_ref[pl.ds(h_i, TR, stride=N_HEADS), :] = p[:, c0:c0 + D_HEAD]


def _inproj(xp, xs_pad, g, w_bf16):
    out_sds = ([jax.ShapeDtypeStruct((NT, A_WIDTH), F32)] * 7
               + [jax.ShapeDtypeStruct((BATCH * W_MAX * N_HEADS, D_HEAD), F32)] * 2)
    win = pl.BlockSpec((TR * N_HEADS, D_HEAD), lambda i: (_window_block(i), 0))
    return pl.pallas_call(
        _inproj_kernel,
        grid=(N_TILES,),
        in_specs=[_PROMPT_ROWS, _SAMPLE_ROWS,
                  pl.BlockSpec((1, D_MODEL), lambda i: (0, 0)),
                  pl.BlockSpec((D_MODEL, N_PROJ), lambda i: (0, 0))],
        out_specs=[pl.BlockSpec((TR, A_WIDTH), lambda i: (i, 0))] * 7 + [win, win],
        out_shape=out_sds,
        compiler_params=_cparams(("arbitrary",)),
        name="inproj",
    )(xp, xs_pad, g, w_bf16)


def _t5_bucket(dist):
    max_exact = NUM_BUCKETS // 2
    d = np.asarray(dist)
    large = max_exact + np.floor(np.log(np.maximum(d, 1) / max_exact)
                                 / np.log(MAX_DISTANCE / max_exact) * (NUM_BUCKETS - max_exact)).astype(np.int32)
    large = np.minimum(large, NUM_BUCKETS - 1)
    return np.where(d < max_exact, d, large).astype(np.int32)


def _attn_bias_tables(rel_bias):
    period = 3 * QB
    m = np.arange(period)
    u = np.where(m < 2 * QB, m, m - period)
    pick = np.zeros((len(DILATED), 2, period, NUM_BUCKETS), np.float32)
    mask = np.zeros((len(DILATED), 2, period, 1), np.float32)
    for bi, (w, d) in enumerate(DILATED):
        nk = w // d
        for vi, off in enumerate((0, QB)):
            j = off - u
            ok = (j >= 0) & (j <= nk)
            pick[bi, vi, m[ok], _t5_bucket(d * j[ok])] = 1.0
            mask[bi, vi, ~ok, 0] = NEG
    vec = jnp.einsum("bvmk,kh->hbvm", pick, rel_bias.astype(F32), precision=lax.Precision.HIGHEST)
    vec = vec + jnp.transpose(mask, (3, 0, 1, 2))
    tiled = jnp.tile(vec, (1, 1, 1, QB))[..., :QB * (period - 1)]
    return tiled.reshape(N_HEADS, len(DILATED), 2, QB, period - 1)[..., :2 * QB]


def _attn_kernel(q_ref, k_ref, v_ref, bias_ref, o_ref,
                 qd, kd, vd, ud, md, sd, u_acc, m_acc, s_acc):
    scale = D_HEAD ** -0.5

    def block_stats(bi, t, nb):
        has_prev = jnp.minimum(t % nb, 1)
        q0 = pl.multiple_of(t * QB, QB)
        k0 = pl.multiple_of((t - has_prev) * QB, QB)
        qb = qd[pl.ds(q0, QB), :]
        kb = kd[pl.ds(k0, 2 * QB), :]
        vb = vd[pl.ds(k0, 2 * QB), :]
        s = lax.dot_general(qb, kb, (((1,), (1,)), ((), ())), preferred_element_type=F32)
        s = s + bias_ref[bi, has_prev]
        mb = jnp.max(s, -1, keepdims=True)
        p = jnp.exp(s - mb)
        sb = jnp.sum(p, -1, keepdims=True)
        u = jnp.dot(p.astype(BF16), vb, preferred_element_type=F32)
        return q0, mb, sb, u

    for bi in (2, 1, 0):
        d = DILATED[bi][1]
        cl = SEQ // d
        nb = cl // QB
        for r in range(d):
            src = pl.ds(r, cl, stride=d) if d > 1 else pl.ds(0, cl)
            dst = pl.ds(r * cl, cl)
            qd[dst, :] = (q_ref[src, :] * scale).astype(BF16)
            kd[dst, :] = k_ref[src, :].astype(BF16)
            vd[dst, :] = v_ref[src, :].astype(BF16)

        if d > 1:
            def body(g, carry, bi=bi, nb=nb):
                for un in range(ATTN_UNROLL):
                    q0, mb, sb, u = block_stats(bi, g * ATTN_UNROLL + un, nb)
                    ud[pl.ds(q0, QB), :] = u
                    md[pl.ds(q0, QB), :] = jnp.broadcast_to(mb, (QB, D_HEAD))
                    sd[pl.ds(q0, QB), :] = jnp.broadcast_to(sb, (QB, D_HEAD))
                return carry
            lax.fori_loop(0, SEQ // QB // ATTN_UNROLL, body, 0)
            for r in range(d):
                pos = pl.ds(r, cl, stride=d)
                cls = pl.ds(r * cl, cl)
                if bi == 2:
                    u_acc[pos, :] = ud[cls, :]
                    m_acc[pos, :] = md[cls, :]
                    s_acc[pos, :] = sd[cls, :]
                else:
                    m_old = m_acc[pos, :]
                    m_blk = md[cls, :]
                    m_new = jnp.maximum(m_old, m_blk)
                    a = jnp.exp(m_old - m_new)
                    b = jnp.exp(m_blk - m_new)
                    u_acc[pos, :] = a * u_acc[pos, :] + b * ud[cls, :]
                    s_acc[pos, :] = a * s_acc[pos, :] + b * sd[cls, :]
                    m_acc[pos, :] = m_new
        else:
            def body(g, carry, bi=bi, nb=nb):
                for un in range(ATTN_UNROLL):
                    q0, mb, sb, u = block_stats(bi, g * ATTN_UNROLL + un, nb)
                    rows = pl.ds(q0, QB)
                    m_old = m_acc[rows, :]
                    m_new = jnp.maximum(m_old, mb)
                    a = jnp.exp(m_old - m_new)
                    b = jnp.exp(mb - m_new)
                    num = a * u_acc[rows, :] + b * u
                    den = a * s_acc[rows, :] + b * sb
                    o_ref[rows, :] = (num / den).astype(o_ref.dtype)
                return carry
            lax.fori_loop(0, SEQ // QB // ATTN_UNROLL, body, 0)


def _attention_prompt(q, k, v, bias_tabs):
    blk = pl.BlockSpec((SEQ, D_HEAD), lambda b, h: (b, h))
    return pl.pallas_call(
        _attn_kernel,
        grid=(BATCH, N_HEADS),
        in_specs=[blk, blk, blk,
                  pl.BlockSpec((None, 3, 2, QB, 2 * QB), lambda b, h: (h, 0, 0, 0, 0))],
        out_specs=blk,
        out_shape=jax.ShapeDtypeStruct((NP, A_WIDTH), BF16),
        scratch_shapes=[pltpu.VMEM((SEQ, D_HEAD), BF16)] * 3
                       + [pltpu.VMEM((SEQ, D_HEAD), F32)] * 6,
        compiler_params=_cparams(("arbitrary", "arbitrary")),
        name="attn_prompt",
    )(q, k, v, bias_tabs)


def _hgrn_tables():
    t = np.arange(CH)
    u = np.arange(CH)
    sums_q = np.zeros((1 + N_LEVELS, CH, CH), np.float32)
    sums_k = np.zeros((2 + N_LEVELS, CH, CH), np.float32)
    sums_q[0] = (u[None, :] <= t[:, None])
    sums_k[0] = (u[None, :] > t[:, None])
    sums_k[1 + N_LEVELS] = 1.0
    pair = np.zeros((N_LEVELS, CH, CH), np.float32)
    for l in range(N_LEVELS):
        h = CH >> (l + 1)
        is_q = (t // h) % 2 == 1
        half_start = (t // h) * h
        half_end = half_start + h
        sel_q = (u[None, :] >= half_start[:, None]) & (u[None, :] <= t[:, None])
        sel_k = (u[None, :] > t[:, None]) & (u[None, :] < half_end[:, None])
        sums_q[1 + l] = sel_q & is_q[:, None]
        sums_k[1 + l] = sel_k & ~is_q[:, None]
        same = (t[:, None] // (2 * h)) == (t[None, :] // (2 * h))
        pair[l] = same & is_q[:, None] & (~is_q)[None, :]
    sums_kt = np.transpose(sums_k, (2, 0, 1)).reshape(CH, (2 + N_LEVELS) * CH)
    return (jnp.asarray(sums_q.reshape((1 + N_LEVELS) * CH, CH), BF16), jnp.asarray(sums_kt, BF16),
            jnp.asarray(pair), jnp.asarray(np.eye(CH, dtype=np.float32)))


def _split_bf16(x):
    hi = x.astype(BF16)
    return hi, (x - hi.astype(F32)).astype(BF16)


def _hgrn_kernel(q_ref, f_ref, i_ref, g_ref, lb_ref, gn_ref, sq_ref, skt_ref, pair_ref, eye_ref, o_ref, s_ref):
    lb = lb_ref[...]
    gn = gn_ref[...]

    def chunk(c, st):
        rows = pl.ds(pl.multiple_of(c * CH, CH), CH)
        q = _silu(q_ref[rows, :])
        f = lb + (1.0 - lb) * _sigmoid(f_ref[rows, :])
        lf = jnp.log(f)
        k = 1.0 - f
        v16 = i_ref[rows, :].astype(BF16)
        kt = k.T
        lf_hi, lf_lo = _split_bf16(lf)
        lft_hi, lft_lo = _split_bf16(lf.T)
        sq = sq_ref[...]
        skt = skt_ref[...]
        exq = jnp.exp(jnp.dot(sq, lf_hi, preferred_element_type=F32)
                      + jnp.dot(sq, lf_lo, preferred_element_type=F32))
        exk = jnp.exp(jnp.dot(lft_hi, skt, preferred_element_type=F32)
                      + jnp.dot(lft_lo, skt, preferred_element_type=F32))
        inter = jnp.dot((q * exq[0:CH]).astype(BF16), st.astype(BF16), preferred_element_type=F32)
        sc = eye_ref[...] * jnp.sum(q * k, -1, keepdims=True)
        for l in range(N_LEVELS):
            ql = (q * exq[(1 + l) * CH:(2 + l) * CH]).astype(BF16)
            klt = (kt * exk[:, (1 + l) * CH:(2 + l) * CH]).astype(BF16)
            sc = sc + pair_ref[l] * jnp.dot(ql, klt, preferred_element_type=F32)
        o = inter + jnp.dot(sc.astype(BF16), v16, preferred_element_type=F32)
        st_new = (st * exk[:, (1 + N_LEVELS) * CH:]
                  + jnp.dot((kt * exk[:, 0:CH]).astype(BF16), v16, preferred_element_type=F32))
        o = _rms(o, gn) * _silu(g_ref[rows, :])
        o_ref[rows, :] = o.astype(o_ref.dtype)
        return st_new

    def two_chunks(c2, st):
        return chunk(2 * c2 + 1, chunk(2 * c2, st))

    s_ref[...] = lax.fori_loop(0, SEQ // CH // 2, two_chunks, jnp.zeros((D_HEAD, D_HEAD), F32))


def _hgrn_prompt(hq, hf, hi, hg, lb, gnorm):
    tables = _hgrn_tables()
    blk = pl.BlockSpec((SEQ, D_HEAD), lambda b, h: (b, h))

    def full(a):
        return pl.BlockSpec(a.shape, lambda b, h, n=a.ndim: (0,) * n)

    return pl.pallas_call(
        _hgrn_kernel,
        grid=(BATCH, N_HEADS),
        in_specs=[blk, blk, blk, blk,
                  pl.BlockSpec((1, D_HEAD), lambda b, h: (0, h)),
                  pl.BlockSpec((1, D_HEAD), lambda b, h: (0, 0))] + [full(a) for a in tables],
        out_specs=[blk, pl.BlockSpec((None, None, D_HEAD, D_HEAD), lambda b, h: (b, h, 0, 0))],
        out_shape=[jax.ShapeDtypeStruct((NP, A_WIDTH), BF16),
                   jax.ShapeDtypeStruct((BATCH, N_HEADS, D_HEAD, D_HEAD), F32)],
        compiler_params=_cparams(("arbitrary", "arbitrary")),
        name="hgrn_prompt",
    )(hq, hf, hi, hg, lb, gnorm, *tables)


def _sample_bias_tables(rel_bias):
    j = np.concatenate([QB - np.arange(QB), [0]])
    pick = np.zeros((len(DILATED), QB + 1, NUM_BUCKETS), np.float32)
    for bi, (w, d) in enumerate(DILATED):
        pick[bi, np.arange(QB + 1), _t5_bucket(d * j)] = 1.0
    return jnp.einsum("bjk,kh->bhj", pick, rel_bias.astype(F32), precision=lax.Precision.HIGHEST)[..., None]


def _bf16_round(x):
    return x.astype(BF16).astype(F32)


def _col(row, eye):
    return jnp.sum(eye * row, -1, keepdims=True)


def _sample_kernel(q_ref, k_ref, v_ref, hq_ref, hf_ref, hi_ref, hg_ref, kc_ref, vc_ref, s0_ref,
                   bias_ref, lb_ref, gn_ref, eye_ref, oa_ref, ob_ref, s_ref, oa_acc, ob_acc):
    b = pl.program_id(0)
    row = pl.ds(b, 1)
    scale = D_HEAD ** -0.5
    q = _bf16_round(q_ref[row, :] * scale)
    kn = _bf16_round(k_ref[row, :])
    vn = _bf16_round(v_ref[row, :])
    eye = eye_ref[...]

    stats = []
    for bi, (w, d) in enumerate(DILATED):
        per_head = []
        for h in range(N_HEADS):
            hs = slice(h * D_HEAD, (h + 1) * D_HEAD)
            rows = pl.ds((W_MAX - QB * d) * N_HEADS + h, QB, stride=d * N_HEADS)
            kb = _bf16_round(kc_ref[rows, :])
            vb = _bf16_round(vc_ref[rows, :])
            s = jnp.sum(kb * q[:, hs], -1, keepdims=True) + bias_ref[bi, h, 0:QB]
            s0 = jnp.sum(q[:, hs] * kn[:, hs], -1, keepdims=True) + bias_ref[bi, h, QB:QB + 1]
            m = jnp.maximum(jnp.max(s, 0, keepdims=True), s0)
            p = jnp.exp(s - m)
            p0 = jnp.exp(s0 - m)
            ssum = jnp.sum(p, 0, keepdims=True) + p0
            u = jnp.sum(_bf16_round(p) * vb, 0, keepdims=True) + _bf16_round(p0) * vn[:, hs]
            per_head.append((m, ssum, u))
        stats.append(per_head)
    outs = []
    for h in range(N_HEADS):
        m_all = functools.reduce(jnp.maximum, [stats[bi][h][0] for bi in range(3)])
        num = 0.0
        den = 0.0
        for bi in range(3):
            m, ssum, u = stats[bi][h]
            c = jnp.exp(m - m_all)
            num = num + c * u
            den = den + c * ssum
        outs.append(num / den)
    oa_acc[row, :] = jnp.concatenate(outs, -1)

    qh = _silu(hq_ref[row, :])
    lb = lb_ref[...]
    f = lb + (1.0 - lb) * _sigmoid(hf_ref[row, :])
    vi = hi_ref[row, :]
    gate = _silu(hg_ref[row, :])
    gn = gn_ref[...]
    obs = []
    for h in range(N_HEADS):
        hs = slice(h * D_HEAD, (h + 1) * D_HEAD)
        f_col = _col(f[:, hs], eye)
        q_col = _col(qh[:, hs], eye)
        s_old = s0_ref[h]
        s_ref[h] = f_col * s_old + (1.0 - f_col) * vi[:, hs]
        inter = jnp.sum(_bf16_round(q_col * f_col) * _bf16_round(s_old), 0, keepdims=True)
        qk = jnp.sum(qh[:, hs] * (1.0 - f[:, hs]), -1, keepdims=True)
        o = inter + qk * vi[:, hs]
        obs.append(_rms(o, gn) * gate[:, hs])
    ob_acc[row, :] = jnp.concatenate(obs, -1)

    @pl.when(b == DEC_BATCH - 1)
    def _():
        pad = jnp.zeros((TR - DEC_BATCH, A_WIDTH), F32)
        oa_ref[...] = jnp.concatenate([oa_acc[...], pad], 0).astype(oa_ref.dtype)
        ob_ref[...] = jnp.concatenate([ob_acc[...], pad], 0).astype(ob_ref.dtype)


def _sample_mixers(q, k, v, hq, hf, hi, hg, cache_k, cache_v, state, bias_s, lb, gnorm):
    tile = pl.BlockSpec((TR, A_WIDTH), lambda b: (SAMPLE_TILE, 0))
    out_tile = pl.BlockSpec((TR, A_WIDTH), lambda b: (0, 0))
    cache = pl.BlockSpec((None, W_MAX * N_HEADS, D_HEAD), lambda b: (b, 0, 0))
    eye = jnp.eye(D_HEAD, dtype=F32)

    return pl.pallas_call(
        _sample_kernel,
        grid=(DEC_BATCH,),
        in_specs=[tile, tile, tile, tile, tile, tile, tile, cache, cache,
                  pl.BlockSpec((None, N_HEADS, D_HEAD, D_HEAD), lambda b: (b, 0, 0, 0)),
                  pl.BlockSpec(bias_s.shape, lambda b: (0, 0, 0, 0)),
                  pl.BlockSpec((1, A_WIDTH), lambda b: (0, 0)),
                  pl.BlockSpec((1, D_HEAD), lambda b: (0, 0)),
                  pl.BlockSpec((D_HEAD, D_HEAD), lambda b: (0, 0))],
        out_specs=[out_tile, out_tile,
                   pl.BlockSpec((None, N_HEADS, D_HEAD, D_HEAD), lambda b: (b, 0, 0, 0))],
        out_shape=[jax.ShapeDtypeStruct((TR, A_WIDTH), BF16), jax.ShapeDtypeStruct((TR, A_WIDTH), BF16),
                   jax.ShapeDtypeStruct((DEC_BATCH, N_HEADS, D_HEAD, D_HEAD), F32)],
        scratch_shapes=[pltpu.VMEM((DEC_BATCH, A_WIDTH), F32)] * 2,
        compiler_params=_cparams(("arbitrary",)),
        name="sample_mixers",
    )(q, k, v, hq, hf, hi, hg, cache_k, cache_v, state, bias_s, lb, gnorm, eye)


def _store_rows_as_tiles(ref, val):
    n = val.shape[0]
    for j in range(ROW_VREGS):
        ref[pl.ds(j, n, stride=ROW_VREGS), :] = val[:, j * LANES:(j + 1) * LANES]


def _load_rows_from_tiles(ref, n):
    return jnp.concatenate([ref[pl.ds(j, n, stride=ROW_VREGS), :] for j in range(ROW_VREGS)], -1)


def _route(h2, wr_ref, br_ref, tri_ref, cnt_ref, meta_ref, gate_ref, cnt_out_ref):
    i = pl.program_id(0)

    @pl.when(i == 0)
    def _():
        cnt_ref[...] = jnp.zeros_like(cnt_ref)

    logits = jnp.dot(h2.astype(BF16), wr_ref[...], preferred_element_type=F32) + br_ref[...]
    lane = lax.broadcasted_iota(I32, (TR, LANES), 1).astype(F32)
    big = float(1 << 20)
    is_g = lane < N_GROUPS
    gl = jnp.where(is_g, logits, NEG)
    gmax = jnp.max(gl, -1, keepdims=True)
    gsel = jnp.min(jnp.where(gl == gmax, lane, big), -1, keepdims=True)
    pg = 1.0 / jnp.sum(jnp.where(is_g, jnp.exp(gl - gmax), 0.0), -1, keepdims=True)
    lo = N_GROUPS + EXPERTS_PER_GROUP * gsel
    in_grp = jnp.logical_and(lane >= lo, lane < lo + EXPERTS_PER_GROUP)
    el = jnp.where(in_grp, logits, NEG)
    m1 = jnp.max(el, -1, keepdims=True)
    i1 = jnp.min(jnp.where(el == m1, lane, big), -1, keepdims=True)
    el2 = jnp.where(lane == i1, NEG, el)
    m2 = jnp.max(el2, -1, keepdims=True)
    i2 = jnp.min(jnp.where(el2 == m2, lane, big), -1, keepdims=True)
    r = jnp.exp(m2 - m1)
    g1 = pg / (1.0 + r)
    g2 = pg * r / (1.0 + r)
    e1 = i1 - N_GROUPS
    e2 = i2 - N_GROUPS

    tok = i * TR + lax.broadcasted_iota(I32, (TR, 1), 0)
    valid = tok < NV
    oh1 = jnp.logical_and(lane == e1, valid)
    oh2 = jnp.logical_and(lane == e2, valid)
    oh = jnp.where(jnp.logical_or(oh1, oh2), 1.0, 0.0)
    before = jnp.dot(tri_ref[...], oh.astype(BF16), preferred_element_type=F32) + cnt_ref[...]
    rank1 = jnp.sum(jnp.where(oh1, before, 0.0), -1, keepdims=True)
    rank2 = jnp.sum(jnp.where(oh2, before, 0.0), -1, keepdims=True)
    cnt_ref[...] = cnt_ref[...] + jnp.sum(oh, 0, keepdims=True)

    meta = jnp.where(lane == 0, e1, jnp.where(lane == 1, e2, jnp.where(lane == 2, rank1,
                     jnp.where(lane == 3, rank2, 0.0))))
    gates = jnp.where(lane == 0, g1, jnp.where(lane == 1, g2, 0.0))
    gates = jnp.where(valid, gates, 0.0)
    meta_ref[...] = meta[:, 0:8].astype(I32)
    gate_ref[...] = gates[:, 0:8]
    cnt_out_ref[...] = cnt_ref[...]


def _router_operands(wg_group, bg_group, wg_exp, bg_exp):
    wr = jnp.zeros((D_MODEL, LANES), F32)
    wr = wr.at[:, 0:N_GROUPS].set(wg_group.astype(F32)).at[:, N_GROUPS:N_GROUPS + N_EXPERTS].set(wg_exp.astype(F32))
    br = jnp.zeros((1, LANES), F32)
    br = br.at[0, 0:N_GROUPS].set(bg_group.astype(F32)).at[0, N_GROUPS:N_GROUPS + N_EXPERTS].set(bg_exp.astype(F32))
    tri = jnp.asarray(np.tril(np.ones((TR, TR), np.float32), -1), BF16)
    return wr.astype(BF16), br, tri


_ROUTE_OUT_SPECS = [pl.BlockSpec((TR, 8), lambda i: (i, 0)),
                    pl.BlockSpec((TR, 8), lambda i: (i, 0)),
                    pl.BlockSpec((1, LANES), lambda i: (0, 0))]
_ROUTE_OUT_SHAPES = [jax.ShapeDtypeStruct((NT, 8), I32), jax.ShapeDtypeStruct((NT, 8), F32),
                     jax.ShapeDtypeStruct((1, LANES), F32)]


def _const_spec(a):
    return pl.BlockSpec(a.shape, lambda i, n=a.ndim: (0,) * n)


def _outproj_kernel(xp_ref, xs_ref, oa_ref, oas_ref, ob_ref, obs_ref, w_ref, g_ref, wr_ref, br_ref, tri_ref,
                    x1_ref, h2_ref, meta_ref, gate_ref, cnt_out_ref, cnt_ref):
    i = pl.program_id(0)
    x = _token_tile(i, xp_ref, xs_ref)
    mixed = jnp.concatenate([_token_tile(i, oa_ref, oas_ref), _token_tile(i, ob_ref, obs_ref)], -1)
    x1 = x + jnp.dot(mixed, w_ref[...], preferred_element_type=F32)
    x1_ref[...] = x1
    h2 = _rms(x1, g_ref[...])
    _store_rows_as_tiles(h2_ref, h2)
    _route(h2, wr_ref, br_ref, tri_ref, cnt_ref, meta_ref, gate_ref, cnt_out_ref)


def _outproj_route(xp, xs_pad, oa, oa_s, ob, ob_s, w_out_bf16, g_ffn, wr, br, tri):
    row = pl.BlockSpec((TR, D_MODEL), lambda i: (i, 0))
    half = pl.BlockSpec((TR, A_WIDTH), lambda i: (jnp.minimum(i, SAMPLE_TILE - 1), 0))
    half_s = pl.BlockSpec((TR, A_WIDTH), lambda i: (0, 0))
    return pl.pallas_call(
        _outproj_kernel,
        grid=(N_TILES,),
        in_specs=[_PROMPT_ROWS, _SAMPLE_ROWS, half, half_s, half, half_s,
                  _const_spec(w_out_bf16), _const_spec(g_ffn),
                  _const_spec(wr), _const_spec(br), _const_spec(tri)],
        out_specs=[row, pl.BlockSpec((TR * ROW_VREGS, LANES), lambda i: (i, 0))] + _ROUTE_OUT_SPECS,
        out_shape=[jax.ShapeDtypeStruct((NT, D_MODEL), F32),
                   jax.ShapeDtypeStruct((NT * ROW_VREGS, LANES), F32)] + _ROUTE_OUT_SHAPES,
        scratch_shapes=[pltpu.VMEM((1, LANES), F32)],
        compiler_params=_cparams(("arbitrary",)),
        name="outproj_route",
    )(xp, xs_pad, oa, oa_s, ob, ob_s, w_out_bf16, g_ffn, wr, br, tri)


def _plan(meta, counts):
    cnt = counts[0, :N_EXPERTS].astype(I32)
    nblk = (cnt + TM - 1) // TM
    blk_end = jnp.cumsum(nblk)
    pstart = (blk_end - nblk) * TM
    n_used = blk_end[-1]
    blk = jnp.minimum(jnp.arange(N_SLOTS_BLK, dtype=I32), n_used - 1)
    blk_exp = jnp.sum((blk[:, None] >= blk_end[None, :]).astype(I32), -1)
    blk_exp = jnp.minimum(blk_exp, N_EXPERTS - 1)
    experts = jnp.arange(N_EXPERTS, dtype=I32)
    start_of = jnp.sum(jnp.where(meta[:, 0:2, None] == experts, pstart, 0), -1)
    tok = jnp.arange(NT, dtype=I32)[:, None]
    dest = jnp.where(tok < NV, start_of + meta[:, 2:4], 0).astype(I32).reshape(-1)
    pad_lo = pstart + cnt
    pad_hi = pstart + nblk * TM
    return dest, blk_exp, n_used.reshape(1).astype(I32), pad_lo.astype(I32), pad_hi.astype(I32)


def _row(ref, r):
    return ref.at[pl.ds(pl.multiple_of(r * ROW_VREGS, ROW_VREGS), ROW_VREGS), :]


def _dispatch_kernel(dest_ref, plo_ref, phi_ref, nused_ref, h_ref, xs_ref, zero_ref, sem, zsem):
    i = pl.program_id(0)
    last = pl.num_programs(0) - 1

    def scatter(n_tok):
        def issue(t, carry):
            tok = i * TR + t
            for kk in range(2):
                pltpu.make_async_copy(_row(h_ref, t), _row(xs_ref, dest_ref[2 * tok + kk]), sem).start()
            return carry
        lax.fori_loop(0, n_tok, issue, 0)
        n_rows = 2 * n_tok * ROW_VREGS
        pltpu.make_async_copy(h_ref.at[pl.ds(0, n_rows // 2), :], xs_ref.at[pl.ds(0, n_rows // 2), :], sem).wait()
        pltpu.make_async_copy(h_ref.at[pl.ds(0, n_rows // 2), :], xs_ref.at[pl.ds(0, n_rows // 2), :], sem).wait()

    @pl.when(i < last)
    def _():
        scatter(TR)

    @pl.when(i == last)
    def _():
        scatter(NV - (N_TILES - 1) * TR)
        zero_ref[...] = jnp.zeros_like(zero_ref)

        zero_row = zero_ref.at[pl.ds(0, ROW_VREGS), :]

        def per_expert(e, carry):
            def fill(p, c2):
                pltpu.make_async_copy(zero_row, _row(xs_ref, p), zsem).start()
                return c2
            lax.fori_loop(plo_ref[e], phi_ref[e], fill, 0)

            def fill_wait(p, c2):
                pltpu.make_async_copy(zero_row, _row(xs_ref, 0), zsem).wait()
                return c2
            lax.fori_loop(plo_ref[e], phi_ref[e], fill_wait, 0)
            return carry
        lax.fori_loop(0, N_EXPERTS, per_expert, 0)

        def block_of(nb):
            return xs_ref.at[pl.ds(pl.multiple_of(nb * (TM * ROW_VREGS), TM * ROW_VREGS), TM * ROW_VREGS), :]

        def fill_blk(nb, carry):
            pltpu.make_async_copy(zero_ref, block_of(nb), zsem).start()
            return carry
        lax.fori_loop(nused_ref[0], N_SLOTS_BLK, fill_blk, 0)

        def fill_blk_wait(nb, carry):
            pltpu.make_async_copy(zero_ref, block_of(0), zsem).wait()
            return carry
        lax.fori_loop(nused_ref[0], N_SLOTS_BLK, fill_blk_wait, 0)


def _dispatch(h2_tiles, dest, pad_lo, pad_hi, n_used):
    return pl.pallas_call(
        _dispatch_kernel,
        grid_spec=pltpu.PrefetchScalarGridSpec(
            num_scalar_prefetch=4, grid=(N_TILES,),
            in_specs=[pl.BlockSpec((TR * ROW_VREGS, LANES), lambda i, *_: (i, 0))],
            out_specs=pl.BlockSpec(memory_space=pl.ANY),
            scratch_shapes=[pltpu.VMEM((TM * ROW_VREGS, LANES), F32),
                            pltpu.SemaphoreType.DMA(()), pltpu.SemaphoreType.DMA(())]),
        out_shape=jax.ShapeDtypeStruct((N_SLOTS * ROW_VREGS, LANES), F32),
        compiler_params=_cparams(("arbitrary",)),
        name="dispatch",
    )(dest, pad_lo, pad_hi, n_used, h2_tiles)


class _ExpertRows:
    scratch = [pltpu.VMEM((2, 2 * TR * ROW_VREGS, LANES), F32), pltpu.SemaphoreType.DMA((2,))]

    def __init__(self, dest_ref, ys_ref, buf, sems):
        self.dest_ref, self.ys_ref, self.buf, self.sems = dest_ref, ys_ref, buf, sems

    def _copy(self, src_row, slot, dst_row):
        dst = self.buf.at[slot, pl.ds(pl.multiple_of(dst_row * ROW_VREGS, ROW_VREGS), ROW_VREGS), :]
        return pltpu.make_async_copy(_row(self.ys_ref, src_row), dst, self.sems.at[slot])

    def start(self, tile):
        slot = tile % 2

        def issue(t, carry):
            for kk in range(2):
                self._copy(self.dest_ref[2 * (tile * TR + t) + kk], slot, kk * TR + t).start()
            return carry
        lax.fori_loop(0, TR, issue, 0)

    def wait(self, tile):
        slot = tile % 2
        pltpu.make_async_copy(self.ys_ref.at[pl.ds(0, 2 * TR * ROW_VREGS), :], self.buf.at[slot],
                              self.sems.at[slot]).wait()

    def combine(self, tile, x, gate_ref):
        rows = self.buf.at[tile % 2]
        g = gate_ref[...]
        for kk in range(2):
            y = jnp.concatenate([rows[pl.ds(kk * TR * ROW_VREGS + j, TR, stride=ROW_VREGS), :]
                                 for j in range(ROW_VREGS)], -1)
            x = x + g[:, kk:kk + 1] * y
        return x

    def fetch_combine(self, x, gate_ref):
        i = pl.program_id(0)

        @pl.when(i == 0)
        def _():
            self.start(i)

        @pl.when(i + 1 < pl.num_programs(0))
        def _():
            self.start(i + 1)

        self.wait(i)
        return self.combine(i, x, gate_ref)


def _ffn_kernel(be_ref, nused_ref, xs_ref, w1_ref, w3_ref, w2_ref, y_ref, w1b, w3b, w2b):
    nb = pl.program_id(0)

    @pl.when(nb < nused_ref[0])
    def _():
        prev = be_ref[jnp.maximum(nb - 1, 0)]
        fresh = jnp.logical_or(nb == 0, be_ref[nb] != prev)

        @pl.when(fresh)
        def _():
            w1b[...] = w1_ref[...].astype(BF16)
            w3b[...] = w3_ref[...].astype(BF16)
            w2b[...] = w2_ref[...].astype(BF16)

        x = _load_rows_from_tiles(xs_ref, TM).astype(BF16)
        a = jnp.dot(x, w1b[...], preferred_element_type=F32)
        b = jnp.dot(x, w3b[...], preferred_element_type=F32)
        mid = (_silu(a) * b).astype(BF16)
        y = jnp.dot(mid, w2b[...], preferred_element_type=F32)
        _store_rows_as_tiles(y_ref, y)

    @pl.when(nb >= nused_ref[0])
    def _():
        y_ref[...] = jnp.zeros_like(y_ref)


def _experts(xs, blk_exp, n_used, layer, w1, w3, w2):
    rows = pl.BlockSpec((TM * ROW_VREGS, LANES), lambda nb, be, nu: (nb, 0))

    def wspec(a, b):
        return pl.BlockSpec((None, None, a, b), lambda nb, be, nu: (layer, be[nb], 0, 0))

    return pl.pallas_call(
        _ffn_kernel,
        grid_spec=pltpu.PrefetchScalarGridSpec(
            num_scalar_prefetch=2, grid=(N_SLOTS_BLK,),
            in_specs=[rows, wspec(D_MODEL, D_EXPERT), wspec(D_MODEL, D_EXPERT), wspec(D_EXPERT, D_MODEL)],
            out_specs=rows,
            scratch_shapes=[pltpu.VMEM((D_MODEL, D_EXPERT), BF16), pltpu.VMEM((D_MODEL, D_EXPERT), BF16),
                            pltpu.VMEM((D_EXPERT, D_MODEL), BF16)]),
        out_shape=jax.ShapeDtypeStruct((N_SLOTS * ROW_VREGS, LANES), F32),
        compiler_params=_cparams(("arbitrary",)),
        name="experts",
    )(blk_exp, n_used, xs, w1, w3, w2)


def _moe(h2_tiles, meta, counts, layer, w1, w3, w2):
    dest, blk_exp, n_used, pad_lo, pad_hi = _plan(meta, counts)
    xs = _dispatch(h2_tiles, dest, pad_lo, pad_hi, n_used)
    return _experts(xs, blk_exp, n_used, layer, w1, w3, w2), dest


def _pool_kernel(dest_ref, x1_ref, ys_ref, gate0_ref, gmix_ref, sp_ref, pw_ref, ps_ref, gffn_ref,
                 wr_ref, br_ref, tri_ref,
                 x3_ref, h2_ref, hkeep_ref, hs_ref, meta_ref, gate_ref, cnt_out_ref,
                 cnt_ref, ext_ref, e1_ref, e2_ref, e3_ref, e4_ref, mixed_ref, ybuf, ysems):
    i = pl.program_id(0)
    x2 = _ExpertRows(dest_ref, ys_ref, ybuf, ysems).fetch_combine(x1_ref[...], gate0_ref)
    h = _rms(x2, gmix_ref[...])
    H = POOL_HALO
    lvl_refs = (e1_ref, e2_ref, e3_ref, e4_ref)

    @pl.when(i < SAMPLE_TILE)
    def _():
        hkeep_ref[...] = h[TR - 16:TR]
        @pl.when(i % TILES_PER_SEQ == 0)
        def _():
            ext_ref[0:H, :] = jnp.zeros((H, D_MODEL), F32)
        ext_ref[H:H + TR, :] = h
        src = ext_ref
        for lv in range(4):
            sh = 1 << lv
            lo = 8 * (lv + 1)
            c0 = POOL_GROUP * lv
            dst = lvl_refs[lv]
            dst[lo:H + TR, c0:] = src[lo:H + TR, c0:] + src[lo - sh:H + TR - sh, c0:]
            src = dst
        pos = (i % TILES_PER_SEQ) * TR + lax.broadcasted_iota(I32, (TR, 1), 0)
        for gi, w in enumerate(POOL_WINDOWS):
            cs = slice(gi * POOL_GROUP, (gi + 1) * POOL_GROUP)
            inv = 1.0 / jnp.minimum(w, pos + 1).astype(F32)
            mixed_ref[:, cs] = lvl_refs[gi][H:H + TR, cs] * inv - h[:, cs]
        ext_ref[0:H, :] = h[TR - H:TR]

    @pl.when(i == SAMPLE_TILE)
    def _():
        hs = h[0:DEC_BATCH]
        hs_ref[...] = hs
        mixed_ref[...] = jnp.zeros_like(mixed_ref)
        for gi, w in enumerate(POOL_WINDOWS):
            cs = slice(gi * POOL_GROUP, (gi + 1) * POOL_GROUP)
            win = hs[:, cs]
            for dlt in range(1, w):
                win = win + sp_ref[POOL_KEEP - dlt][:, cs]
            mixed_ref[0:DEC_BATCH, cs] = win * (1.0 / w) - hs[:, cs]

    mixed = mixed_ref[...].astype(BF16)
    outs = [jnp.dot(mixed[:, gi * POOL_GROUP:(gi + 1) * POOL_GROUP], pw_ref[gi], preferred_element_type=F32)
            for gi in range(len(POOL_WINDOWS))]
    x3 = x2 + jnp.concatenate(outs, -1) * ps_ref[...]
    x3_ref[...] = x3
    h2 = _rms(x3, gffn_ref[...])
    _store_rows_as_tiles(h2_ref, h2)
    _route(h2, wr_ref, br_ref, tri_ref, cnt_ref, meta_ref, gate_ref, cnt_out_ref)


def _const_spec_p(a):
    return pl.BlockSpec(a.shape, lambda i, *_, n=a.ndim: (0,) * n)


def _pool_route(dest, x1, ys, gates0, g_mix, sp_t, pw_bf16, p_scale, g_ffn, wr, br, tri):
    def rows(shape):
        return pl.BlockSpec(shape, lambda i, *_: (i, 0))

    ext = pltpu.VMEM((POOL_HALO + TR, D_MODEL), F32)
    consts = [g_mix, sp_t, pw_bf16, p_scale, g_ffn, wr, br, tri]
    return pl.pallas_call(
        _pool_kernel,
        grid_spec=pltpu.PrefetchScalarGridSpec(
            num_scalar_prefetch=1, grid=(N_TILES,),
            in_specs=[rows((TR, D_MODEL)), pl.BlockSpec(memory_space=pl.ANY), rows((TR, 8))]
                     + [_const_spec_p(a) for a in consts],
            out_specs=[rows((TR, D_MODEL)), rows((TR * ROW_VREGS, LANES)),
                       pl.BlockSpec((None, 16, D_MODEL),
                                    lambda i, *_: (jnp.minimum(i // TILES_PER_SEQ, BATCH - 1), 0, 0)),
                       pl.BlockSpec((DEC_BATCH, D_MODEL), lambda i, *_: (0, 0)),
                       rows((TR, 8)), rows((TR, 8)), pl.BlockSpec((1, LANES), lambda i, *_: (0, 0))],
            scratch_shapes=[pltpu.VMEM((1, LANES), F32), ext, ext, ext, ext, ext,
                            pltpu.VMEM((TR, D_MODEL), F32)] + _ExpertRows.scratch),
        out_shape=[jax.ShapeDtypeStruct((NT, D_MODEL), F32),
                   jax.ShapeDtypeStruct((NT * ROW_VREGS, LANES), F32),
                   jax.ShapeDtypeStruct((BATCH, 16, D_MODEL), F32),
                   jax.ShapeDtypeStruct((DEC_BATCH, D_MODEL), F32)] + _ROUTE_OUT_SHAPES,
        compiler_params=_cparams(("arbitrary",)),
        name="pool_route",
    )(dest, x1, ys, gates0, *consts)


def _final_kernel(dest_ref, x_ref, ys_ref, gate_ref, g_ref, yp_ref, ysm_ref, ybuf, ysems):
    i = pl.program_id(0)
    y = _rms(_ExpertRows(dest_ref, ys_ref, ybuf, ysems).fetch_combine(x_ref[...], gate_ref), g_ref[...])

    @pl.when(i < SAMPLE_TILE)
    def _():
        yp_ref[...] = y

    @pl.when(i == SAMPLE_TILE)
    def _():
        ysm_ref[...] = y[0:DEC_BATCH]


def _final(dest, x3, ys, gates, g_final):
    return pl.pallas_call(
        _final_kernel,
        grid_spec=pltpu.PrefetchScalarGridSpec(
            num_scalar_prefetch=1, grid=(N_TILES,),
            in_specs=[pl.BlockSpec((TR, D_MODEL), lambda i, *_: (i, 0)), pl.BlockSpec(memory_space=pl.ANY),
                      pl.BlockSpec((TR, 8), lambda i, *_: (i, 0)), _const_spec_p(g_final)],
            out_specs=[pl.BlockSpec((TR, D_MODEL), lambda i, *_: (jnp.minimum(i, SAMPLE_TILE - 1), 0)),
                       pl.BlockSpec((DEC_BATCH, D_MODEL), lambda i, *_: (0, 0))],
            scratch_shapes=_ExpertRows.scratch),
        out_shape=[jax.ShapeDtypeStruct((NP, D_MODEL), F32), jax.ShapeDtypeStruct((DEC_BATCH, D_MODEL), F32)],
        compiler_params=_cparams(("arbitrary",)),
        name="final_norm",
    )(dest, x3, ys, gates, g_final)


def kernel(x_prompt, x_sample, cache_win_k, cache_win_v, state_hgrn, state_pool, rel_bias, norm_mix, norm_ffn,
           norm_final, w_in, w_out, hgrn_lb, hgrn_gnorm, pool_w, pool_scale, moe_wg_group, moe_bg_group,
           moe_wg_exp, moe_bg_exp, moe_w1, moe_w3, moe_w2):
    xp = x_prompt.reshape(NP, D_MODEL)
    xs_pad = jnp.pad(x_sample.reshape(DEC_BATCH, D_MODEL), ((0, TR - DEC_BATCH), (0, 0)))
    lb = jnp.cumsum(jax.nn.softmax(hgrn_lb.astype(F32), axis=0), axis=0)[0:1]
    gnorm = hgrn_gnorm[0:1].astype(F32)

    q, k, v, hq, hf, hi, hg, k_win, v_win = _inproj(xp, xs_pad, norm_mix[0:1], w_in[0].astype(BF16))
    oa = _attention_prompt(q, k, v, _attn_bias_tables(rel_bias))
    ob, s_prompt = _hgrn_prompt(hq, hf, hi, hg, lb, gnorm)
    oa_s, ob_s, s_sample = _sample_mixers(
        q, k, v, hq, hf, hi, hg,
        cache_win_k[0].reshape(DEC_BATCH, W_MAX * N_HEADS, D_HEAD),
        cache_win_v[0].reshape(DEC_BATCH, W_MAX * N_HEADS, D_HEAD),
        state_hgrn[0], _sample_bias_tables(rel_bias), lb, gnorm)
    wr, br, tri = _router_operands(moe_wg_group[0], moe_bg_group[0], moe_wg_exp[0], moe_bg_exp[0])
    x1, h2, meta, gates0, counts = _outproj_route(xp, xs_pad, oa, oa_s, ob, ob_s, w_out[0].astype(BF16),
                                                  norm_ffn[0:1], wr, br, tri)
    ys, dest = _moe(h2, meta, counts, 0, moe_w1, moe_w3, moe_w2)

    wr, br, tri = _router_operands(moe_wg_group[1], moe_bg_group[1], moe_wg_exp[1], moe_bg_exp[1])
    sp_t = jnp.transpose(state_pool[0], (1, 0, 2))
    x3, h2, hkeep, hsample, meta, gates1, counts = _pool_route(
        dest, x1, ys, gates0, norm_mix[1:2], sp_t, pool_w[0].astype(BF16), pool_scale[0:1], norm_ffn[1:2],
        wr, br, tri)
    ys, dest = _moe(h2, meta, counts, 1, moe_w1, moe_w3, moe_w2)
    y_prompt, y_sample = _final(dest, x3, ys, gates1, norm_final.reshape(1, D_MODEL))

    k_s = k[NP:NV].reshape(1, DEC_BATCH, 1, N_HEADS, D_HEAD)
    v_s = v[NP:NV].reshape(1, DEC_BATCH, 1, N_HEADS, D_HEAD)
    return (y_prompt.reshape(BATCH, SEQ, D_MODEL), y_sample.reshape(DEC_BATCH, 1, D_MODEL),
            k_win.reshape(1, BATCH, W_MAX, N_HEADS, D_HEAD), v_win.reshape(1, BATCH, W_MAX, N_HEADS, D_HEAD),
            s_prompt[None], hkeep[:, 16 - POOL_KEEP:][None],
            k_s, v_s, s_sample[None], hsample.reshape(1, DEC_BATCH, 1, D_MODEL))
```

```python
import functools

import numpy as np
import jax
import jax.numpy as jnp
from jax import lax
from jax.experimental import pallas as pl
from jax.experimental.pallas import tpu as pltpu

F32 = jnp.float32
BF16 = jnp.bfloat16
I32 = jnp.int32

D_MODEL = 1024
BATCH = 4
SEQ = 4096
DEC_BATCH = 32
PAST_LEN = 8192
W_MAX = 2048
N_HEADS = 4
D_HEAD = 128
A_WIDTH = N_HEADS * D_HEAD
N_PROJ = 7 * A_WIDTH
DILATED = ((128, 1), (512, 4), (2048, 16))
NUM_BUCKETS = 32
MAX_DISTANCE = 2048
POOL_WINDOWS = (2, 4, 8, 16)
POOL_GROUP = 256
POOL_KEEP = 15
N_GROUPS = 4
EXPERTS_PER_GROUP = 8
N_EXPERTS = 32
D_EXPERT = 512
EPS = 1e-6
NEG = -1e30

LANES = 128
SUBLANES = 8
ROW_VREGS = D_MODEL // LANES
TR = 256
NP = BATCH * SEQ
NV = NP + DEC_BATCH
NT = NP + TR
N_TILES = NT // TR
SAMPLE_TILE = NP // TR
TILES_PER_SEQ = SEQ // TR
TM = 256
N_SLOTS_BLK = (2 * NV + N_EXPERTS * (TM - 1) + TM - 1) // TM
N_SLOTS = N_SLOTS_BLK * TM
QB = 128
ATTN_UNROLL = 8
CH = 128
HGRN_HEADS_PER_STEP = 2
N_LEVELS = 7
POOL_HALO = 32
VMEM_LIMIT = 56 * 1024 * 1024


def _cparams(sem=None, vmem=VMEM_LIMIT):
    kw = dict(vmem_limit_bytes=vmem)
    if sem is not None:
        kw["dimension_semantics"] = sem
    return pltpu.CompilerParams(**kw)


def _rms(x, g):
    return x * lax.rsqrt(jnp.mean(x * x, -1, keepdims=True) + EPS) * g


def _sigmoid(x):
    return 1.0 / (1.0 + jnp.exp(-x))


def _silu(x):
    return x * _sigmoid(x)


def _token_tile(i, xp_ref, xs_ref):
    return jnp.where(i == SAMPLE_TILE, xs_ref[...], xp_ref[...])


_PROMPT_ROWS = pl.BlockSpec((TR, D_MODEL), lambda i: (jnp.minimum(i, SAMPLE_TILE - 1), 0))
_SAMPLE_ROWS = pl.BlockSpec((TR, D_MODEL), lambda i: (0, 0))
WIN_TILES = W_MAX // TR


def _window_block(i):
    seq = jnp.minimum(i // TILES_PER_SEQ, BATCH - 1)
    j = jnp.clip(i % TILES_PER_SEQ - (TILES_PER_SEQ - WIN_TILES), 0, WIN_TILES - 1)
    return jnp.where(i >= SAMPLE_TILE, BATCH * WIN_TILES - 1, seq * WIN_TILES + j)


def _inproj_kernel(xp_ref, xs_ref, g_ref, w_ref, *out_refs):
    i = pl.program_id(0)
    h = _rms(_token_tile(i, xp_ref, xs_ref), g_ref[...])
    p = jnp.dot(h.astype(BF16), w_ref[...], preferred_element_type=F32)
    for n, o_ref in enumerate(out_refs[:7]):
        o_ref[...] = p[:, n * A_WIDTH:(n + 1) * A_WIDTH]

    @pl.when(jnp.logical_and(i < SAMPLE_TILE, i % TILES_PER_SEQ >= TILES_PER_SEQ - WIN_TILES))
    def _():
        for n, o_ref in ((1, out_refs[7]), (2, out_refs[8])):
            for h_i in range(N_HEADS):
                c0 = n * A_WIDTH + h_i * D_HEAD
                o_ref[pl.ds(h_i, TR, stride=N_HEADS), :] = p[:, c0:c0 + D_HEAD]


def _inproj(xp, xs_pad, g, w_bf16):
    out_sds = ([jax.ShapeDtypeStruct((NT, A_WIDTH), F32)] * 7
               + [jax.ShapeDtypeStruct((BATCH * W_MAX * N_HEADS, D_HEAD), F32)] * 2)
    win = pl.BlockSpec((TR * N_HEADS, D_HEAD), lambda i: (_window_block(i), 0))
    return pl.pallas_call(
        _inproj_kernel,
        grid=(N_TILES,),
        in_specs=[_PROMPT_ROWS, _SAMPLE_ROWS,
                  pl.BlockSpec((1, D_MODEL), lambda i: (0, 0)),
                  pl.BlockSpec((D_MODEL, N_PROJ), lambda i: (0, 0))],
        out_specs=[pl.BlockSpec((TR, A_WIDTH), lambda i: (i, 0))] * 7 + [win, win],
        out_shape=out_sds,
        compiler_params=_cparams(("arbitrary",)),
        name="inproj",
    )(xp, xs_pad, g, w_bf16)


def _t5_bucket(dist):
    max_exact = NUM_BUCKETS // 2
    d = np.asarray(dist)
    large = max_exact + np.floor(np.log(np.maximum(d, 1) / max_exact)
                                 / np.log(MAX_DISTANCE / max_exact) * (NUM_BUCKETS - max_exact)).astype(np.int32)
    large = np.minimum(large, NUM_BUCKETS - 1)
    return np.where(d < max_exact, d, large).astype(np.int32)


def _attn_bias_tables(rel_bias):
    period = 3 * QB
    m = np.arange(period)
    u = np.where(m < 2 * QB, m, m - period)
    pick = np.zeros((len(DILATED), 2, period, NUM_BUCKETS), np.float32)
    mask = np.zeros((len(DILATED), 2, period, 1), np.float32)
    for bi, (w, d) in enumerate(DILATED):
        nk = w // d
        for vi, off in enumerate((0, QB)):
            j = off - u
            ok = (j >= 0) & (j <= nk)
            pick[bi, vi, m[ok], _t5_bucket(d * j[ok])] = 1.0
            mask[bi, vi, ~ok, 0] = NEG
    vec = jnp.einsum("bvmk,kh->hbvm", pick, rel_bias.astype(F32), precision=lax.Precision.HIGHEST)
    vec = vec + jnp.transpose(mask, (3, 0, 1, 2))
    tiled = jnp.tile(vec, (1, 1, 1, QB))[..., :QB * (period - 1)]
    return tiled.reshape(N_HEADS, len(DILATED), 2, QB, period - 1)[..., :2 * QB]


def _attn_kernel(q_ref, k_ref, v_ref, bias_ref, o_ref,
                 qd, kd, vd, ud, md, sd, u_acc, m_acc, s_acc):
    scale = D_HEAD ** -0.5

    def block_stats(bi, t, nb):
        has_prev = jnp.minimum(t % nb, 1)
        q0 = pl.multiple_of(t * QB, QB)
        k0 = pl.multiple_of((t - has_prev) * QB, QB)
        qb = qd[pl.ds(q0, QB), :]
        kb = kd[pl.ds(k0, 2 * QB), :]
        vb = vd[pl.ds(k0, 2 * QB), :]
        s = lax.dot_general(qb, kb, (((1,), (1,)), ((), ())), preferred_element_type=F32)
        s = s + bias_ref[bi, has_prev]
        mb = jnp.max(s, -1, keepdims=True)
        p = jnp.exp(s - mb)
        sb = jnp.sum(p, -1, keepdims=True)
        u = jnp.dot(p.astype(BF16), vb, preferred_element_type=F32)
        return q0, mb, sb, u

    for bi in (2, 1, 0):
        d = DILATED[bi][1]
        cl = SEQ // d
        nb = cl // QB
        for r in range(d):
            src = pl.ds(r, cl, stride=d) if d > 1 else pl.ds(0, cl)
            dst = pl.ds(r * cl, cl)
            qd[dst, :] = (q_ref[src, :] * scale).astype(BF16)
            kd[dst, :] = k_ref[src, :].astype(BF16)
            vd[dst, :] = v_ref[src, :].astype(BF16)

        if d > 1:
            def body(g, carry, bi=bi, nb=nb):
                for un in range(ATTN_UNROLL):
                    q0, mb, sb, u = block_stats(bi, g * ATTN_UNROLL + un, nb)
                    ud[pl.ds(q0, QB), :] = u
                    md[pl.ds(q0, QB), :] = jnp.broadcast_to(mb, (QB, D_HEAD))
                    sd[pl.ds(q0, QB), :] = jnp.broadcast_to(sb, (QB, D_HEAD))
                return carry
            lax.fori_loop(0, SEQ // QB // ATTN_UNROLL, body, 0)
            for r in range(d):
                pos = pl.ds(r, cl, stride=d)
                cls = pl.ds(r * cl, cl)
                if bi == 2:
                    u_acc[pos, :] = ud[cls, :]
                    m_acc[pos, :] = md[cls, :]
                    s_acc[pos, :] = sd[cls, :]
                else:
                    m_old = m_acc[pos, :]
                    m_blk = md[cls, :]
                    m_new = jnp.maximum(m_old, m_blk)
                    a = jnp.exp(m_old - m_new)
                    b = jnp.exp(m_blk - m_new)
                    u_acc[pos, :] = a * u_acc[pos, :] + b * ud[cls, :]
                    s_acc[pos, :] = a * s_acc[pos, :] + b * sd[cls, :]
                    m_acc[pos, :] = m_new
        else:
            def body(g, carry, bi=bi, nb=nb):
                for un in range(ATTN_UNROLL):
                    q0, mb, sb, u = block_stats(bi, g * ATTN_UNROLL + un, nb)
                    rows = pl.ds(q0, QB)
                    m_old = m_acc[rows, :]
                    m_new = jnp.maximum(m_old, mb)
                    a = jnp.exp(m_old - m_new)
                    b = jnp.exp(mb - m_new)
                    num = a * u_acc[rows, :] + b * u
                    den = a * s_acc[rows, :] + b * sb
                    o_ref[rows, :] = (num / den).astype(o_ref.dtype)
                return carry
            lax.fori_loop(0, SEQ // QB // ATTN_UNROLL, body, 0)


def _attention_prompt(q, k, v, bias_tabs):
    blk = pl.BlockSpec((SEQ, D_HEAD), lambda b, h: (b, h))
    return pl.pallas_call(
        _attn_kernel,
        grid=(BATCH, N_HEADS),
        in_specs=[blk, blk, blk,
                  pl.BlockSpec((None, 3, 2, QB, 2 * QB), lambda b, h: (h, 0, 0, 0, 0))],
        out_specs=blk,
        out_shape=jax.ShapeDtypeStruct((NP, A_WIDTH), BF16),
        scratch_shapes=[pltpu.VMEM((SEQ, D_HEAD), BF16)] * 3
                       + [pltpu.VMEM((SEQ, D_HEAD), F32)] * 6,
        compiler_params=_cparams(("arbitrary", "arbitrary")),
        name="attn_prompt",
    )(q, k, v, bias_tabs)


def _hgrn_tables():
    t = np.arange(CH)
    u = np.arange(CH)
    sums_q = np.zeros((1 + N_LEVELS, CH, CH), np.float32)
    sums_k = np.zeros((2 + N_LEVELS, CH, CH), np.float32)
    sums_q[0] = (u[None, :] <= t[:, None])
    sums_k[0] = (u[None, :] > t[:, None])
    sums_k[1 + N_LEVELS] = 1.0
    pair = np.zeros((N_LEVELS, CH, CH), np.float32)
    for l in range(N_LEVELS):
        h = CH >> (l + 1)
        is_q = (t // h) % 2 == 1
        half_start = (t // h) * h
        half_end = half_start + h
        sel_q = (u[None, :] >= half_start[:, None]) & (u[None, :] <= t[:, None])
        sel_k = (u[None, :] > t[:, None]) & (u[None, :] < half_end[:, None])
        sums_q[1 + l] = sel_q & is_q[:, None]
        sums_k[1 + l] = sel_k & ~is_q[:, None]
        same = (t[:, None] // (2 * h)) == (t[None, :] // (2 * h))
        pair[l] = same & is_q[:, None] & (~is_q)[None, :]
    sums_kt = np.transpose(sums_k, (2, 0, 1)).reshape(CH, (2 + N_LEVELS) * CH)
    sums_q = sums_q.reshape((1 + N_LEVELS) * CH, CH)
    return (jnp.asarray(np.concatenate([sums_q, sums_q], 1), BF16),
            jnp.asarray(np.concatenate([sums_kt, sums_kt], 0), BF16),
            jnp.asarray(pair), jnp.asarray(np.eye(CH, dtype=np.float32)))


def _split_bf16(x):
    hi = x.astype(BF16)
    return hi, (x - hi.astype(F32)).astype(BF16)


def _hgrn_kernel(q_ref, f_ref, i_ref, g_ref, lb_ref, gn_ref, sq_ref, skt_ref, pair_ref, eye_ref, o_ref, s_ref):
    gn = gn_ref[...]

    def chunk(c, st, hh):
        rows = pl.ds(pl.multiple_of(c * CH, CH), CH)
        cols = slice(hh * D_HEAD, (hh + 1) * D_HEAD)
        lb = lb_ref[:, cols]
        q = _silu(q_ref[rows, cols])
        f = lb + (1.0 - lb) * _sigmoid(f_ref[rows, cols])
        lf = jnp.log(f)
        k = 1.0 - f
        v16 = i_ref[rows, cols].astype(BF16)
        kt = k.T
        exq = jnp.exp(jnp.dot(sq_ref[...], jnp.concatenate(_split_bf16(lf), 0),
                              preferred_element_type=F32))
        exk = jnp.exp(jnp.dot(jnp.concatenate(_split_bf16(lf.T), 1), skt_ref[...],
                              preferred_element_type=F32))
        inter = jnp.dot((q * exq[0:CH]).astype(BF16), st.astype(BF16), preferred_element_type=F32)
        sc = eye_ref[...] * jnp.sum(q * k, -1, keepdims=True)
        for l in range(N_LEVELS):
            ql = (q * exq[(1 + l) * CH:(2 + l) * CH]).astype(BF16)
            klt = (kt * exk[:, (1 + l) * CH:(2 + l) * CH]).astype(BF16)
            sc = sc + pair_ref[l] * jnp.dot(ql, klt, preferred_element_type=F32)
        o = inter + jnp.dot(sc.astype(BF16), v16, preferred_element_type=F32)
        st_new = (st * exk[:, (1 + N_LEVELS) * CH:]
                  + jnp.dot((kt * exk[:, 0:CH]).astype(BF16), v16, preferred_element_type=F32))
        o = _rms(o, gn) * _silu(g_ref[rows, cols])
        o_ref[rows, cols] = o.astype(o_ref.dtype)
        return st_new

    def step(c, states):
        return tuple(chunk(c, st, hh) for hh, st in enumerate(states))

    zero = jnp.zeros((D_HEAD, D_HEAD), F32)
    states = lax.fori_loop(0, SEQ // CH, step, (zero,) * HGRN_HEADS_PER_STEP)
    for hh, st in enumerate(states):
        s_ref[hh] = st


def _hgrn_prompt(hq, hf, hi, hg, lb, gnorm):
    tables = _hgrn_tables()
    width = HGRN_HEADS_PER_STEP * D_HEAD
    blk = pl.BlockSpec((SEQ, width), lambda b, h: (b, h))

    def full(a):
        return pl.BlockSpec(a.shape, lambda b, h, n=a.ndim: (0,) * n)

    return pl.pallas_call(
        _hgrn_kernel,
        grid=(BATCH, N_HEADS // HGRN_HEADS_PER_STEP),
        in_specs=[blk, blk, blk, blk,
                  pl.BlockSpec((1, width), lambda b, h: (0, h)),
                  pl.BlockSpec((1, D_HEAD), lambda b, h: (0, 0))] + [full(a) for a in tables],
        out_specs=[blk, pl.BlockSpec((None, HGRN_HEADS_PER_STEP, D_HEAD, D_HEAD), lambda b, h: (b, h, 0, 0))],
        out_shape=[jax.ShapeDtypeStruct((NP, A_WIDTH), BF16),
                   jax.ShapeDtypeStruct((BATCH, N_HEADS, D_HEAD, D_HEAD), F32)],
        compiler_params=_cparams(("arbitrary", "arbitrary")),
        name="hgrn_prompt",
    )(hq, hf, hi, hg, lb, gnorm, *tables)


def _sample_bias_tables(rel_bias):
    j = np.concatenate([QB - np.arange(QB), [0]])
    pick = np.zeros((len(DILATED), QB + 1, NUM_BUCKETS), np.float32)
    for bi, (w, d) in enumerate(DILATED):
        pick[bi, np.arange(QB + 1), _t5_bucket(d * j)] = 1.0
    return jnp.einsum("bjk,kh->bhj", pick, rel_bias.astype(F32), precision=lax.Precision.HIGHEST)[..., None]


def _bf16_round(x):
    return x.astype(BF16).astype(F32)


def _col(row, eye):
    return jnp.sum(eye * row, -1, keepdims=True)


def _sample_kernel(q_ref, k_ref, v_ref, hq_ref, hf_ref, hi_ref, hg_ref, kc_ref, vc_ref, s0_ref,
                   bias_ref, lb_ref, gn_ref, eye_ref, oa_ref, ob_ref, s_ref, oa_acc, ob_acc):
    b = pl.program_id(0)
    row = pl.ds(b, 1)
    scale = D_HEAD ** -0.5
    q = _bf16_round(q_ref[row, :] * scale)
    kn = _bf16_round(k_ref[row, :])
    vn = _bf16_round(v_ref[row, :])
    eye = eye_ref[...]

    stats = []
    for bi, (w, d) in enumerate(DILATED):
        per_head = []
        for h in range(N_HEADS):
            hs = slice(h * D_HEAD, (h + 1) * D_HEAD)
            rows = pl.ds((W_MAX - QB * d) * N_HEADS + h, QB, stride=d * N_HEADS)
            kb = _bf16_round(kc_ref[rows, :])
            vb = _bf16_round(vc_ref[rows, :])
            s = jnp.sum(kb * q[:, hs], -1, keepdims=True) + bias_ref[bi, h, 0:QB]
            s0 = jnp.sum(q[:, hs] * kn[:, hs], -1, keepdims=True) + bias_ref[bi, h, QB:QB + 1]
            m = jnp.maximum(jnp.max(s, 0, keepdims=True), s0)
            p = jnp.exp(s - m)
            p0 = jnp.exp(s0 - m)
            ssum = jnp.sum(p, 0, keepdims=True) + p0
            u = jnp.sum(_bf16_round(p) * vb, 0, keepdims=True) + _bf16_round(p0) * vn[:, hs]
            per_head.append((m, ssum, u))
        stats.append(per_head)
    outs = []
    for h in range(N_HEADS):
        m_all = functools.reduce(jnp.maximum, [stats[bi][h][0] for bi in range(3)])
        num = 0.0
        den = 0.0
        for bi in range(3):
            m, ssum, u = stats[bi][h]
            c = jnp.exp(m - m_all)
            num = num + c * u
            den = den + c * ssum
        outs.append(num / den)
    oa_acc[row, :] = jnp.concatenate(outs, -1)

    qh = _silu(hq_ref[row, :])
    lb = lb_ref[...]
    f = lb + (1.0 - lb) * _sigmoid(hf_ref[row, :])
    vi = hi_ref[row, :]
    gate = _silu(hg_ref[row, :])
    gn = gn_ref[...]
    obs = []
    for h in range(N_HEADS):
        hs = slice(h * D_HEAD, (h + 1) * D_HEAD)
        f_col = _col(f[:, hs], eye)
        q_col = _col(qh[:, hs], eye)
        s_old = s0_ref[h]
        s_ref[h] = f_col * s_old + (1.0 - f_col) * vi[:, hs]
        inter = jnp.sum(_bf16_round(q_col * f_col) * _bf16_round(s_old), 0, keepdims=True)
        qk = jnp.sum(qh[:, hs] * (1.0 - f[:, hs]), -1, keepdims=True)
        o = inter + qk * vi[:, hs]
        obs.append(_rms(o, gn) * gate[:, hs])
    ob_acc[row, :] = jnp.concatenate(obs, -1)

    @pl.when(b == DEC_BATCH - 1)
    def _():
        pad = jnp.zeros((TR - DEC_BATCH, A_WIDTH), F32)
        oa_ref[...] = jnp.concatenate([oa_acc[...], pad], 0).astype(oa_ref.dtype)
        ob_ref[...] = jnp.concatenate([ob_acc[...], pad], 0).astype(ob_ref.dtype)


def _sample_mixers(q, k, v, hq, hf, hi, hg, cache_k, cache_v, state, bias_s, lb, gnorm):
    tile = pl.BlockSpec((TR, A_WIDTH), lambda b: (SAMPLE_TILE, 0))
    out_tile = pl.BlockSpec((TR, A_WIDTH), lambda b: (0, 0))
    cache = pl.BlockSpec((None, W_MAX * N_HEADS, D_HEAD), lambda b: (b, 0, 0))
    eye = jnp.eye(D_HEAD, dtype=F32)

    return pl.pallas_call(
        _sample_kernel,
        grid=(DEC_BATCH,),
        in_specs=[tile, tile, tile, tile, tile, tile, tile, cache, cache,
                  pl.BlockSpec((None, N_HEADS, D_HEAD, D_HEAD), lambda b: (b, 0, 0, 0)),
                  pl.BlockSpec(bias_s.shape, lambda b: (0, 0, 0, 0)),
                  pl.BlockSpec((1, A_WIDTH), lambda b: (0, 0)),
                  pl.BlockSpec((1, D_HEAD), lambda b: (0, 0)),
                  pl.BlockSpec((D_HEAD, D_HEAD), lambda b: (0, 0))],
        out_specs=[out_tile, out_tile,
                   pl.BlockSpec((None, N_HEADS, D_HEAD, D_HEAD), lambda b: (b, 0, 0, 0))],
        out_shape=[jax.ShapeDtypeStruct((TR, A_WIDTH), BF16), jax.ShapeDtypeStruct((TR, A_WIDTH), BF16),
                   jax.ShapeDtypeStruct((DEC_BATCH, N_HEADS, D_HEAD, D_HEAD), F32)],
        scratch_shapes=[pltpu.VMEM((DEC_BATCH, A_WIDTH), F32)] * 2,
        compiler_params=_cparams(("arbitrary",)),
        name="sample_mixers",
    )(q, k, v, hq, hf, hi, hg, cache_k, cache_v, state, bias_s, lb, gnorm, eye)


def _store_rows_as_tiles(ref, val):
    n = val.shape[0]
    for j in range(ROW_VREGS):
        ref[pl.ds(j, n, stride=ROW_VREGS), :] = val[:, j * LANES:(j + 1) * LANES]


def _load_rows_from_tiles(ref, n):
    return jnp.concatenate([ref[pl.ds(j, n, stride=ROW_VREGS), :] for j in range(ROW_VREGS)], -1)


def _route(h2, wr_ref, br_ref, tri_ref, cnt_ref, meta_ref, gate_ref, cnt_out_ref):
    i = pl.program_id(0)

    @pl.when(i == 0)
    def _():
        cnt_ref[...] = jnp.zeros_like(cnt_ref)

    logits = jnp.dot(h2.astype(BF16), wr_ref[...], preferred_element_type=F32) + br_ref[...]
    lane = lax.broadcasted_iota(I32, (TR, LANES), 1).astype(F32)
    big = float(1 << 20)
    is_g = lane < N_GROUPS
    gl = jnp.where(is_g, logits, NEG)
    gmax = jnp.max(gl, -1, keepdims=True)
    gsel = jnp.min(jnp.where(gl == gmax, lane, big), -1, keepdims=True)
    pg = 1.0 / jnp.sum(jnp.where(is_g, jnp.exp(gl - gmax), 0.0), -1, keepdims=True)
    lo = N_GROUPS + EXPERTS_PER_GROUP * gsel
    in_grp = jnp.logical_and(lane >= lo, lane < lo + EXPERTS_PER_GROUP)
    el = jnp.where(in_grp, logits, NEG)
    m1 = jnp.max(el, -1, keepdims=True)
    i1 = jnp.min(jnp.where(el == m1, lane, big), -1, keepdims=True)
    el2 = jnp.where(lane == i1, NEG, el)
    m2 = jnp.max(el2, -1, keepdims=True)
    i2 = jnp.min(jnp.where(el2 == m2, lane, big), -1, keepdims=True)
    r = jnp.exp(m2 - m1)
    g1 = pg / (1.0 + r)
    g2 = pg * r / (1.0 + r)
    e1 = i1 - N_GROUPS
    e2 = i2 - N_GROUPS

    tok = i * TR + lax.broadcasted_iota(I32, (TR, 1), 0)
    valid = tok < NV
    oh1 = jnp.logical_and(lane == e1, valid)
    oh2 = jnp.logical_and(lane == e2, valid)
    oh = jnp.where(jnp.logical_or(oh1, oh2), 1.0, 0.0)
    before = jnp.dot(tri_ref[...], oh.astype(BF16), preferred_element_type=F32) + cnt_ref[...]
    rank1 = jnp.sum(jnp.where(oh1, before, 0.0), -1, keepdims=True)
    rank2 = jnp.sum(jnp.where(oh2, before, 0.0), -1, keepdims=True)
    cnt_ref[...] = cnt_ref[...] + jnp.sum(oh, 0, keepdims=True)

    meta = jnp.where(lane == 0, e1, jnp.where(lane == 1, e2, jnp.where(lane == 2, rank1,
                     jnp.where(lane == 3, rank2, 0.0))))
    gates = jnp.where(lane == 0, g1, jnp.where(lane == 1, g2, 0.0))
    gates = jnp.where(valid, gates, 0.0)
    meta_ref[...] = meta[:, 0:8].astype(I32)
    gate_ref[...] = gates[:, 0:8]
    cnt_out_ref[...] = cnt_ref[...]


def _router_operands(wg_group, bg_group, wg_exp, bg_exp):
    wr = jnp.zeros((D_MODEL, LANES), F32)
    wr = wr.at[:, 0:N_GROUPS].set(wg_group.astype(F32)).at[:, N_GROUPS:N_GROUPS + N_EXPERTS].set(wg_exp.astype(F32))
    br = jnp.zeros((1, LANES), F32)
    br = br.at[0, 0:N_GROUPS].set(bg_group.astype(F32)).at[0, N_GROUPS:N_GROUPS + N_EXPERTS].set(bg_exp.astype(F32))
    tri = jnp.asarray(np.tril(np.ones((TR, TR), np.float32), -1), BF16)
    return wr.astype(BF16), br, tri


_ROUTE_OUT_SPECS = [pl.BlockSpec((TR, 8), lambda i: (i, 0)),
                    pl.BlockSpec((TR, 8), lambda i: (i, 0)),
                    pl.BlockSpec((1, LANES), lambda i: (0, 0))]
_ROUTE_OUT_SHAPES = [jax.ShapeDtypeStruct((NT, 8), I32), jax.ShapeDtypeStruct((NT, 8), F32),
                     jax.ShapeDtypeStruct((1, LANES), F32)]


def _const_spec(a):
    return pl.BlockSpec(a.shape, lambda i, n=a.ndim: (0,) * n)


def _outproj_kernel(xp_ref, xs_ref, oa_ref, oas_ref, ob_ref, obs_ref, w_ref, g_ref, wr_ref, br_ref, tri_ref,
                    x1_ref, meta_ref, gate_ref, cnt_out_ref, cnt_ref):
    i = pl.program_id(0)
    x = _token_tile(i, xp_ref, xs_ref)
    mixed = jnp.concatenate([_token_tile(i, oa_ref, oas_ref), _token_tile(i, ob_ref, obs_ref)], -1)
    x1 = x + jnp.dot(mixed, w_ref[...], preferred_element_type=F32)
    x1_ref[...] = x1
    h2 = _rms(x1, g_ref[...])
    _route(h2, wr_ref, br_ref, tri_ref, cnt_ref, meta_ref, gate_ref, cnt_out_ref)


def _outproj_route(xp, xs_pad, oa, oa_s, ob, ob_s, w_out_bf16, g_ffn, wr, br, tri):
    row = pl.BlockSpec((TR, D_MODEL), lambda i: (i, 0))
    half = pl.BlockSpec((TR, A_WIDTH), lambda i: (jnp.minimum(i, SAMPLE_TILE - 1), 0))
    half_s = pl.BlockSpec((TR, A_WIDTH), lambda i: (0, 0))
    return pl.pallas_call(
        _outproj_kernel,
        grid=(N_TILES,),
        in_specs=[_PROMPT_ROWS, _SAMPLE_ROWS, half, half_s, half, half_s,
                  _const_spec(w_out_bf16), _const_spec(g_ffn),
                  _const_spec(wr), _const_spec(br), _const_spec(tri)],
        out_specs=[row] + _ROUTE_OUT_SPECS,
        out_shape=[jax.ShapeDtypeStruct((NT, D_MODEL), F32)] + _ROUTE_OUT_SHAPES,
        scratch_shapes=[pltpu.VMEM((1, LANES), F32)],
        compiler_params=_cparams(("arbitrary",)),
        name="outproj_route",
    )(xp, xs_pad, oa, oa_s, ob, ob_s, w_out_bf16, g_ffn, wr, br, tri)


def _plan(meta, counts):
    cnt = counts[0, :N_EXPERTS].astype(I32)
    nblk = (cnt + TM - 1) // TM
    blk_end = jnp.cumsum(nblk)
    pstart = (blk_end - nblk) * TM
    n_used = blk_end[-1]
    blk = jnp.minimum(jnp.arange(N_SLOTS_BLK, dtype=I32), n_used - 1)
    blk_exp = jnp.sum((blk[:, None] >= blk_end[None, :]).astype(I32), -1)
    blk_exp = jnp.minimum(blk_exp, N_EXPERTS - 1)
    experts = jnp.arange(N_EXPERTS, dtype=I32)
    start_of = jnp.sum(jnp.where(meta[:, 0:2, None] == experts, pstart, 0), -1)
    tok = jnp.arange(NT, dtype=I32)[:, None]
    dest = jnp.where(tok < NV, start_of + meta[:, 2:4], 0).astype(I32).reshape(-1)
    pad_lo = pstart + cnt
    pad_hi = pstart + nblk * TM
    return dest, blk_exp, n_used.reshape(1).astype(I32), pad_lo.astype(I32), pad_hi.astype(I32)


def _row(ref, r):
    return ref.at[pl.ds(pl.multiple_of(r * ROW_VREGS, ROW_VREGS), ROW_VREGS), :]


def _dispatch_kernel(dest_ref, plo_ref, phi_ref, nused_ref, x_ref, g_ref, xs_ref, stage, zero_ref, sems, zsem):
    i = pl.program_id(0)
    last = pl.num_programs(0) - 1
    slot = i % 2
    _store_rows_as_tiles(stage.at[slot], _rms(x_ref[...], g_ref[...]))

    def wait_tile(n_tok, s):
        for _ in range(2):
            pltpu.make_async_copy(stage.at[s, pl.ds(0, n_tok * ROW_VREGS), :],
                                  xs_ref.at[pl.ds(0, n_tok * ROW_VREGS), :], sems.at[s]).wait()

    def scatter(n_tok):
        def issue(t, carry):
            tok = i * TR + t
            src = stage.at[slot, pl.ds(pl.multiple_of(t * ROW_VREGS, ROW_VREGS), ROW_VREGS), :]
            for kk in range(2):
                pltpu.make_async_copy(src, _row(xs_ref, dest_ref[2 * tok + kk]), sems.at[slot]).start(priority=kk)
            return carry
        lax.fori_loop(0, n_tok, issue, 0)

    @pl.when(i < last)
    def _():
        scatter(TR)

    @pl.when(i > 0)
    def _():
        wait_tile(TR, 1 - slot)

    @pl.when(i == last)
    def _():
        n_last = NV - (N_TILES - 1) * TR
        scatter(n_last)
        wait_tile(n_last, slot)
        zero_ref[...] = jnp.zeros_like(zero_ref)

        zero_row = zero_ref.at[pl.ds(0, ROW_VREGS), :]

        def per_expert(e, carry):
            def fill(p, c2):
                pltpu.make_async_copy(zero_row, _row(xs_ref, p), zsem).start()
                return c2
            lax.fori_loop(plo_ref[e], phi_ref[e], fill, 0)

            def fill_wait(p, c2):
                pltpu.make_async_copy(zero_row, _row(xs_ref, 0), zsem).wait()
                return c2
            lax.fori_loop(plo_ref[e], phi_ref[e], fill_wait, 0)
            return carry
        lax.fori_loop(0, N_EXPERTS, per_expert, 0)

        def block_of(nb):
            return xs_ref.at[pl.ds(pl.multiple_of(nb * (TM * ROW_VREGS), TM * ROW_VREGS), TM * ROW_VREGS), :]

        def fill_blk(nb, carry):
            pltpu.make_async_copy(zero_ref, block_of(nb), zsem).start()
            return carry
        lax.fori_loop(nused_ref[0], N_SLOTS_BLK, fill_blk, 0)

        def fill_blk_wait(nb, carry):
            pltpu.make_async_copy(zero_ref, block_of(0), zsem).wait()
            return carry
        lax.fori_loop(nused_ref[0], N_SLOTS_BLK, fill_blk_wait, 0)


def _dispatch(x, g_ffn, dest, pad_lo, pad_hi, n_used):
    return pl.pallas_call(
        _dispatch_kernel,
        grid_spec=pltpu.PrefetchScalarGridSpec(
            num_scalar_prefetch=4, grid=(N_TILES,),
            in_specs=[pl.BlockSpec((TR, D_MODEL), lambda i, *_: (i, 0)),
                      pl.BlockSpec((1, D_MODEL), lambda i, *_: (0, 0))],
            out_specs=pl.BlockSpec(memory_space=pl.ANY),
            scratch_shapes=[pltpu.VMEM((2, TR * ROW_VREGS, LANES), F32),
                            pltpu.VMEM((TM * ROW_VREGS, LANES), F32),
                            pltpu.SemaphoreType.DMA((2,)), pltpu.SemaphoreType.DMA(())]),
        out_shape=jax.ShapeDtypeStruct((N_SLOTS * ROW_VREGS, LANES), F32),
        compiler_params=_cparams(("arbitrary",)),
        name="dispatch",
    )(dest, pad_lo, pad_hi, n_used, x, g_ffn)


class _ExpertRows:
    scratch = [pltpu.VMEM((2, 2 * TR * ROW_VREGS, LANES), F32), pltpu.SemaphoreType.DMA((2,))]

    def __init__(self, dest_ref, ys_ref, buf, sems):
        self.dest_ref, self.ys_ref, self.buf, self.sems = dest_ref, ys_ref, buf, sems

    def _copy(self, src_row, slot, dst_row):
        dst = self.buf.at[slot, pl.ds(pl.multiple_of(dst_row * ROW_VREGS, ROW_VREGS), ROW_VREGS), :]
        return pltpu.make_async_copy(_row(self.ys_ref, src_row), dst, self.sems.at[slot])

    def start(self, tile):
        slot = tile % 2

        def issue(t, carry):
            for kk in range(2):
                self._copy(self.dest_ref[2 * (tile * TR + t) + kk], slot, kk * TR + t).start(priority=kk)
            return carry
        lax.fori_loop(0, TR, issue, 0)

    def wait(self, tile):
        slot = tile % 2
        pltpu.make_async_copy(self.ys_ref.at[pl.ds(0, 2 * TR * ROW_VREGS), :], self.buf.at[slot],
                              self.sems.at[slot]).wait()

    def combine(self, tile, x, gate_ref):
        rows = self.buf.at[tile % 2]
        g = gate_ref[...]
        for kk in range(2):
            y = jnp.concatenate([rows[pl.ds(kk * TR * ROW_VREGS + j, TR, stride=ROW_VREGS), :]
                                 for j in range(ROW_VREGS)], -1)
            x = x + g[:, kk:kk + 1] * y
        return x

    def fetch_combine(self, x, gate_ref):
        i = pl.program_id(0)

        @pl.when(i == 0)
        def _():
            self.start(i)

        @pl.when(i + 1 < pl.num_programs(0))
        def _():
            self.start(i + 1)

        self.wait(i)
        return self.combine(i, x, gate_ref)


def _ffn_kernel(be_ref, nused_ref, xs_ref, w1_ref, w3_ref, w2_ref, y_ref, w1b, w3b, w2b):
    nb = pl.program_id(0)

    @pl.when(nb < nused_ref[0])
    def _():
        prev = be_ref[jnp.maximum(nb - 1, 0)]
        fresh = jnp.logical_or(nb == 0, be_ref[nb] != prev)

        @pl.when(fresh)
        def _():
            w1b[...] = w1_ref[...].astype(BF16)
            w3b[...] = w3_ref[...].astype(BF16)
            w2b[...] = w2_ref[...].astype(BF16)

        x = _load_rows_from_tiles(xs_ref, TM).astype(BF16)
        a = jnp.dot(x, w1b[...], preferred_element_type=F32)
        b = jnp.dot(x, w3b[...], preferred_element_type=F32)
        mid = (_silu(a) * b).astype(BF16)
        y = jnp.dot(mid, w2b[...], preferred_element_type=F32)
        _store_rows_as_tiles(y_ref, y)

    @pl.when(nb >= nused_ref[0])
    def _():
        y_ref[...] = jnp.zeros_like(y_ref)


def _experts(xs, blk_exp, n_used, layer, w1, w3, w2):
    rows = pl.BlockSpec((TM * ROW_VREGS, LANES), lambda nb, be, nu: (nb, 0))

    def wspec(a, b):
        return pl.BlockSpec((None, None, a, b), lambda nb, be, nu: (layer, be[nb], 0, 0))

    return pl.pallas_call(
        _ffn_kernel,
        grid_spec=pltpu.PrefetchScalarGridSpec(
            num_scalar_prefetch=2, grid=(N_SLOTS_BLK,),
            in_specs=[rows, wspec(D_MODEL, D_EXPERT), wspec(D_MODEL, D_EXPERT), wspec(D_EXPERT, D_MODEL)],
            out_specs=rows,
            scratch_shapes=[pltpu.VMEM((D_MODEL, D_EXPERT), BF16), pltpu.VMEM((D_MODEL, D_EXPERT), BF16),
                            pltpu.VMEM((D_EXPERT, D_MODEL), BF16)]),
        out_shape=jax.ShapeDtypeStruct((N_SLOTS * ROW_VREGS, LANES), F32),
        compiler_params=_cparams(("arbitrary",)),
        name="experts",
    )(blk_exp, n_used, xs, w1, w3, w2)


def _moe(x, g_ffn, meta, counts, layer, w1, w3, w2):
    dest, blk_exp, n_used, pad_lo, pad_hi = _plan(meta, counts)
    xs = _dispatch(x, g_ffn, dest, pad_lo, pad_hi, n_used)
    return _experts(xs, blk_exp, n_used, layer, w1, w3, w2), dest


def _pool_kernel(dest_ref, x1_ref, ys_ref, gate0_ref, gmix_ref, sp_ref, pw_ref, ps_ref, gffn_ref,
                 wr_ref, br_ref, tri_ref,
                 x3_ref, hkeep_ref, hs_ref, meta_ref, gate_ref, cnt_out_ref,
                 cnt_ref, ext_ref, e1_ref, e2_ref, e3_ref, e4_ref, mixed_ref, ybuf, ysems):
    i = pl.program_id(0)
    x2 = _ExpertRows(dest_ref, ys_ref, ybuf, ysems).fetch_combine(x1_ref[...], gate0_ref)
    h = _rms(x2, gmix_ref[...])
    H = POOL_HALO
    lvl_refs = (e1_ref, e2_ref, e3_ref, e4_ref)

    @pl.when(i < SAMPLE_TILE)
    def _():
        hkeep_ref[...] = h[TR - 16:TR]
        @pl.when(i % TILES_PER_SEQ == 0)
        def _():
            ext_ref[0:H, :] = jnp.zeros((H, D_MODEL), F32)
        ext_ref[H:H + TR, :] = h
        src = ext_ref
        for lv in range(4):
            sh = 1 << lv
            lo = 8 * (lv + 1)
            c0 = POOL_GROUP * lv
            dst = lvl_refs[lv]
            dst[lo:H + TR, c0:] = src[lo:H + TR, c0:] + src[lo - sh:H + TR - sh, c0:]
            src = dst
        pos = (i % TILES_PER_SEQ) * TR + lax.broadcasted_iota(I32, (TR, 1), 0)
        for gi, w in enumerate(POOL_WINDOWS):
            cs = slice(gi * POOL_GROUP, (gi + 1) * POOL_GROUP)
            inv = 1.0 / jnp.minimum(w, pos + 1).astype(F32)
            mixed_ref[:, cs] = lvl_refs[gi][H:H + TR, cs] * inv - h[:, cs]
        ext_ref[0:H, :] = h[TR - H:TR]

    @pl.when(i == SAMPLE_TILE)
    def _():
        hs = h[0:DEC_BATCH]
        hs_ref[...] = hs
        mixed_ref[...] = jnp.zeros_like(mixed_ref)
        for gi, w in enumerate(POOL_WINDOWS):
            cs = slice(gi * POOL_GROUP, (gi + 1) * POOL_GROUP)
            win = hs[:, cs]
            for dlt in range(1, w):
                win = win + sp_ref[POOL_KEEP - dlt][:, cs]
            mixed_ref[0:DEC_BATCH, cs] = win * (1.0 / w) - hs[:, cs]

    mixed = mixed_ref[...].astype(BF16)
    outs = [jnp.dot(mixed[:, gi * POOL_GROUP:(gi + 1) * POOL_GROUP], pw_ref[gi], preferred_element_type=F32)
            for gi in range(len(POOL_WINDOWS))]
    x3 = x2 + jnp.concatenate(outs, -1) * ps_ref[...]
    x3_ref[...] = x3
    h2 = _rms(x3, gffn_ref[...])
    _route(h2, wr_ref, br_ref, tri_ref, cnt_ref, meta_ref, gate_ref, cnt_out_ref)


def _const_spec_p(a):
    return pl.BlockSpec(a.shape, lambda i, *_, n=a.ndim: (0,) * n)


def _pool_route(dest, x1, ys, gates0, g_mix, sp_t, pw_bf16, p_scale, g_ffn, wr, br, tri):
    def rows(shape):
        return pl.BlockSpec(shape, lambda i, *_: (i, 0))

    ext = pltpu.VMEM((POOL_HALO + TR, D_MODEL), F32)
    consts = [g_mix, sp_t, pw_bf16, p_scale, g_ffn, wr, br, tri]
    return pl.pallas_call(
        _pool_kernel,
        grid_spec=pltpu.PrefetchScalarGridSpec(
            num_scalar_prefetch=1, grid=(N_TILES,),
            in_specs=[rows((TR, D_MODEL)), pl.BlockSpec(memory_space=pl.ANY), rows((TR, 8))]
                     + [_const_spec_p(a) for a in consts],
            out_specs=[rows((TR, D_MODEL)),
                       pl.BlockSpec((None, 16, D_MODEL),
                                    lambda i, *_: (jnp.minimum(i // TILES_PER_SEQ, BATCH - 1), 0, 0)),
                       pl.BlockSpec((DEC_BATCH, D_MODEL), lambda i, *_: (0, 0)),
                       rows((TR, 8)), rows((TR, 8)), pl.BlockSpec((1, LANES), lambda i, *_: (0, 0))],
            scratch_shapes=[pltpu.VMEM((1, LANES), F32), ext, ext, ext, ext, ext,
                            pltpu.VMEM((TR, D_MODEL), F32)] + _ExpertRows.scratch),
        out_shape=[jax.ShapeDtypeStruct((NT, D_MODEL), F32),
                   jax.ShapeDtypeStruct((BATCH, 16, D_MODEL), F32),
                   jax.ShapeDtypeStruct((DEC_BATCH, D_MODEL), F32)] + _ROUTE_OUT_SHAPES,
        compiler_params=_cparams(("arbitrary",)),
        name="pool_route",
    )(dest, x1, ys, gates0, *consts)


def _final_kernel(dest_ref, x_ref, ys_ref, gate_ref, g_ref, yp_ref, ysm_ref, ybuf, ysems):
    i = pl.program_id(0)
    y = _rms(_ExpertRows(dest_ref, ys_ref, ybuf, ysems).fetch_combine(x_ref[...], gate_ref), g_ref[...])

    @pl.when(i < SAMPLE_TILE)
    def _():
        yp_ref[...] = y

    @pl.when(i == SAMPLE_TILE)
    def _():
        ysm_ref[...] = y[0:DEC_BATCH]


def _final(dest, x3, ys, gates, g_final):
    return pl.pallas_call(
        _final_kernel,
        grid_spec=pltpu.PrefetchScalarGridSpec(
            num_scalar_prefetch=1, grid=(N_TILES,),
            in_specs=[pl.BlockSpec((TR, D_MODEL), lambda i, *_: (i, 0)), pl.BlockSpec(memory_space=pl.ANY),
                      pl.BlockSpec((TR, 8), lambda i, *_: (i, 0)), _const_spec_p(g_final)],
            out_specs=[pl.BlockSpec((TR, D_MODEL), lambda i, *_: (jnp.minimum(i, SAMPLE_TILE - 1), 0)),
                       pl.BlockSpec((DEC_BATCH, D_MODEL), lambda i, *_: (0, 0))],
            scratch_shapes=_ExpertRows.scratch),
        out_shape=[jax.ShapeDtypeStruct((NP, D_MODEL), F32), jax.ShapeDtypeStruct((DEC_BATCH, D_MODEL), F32)],
        compiler_params=_cparams(("arbitrary",)),
        name="final_norm",
    )(dest, x3, ys, gates, g_final)


def kernel(x_prompt, x_sample, cache_win_k, cache_win_v, state_hgrn, state_pool, rel_bias, norm_mix, norm_ffn,
           norm_final, w_in, w_out, hgrn_lb, hgrn_gnorm, pool_w, pool_scale, moe_wg_group, moe_bg_group,
           moe_wg_exp, moe_bg_exp, moe_w1, moe_w3, moe_w2):
    xp = x_prompt.reshape(NP, D_MODEL)
    xs_pad = jnp.pad(x_sample.reshape(DEC_BATCH, D_MODEL), ((0, TR - DEC_BATCH), (0, 0)))
    lb = jnp.cumsum(jax.nn.softmax(hgrn_lb.astype(F32), axis=0), axis=0)[0:1]
    gnorm = hgrn_gnorm[0:1].astype(F32)

    q, k, v, hq, hf, hi, hg, k_win, v_win = _inproj(xp, xs_pad, norm_mix[0:1], w_in[0].astype(BF16))
    oa = _attention_prompt(q, k, v, _attn_bias_tables(rel_bias))
    ob, s_prompt = _hgrn_prompt(hq, hf, hi, hg, lb, gnorm)
    oa_s, ob_s, s_sample = _sample_mixers(
        q, k, v, hq, hf, hi, hg,
        cache_win_k[0].reshape(DEC_BATCH, W_MAX * N_HEADS, D_HEAD),
        cache_win_v[0].reshape(DEC_BATCH, W_MAX * N_HEADS, D_HEAD),
        state_hgrn[0], _sample_bias_tables(rel_bias), lb, gnorm)
    wr, br, tri = _router_operands(moe_wg_group[0], moe_bg_group[0], moe_wg_exp[0], moe_bg_exp[0])
    x1, meta, gates0, counts = _outproj_route(xp, xs_pad, oa, oa_s, ob, ob_s, w_out[0].astype(BF16),
                                              norm_ffn[0:1], wr, br, tri)
    ys, dest = _moe(x1, norm_ffn[0:1], meta, counts, 0, moe_w1, moe_w3, moe_w2)

    wr, br, tri = _router_operands(moe_wg_group[1], moe_bg_group[1], moe_wg_exp[1], moe_bg_exp[1])
    sp_t = jnp.transpose(state_pool[0], (1, 0, 2))
    x3, hkeep, hsample, meta, gates1, counts = _pool_route(
        dest, x1, ys, gates0, norm_mix[1:2], sp_t, pool_w[0].astype(BF16), pool_scale[0:1], norm_ffn[1:2],
        wr, br, tri)
    ys, dest = _moe(x3, norm_ffn[1:2], meta, counts, 1, moe_w1, moe_w3, moe_w2)
    y_prompt, y_sample = _final(dest, x3, ys, gates1, norm_final.reshape(1, D_MODEL))

    k_s = k[NP:NV].reshape(1, DEC_BATCH, 1, N_HEADS, D_HEAD)
    v_s = v[NP:NV].reshape(1, DEC_BATCH, 1, N_HEADS, D_HEAD)
    return (y_prompt.reshape(BATCH, SEQ, D_MODEL), y_sample.reshape(DEC_BATCH, 1, D_MODEL),
            k_win.reshape(1, BATCH, W_MAX, N_HEADS, D_HEAD), v_win.reshape(1, BATCH, W_MAX, N_HEADS, D_HEAD),
            s_prompt[None], hkeep[:, 16 - POOL_KEEP:][None],
            k_s, v_s, s_sample[None], hsample.reshape(1, DEC_BATCH, 1, D_MODEL))
```

```python
import functools

import numpy as np
import jax
import jax.numpy as jnp
from jax import lax
from jax.experimental import pallas as pl
from jax.experimental.pallas import tpu as pltpu

F32 = jnp.float32
BF16 = jnp.bfloat16
I32 = jnp.int32

D_MODEL = 1024
BATCH = 4
SEQ = 4096
DEC_BATCH = 32
PAST_LEN = 8192
W_MAX = 2048
N_HEADS = 4
D_HEAD = 128
A_WIDTH = N_HEADS * D_HEAD
N_PROJ = 7 * A_WIDTH
DILATED = ((128, 1), (512, 4), (2048, 16))
NUM_BUCKETS = 32
MAX_DISTANCE = 2048
POOL_WINDOWS = (2, 4, 8, 16)
POOL_GROUP = 256
POOL_KEEP = 15
N_GROUPS = 4
EXPERTS_PER_GROUP = 8
N_EXPERTS = 32
D_EXPERT = 512
EPS = 1e-6
NEG = -1e30

LANES = 128
SUBLANES = 8
ROW_VREGS = D_MODEL // LANES
TR = 256
NP = BATCH * SEQ
NV = NP + DEC_BATCH
NT = NP + TR
N_TILES = NT // TR
SAMPLE_TILE = NP // TR
TILES_PER_SEQ = SEQ // TR
TM = 256
N_SLOTS_BLK = (2 * NV + N_EXPERTS * (TM - 1) + TM - 1) // TM
N_SLOTS = N_SLOTS_BLK * TM
QB = 128
ATTN_UNROLL = 8
ROW_DMA_UNROLL = 8
CH = 128
HGRN_HEADS_PER_STEP = 2
N_LEVELS = 7
POOL_HALO = 32
VMEM_LIMIT = 56 * 1024 * 1024


def _cparams(sem=None, vmem=VMEM_LIMIT):
    kw = dict(vmem_limit_bytes=vmem)
    if sem is not None:
        kw["dimension_semantics"] = sem
    return pltpu.CompilerParams(**kw)


def _rms(x, g):
    return x * lax.rsqrt(jnp.mean(x * x, -1, keepdims=True) + EPS) * g


def _sigmoid(x):
    return 1.0 / (1.0 + jnp.exp(-x))


def _silu(x):
    return x * _sigmoid(x)


def _token_tile(i, xp_ref, xs_ref):
    return jnp.where(i == SAMPLE_TILE, xs_ref[...], xp_ref[...])


_PROMPT_ROWS = pl.BlockSpec((TR, D_MODEL), lambda i: (jnp.minimum(i, SAMPLE_TILE - 1), 0))
_SAMPLE_ROWS = pl.BlockSpec((TR, D_MODEL), lambda i: (0, 0))
WIN_TILES = W_MAX // TR


def _window_block(i):
    seq = jnp.minimum(i // TILES_PER_SEQ, BATCH - 1)
    j = jnp.clip(i % TILES_PER_SEQ - (TILES_PER_SEQ - WIN_TILES), 0, WIN_TILES - 1)
    return jnp.where(i >= SAMPLE_TILE, BATCH * WIN_TILES - 1, seq * WIN_TILES + j)


def _inproj_kernel(xp_ref, xs_ref, g_ref, w_ref, *out_refs):
    i = pl.program_id(0)
    h = _rms(_token_tile(i, xp_ref, xs_ref), g_ref[...])
    p = jnp.dot(h.astype(BF16), w_ref[...], preferred_element_type=F32)
    for n, o_ref in enumerate(out_refs[:7]):
        o_ref[...] = p[:, n * A_WIDTH:(n + 1) * A_WIDTH]

    @pl.when(jnp.logical_and(i < SAMPLE_TILE, i % TILES_PER_SEQ >= TILES_PER_SEQ - WIN_TILES))
    def _():
        for n, o_ref in ((1, out_refs[7]), (2, out_refs[8])):
            for h_i in range(N_HEADS):
                c0 = n * A_WIDTH + h_i * D_HEAD
                o_ref[pl.ds(h_i, TR, stride=N_HEADS), :] = p[:, c0:c0 + D_HEAD]


def _inproj(xp, xs_pad, g, w_bf16):
    out_sds = ([jax.ShapeDtypeStruct((NT, A_WIDTH), F32)] * 7
               + [jax.ShapeDtypeStruct((BATCH * W_MAX * N_HEADS, D_HEAD), F32)] * 2)
    win = pl.BlockSpec((TR * N_HEADS, D_HEAD), lambda i: (_window_block(i), 0))
    return pl.pallas_call(
        _inproj_kernel,
        grid=(N_TILES,),
        in_specs=[_PROMPT_ROWS, _SAMPLE_ROWS,
                  pl.BlockSpec((1, D_MODEL), lambda i: (0, 0)),
                  pl.BlockSpec((D_MODEL, N_PROJ), lambda i: (0, 0))],
        out_specs=[pl.BlockSpec((TR, A_WIDTH), lambda i: (i, 0))] * 7 + [win, win],
        out_shape=out_sds,
        compiler_params=_cparams(("arbitrary",)),
        name="inproj",
    )(xp, xs_pad, g, w_bf16)


def _t5_bucket(dist):
    max_exact = NUM_BUCKETS // 2
    d = np.asarray(dist)
    large = max_exact + np.floor(np.log(np.maximum(d, 1) / max_exact)
                                 / np.log(MAX_DISTANCE / max_exact) * (NUM_BUCKETS - max_exact)).astype(np.int32)
    large = np.minimum(large, NUM_BUCKETS - 1)
    return np.where(d < max_exact, d, large).astype(np.int32)


def _attn_bias_tables(rel_bias):
    period = 3 * QB
    m = np.arange(period)
    u = np.where(m < 2 * QB, m, m - period)
    pick = np.zeros((len(DILATED), 2, period, NUM_BUCKETS), np.float32)
    mask = np.zeros((len(DILATED), 2, period, 1), np.float32)
    for bi, (w, d) in enumerate(DILATED):
        nk = w // d
        for vi, off in enumerate((0, QB)):
            j = off - u
            ok = (j >= 0) & (j <= nk)
            pick[bi, vi, m[ok], _t5_bucket(d * j[ok])] = 1.0
            mask[bi, vi, ~ok, 0] = NEG
    vec = jnp.einsum("bvmk,kh->hbvm", pick, rel_bias.astype(F32), precision=lax.Precision.HIGHEST)
    vec = vec + jnp.transpose(mask, (3, 0, 1, 2))
    tiled = jnp.tile(vec, (1, 1, 1, QB))[..., :QB * (period - 1)]
    return tiled.reshape(N_HEADS, len(DILATED), 2, QB, period - 1)[..., :2 * QB]


def _attn_kernel(q_ref, k_ref, v_ref, bias_ref, o_ref,
                 qd, kd, vd, ud, md, sd, u_acc, m_acc, s_acc):
    scale = D_HEAD ** -0.5

    def block_stats(bi, t, nb):
        has_prev = jnp.minimum(t % nb, 1)
        q0 = pl.multiple_of(t * QB, QB)
        k0 = pl.multiple_of((t - has_prev) * QB, QB)
        qb = qd[pl.ds(q0, QB), :]
        kb = kd[pl.ds(k0, 2 * QB), :]
        vb = vd[pl.ds(k0, 2 * QB), :]
        s = lax.dot_general(qb, kb, (((1,), (1,)), ((), ())), preferred_element_type=F32)
        s = s + bias_ref[bi, has_prev]
        mb = jnp.max(s, -1, keepdims=True)
        p = jnp.exp(s - mb)
        sb = jnp.sum(p, -1, keepdims=True)
        u = jnp.dot(p.astype(BF16), vb, preferred_element_type=F32)
        return q0, mb, sb, u

    for bi in (2, 1, 0):
        d = DILATED[bi][1]
        cl = SEQ // d
        nb = cl // QB
        for r in range(d):
            src = pl.ds(r, cl, stride=d) if d > 1 else pl.ds(0, cl)
            dst = pl.ds(r * cl, cl)
            qd[dst, :] = (q_ref[src, :] * scale).astype(BF16)
            kd[dst, :] = k_ref[src, :].astype(BF16)
            vd[dst, :] = v_ref[src, :].astype(BF16)

        if d > 1:
            def body(g, carry, bi=bi, nb=nb):
                for un in range(ATTN_UNROLL):
                    q0, mb, sb, u = block_stats(bi, g * ATTN_UNROLL + un, nb)
                    ud[pl.ds(q0, QB), :] = u
                    md[pl.ds(q0, QB), :] = jnp.broadcast_to(mb, (QB, D_HEAD))
                    sd[pl.ds(q0, QB), :] = jnp.broadcast_to(sb, (QB, D_HEAD))
                return carry
            lax.fori_loop(0, SEQ // QB // ATTN_UNROLL, body, 0)
            for r in range(d):
                pos = pl.ds(r, cl, stride=d)
                cls = pl.ds(r * cl, cl)
                if bi == 2:
                    u_acc[pos, :] = ud[cls, :]
                    m_acc[pos, :] = md[cls, :]
                    s_acc[pos, :] = sd[cls, :]
                else:
                    m_old = m_acc[pos, :]
                    m_blk = md[cls, :]
                    m_new = jnp.maximum(m_old, m_blk)
                    a = jnp.exp(m_old - m_new)
                    b = jnp.exp(m_blk - m_new)
                    u_acc[pos, :] = a * u_acc[pos, :] + b * ud[cls, :]
                    s_acc[pos, :] = a * s_acc[pos, :] + b * sd[cls, :]
                    m_acc[pos, :] = m_new
        else:
            def body(g, carry, bi=bi, nb=nb):
                for un in range(ATTN_UNROLL):
                    q0, mb, sb, u = block_stats(bi, g * ATTN_UNROLL + un, nb)
                    rows = pl.ds(q0, QB)
                    m_old = m_acc[rows, :]
                    m_new = jnp.maximum(m_old, mb)
                    a = jnp.exp(m_old - m_new)
                    b = jnp.exp(mb - m_new)
                    num = a * u_acc[rows, :] + b * u
                    den = a * s_acc[rows, :] + b * sb
                    o_ref[rows, :] = (num / den).astype(o_ref.dtype)
                return carry
            lax.fori_loop(0, SEQ // QB // ATTN_UNROLL, body, 0)


def _attention_prompt(q, k, v, bias_tabs):
    blk = pl.BlockSpec((SEQ, D_HEAD), lambda b, h: (b, h))
    return pl.pallas_call(
        _attn_kernel,
        grid=(BATCH, N_HEADS),
        in_specs=[blk, blk, blk,
                  pl.BlockSpec((None, 3, 2, QB, 2 * QB), lambda b, h: (h, 0, 0, 0, 0))],
        out_specs=blk,
        out_shape=jax.ShapeDtypeStruct((NP, A_WIDTH), BF16),
        scratch_shapes=[pltpu.VMEM((SEQ, D_HEAD), BF16)] * 3
                       + [pltpu.VMEM((SEQ, D_HEAD), F32)] * 6,
        compiler_params=_cparams(("arbitrary", "arbitrary")),
        name="attn_prompt",
    )(q, k, v, bias_tabs)


def _hgrn_tables():
    t = np.arange(CH)
    u = np.arange(CH)
    sums_q = np.zeros((1 + N_LEVELS, CH, CH), np.float32)
    sums_k = np.zeros((2 + N_LEVELS, CH, CH), np.float32)
    sums_q[0] = (u[None, :] <= t[:, None])
    sums_k[0] = (u[None, :] > t[:, None])
    sums_k[1 + N_LEVELS] = 1.0
    pair = np.zeros((N_LEVELS, CH, CH), np.float32)
    for l in range(N_LEVELS):
        h = CH >> (l + 1)
        is_q = (t // h) % 2 == 1
        half_start = (t // h) * h
        half_end = half_start + h
        sel_q = (u[None, :] >= half_start[:, None]) & (u[None, :] <= t[:, None])
        sel_k = (u[None, :] > t[:, None]) & (u[None, :] < half_end[:, None])
        sums_q[1 + l] = sel_q & is_q[:, None]
        sums_k[1 + l] = sel_k & ~is_q[:, None]
        same = (t[:, None] // (2 * h)) == (t[None, :] // (2 * h))
        pair[l] = same & is_q[:, None] & (~is_q)[None, :]
    sums_kt = np.transpose(sums_k, (2, 0, 1)).reshape(CH, (2 + N_LEVELS) * CH)
    sums_q = sums_q.reshape((1 + N_LEVELS) * CH, CH)
    return (jnp.asarray(np.concatenate([sums_q, sums_q], 1), BF16),
            jnp.asarray(np.concatenate([sums_kt, sums_kt], 0), BF16),
            jnp.asarray(pair), jnp.asarray(np.eye(CH, dtype=np.float32)))


def _split_bf16(x):
    hi = x.astype(BF16)
    return hi, (x - hi.astype(F32)).astype(BF16)


def _hgrn_kernel(q_ref, f_ref, i_ref, g_ref, lb_ref, gn_ref, sq_ref, skt_ref, pair_ref, eye_ref, o_ref, s_ref):
    gn = gn_ref[...]

    def chunk(c, st, hh):
        rows = pl.ds(pl.multiple_of(c * CH, CH), CH)
        cols = slice(hh * D_HEAD, (hh + 1) * D_HEAD)
        lb = lb_ref[:, cols]
        q = _silu(q_ref[rows, cols])
        f = lb + (1.0 - lb) * _sigmoid(f_ref[rows, cols])
        lf = jnp.log(f)
        k = 1.0 - f
        v16 = i_ref[rows, cols].astype(BF16)
        kt = k.T
        exq = jnp.exp(jnp.dot(sq_ref[...], jnp.concatenate(_split_bf16(lf), 0),
                              preferred_element_type=F32))
        exk = jnp.exp(jnp.dot(jnp.concatenate(_split_bf16(lf.T), 1), skt_ref[...],
                              preferred_element_type=F32))
        inter = jnp.dot((q * exq[0:CH]).astype(BF16), st.astype(BF16), preferred_element_type=F32)
        sc = eye_ref[...] * jnp.sum(q * k, -1, keepdims=True)
        for l in range(N_LEVELS):
            ql = (q * exq[(1 + l) * CH:(2 + l) * CH]).astype(BF16)
            klt = (kt * exk[:, (1 + l) * CH:(2 + l) * CH]).astype(BF16)
            sc = sc + pair_ref[l] * jnp.dot(ql, klt, preferred_element_type=F32)
        o = inter + jnp.dot(sc.astype(BF16), v16, preferred_element_type=F32)
        st_new = (st * exk[:, (1 + N_LEVELS) * CH:]
                  + jnp.dot((kt * exk[:, 0:CH]).astype(BF16), v16, preferred_element_type=F32))
        o = _rms(o, gn) * _silu(g_ref[rows, cols])
        o_ref[rows, cols] = o.astype(o_ref.dtype)
        return st_new

    def step(c, states):
        return tuple(chunk(c, st, hh) for hh, st in enumerate(states))

    zero = jnp.zeros((D_HEAD, D_HEAD), F32)
    states = lax.fori_loop(0, SEQ // CH, step, (zero,) * HGRN_HEADS_PER_STEP)
    for hh, st in enumerate(states):
        s_ref[hh] = st


def _hgrn_prompt(hq, hf, hi, hg, lb, gnorm):
    tables = _hgrn_tables()
    width = HGRN_HEADS_PER_STEP * D_HEAD
    blk = pl.BlockSpec((SEQ, width), lambda b, h: (b, h))

    def full(a):
        return pl.BlockSpec(a.shape, lambda b, h, n=a.ndim: (0,) * n)

    return pl.pallas_call(
        _hgrn_kernel,
        grid=(BATCH, N_HEADS // HGRN_HEADS_PER_STEP),
        in_specs=[blk, blk, blk, blk,
                  pl.BlockSpec((1, width), lambda b, h: (0, h)),
                  pl.BlockSpec((1, D_HEAD), lambda b, h: (0, 0))] + [full(a) for a in tables],
        out_specs=[blk, pl.BlockSpec((None, HGRN_HEADS_PER_STEP, D_HEAD, D_HEAD), lambda b, h: (b, h, 0, 0))],
        out_shape=[jax.ShapeDtypeStruct((NP, A_WIDTH), BF16),
                   jax.ShapeDtypeStruct((BATCH, N_HEADS, D_HEAD, D_HEAD), F32)],
        compiler_params=_cparams(("arbitrary", "arbitrary")),
        name="hgrn_prompt",
    )(hq, hf, hi, hg, lb, gnorm, *tables)


def _sample_bias_tables(rel_bias):
    j = np.concatenate([QB - np.arange(QB), [0]])
    pick = np.zeros((len(DILATED), QB + 1, NUM_BUCKETS), np.float32)
    for bi, (w, d) in enumerate(DILATED):
        pick[bi, np.arange(QB + 1), _t5_bucket(d * j)] = 1.0
    return jnp.einsum("bjk,kh->bhj", pick, rel_bias.astype(F32), precision=lax.Precision.HIGHEST)[..., None]


def _bf16_round(x):
    return x.astype(BF16).astype(F32)


def _col(row, eye):
    return jnp.sum(eye * row, -1, keepdims=True)


def _sample_kernel(q_ref, k_ref, v_ref, hq_ref, hf_ref, hi_ref, hg_ref, kc_ref, vc_ref, s0_ref,
                   bias_ref, lb_ref, gn_ref, eye_ref, oa_ref, ob_ref, s_ref, oa_acc, ob_acc):
    b = pl.program_id(0)
    row = pl.ds(b, 1)
    scale = D_HEAD ** -0.5
    q = _bf16_round(q_ref[row, :] * scale)
    kn = _bf16_round(k_ref[row, :])
    vn = _bf16_round(v_ref[row, :])
    eye = eye_ref[...]

    stats = []
    for bi, (w, d) in enumerate(DILATED):
        per_head = []
        for h in range(N_HEADS):
            hs = slice(h * D_HEAD, (h + 1) * D_HEAD)
            rows = pl.ds((W_MAX - QB * d) * N_HEADS + h, QB, stride=d * N_HEADS)
            kb = _bf16_round(kc_ref[rows, :])
            vb = _bf16_round(vc_ref[rows, :])
            s = jnp.sum(kb * q[:, hs], -1, keepdims=True) + bias_ref[bi, h, 0:QB]
            s0 = jnp.sum(q[:, hs] * kn[:, hs], -1, keepdims=True) + bias_ref[bi, h, QB:QB + 1]
            m = jnp.maximum(jnp.max(s, 0, keepdims=True), s0)
            p = jnp.exp(s - m)
            p0 = jnp.exp(s0 - m)
            ssum = jnp.sum(p, 0, keepdims=True) + p0
            u = jnp.sum(_bf16_round(p) * vb, 0, keepdims=True) + _bf16_round(p0) * vn[:, hs]
            per_head.append((m, ssum, u))
        stats.append(per_head)
    outs = []
    for h in range(N_HEADS):
        m_all = functools.reduce(jnp.maximum, [stats[bi][h][0] for bi in range(3)])
        num = 0.0
        den = 0.0
        for bi in range(3):
            m, ssum, u = stats[bi][h]
            c = jnp.exp(m - m_all)
            num = num + c * u
            den = den + c * ssum
        outs.append(num / den)
    oa_acc[row, :] = jnp.concatenate(outs, -1)

    qh = _silu(hq_ref[row, :])
    lb = lb_ref[...]
    f = lb + (1.0 - lb) * _sigmoid(hf_ref[row, :])
    vi = hi_ref[row, :]
    gate = _silu(hg_ref[row, :])
    gn = gn_ref[...]
    obs = []
    for h in range(N_HEADS):
        hs = slice(h * D_HEAD, (h + 1) * D_HEAD)
        f_col = _col(f[:, hs], eye)
        q_col = _col(qh[:, hs], eye)
        s_old = s0_ref[h]
        s_ref[h] = f_col * s_old + (1.0 - f_col) * vi[:, hs]
        inter = jnp.sum(_bf16_round(q_col * f_col) * _bf16_round(s_old), 0, keepdims=True)
        qk = jnp.sum(qh[:, hs] * (1.0 - f[:, hs]), -1, keepdims=True)
        o = inter + qk * vi[:, hs]
        obs.append(_rms(o, gn) * gate[:, hs])
    ob_acc[row, :] = jnp.concatenate(obs, -1)

    @pl.when(b == DEC_BATCH - 1)
    def _():
        pad = jnp.zeros((TR - DEC_BATCH, A_WIDTH), F32)
        oa_ref[...] = jnp.concatenate([oa_acc[...], pad], 0).astype(oa_ref.dtype)
        ob_ref[...] = jnp.concatenate([ob_acc[...], pad], 0).astype(ob_ref.dtype)


def _sample_mixers(q, k, v, hq, hf, hi, hg, cache_k, cache_v, state, bias_s, lb, gnorm):
    tile = pl.BlockSpec((TR, A_WIDTH), lambda b: (SAMPLE_TILE, 0))
    out_tile = pl.BlockSpec((TR, A_WIDTH), lambda b: (0, 0))
    cache = pl.BlockSpec((None, W_MAX * N_HEADS, D_HEAD), lambda b: (b, 0, 0))
    eye = jnp.eye(D_HEAD, dtype=F32)

    return pl.pallas_call(
        _sample_kernel,
        grid=(DEC_BATCH,),
        in_specs=[tile, tile, tile, tile, tile, tile, tile, cache, cache,
                  pl.BlockSpec((None, N_HEADS, D_HEAD, D_HEAD), lambda b: (b, 0, 0, 0)),
                  pl.BlockSpec(bias_s.shape, lambda b: (0, 0, 0, 0)),
                  pl.BlockSpec((1, A_WIDTH), lambda b: (0, 0)),
                  pl.BlockSpec((1, D_HEAD), lambda b: (0, 0)),
                  pl.BlockSpec((D_HEAD, D_HEAD), lambda b: (0, 0))],
        out_specs=[out_tile, out_tile,
                   pl.BlockSpec((None, N_HEADS, D_HEAD, D_HEAD), lambda b: (b, 0, 0, 0))],
        out_shape=[jax.ShapeDtypeStruct((TR, A_WIDTH), BF16), jax.ShapeDtypeStruct((TR, A_WIDTH), BF16),
                   jax.ShapeDtypeStruct((DEC_BATCH, N_HEADS, D_HEAD, D_HEAD), F32)],
        scratch_shapes=[pltpu.VMEM((DEC_BATCH, A_WIDTH), F32)] * 2,
        compiler_params=_cparams(("arbitrary",)),
        name="sample_mixers",
    )(q, k, v, hq, hf, hi, hg, cache_k, cache_v, state, bias_s, lb, gnorm, eye)


def _store_rows_as_tiles(ref, val):
    n = val.shape[0]
    for j in range(ROW_VREGS):
        ref[pl.ds(j, n, stride=ROW_VREGS), :] = val[:, j * LANES:(j + 1) * LANES]


def _load_rows_from_tiles(ref, n):
    return jnp.concatenate([ref[pl.ds(j, n, stride=ROW_VREGS), :] for j in range(ROW_VREGS)], -1)


def _route(h2, wr_ref, br_ref, tri_ref, cnt_ref, meta_ref, gate_ref, cnt_out_ref):
    i = pl.program_id(0)

    @pl.when(i == 0)
    def _():
        cnt_ref[...] = jnp.zeros_like(cnt_ref)

    logits = jnp.dot(h2.astype(BF16), wr_ref[...], preferred_element_type=F32) + br_ref[...]
    lane = lax.broadcasted_iota(I32, (TR, LANES), 1).astype(F32)
    big = float(1 << 20)
    is_g = lane < N_GROUPS
    gl = jnp.where(is_g, logits, NEG)
    gmax = jnp.max(gl, -1, keepdims=True)
    gsel = jnp.min(jnp.where(gl == gmax, lane, big), -1, keepdims=True)
    pg = 1.0 / jnp.sum(jnp.where(is_g, jnp.exp(gl - gmax), 0.0), -1, keepdims=True)
    lo = N_GROUPS + EXPERTS_PER_GROUP * gsel
    in_grp = jnp.logical_and(lane >= lo, lane < lo + EXPERTS_PER_GROUP)
    el = jnp.where(in_grp, logits, NEG)
    m1 = jnp.max(el, -1, keepdims=True)
    i1 = jnp.min(jnp.where(el == m1, lane, big), -1, keepdims=True)
    el2 = jnp.where(lane == i1, NEG, el)
    m2 = jnp.max(el2, -1, keepdims=True)
    i2 = jnp.min(jnp.where(el2 == m2, lane, big), -1, keepdims=True)
    r = jnp.exp(m2 - m1)
    g1 = pg / (1.0 + r)
    g2 = pg * r / (1.0 + r)
    e1 = i1 - N_GROUPS
    e2 = i2 - N_GROUPS

    tok = i * TR + lax.broadcasted_iota(I32, (TR, 1), 0)
    valid = tok < NV
    oh1 = jnp.logical_and(lane == e1, valid)
    oh2 = jnp.logical_and(lane == e2, valid)
    oh = jnp.where(jnp.logical_or(oh1, oh2), 1.0, 0.0)
    before = jnp.dot(tri_ref[...], oh.astype(BF16), preferred_element_type=F32) + cnt_ref[...]
    rank1 = jnp.sum(jnp.where(oh1, before, 0.0), -1, keepdims=True)
    rank2 = jnp.sum(jnp.where(oh2, before, 0.0), -1, keepdims=True)
    tile_cnt = jnp.sum(oh, 0, keepdims=True)
    cnt_ref[...] = cnt_ref[...] + tile_cnt

    meta = jnp.where(lane == 0, e1, jnp.where(lane == 1, e2, jnp.where(lane == 2, rank1,
                     jnp.where(lane == 3, rank2, 0.0))))
    gates = jnp.where(lane == 0, g1, jnp.where(lane == 1, g2, 0.0))
    gates = jnp.where(valid, gates, 0.0)
    meta_ref[...] = meta[:, 0:8].astype(I32)
    gate_ref[...] = gates[:, 0:8]
    cnt_out_ref[0] = tile_cnt


def _router_operands(wg_group, bg_group, wg_exp, bg_exp):
    wr = jnp.zeros((D_MODEL, LANES), F32)
    wr = wr.at[:, 0:N_GROUPS].set(wg_group.astype(F32)).at[:, N_GROUPS:N_GROUPS + N_EXPERTS].set(wg_exp.astype(F32))
    br = jnp.zeros((1, LANES), F32)
    br = br.at[0, 0:N_GROUPS].set(bg_group.astype(F32)).at[0, N_GROUPS:N_GROUPS + N_EXPERTS].set(bg_exp.astype(F32))
    tri = jnp.asarray(np.tril(np.ones((TR, TR), np.float32), -1), BF16)
    return wr.astype(BF16), br, tri


_ROUTE_OUT_SPECS = [pl.BlockSpec((TR, 8), lambda i: (i, 0)),
                    pl.BlockSpec((TR, 8), lambda i: (i, 0)),
                    pl.BlockSpec((1, 1, LANES), lambda i: (i, 0, 0))]
_ROUTE_OUT_SHAPES = [jax.ShapeDtypeStruct((NT, 8), I32), jax.ShapeDtypeStruct((NT, 8), F32),
                     jax.ShapeDtypeStruct((N_TILES, 1, LANES), F32)]


def _const_spec(a):
    return pl.BlockSpec(a.shape, lambda i, n=a.ndim: (0,) * n)


def _outproj_kernel(xp_ref, xs_ref, oa_ref, oas_ref, ob_ref, obs_ref, w_ref, g_ref, wr_ref, br_ref, tri_ref,
                    x1_ref, meta_ref, gate_ref, cnt_out_ref, cnt_ref):
    i = pl.program_id(0)
    x = _token_tile(i, xp_ref, xs_ref)
    mixed = jnp.concatenate([_token_tile(i, oa_ref, oas_ref), _token_tile(i, ob_ref, obs_ref)], -1)
    x1 = x + jnp.dot(mixed, w_ref[...], preferred_element_type=F32)
    x1_ref[...] = x1
    h2 = _rms(x1, g_ref[...])
    _route(h2, wr_ref, br_ref, tri_ref, cnt_ref, meta_ref, gate_ref, cnt_out_ref)


def _outproj_route(xp, xs_pad, oa, oa_s, ob, ob_s, w_out_bf16, g_ffn, wr, br, tri):
    row = pl.BlockSpec((TR, D_MODEL), lambda i: (i, 0))
    half = pl.BlockSpec((TR, A_WIDTH), lambda i: (jnp.minimum(i, SAMPLE_TILE - 1), 0))
    half_s = pl.BlockSpec((TR, A_WIDTH), lambda i: (0, 0))
    return pl.pallas_call(
        _outproj_kernel,
        grid=(N_TILES,),
        in_specs=[_PROMPT_ROWS, _SAMPLE_ROWS, half, half_s, half, half_s,
                  _const_spec(w_out_bf16), _const_spec(g_ffn),
                  _const_spec(wr), _const_spec(br), _const_spec(tri)],
        out_specs=[row] + _ROUTE_OUT_SPECS,
        out_shape=[jax.ShapeDtypeStruct((NT, D_MODEL), F32)] + _ROUTE_OUT_SHAPES,
        scratch_shapes=[pltpu.VMEM((1, LANES), F32)],
        compiler_params=_cparams(("arbitrary",)),
        name="outproj_route",
    )(xp, xs_pad, oa, oa_s, ob, ob_s, w_out_bf16, g_ffn, wr, br, tri)


def _plan(meta, tile_counts):
    tcnt = tile_counts[:, 0, :N_EXPERTS].astype(I32)
    cnt = jnp.sum(tcnt, 0)
    nblk = (cnt + TM - 1) // TM
    blk_end = jnp.cumsum(nblk)
    pstart = (blk_end - nblk) * TM
    n_used = blk_end[-1]
    blk = jnp.minimum(jnp.arange(N_SLOTS_BLK, dtype=I32), n_used - 1)
    blk_exp = jnp.sum((blk[:, None] >= blk_end[None, :]).astype(I32), -1)
    blk_exp = jnp.minimum(blk_exp, N_EXPERTS - 1)
    experts = jnp.arange(N_EXPERTS, dtype=I32)
    start_of = jnp.sum(jnp.where(meta[:, 0:2, None] == experts, pstart, 0), -1)
    tok = jnp.arange(NT, dtype=I32)[:, None]
    dest = jnp.where(tok < NV, start_of + meta[:, 2:4], 0).astype(I32).reshape(-1)
    pad_lo = pstart + cnt
    pad_hi = pstart + nblk * TM
    before_tile = jnp.cumsum(tcnt, 0) - tcnt
    run_slot = pstart[None, :] + before_tile
    local_start = jnp.cumsum(tcnt, 1) - tcnt
    local_off = jnp.zeros((N_TILES, 1, LANES), F32).at[:, 0, :N_EXPERTS].set((local_start - before_tile).astype(F32))
    return (dest, blk_exp, n_used.reshape(1).astype(I32), pad_lo.astype(I32), pad_hi.astype(I32),
            tcnt.reshape(-1), run_slot.astype(I32).reshape(-1), local_off)


def _row(ref, r):
    return ref.at[pl.ds(pl.multiple_of(r * ROW_VREGS, ROW_VREGS), ROW_VREGS), :]


STRIP_SIZES = (16, 4, 1)


def _dispatch_kernel(tcnt_ref, rslot_ref, plo_ref, phi_ref, nused_ref, x_ref, g_ref, meta_ref, loff_ref, eye_ref,
                     xs_ref, stage, zero_ref, sems, zsem):
    i = pl.program_id(0)
    last = pl.num_programs(0) - 1
    slot = i % 2
    h2 = _rms(x_ref[...], g_ref[...]).astype(BF16)

    meta = meta_ref[...].astype(F32)
    lane = lax.broadcasted_iota(I32, (TR, LANES), 1).astype(F32)
    loff = loff_ref[0]
    valid = i * TR + lax.broadcasted_iota(I32, (TR, 1), 0) < NV
    pos_rows = []
    for kk in range(2):
        off = jnp.sum(jnp.where(lane == meta[:, kk:kk + 1], loff, 0.0), -1, keepdims=True)
        pos = jnp.where(valid, meta[:, 2 + kk:3 + kk] + off, -1.0)
        pos_rows.append(jnp.sum(eye_ref[...] * pos, 0, keepdims=True))
    row_id = lax.broadcasted_iota(I32, (2 * TR, TR), 0).astype(F32)
    pick = jnp.where(jnp.logical_or(row_id == pos_rows[0], row_id == pos_rows[1]), 1.0, 0.0).astype(BF16)
    ordered = jnp.dot(pick, h2, preferred_element_type=F32)
    _store_rows_as_tiles(stage.at[slot], ordered)

    def wait_tile(n_tok, s):
        pltpu.make_async_copy(stage.at[s, pl.ds(0, 2 * n_tok * ROW_VREGS), :],
                              xs_ref.at[pl.ds(0, 2 * n_tok * ROW_VREGS), :], sems.at[s]).wait()

    def strip(src_row, dst_row, n_rows):
        src = stage.at[slot, pl.ds(pl.multiple_of(src_row * ROW_VREGS, ROW_VREGS), n_rows * ROW_VREGS), :]
        dst = xs_ref.at[pl.ds(pl.multiple_of(dst_row * ROW_VREGS, ROW_VREGS), n_rows * ROW_VREGS), :]
        return pltpu.make_async_copy(src, dst, sems.at[slot])

    def per_expert(e, lo):
        n = tcnt_ref[i * N_EXPERTS + e]
        src, dst, left = lo, rslot_ref[i * N_EXPERTS + e], n
        for size in STRIP_SIZES:
            shift = size.bit_length() - 1
            cnt = lax.shift_right_logical(left, shift)

            def issue(j, carry, src=src, dst=dst, size=size):
                strip(src + j * size, dst + j * size, size).start()
                return carry
            lax.fori_loop(0, cnt, issue, 0)
            done = lax.shift_left(cnt, shift)
            src, dst, left = src + done, dst + done, left - done
        return lo + n
    lax.fori_loop(0, N_EXPERTS, per_expert, jnp.int32(0))

    @pl.when(i > 0)
    def _():
        wait_tile(TR, 1 - slot)

    @pl.when(i == last)
    def _():
        wait_tile(NV - (N_TILES - 1) * TR, slot)
        zero_ref[...] = jnp.zeros_like(zero_ref)

        zero_row = zero_ref.at[pl.ds(0, ROW_VREGS), :]

        def per_expert(e, carry):
            def fill(p, c2):
                pltpu.make_async_copy(zero_row, _row(xs_ref, p), zsem).start()
                return c2
            lax.fori_loop(plo_ref[e], phi_ref[e], fill, 0)

            def fill_wait(p, c2):
                pltpu.make_async_copy(zero_row, _row(xs_ref, 0), zsem).wait()
                return c2
            lax.fori_loop(plo_ref[e], phi_ref[e], fill_wait, 0)
            return carry
        lax.fori_loop(0, N_EXPERTS, per_expert, 0)

        def block_of(nb):
            return xs_ref.at[pl.ds(pl.multiple_of(nb * (TM * ROW_VREGS), TM * ROW_VREGS), TM * ROW_VREGS), :]

        def fill_blk(nb, carry):
            pltpu.make_async_copy(zero_ref, block_of(nb), zsem).start()
            return carry
        lax.fori_loop(nused_ref[0], N_SLOTS_BLK, fill_blk, 0)

        def fill_blk_wait(nb, carry):
            pltpu.make_async_copy(zero_ref, block_of(0), zsem).wait()
            return carry
        lax.fori_loop(nused_ref[0], N_SLOTS_BLK, fill_blk_wait, 0)


def _dispatch(x, g_ffn, meta, tile_cnt, run_slot, local_off, pad_lo, pad_hi, n_used):
    eye = jnp.eye(TR, dtype=F32)
    return pl.pallas_call(
        _dispatch_kernel,
        grid_spec=pltpu.PrefetchScalarGridSpec(
            num_scalar_prefetch=5, grid=(N_TILES,),
            in_specs=[pl.BlockSpec((TR, D_MODEL), lambda i, *_: (i, 0)),
                      pl.BlockSpec((1, D_MODEL), lambda i, *_: (0, 0)),
                      pl.BlockSpec((TR, 8), lambda i, *_: (i, 0)),
                      pl.BlockSpec((1, 1, LANES), lambda i, *_: (i, 0, 0)),
                      pl.BlockSpec((TR, TR), lambda i, *_: (0, 0))],
            out_specs=pl.BlockSpec(memory_space=pl.ANY),
            scratch_shapes=[pltpu.VMEM((2, 2 * TR * ROW_VREGS, LANES), F32),
                            pltpu.VMEM((TM * ROW_VREGS, LANES), F32),
                            pltpu.SemaphoreType.DMA((2,)), pltpu.SemaphoreType.DMA(())]),
        out_shape=jax.ShapeDtypeStruct((N_SLOTS * ROW_VREGS, LANES), F32),
        compiler_params=_cparams(("arbitrary",)),
        name="dispatch",
    )(tile_cnt, run_slot, pad_lo, pad_hi, n_used, x, g_ffn, meta, local_off, eye)


class _ExpertRows:
    scratch = [pltpu.VMEM((2, 2 * TR * ROW_VREGS, LANES), F32), pltpu.SemaphoreType.DMA((2,))]

    def __init__(self, dest_ref, ys_ref, buf, sems):
        self.dest_ref, self.ys_ref, self.buf, self.sems = dest_ref, ys_ref, buf, sems

    def _copy(self, src_row, slot, dst_row):
        dst = self.buf.at[slot, pl.ds(pl.multiple_of(dst_row * ROW_VREGS, ROW_VREGS), ROW_VREGS), :]
        return pltpu.make_async_copy(_row(self.ys_ref, src_row), dst, self.sems.at[slot])

    def start(self, tile):
        slot = tile % 2

        def issue(g, carry):
            for un in range(ROW_DMA_UNROLL):
                t = g * ROW_DMA_UNROLL + un
                for kk in range(2):
                    self._copy(self.dest_ref[2 * (tile * TR + t) + kk], slot, kk * TR + t).start(priority=kk)
            return carry
        lax.fori_loop(0, TR // ROW_DMA_UNROLL, issue, 0)

    def wait(self, tile):
        slot = tile % 2
        pltpu.make_async_copy(self.ys_ref.at[pl.ds(0, 2 * TR * ROW_VREGS), :], self.buf.at[slot],
                              self.sems.at[slot]).wait()

    def combine(self, tile, x, gate_ref):
        rows = self.buf.at[tile % 2]
        g = gate_ref[...]
        for kk in range(2):
            y = jnp.concatenate([rows[pl.ds(kk * TR * ROW_VREGS + j, TR, stride=ROW_VREGS), :]
                                 for j in range(ROW_VREGS)], -1)
            x = x + g[:, kk:kk + 1] * y
        return x

    def fetch_combine(self, x, gate_ref):
        i = pl.program_id(0)

        @pl.when(i == 0)
        def _():
            self.start(i)

        @pl.when(i + 1 < pl.num_programs(0))
        def _():
            self.start(i + 1)

        self.wait(i)
        return self.combine(i, x, gate_ref)


def _ffn_kernel(be_ref, nused_ref, xs_ref, w1_ref, w3_ref, w2_ref, y_ref, w1b, w3b, w2b):
    nb = pl.program_id(0)

    @pl.when(nb < nused_ref[0])
    def _():
        prev = be_ref[jnp.maximum(nb - 1, 0)]
        fresh = jnp.logical_or(nb == 0, be_ref[nb] != prev)

        @pl.when(fresh)
        def _():
            w1b[...] = w1_ref[...].astype(BF16)
            w3b[...] = w3_ref[...].astype(BF16)
            w2b[...] = w2_ref[...].astype(BF16)

        x = _load_rows_from_tiles(xs_ref, TM).astype(BF16)
        a = jnp.dot(x, w1b[...], preferred_element_type=F32)
        b = jnp.dot(x, w3b[...], preferred_element_type=F32)
        mid = (_silu(a) * b).astype(BF16)
        y = jnp.dot(mid, w2b[...], preferred_element_type=F32)
        _store_rows_as_tiles(y_ref, y)

    @pl.when(nb >= nused_ref[0])
    def _():
        y_ref[...] = jnp.zeros_like(y_ref)


def _experts(xs, blk_exp, n_used, layer, w1, w3, w2):
    rows = pl.BlockSpec((TM * ROW_VREGS, LANES), lambda nb, be, nu: (nb, 0))
    rows_in = rows

    def wspec(a, b):
        return pl.BlockSpec((None, None, a, b), lambda nb, be, nu: (layer, be[nb], 0, 0))

    return pl.pallas_call(
        _ffn_kernel,
        grid_spec=pltpu.PrefetchScalarGridSpec(
            num_scalar_prefetch=2, grid=(N_SLOTS_BLK,),
            in_specs=[rows_in, wspec(D_MODEL, D_EXPERT), wspec(D_MODEL, D_EXPERT), wspec(D_EXPERT, D_MODEL)],
            out_specs=rows,
            scratch_shapes=[pltpu.VMEM((D_MODEL, D_EXPERT), BF16), pltpu.VMEM((D_MODEL, D_EXPERT), BF16),
                            pltpu.VMEM((D_EXPERT, D_MODEL), BF16)]),
        out_shape=jax.ShapeDtypeStruct((N_SLOTS * ROW_VREGS, LANES), F32),
        compiler_params=_cparams(("arbitrary",)),
        name="experts",
    )(blk_exp, n_used, xs, w1, w3, w2)


def _moe(x, g_ffn, meta, counts, layer, w1, w3, w2):
    dest, blk_exp, n_used, pad_lo, pad_hi, tile_cnt, run_slot, local_off = _plan(meta, counts)
    xs = _dispatch(x, g_ffn, meta, tile_cnt, run_slot, local_off, pad_lo, pad_hi, n_used)
    return _experts(xs, blk_exp, n_used, layer, w1, w3, w2), dest


def _pool_kernel(dest_ref, x1_ref, ys_ref, gate0_ref, gmix_ref, sp_ref, pw_ref, ps_ref, gffn_ref,
                 wr_ref, br_ref, tri_ref,
                 x3_ref, hkeep_ref, hs_ref, meta_ref, gate_ref, cnt_out_ref,
                 cnt_ref, ext_ref, e1_ref, e2_ref, e3_ref, e4_ref, mixed_ref, ybuf, ysems):
    i = pl.program_id(0)
    x2 = _ExpertRows(dest_ref, ys_ref, ybuf, ysems).fetch_combine(x1_ref[...], gate0_ref)
    h = _rms(x2, gmix_ref[...])
    H = POOL_HALO
    lvl_refs = (e1_ref, e2_ref, e3_ref, e4_ref)

    @pl.when(i < SAMPLE_TILE)
    def _():
        hkeep_ref[...] = h[TR - 16:TR]
        @pl.when(i % TILES_PER_SEQ == 0)
        def _():
            ext_ref[0:H, :] = jnp.zeros((H, D_MODEL), F32)
        ext_ref[H:H + TR, :] = h
        src = ext_ref
        for lv in range(4):
            sh = 1 << lv
            lo = 8 * (lv + 1)
            c0 = POOL_GROUP * lv
            dst = lvl_refs[lv]
            dst[lo:H + TR, c0:] = src[lo:H + TR, c0:] + src[lo - sh:H + TR - sh, c0:]
            src = dst
        pos = (i % TILES_PER_SEQ) * TR + lax.broadcasted_iota(I32, (TR, 1), 0)
        for gi, w in enumerate(POOL_WINDOWS):
            cs = slice(gi * POOL_GROUP, (gi + 1) * POOL_GROUP)
            inv = 1.0 / jnp.minimum(w, pos + 1).astype(F32)
            mixed_ref[:, cs] = lvl_refs[gi][H:H + TR, cs] * inv - h[:, cs]
        ext_ref[0:H, :] = h[TR - H:TR]

    @pl.when(i == SAMPLE_TILE)
    def _():
        hs = h[0:DEC_BATCH]
        hs_ref[...] = hs
        mixed_ref[...] = jnp.zeros_like(mixed_ref)
        for gi, w in enumerate(POOL_WINDOWS):
            cs = slice(gi * POOL_GROUP, (gi + 1) * POOL_GROUP)
            win = hs[:, cs]
            for dlt in range(1, w):
                win = win + sp_ref[POOL_KEEP - dlt][:, cs]
            mixed_ref[0:DEC_BATCH, cs] = win * (1.0 / w) - hs[:, cs]

    mixed = mixed_ref[...].astype(BF16)
    outs = [jnp.dot(mixed[:, gi * POOL_GROUP:(gi + 1) * POOL_GROUP], pw_ref[gi], preferred_element_type=F32)
            for gi in range(len(POOL_WINDOWS))]
    x3 = x2 + jnp.concatenate(outs, -1) * ps_ref[...]
    x3_ref[...] = x3
    h2 = _rms(x3, gffn_ref[...])
    _route(h2, wr_ref, br_ref, tri_ref, cnt_ref, meta_ref, gate_ref, cnt_out_ref)


def _const_spec_p(a):
    return pl.BlockSpec(a.shape, lambda i, *_, n=a.ndim: (0,) * n)


def _pool_route(dest, x1, ys, gates0, g_mix, sp_t, pw_bf16, p_scale, g_ffn, wr, br, tri):
    def rows(shape):
        return pl.BlockSpec(shape, lambda i, *_: (i, 0))

    ext = pltpu.VMEM((POOL_HALO + TR, D_MODEL), F32)
    consts = [g_mix, sp_t, pw_bf16, p_scale, g_ffn, wr, br, tri]
    return pl.pallas_call(
        _pool_kernel,
        grid_spec=pltpu.PrefetchScalarGridSpec(
            num_scalar_prefetch=1, grid=(N_TILES,),
            in_specs=[rows((TR, D_MODEL)), pl.BlockSpec(memory_space=pl.ANY), rows((TR, 8))]
                     + [_const_spec_p(a) for a in consts],
            out_specs=[rows((TR, D_MODEL)),
                       pl.BlockSpec((None, 16, D_MODEL),
                                    lambda i, *_: (jnp.minimum(i // TILES_PER_SEQ, BATCH - 1), 0, 0)),
                       pl.BlockSpec((DEC_BATCH, D_MODEL), lambda i, *_: (0, 0)),
                       rows((TR, 8)), rows((TR, 8)), pl.BlockSpec((1, 1, LANES), lambda i, *_: (i, 0, 0))],
            scratch_shapes=[pltpu.VMEM((1, LANES), F32), ext, ext, ext, ext, ext,
                            pltpu.VMEM((TR, D_MODEL), F32)] + _ExpertRows.scratch),
        out_shape=[jax.ShapeDtypeStruct((NT, D_MODEL), F32),
                   jax.ShapeDtypeStruct((BATCH, 16, D_MODEL), F32),
                   jax.ShapeDtypeStruct((DEC_BATCH, D_MODEL), F32)] + _ROUTE_OUT_SHAPES,
        compiler_params=_cparams(("arbitrary",)),
        name="pool_route",
    )(dest, x1, ys, gates0, *consts)


def _final_kernel(dest_ref, x_ref, ys_ref, gate_ref, g_ref, yp_ref, ysm_ref, ybuf, ysems):
    i = pl.program_id(0)
    y = _rms(_ExpertRows(dest_ref, ys_ref, ybuf, ysems).fetch_combine(x_ref[...], gate_ref), g_ref[...])

    @pl.when(i < SAMPLE_TILE)
    def _():
        yp_ref[...] = y

    @pl.when(i == SAMPLE_TILE)
    def _():
        ysm_ref[...] = y[0:DEC_BATCH]


def _final(dest, x3, ys, gates, g_final):
    return pl.pallas_call(
        _final_kernel,
        grid_spec=pltpu.PrefetchScalarGridSpec(
            num_scalar_prefetch=1, grid=(N_TILES,),
            in_specs=[pl.BlockSpec((TR, D_MODEL), lambda i, *_: (i, 0)), pl.BlockSpec(memory_space=pl.ANY),
                      pl.BlockSpec((TR, 8), lambda i, *_: (i, 0)), _const_spec_p(g_final)],
            out_specs=[pl.BlockSpec((TR, D_MODEL), lambda i, *_: (jnp.minimum(i, SAMPLE_TILE - 1), 0)),
                       pl.BlockSpec((DEC_BATCH, D_MODEL), lambda i, *_: (0, 0))],
            scratch_shapes=_ExpertRows.scratch),
        out_shape=[jax.ShapeDtypeStruct((NP, D_MODEL), F32), jax.ShapeDtypeStruct((DEC_BATCH, D_MODEL), F32)],
        compiler_params=_cparams(("arbitrary",)),
        name="final_norm",
    )(dest, x3, ys, gates, g_final)


def kernel(x_prompt, x_sample, cache_win_k, cache_win_v, state_hgrn, state_pool, rel_bias, norm_mix, norm_ffn,
           norm_final, w_in, w_out, hgrn_lb, hgrn_gnorm, pool_w, pool_scale, moe_wg_group, moe_bg_group,
           moe_wg_exp, moe_bg_exp, moe_w1, moe_w3, moe_w2):
    xp = x_prompt.reshape(NP, D_MODEL)
    xs_pad = jnp.pad(x_sample.reshape(DEC_BATCH, D_MODEL), ((0, TR - DEC_BATCH), (0, 0)))
    lb = jnp.cumsum(jax.nn.softmax(hgrn_lb.astype(F32), axis=0), axis=0)[0:1]
    gnorm = hgrn_gnorm[0:1].astype(F32)

    q, k, v, hq, hf, hi, hg, k_win, v_win = _inproj(xp, xs_pad, norm_mix[0:1], w_in[0].astype(BF16))
    oa = _attention_prompt(q, k, v, _attn_bias_tables(rel_bias))
    ob, s_prompt = _hgrn_prompt(hq, hf, hi, hg, lb, gnorm)
    oa_s, ob_s, s_sample = _sample_mixers(
        q, k, v, hq, hf, hi, hg,
        cache_win_k[0].reshape(DEC_BATCH, W_MAX * N_HEADS, D_HEAD),
        cache_win_v[0].reshape(DEC_BATCH, W_MAX * N_HEADS, D_HEAD),
        state_hgrn[0], _sample_bias_tables(rel_bias), lb, gnorm)
    wr, br, tri = _router_operands(moe_wg_group[0], moe_bg_group[0], moe_wg_exp[0], moe_bg_exp[0])
    x1, meta, gates0, counts = _outproj_route(xp, xs_pad, oa, oa_s, ob, ob_s, w_out[0].astype(BF16),
                                              norm_ffn[0:1], wr, br, tri)
    ys, dest = _moe(x1, norm_ffn[0:1], meta, counts, 0, moe_w1, moe_w3, moe_w2)

    wr, br, tri = _router_operands(moe_wg_group[1], moe_bg_group[1], moe_wg_exp[1], moe_bg_exp[1])
    sp_t = jnp.transpose(state_pool[0], (1, 0, 2))
    x3, hkeep, hsample, meta, gates1, counts = _pool_route(
        dest, x1, ys, gates0, norm_mix[1:2], sp_t, pool_w[0].astype(BF16), pool_scale[0:1], norm_ffn[1:2],
        wr, br, tri)
    ys, dest = _moe(x3, norm_ffn[1:2], meta, counts, 1, moe_w1, moe_w3, moe_w2)
    y_prompt, y_sample = _final(dest, x3, ys, gates1, norm_final.reshape(1, D_MODEL))

    k_s = k[NP:NV].reshape(1, DEC_BATCH, 1, N_HEADS, D_HEAD)
    v_s = v[NP:NV].reshape(1, DEC_BATCH, 1, N_HEADS, D_HEAD)
    return (y_prompt.reshape(BATCH, SEQ, D_MODEL), y_sample.reshape(DEC_BATCH, 1, D_MODEL),
            k_win.reshape(1, BATCH, W_MAX, N_HEADS, D_HEAD), v_win.reshape(1, BATCH, W_MAX, N_HEADS, D_HEAD),
            s_prompt[None], hkeep[:, 16 - POOL_KEEP:][None],
            k_s, v_s, s_sample[None], hsample.reshape(1, DEC_BATCH, 1, D_MODEL))
```

```python
import functools

import numpy as np
import jax
import jax.numpy as jnp
from jax import lax
from jax.experimental import pallas as pl
from jax.experimental.pallas import tpu as pltpu

F32 = jnp.float32
BF16 = jnp.bfloat16
I32 = jnp.int32

D_MODEL = 1024
BATCH = 4
SEQ = 4096
DEC_BATCH = 32
PAST_LEN = 8192
W_MAX = 2048
N_HEADS = 4
D_HEAD = 128
A_WIDTH = N_HEADS * D_HEAD
N_PROJ = 7 * A_WIDTH
DILATED = ((128, 1), (512, 4), (2048, 16))
NUM_BUCKETS = 32
MAX_DISTANCE = 2048
POOL_WINDOWS = (2, 4, 8, 16)
POOL_GROUP = 256
POOL_KEEP = 15
N_GROUPS = 4
EXPERTS_PER_GROUP = 8
N_EXPERTS = 32
D_EXPERT = 512
EPS = 1e-6
NEG = -1e30

LANES = 128
SUBLANES = 8
ROW_VREGS = D_MODEL // LANES
TR = 256
NP = BATCH * SEQ
NV = NP + DEC_BATCH
NT = NP + TR
N_TILES = NT // TR
SAMPLE_TILE = NP // TR
TILES_PER_SEQ = SEQ // TR
TM = 256
N_SLOTS_BLK = (2 * NV + N_EXPERTS * (TM - 1) + TM - 1) // TM
N_SLOTS = N_SLOTS_BLK * TM
QB = 128
ATTN_UNROLL = 8
ATTN_PERIOD = 3 * QB
ROW_DMA_UNROLL = 8
CH = 128
HGRN_HEADS_PER_STEP = 2
N_LEVELS = 7
POOL_HALO = 32
VMEM_LIMIT = 56 * 1024 * 1024


def _cparams(sem=None, vmem=VMEM_LIMIT):
    kw = dict(vmem_limit_bytes=vmem)
    if sem is not None:
        kw["dimension_semantics"] = sem
    return pltpu.CompilerParams(**kw)


def _rms(x, g):
    return x * lax.rsqrt(jnp.mean(x * x, -1, keepdims=True) + EPS) * g


def _sigmoid(x):
    return 1.0 / (1.0 + jnp.exp(-x))


def _silu(x):
    return x * _sigmoid(x)


def _token_tile(i, xp_ref, xs_ref):
    return jnp.where(i == SAMPLE_TILE, xs_ref[...], xp_ref[...])


_PROMPT_ROWS = pl.BlockSpec((TR, D_MODEL), lambda i: (jnp.minimum(i, SAMPLE_TILE - 1), 0))
_SAMPLE_ROWS = pl.BlockSpec((TR, D_MODEL), lambda i: (0, 0))
WIN_TILES = W_MAX // TR


def _window_block(i):
    seq = jnp.minimum(i // TILES_PER_SEQ, BATCH - 1)
    j = jnp.clip(i % TILES_PER_SEQ - (TILES_PER_SEQ - WIN_TILES), 0, WIN_TILES - 1)
    return jnp.where(i >= SAMPLE_TILE, BATCH * WIN_TILES - 1, seq * WIN_TILES + j)


def _inproj_kernel(xp_ref, xs_ref, g_ref, w_ref, *out_refs):
    i = pl.program_id(0)
    h = _rms(_token_tile(i, xp_ref, xs_ref), g_ref[...])
    p = jnp.dot(h.astype(BF16), w_ref[...], preferred_element_type=F32)
    for n, o_ref in enumerate(out_refs[:7]):
        o_ref[...] = p[:, n * A_WIDTH:(n + 1) * A_WIDTH]

    @pl.when(jnp.logical_and(i < SAMPLE_TILE, i % TILES_PER_SEQ >= TILES_PER_SEQ - WIN_TILES))
    def _():
        for n, o_ref in ((1, out_refs[7]), (2, out_refs[8])):
            for h_i in range(N_HEADS):
                c0 = n * A_WIDTH + h_i * D_HEAD
                o_ref[pl.ds(h_i, TR, stride=N_HEADS), :] = p[:, c0:c0 + D_HEAD]


def _inproj(xp, xs_pad, g, w_bf16):
    out_sds = ([jax.ShapeDtypeStruct((NT, A_WIDTH), F32)] * 7
               + [jax.ShapeDtypeStruct((BATCH * W_MAX * N_HEADS, D_HEAD), F32)] * 2)
    win = pl.BlockSpec((TR * N_HEADS, D_HEAD), lambda i: (_window_block(i), 0))
    return pl.pallas_call(
        _inproj_kernel,
        grid=(N_TILES,),
        in_specs=[_PROMPT_ROWS, _SAMPLE_ROWS,
                  pl.BlockSpec((1, D_MODEL), lambda i: (0, 0)),
                  pl.BlockSpec((D_MODEL, N_PROJ), lambda i: (0, 0))],
        out_specs=[pl.BlockSpec((TR, A_WIDTH), lambda i: (i, 0))] * 7 + [win, win],
        out_shape=out_sds,
        compiler_params=_cparams(("arbitrary",)),
        name="inproj",
    )(xp, xs_pad, g, w_bf16)


def _t5_bucket(dist):
    max_exact = NUM_BUCKETS // 2
    d = np.asarray(dist)
    large = max_exact + np.floor(np.log(np.maximum(d, 1) / max_exact)
                                 / np.log(MAX_DISTANCE / max_exact) * (NUM_BUCKETS - max_exact)).astype(np.int32)
    large = np.minimum(large, NUM_BUCKETS - 1)
    return np.where(d < max_exact, d, large).astype(np.int32)


def _attn_bias_tables(rel_bias):
    period = ATTN_PERIOD
    m = np.arange(period)
    u = np.where(m < 2 * QB, m, m - period)
    pick = np.zeros((len(DILATED), 2, period, NUM_BUCKETS), np.float32)
    mask = np.zeros((len(DILATED), 2, period, 1), np.float32)
    for bi, (w, d) in enumerate(DILATED):
        nk = w // d
        for vi, off in enumerate((0, QB)):
            j = off - u
            ok = (j >= 0) & (j <= nk)
            pick[bi, vi, m[ok], _t5_bucket(d * j[ok])] = 1.0
            mask[bi, vi, ~ok, 0] = NEG
    vec = jnp.einsum("bvmk,kh->hbvm", pick, rel_bias.astype(F32), precision=lax.Precision.HIGHEST)
    vec = vec + jnp.transpose(mask, (3, 0, 1, 2))
    return vec[:, :, :, None, :]


def _attn_kernel(q_ref, k_ref, v_ref, vec_ref, o_ref,
                 qd, kd, vd, ud, md, sd, u_acc, m_acc, s_acc, bias_ref):
    scale = D_HEAD ** -0.5
    for bi in range(len(DILATED)):
        for vi in range(2):
            rows = jnp.broadcast_to(vec_ref[bi, vi], (QB, ATTN_PERIOD))
            bias_ref[bi, vi] = pltpu.roll(rows, 0, 1, stride=1, stride_axis=0)[:, :2 * QB]

    def block_stats(bi, t, nb):
        has_prev = jnp.minimum(t % nb, 1)
        q0 = pl.multiple_of(t * QB, QB)
        k0 = pl.multiple_of((t - has_prev) * QB, QB)
        qb = qd[pl.ds(q0, QB), :]
        kb = kd[pl.ds(k0, 2 * QB), :]
        vb = vd[pl.ds(k0, 2 * QB), :]
        s = lax.dot_general(qb, kb, (((1,), (1,)), ((), ())), preferred_element_type=F32)
        s = s + bias_ref[bi, has_prev]
        mb = jnp.max(s, -1, keepdims=True)
        p = jnp.exp(s - mb)
        sb = jnp.sum(p, -1, keepdims=True)
        u = jnp.dot(p.astype(BF16), vb, preferred_element_type=F32)
        return q0, mb, sb, u

    for bi in (2, 1, 0):
        d = DILATED[bi][1]
        cl = SEQ // d
        nb = cl // QB
        for r in range(d):
            src = pl.ds(r, cl, stride=d) if d > 1 else pl.ds(0, cl)
            dst = pl.ds(r * cl, cl)
            qd[dst, :] = (q_ref[src, :] * scale).astype(BF16)
            kd[dst, :] = k_ref[src, :].astype(BF16)
            vd[dst, :] = v_ref[src, :].astype(BF16)

        if d > 1:
            def body(g, carry, bi=bi, nb=nb):
                for un in range(ATTN_UNROLL):
                    q0, mb, sb, u = block_stats(bi, g * ATTN_UNROLL + un, nb)
                    ud[pl.ds(q0, QB), :] = u
                    md[pl.ds(q0, QB), :] = jnp.broadcast_to(mb, (QB, D_HEAD))
                    sd[pl.ds(q0, QB), :] = jnp.broadcast_to(sb, (QB, D_HEAD))
                return carry
            lax.fori_loop(0, SEQ // QB // ATTN_UNROLL, body, 0)
            for r in range(d):
                pos = pl.ds(r, cl, stride=d)
                cls = pl.ds(r * cl, cl)
                if bi == 2:
                    u_acc[pos, :] = ud[cls, :]
                    m_acc[pos, :] = md[cls, :]
                    s_acc[pos, :] = sd[cls, :]
                else:
                    m_old = m_acc[pos, :]
                    m_blk = md[cls, :]
                    m_new = jnp.maximum(m_old, m_blk)
                    a = jnp.exp(m_old - m_new)
                    b = jnp.exp(m_blk - m_new)
                    u_acc[pos, :] = a * u_acc[pos, :] + b * ud[cls, :]
                    s_acc[pos, :] = a * s_acc[pos, :] + b * sd[cls, :]
                    m_acc[pos, :] = m_new
        else:
            def body(g, carry, bi=bi, nb=nb):
                for un in range(ATTN_UNROLL):
                    q0, mb, sb, u = block_stats(bi, g * ATTN_UNROLL + un, nb)
                    rows = pl.ds(q0, QB)
                    m_old = m_acc[rows, :]
                    m_new = jnp.maximum(m_old, mb)
                    a = jnp.exp(m_old - m_new)
                    b = jnp.exp(mb - m_new)
                    num = a * u_acc[rows, :] + b * u
                    den = a * s_acc[rows, :] + b * sb
                    o_ref[rows, :] = (num / den).astype(o_ref.dtype)
                return carry
            lax.fori_loop(0, SEQ // QB // ATTN_UNROLL, body, 0)


def _attention_prompt(q, k, v, bias_tabs):
    blk = pl.BlockSpec((SEQ, D_HEAD), lambda b, h: (b, h))
    return pl.pallas_call(
        _attn_kernel,
        grid=(BATCH, N_HEADS),
        in_specs=[blk, blk, blk,
                  pl.BlockSpec((None, 3, 2, 1, ATTN_PERIOD), lambda b, h: (h, 0, 0, 0, 0))],
        out_specs=blk,
        out_shape=jax.ShapeDtypeStruct((NP, A_WIDTH), BF16),
        scratch_shapes=[pltpu.VMEM((SEQ, D_HEAD), BF16)] * 3
                       + [pltpu.VMEM((SEQ, D_HEAD), F32)] * 6
                       + [pltpu.VMEM((len(DILATED), 2, QB, 2 * QB), F32)],
        compiler_params=_cparams(("arbitrary", "arbitrary")),
        name="attn_prompt",
    )(q, k, v, bias_tabs)


def _hgrn_tables():
    t = np.arange(CH)
    u = np.arange(CH)
    sums_q = np.zeros((1 + N_LEVELS, CH, CH), np.float32)
    sums_k = np.zeros((2 + N_LEVELS, CH, CH), np.float32)
    sums_q[0] = (u[None, :] <= t[:, None])
    sums_k[0] = (u[None, :] > t[:, None])
    sums_k[1 + N_LEVELS] = 1.0
    pair = np.zeros((N_LEVELS, CH, CH), np.float32)
    for l in range(N_LEVELS):
        h = CH >> (l + 1)
        is_q = (t // h) % 2 == 1
        half_start = (t // h) * h
        half_end = half_start + h
        sel_q = (u[None, :] >= half_start[:, None]) & (u[None, :] <= t[:, None])
        sel_k = (u[None, :] > t[:, None]) & (u[None, :] < half_end[:, None])
        sums_q[1 + l] = sel_q & is_q[:, None]
        sums_k[1 + l] = sel_k & ~is_q[:, None]
        same = (t[:, None] // (2 * h)) == (t[None, :] // (2 * h))
        pair[l] = same & is_q[:, None] & (~is_q)[None, :]
    sums_kt = np.transpose(sums_k, (2, 0, 1)).reshape(CH, (2 + N_LEVELS) * CH)
    sums_q = sums_q.reshape((1 + N_LEVELS) * CH, CH)
    return (jnp.asarray(np.concatenate([sums_q, sums_q], 1), BF16),
            jnp.asarray(np.concatenate([sums_kt, sums_kt], 0), BF16),
            jnp.asarray(pair), jnp.asarray(np.eye(CH, dtype=np.float32)))


def _split_bf16(x):
    hi = x.astype(BF16)
    return hi, (x - hi.astype(F32)).astype(BF16)


def _hgrn_kernel(q_ref, f_ref, i_ref, g_ref, lb_ref, gn_ref, sq_ref, skt_ref, pair_ref, eye_ref, o_ref, s_ref):
    gn = gn_ref[...]

    def chunk(c, st, hh):
        rows = pl.ds(pl.multiple_of(c * CH, CH), CH)
        cols = slice(hh * D_HEAD, (hh + 1) * D_HEAD)
        lb = lb_ref[:, cols]
        q = _silu(q_ref[rows, cols])
        f = lb + (1.0 - lb) * _sigmoid(f_ref[rows, cols])
        lf = jnp.log(f)
        k = 1.0 - f
        v16 = i_ref[rows, cols].astype(BF16)
        kt = k.T
        exq = jnp.exp(jnp.dot(sq_ref[...], jnp.concatenate(_split_bf16(lf), 0),
                              preferred_element_type=F32))
        exk = jnp.exp(jnp.dot(jnp.concatenate(_split_bf16(lf.T), 1), skt_ref[...],
                              preferred_element_type=F32))
        inter = jnp.dot((q * exq[0:CH]).astype(BF16), st.astype(BF16), preferred_element_type=F32)
        sc = eye_ref[...] * jnp.sum(q * k, -1, keepdims=True)
        for l in range(N_LEVELS):
            ql = (q * exq[(1 + l) * CH:(2 + l) * CH]).astype(BF16)
            klt = (kt * exk[:, (1 + l) * CH:(2 + l) * CH]).astype(BF16)
            sc = sc + pair_ref[l] * jnp.dot(ql, klt, preferred_element_type=F32)
        o = inter + jnp.dot(sc.astype(BF16), v16, preferred_element_type=F32)
        st_new = (st * exk[:, (1 + N_LEVELS) * CH:]
                  + jnp.dot((kt * exk[:, 0:CH]).astype(BF16), v16, preferred_element_type=F32))
        o = _rms(o, gn) * _silu(g_ref[rows, cols])
        o_ref[rows, cols] = o.astype(o_ref.dtype)
        return st_new

    def step(c, states):
        return tuple(chunk(c, st, hh) for hh, st in enumerate(states))

    zero = jnp.zeros((D_HEAD, D_HEAD), F32)
    states = lax.fori_loop(0, SEQ // CH, step, (zero,) * HGRN_HEADS_PER_STEP)
    for hh, st in enumerate(states):
        s_ref[hh] = st


def _hgrn_prompt(hq, hf, hi, hg, lb, gnorm):
    tables = _hgrn_tables()
    width = HGRN_HEADS_PER_STEP * D_HEAD
    blk = pl.BlockSpec((SEQ, width), lambda b, h: (b, h))

    def full(a):
        return pl.BlockSpec(a.shape, lambda b, h, n=a.ndim: (0,) * n)

    return pl.pallas_call(
        _hgrn_kernel,
        grid=(BATCH, N_HEADS // HGRN_HEADS_PER_STEP),
        in_specs=[blk, blk, blk, blk,
                  pl.BlockSpec((1, width), lambda b, h: (0, h)),
                  pl.BlockSpec((1, D_HEAD), lambda b, h: (0, 0))] + [full(a) for a in tables],
        out_specs=[blk, pl.BlockSpec((None, HGRN_HEADS_PER_STEP, D_HEAD, D_HEAD), lambda b, h: (b, h, 0, 0))],
        out_shape=[jax.ShapeDtypeStruct((NP, A_WIDTH), BF16),
                   jax.ShapeDtypeStruct((BATCH, N_HEADS, D_HEAD, D_HEAD), F32)],
        compiler_params=_cparams(("arbitrary", "arbitrary")),
        name="hgrn_prompt",
    )(hq, hf, hi, hg, lb, gnorm, *tables)


def _sample_bias_tables(rel_bias):
    j = np.concatenate([QB - np.arange(QB), [0]])
    pick = np.zeros((len(DILATED), QB + 1, NUM_BUCKETS), np.float32)
    for bi, (w, d) in enumerate(DILATED):
        pick[bi, np.arange(QB + 1), _t5_bucket(d * j)] = 1.0
    return jnp.einsum("bjk,kh->bhj", pick, rel_bias.astype(F32), precision=lax.Precision.HIGHEST)[..., None]


def _bf16_round(x):
    return x.astype(BF16).astype(F32)


def _col(row, eye):
    return jnp.sum(eye * row, -1, keepdims=True)


NEAR_POS = 4 * QB
FAR_STEPS = (W_MAX - NEAR_POS) // 16


def _cached_rows(near_ref, far_ref, d, h):
    def near(first_pos, n, step):
        return near_ref[pl.ds((first_pos - (W_MAX - NEAR_POS)) * N_HEADS + h, n, stride=step * N_HEADS), :]

    if QB * d <= NEAR_POS:
        return near(W_MAX - QB * d, QB, d)
    return jnp.concatenate([far_ref[:, h, :], near(W_MAX - NEAR_POS, QB - FAR_STEPS, d)], 0)


def _sample_kernel(q_ref, k_ref, v_ref, hq_ref, hf_ref, hi_ref, hg_ref, kn_ref, kf_ref, vn_ref, vf_ref, s0_ref,
                   bias_ref, lb_ref, gn_ref, eye_ref, oa_ref, ob_ref, s_ref, oa_acc, ob_acc):
    b = pl.program_id(0)
    row = pl.ds(b, 1)
    scale = D_HEAD ** -0.5
    q = _bf16_round(q_ref[row, :] * scale)
    kn = _bf16_round(k_ref[row, :])
    vn = _bf16_round(v_ref[row, :])
    eye = eye_ref[...]

    stats = []
    for bi, (w, d) in enumerate(DILATED):
        per_head = []
        for h in range(N_HEADS):
            hs = slice(h * D_HEAD, (h + 1) * D_HEAD)
            kb = _bf16_round(_cached_rows(kn_ref, kf_ref, d, h))
            vb = _bf16_round(_cached_rows(vn_ref, vf_ref, d, h))
            s = jnp.sum(kb * q[:, hs], -1, keepdims=True) + bias_ref[bi, h, 0:QB]
            s0 = jnp.sum(q[:, hs] * kn[:, hs], -1, keepdims=True) + bias_ref[bi, h, QB:QB + 1]
            m = jnp.maximum(jnp.max(s, 0, keepdims=True), s0)
            p = jnp.exp(s - m)
            p0 = jnp.exp(s0 - m)
            ssum = jnp.sum(p, 0, keepdims=True) + p0
            u = jnp.sum(_bf16_round(p) * vb, 0, keepdims=True) + _bf16_round(p0) * vn[:, hs]
            per_head.append((m, ssum, u))
        stats.append(per_head)
    outs = []
    for h in range(N_HEADS):
        m_all = functools.reduce(jnp.maximum, [stats[bi][h][0] for bi in range(3)])
        num = 0.0
        den = 0.0
        for bi in range(3):
            m, ssum, u = stats[bi][h]
            c = jnp.exp(m - m_all)
            num = num + c * u
            den = den + c * ssum
        outs.append(num / den)
    oa_acc[row, :] = jnp.concatenate(outs, -1)

    qh = _silu(hq_ref[row, :])
    lb = lb_ref[...]
    f = lb + (1.0 - lb) * _sigmoid(hf_ref[row, :])
    vi = hi_ref[row, :]
    gate = _silu(hg_ref[row, :])
    gn = gn_ref[...]
    obs = []
    for h in range(N_HEADS):
        hs = slice(h * D_HEAD, (h + 1) * D_HEAD)
        f_col = _col(f[:, hs], eye)
        q_col = _col(qh[:, hs], eye)
        s_old = s0_ref[h]
        s_ref[h] = f_col * s_old + (1.0 - f_col) * vi[:, hs]
        inter = jnp.sum(_bf16_round(q_col * f_col) * _bf16_round(s_old), 0, keepdims=True)
        qk = jnp.sum(qh[:, hs] * (1.0 - f[:, hs]), -1, keepdims=True)
        o = inter + qk * vi[:, hs]
        obs.append(_rms(o, gn) * gate[:, hs])
    ob_acc[row, :] = jnp.concatenate(obs, -1)

    @pl.when(b == DEC_BATCH - 1)
    def _():
        pad = jnp.zeros((TR - DEC_BATCH, A_WIDTH), F32)
        oa_ref[...] = jnp.concatenate([oa_acc[...], pad], 0).astype(oa_ref.dtype)
        ob_ref[...] = jnp.concatenate([ob_acc[...], pad], 0).astype(ob_ref.dtype)


def _sample_mixers(q, k, v, hq, hf, hi, hg, cache_k, cache_v, state, bias_s, lb, gnorm):
    tile = pl.BlockSpec((TR, A_WIDTH), lambda b: (SAMPLE_TILE, 0))
    out_tile = pl.BlockSpec((TR, A_WIDTH), lambda b: (0, 0))
    near = pl.BlockSpec((None, NEAR_POS * N_HEADS, D_HEAD), lambda b: (b, W_MAX // NEAR_POS - 1, 0))
    far = pl.BlockSpec((None, FAR_STEPS, SUBLANES, D_HEAD), lambda b: (b, 0, 0, 0))
    rows = (DEC_BATCH, W_MAX * N_HEADS, D_HEAD)
    groups = (DEC_BATCH, W_MAX // 16, 16 * N_HEADS, D_HEAD)
    eye = jnp.eye(D_HEAD, dtype=F32)

    return pl.pallas_call(
        _sample_kernel,
        grid=(DEC_BATCH,),
        in_specs=[tile, tile, tile, tile, tile, tile, tile, near, far, near, far,
                  pl.BlockSpec((None, N_HEADS, D_HEAD, D_HEAD), lambda b: (b, 0, 0, 0)),
                  pl.BlockSpec(bias_s.shape, lambda b: (0, 0, 0, 0)),
                  pl.BlockSpec((1, A_WIDTH), lambda b: (0, 0)),
                  pl.BlockSpec((1, D_HEAD), lambda b: (0, 0)),
                  pl.BlockSpec((D_HEAD, D_HEAD), lambda b: (0, 0))],
        out_specs=[out_tile, out_tile,
                   pl.BlockSpec((None, N_HEADS, D_HEAD, D_HEAD), lambda b: (b, 0, 0, 0))],
        out_shape=[jax.ShapeDtypeStruct((TR, A_WIDTH), BF16), jax.ShapeDtypeStruct((TR, A_WIDTH), BF16),
                   jax.ShapeDtypeStruct((DEC_BATCH, N_HEADS, D_HEAD, D_HEAD), F32)],
        scratch_shapes=[pltpu.VMEM((DEC_BATCH, A_WIDTH), F32)] * 2,
        compiler_params=_cparams(("arbitrary",)),
        name="sample_mixers",
    )(q, k, v, hq, hf, hi, hg, cache_k.reshape(rows), cache_k.reshape(groups),
      cache_v.reshape(rows), cache_v.reshape(groups), state, bias_s, lb, gnorm, eye)


def _store_rows_as_tiles(ref, val):
    n = val.shape[0]
    for j in range(ROW_VREGS):
        ref[pl.ds(j, n, stride=ROW_VREGS), :] = val[:, j * LANES:(j + 1) * LANES]


def _load_rows_from_tiles(ref, n):
    return jnp.concatenate([ref[pl.ds(j, n, stride=ROW_VREGS), :] for j in range(ROW_VREGS)], -1)


def _route(h2, wr_ref, br_ref, tri_ref, cnt_ref, meta_ref, gate_ref, cnt_out_ref):
    i = pl.program_id(0)

    @pl.when(i == 0)
    def _():
        cnt_ref[...] = jnp.zeros_like(cnt_ref)

    logits = jnp.dot(h2.astype(BF16), wr_ref[...], preferred_element_type=F32) + br_ref[...]
    lane = lax.broadcasted_iota(I32, (TR, LANES), 1).astype(F32)
    big = float(1 << 20)
    is_g = lane < N_GROUPS
    gl = jnp.where(is_g, logits, NEG)
    gmax = jnp.max(gl, -1, keepdims=True)
    gsel = jnp.min(jnp.where(gl == gmax, lane, big), -1, keepdims=True)
    pg = 1.0 / jnp.sum(jnp.where(is_g, jnp.exp(gl - gmax), 0.0), -1, keepdims=True)
    lo = N_GROUPS + EXPERTS_PER_GROUP * gsel
    in_grp = jnp.logical_and(lane >= lo, lane < lo + EXPERTS_PER_GROUP)
    el = jnp.where(in_grp, logits, NEG)
    m1 = jnp.max(el, -1, keepdims=True)
    i1 = jnp.min(jnp.where(el == m1, lane, big), -1, keepdims=True)
    el2 = jnp.where(lane == i1, NEG, el)
    m2 = jnp.max(el2, -1, keepdims=True)
    i2 = jnp.min(jnp.where(el2 == m2, lane, big), -1, keepdims=True)
    r = jnp.exp(m2 - m1)
    g1 = pg / (1.0 + r)
    g2 = pg * r / (1.0 + r)
    e1 = i1 - N_GROUPS
    e2 = i2 - N_GROUPS

    tok = i * TR + lax.broadcasted_iota(I32, (TR, 1), 0)
    valid = tok < NV
    oh1 = jnp.logical_and(lane == e1, valid)
    oh2 = jnp.logical_and(lane == e2, valid)
    oh = jnp.where(jnp.logical_or(oh1, oh2), 1.0, 0.0)
    before = jnp.dot(tri_ref[...], oh.astype(BF16), preferred_element_type=F32) + cnt_ref[...]
    rank1 = jnp.sum(jnp.where(oh1, before, 0.0), -1, keepdims=True)
    rank2 = jnp.sum(jnp.where(oh2, before, 0.0), -1, keepdims=True)
    tile_cnt = jnp.sum(oh, 0, keepdims=True)
    cnt_ref[...] = cnt_ref[...] + tile_cnt

    eye = jnp.where(lax.broadcasted_iota(I32, (TR, TR), 0) == lax.broadcasted_iota(I32, (TR, TR), 1), 1.0, 0.0)
    rows = [jnp.sum(eye * col, 0, keepdims=True) for col in (e1, e2, rank1, rank2)]
    meta_ref[0] = jnp.concatenate(rows + [jnp.zeros((4, TR), F32)], 0).astype(I32)
    gates = jnp.where(lane == 0, g1, jnp.where(lane == 1, g2, 0.0))
    gates = jnp.where(valid, gates, 0.0)
    gate_ref[...] = gates[:, 0:8]
    cnt_out_ref[0] = tile_cnt


def _router_operands(wg_group, bg_group, wg_exp, bg_exp):
    wr = jnp.zeros((D_MODEL, LANES), F32)
    wr = wr.at[:, 0:N_GROUPS].set(wg_group.astype(F32)).at[:, N_GROUPS:N_GROUPS + N_EXPERTS].set(wg_exp.astype(F32))
    br = jnp.zeros((1, LANES), F32)
    br = br.at[0, 0:N_GROUPS].set(bg_group.astype(F32)).at[0, N_GROUPS:N_GROUPS + N_EXPERTS].set(bg_exp.astype(F32))
    tri = jnp.asarray(np.tril(np.ones((TR, TR), np.float32), -1), BF16)
    return wr.astype(BF16), br, tri


_ROUTE_OUT_SPECS = [pl.BlockSpec((1, 8, TR), lambda i: (i, 0, 0)),
                    pl.BlockSpec((TR, 8), lambda i: (i, 0)),
                    pl.BlockSpec((1, 1, LANES), lambda i: (i, 0, 0))]
_ROUTE_OUT_SHAPES = [jax.ShapeDtypeStruct((N_TILES, 8, TR), I32), jax.ShapeDtypeStruct((NT, 8), F32),
                     jax.ShapeDtypeStruct((N_TILES, 1, LANES), F32)]


def _const_spec(a):
    return pl.BlockSpec(a.shape, lambda i, n=a.ndim: (0,) * n)


def _outproj_kernel(xp_ref, xs_ref, oa_ref, oas_ref, ob_ref, obs_ref, w_ref, g_ref, wr_ref, br_ref, tri_ref,
                    x1_ref, meta_ref, gate_ref, cnt_out_ref, cnt_ref):
    i = pl.program_id(0)
    x = _token_tile(i, xp_ref, xs_ref)
    mixed = jnp.concatenate([_token_tile(i, oa_ref, oas_ref), _token_tile(i, ob_ref, obs_ref)], -1)
    x1 = x + jnp.dot(mixed, w_ref[...], preferred_element_type=F32)
    x1_ref[...] = x1
    h2 = _rms(x1, g_ref[...])
    _route(h2, wr_ref, br_ref, tri_ref, cnt_ref, meta_ref, gate_ref, cnt_out_ref)


def _outproj_route(xp, xs_pad, oa, oa_s, ob, ob_s, w_out_bf16, g_ffn, wr, br, tri):
    row = pl.BlockSpec((TR, D_MODEL), lambda i: (i, 0))
    half = pl.BlockSpec((TR, A_WIDTH), lambda i: (jnp.minimum(i, SAMPLE_TILE - 1), 0))
    half_s = pl.BlockSpec((TR, A_WIDTH), lambda i: (0, 0))
    return pl.pallas_call(
        _outproj_kernel,
        grid=(N_TILES,),
        in_specs=[_PROMPT_ROWS, _SAMPLE_ROWS, half, half_s, half, half_s,
                  _const_spec(w_out_bf16), _const_spec(g_ffn),
                  _const_spec(wr), _const_spec(br), _const_spec(tri)],
        out_specs=[row] + _ROUTE_OUT_SPECS,
        out_shape=[jax.ShapeDtypeStruct((NT, D_MODEL), F32)] + _ROUTE_OUT_SHAPES,
        scratch_shapes=[pltpu.VMEM((1, LANES), F32)],
        compiler_params=_cparams(("arbitrary",)),
        name="outproj_route",
    )(xp, xs_pad, oa, oa_s, ob, ob_s, w_out_bf16, g_ffn, wr, br, tri)


def _plan(meta, tile_counts):
    tcnt = tile_counts[:, 0, :N_EXPERTS].astype(I32)
    cnt = jnp.sum(tcnt, 0)
    nblk = (cnt + TM - 1) // TM
    blk_end = jnp.cumsum(nblk)
    pstart = (blk_end - nblk) * TM
    n_used = blk_end[-1]
    blk = jnp.minimum(jnp.arange(N_SLOTS_BLK, dtype=I32), n_used - 1)
    blk_exp = jnp.sum((blk[:, None] >= blk_end[None, :]).astype(I32), -1)
    blk_exp = jnp.minimum(blk_exp, N_EXPERTS - 1)
    eid = meta[:, 0:2, :]
    experts = jnp.arange(N_EXPERTS, dtype=I32)[:, None, None, None]
    start_of = jnp.sum(jnp.where(eid[None] == experts, pstart[:, None, None, None], 0), 0)
    tok = (jnp.arange(N_TILES, dtype=I32) * TR)[:, None, None] + jnp.arange(TR, dtype=I32)
    dest = jnp.where(tok < NV, start_of + meta[:, 2:4, :], 0).astype(I32).reshape(-1)
    pad_lo = pstart + cnt
    pad_hi = pstart + nblk * TM
    return dest, blk_exp, n_used.reshape(1).astype(I32), pad_lo.astype(I32), pad_hi.astype(I32)


def _row(ref, r):
    return ref.at[pl.ds(pl.multiple_of(r * ROW_VREGS, ROW_VREGS), ROW_VREGS), :]


def _dest_index(tile, kk, t):
    return (2 * tile + kk) * TR + t


def _dispatch_kernel(dest_ref, plo_ref, phi_ref, nused_ref, x_ref, g_ref, xs_ref, stage, zero_ref, sems, zsem):
    i = pl.program_id(0)
    last = pl.num_programs(0) - 1
    slot = i % 2
    _store_rows_as_tiles(stage.at[slot], _rms(x_ref[...], g_ref[...]))

    def wait_tile(n_tok, s):
        for _ in range(2):
            pltpu.make_async_copy(stage.at[s, pl.ds(0, n_tok * ROW_VREGS), :],
                                  xs_ref.at[pl.ds(0, n_tok * ROW_VREGS), :], sems.at[s]).wait()

    def scatter(n_tok):
        def issue(t, carry):
            src = stage.at[slot, pl.ds(pl.multiple_of(t * ROW_VREGS, ROW_VREGS), ROW_VREGS), :]
            for kk in range(2):
                pltpu.make_async_copy(src, _row(xs_ref, dest_ref[_dest_index(i, kk, t)]),
                                      sems.at[slot]).start(priority=kk)
            return carry
        lax.fori_loop(0, n_tok, issue, 0)

    @pl.when(i < last)
    def _():
        scatter(TR)

    @pl.when(i > 0)
    def _():
        wait_tile(TR, 1 - slot)

    @pl.when(i == last)
    def _():
        n_last = NV - (N_TILES - 1) * TR
        scatter(n_last)
        wait_tile(n_last, slot)
        zero_ref[...] = jnp.zeros_like(zero_ref)

        zero_row = zero_ref.at[pl.ds(0, ROW_VREGS), :]

        def per_expert(e, carry):
            def fill(p, c2):
                pltpu.make_async_copy(zero_row, _row(xs_ref, p), zsem).start()
                return c2
            lax.fori_loop(plo_ref[e], phi_ref[e], fill, 0)

            def fill_wait(p, c2):
                pltpu.make_async_copy(zero_row, _row(xs_ref, 0), zsem).wait()
                return c2
            lax.fori_loop(plo_ref[e], phi_ref[e], fill_wait, 0)
            return carry
        lax.fori_loop(0, N_EXPERTS, per_expert, 0)

        def block_of(nb):
            return xs_ref.at[pl.ds(pl.multiple_of(nb * (TM * ROW_VREGS), TM * ROW_VREGS), TM * ROW_VREGS), :]

        def fill_blk(nb, carry):
            pltpu.make_async_copy(zero_ref, block_of(nb), zsem).start()
            return carry
        lax.fori_loop(nused_ref[0], N_SLOTS_BLK, fill_blk, 0)

        def fill_blk_wait(nb, carry):
            pltpu.make_async_copy(zero_ref, block_of(0), zsem).wait()
            return carry
        lax.fori_loop(nused_ref[0], N_SLOTS_BLK, fill_blk_wait, 0)


def _dispatch(x, g_ffn, dest, pad_lo, pad_hi, n_used):
    return pl.pallas_call(
        _dispatch_kernel,
        grid_spec=pltpu.PrefetchScalarGridSpec(
            num_scalar_prefetch=4, grid=(N_TILES,),
            in_specs=[pl.BlockSpec((TR, D_MODEL), lambda i, *_: (i, 0)),
                      pl.BlockSpec((1, D_MODEL), lambda i, *_: (0, 0))],
            out_specs=pl.BlockSpec(memory_space=pl.ANY),
            scratch_shapes=[pltpu.VMEM((2, TR * ROW_VREGS, LANES), F32),
                            pltpu.VMEM((TM * ROW_VREGS, LANES), F32),
                            pltpu.SemaphoreType.DMA((2,)), pltpu.SemaphoreType.DMA(())]),
        out_shape=jax.ShapeDtypeStruct((N_SLOTS * ROW_VREGS, LANES), F32),
        compiler_params=_cparams(("arbitrary",)),
        name="dispatch",
    )(dest, pad_lo, pad_hi, n_used, x, g_ffn)


class _ExpertRows:
    scratch = [pltpu.VMEM((2, 2 * TR * ROW_VREGS, LANES), F32), pltpu.SemaphoreType.DMA((2,))]

    def __init__(self, dest_ref, ys_ref, buf, sems):
        self.dest_ref, self.ys_ref, self.buf, self.sems = dest_ref, ys_ref, buf, sems

    def _copy(self, src_row, slot, dst_row):
        dst = self.buf.at[slot, pl.ds(pl.multiple_of(dst_row * ROW_VREGS, ROW_VREGS), ROW_VREGS), :]
        return pltpu.make_async_copy(_row(self.ys_ref, src_row), dst, self.sems.at[slot])

    def start(self, tile):
        slot = tile % 2

        def issue(g, carry):
            for un in range(ROW_DMA_UNROLL):
                t = g * ROW_DMA_UNROLL + un
                for kk in range(2):
                    self._copy(self.dest_ref[_dest_index(tile, kk, t)], slot, kk * TR + t).start(priority=kk)
            return carry
        lax.fori_loop(0, TR // ROW_DMA_UNROLL, issue, 0)

    def wait(self, tile):
        slot = tile % 2
        pltpu.make_async_copy(self.ys_ref.at[pl.ds(0, 2 * TR * ROW_VREGS), :], self.buf.at[slot],
                              self.sems.at[slot]).wait()

    def combine(self, tile, x, gate_ref):
        rows = self.buf.at[tile % 2]
        g = gate_ref[...]
        for kk in range(2):
            y = jnp.concatenate([rows[pl.ds(kk * TR * ROW_VREGS + j, TR, stride=ROW_VREGS), :]
                                 for j in range(ROW_VREGS)], -1)
            x = x + g[:, kk:kk + 1] * y
        return x

    def fetch_combine(self, x, gate_ref):
        i = pl.program_id(0)

        @pl.when(i == 0)
        def _():
            self.start(i)

        @pl.when(i + 1 < pl.num_programs(0))
        def _():
            self.start(i + 1)

        self.wait(i)
        return self.combine(i, x, gate_ref)


def _ffn_kernel(be_ref, nused_ref, xs_ref, w1_ref, w3_ref, w2_ref, y_ref, w1b, w3b, w2b):
    nb = pl.program_id(0)

    @pl.when(nb < nused_ref[0])
    def _():
        prev = be_ref[jnp.maximum(nb - 1, 0)]
        fresh = jnp.logical_or(nb == 0, be_ref[nb] != prev)

        @pl.when(fresh)
        def _():
            w1b[...] = w1_ref[...].astype(BF16)
            w3b[...] = w3_ref[...].astype(BF16)
            w2b[...] = w2_ref[...].astype(BF16)

        x = _load_rows_from_tiles(xs_ref, TM).astype(BF16)
        a = jnp.dot(x, w1b[...], preferred_element_type=F32)
        b = jnp.dot(x, w3b[...], preferred_element_type=F32)
        mid = (_silu(a) * b).astype(BF16)
        y = jnp.dot(mid, w2b[...], preferred_element_type=F32)
        _store_rows_as_tiles(y_ref, y)

    @pl.when(nb >= nused_ref[0])
    def _():
        y_ref[...] = jnp.zeros_like(y_ref)


def _experts(xs, blk_exp, n_used, layer, w1, w3, w2):
    rows = pl.BlockSpec((TM * ROW_VREGS, LANES), lambda nb, be, nu: (nb, 0))
    rows_in = rows

    def wspec(a, b):
        return pl.BlockSpec((None, None, a, b), lambda nb, be, nu: (layer, be[nb], 0, 0))

    return pl.pallas_call(
        _ffn_kernel,
        grid_spec=pltpu.PrefetchScalarGridSpec(
            num_scalar_prefetch=2, grid=(N_SLOTS_BLK,),
            in_specs=[rows_in, wspec(D_MODEL, D_EXPERT), wspec(D_MODEL, D_EXPERT), wspec(D_EXPERT, D_MODEL)],
            out_specs=rows,
            scratch_shapes=[pltpu.VMEM((D_MODEL, D_EXPERT), BF16), pltpu.VMEM((D_MODEL, D_EXPERT), BF16),
                            pltpu.VMEM((D_EXPERT, D_MODEL), BF16)]),
        out_shape=jax.ShapeDtypeStruct((N_SLOTS * ROW_VREGS, LANES), F32),
        compiler_params=_cparams(("arbitrary",)),
        name="experts",
    )(blk_exp, n_used, xs, w1, w3, w2)


def _moe(x, g_ffn, meta, counts, layer, w1, w3, w2):
    dest, blk_exp, n_used, pad_lo, pad_hi = _plan(meta, counts)
    xs = _dispatch(x, g_ffn, dest, pad_lo, pad_hi, n_used)
    return _experts(xs, blk_exp, n_used, layer, w1, w3, w2), dest


def _pool_kernel(dest_ref, x1_ref, ys_ref, gate0_ref, gmix_ref, sp_ref, pw_ref, ps_ref, gffn_ref,
                 wr_ref, br_ref, tri_ref,
                 x3_ref, hkeep_ref, hs_ref, meta_ref, gate_ref, cnt_out_ref,
                 cnt_ref, ext_ref, e1_ref, e2_ref, e3_ref, e4_ref, mixed_ref, ybuf, ysems):
    i = pl.program_id(0)
    x2 = _ExpertRows(dest_ref, ys_ref, ybuf, ysems).fetch_combine(x1_ref[...], gate0_ref)
    h = _rms(x2, gmix_ref[...])
    H = POOL_HALO
    lvl_refs = (e1_ref, e2_ref, e3_ref, e4_ref)

    @pl.when(i < SAMPLE_TILE)
    def _():
        hkeep_ref[...] = h[TR - 16:TR]
        @pl.when(i % TILES_PER_SEQ == 0)
        def _():
            ext_ref[0:H, :] = jnp.zeros((H, D_MODEL), F32)
        ext_ref[H:H + TR, :] = h
        src = ext_ref
        for lv in range(4):
            sh = 1 << lv
            lo = 8 * (lv + 1)
            c0 = POOL_GROUP * lv
            dst = lvl_refs[lv]
            dst[lo:H + TR, c0:] = src[lo:H + TR, c0:] + src[lo - sh:H + TR - sh, c0:]
            src = dst
        pos = (i % TILES_PER_SEQ) * TR + lax.broadcasted_iota(I32, (TR, 1), 0)
        for gi, w in enumerate(POOL_WINDOWS):
            cs = slice(gi * POOL_GROUP, (gi + 1) * POOL_GROUP)
            inv = 1.0 / jnp.minimum(w, pos + 1).astype(F32)
            mixed_ref[:, cs] = lvl_refs[gi][H:H + TR, cs] * inv - h[:, cs]
        ext_ref[0:H, :] = h[TR - H:TR]

    @pl.when(i == SAMPLE_TILE)
    def _():
        hs = h[0:DEC_BATCH]
        hs_ref[...] = hs
        mixed_ref[...] = jnp.zeros_like(mixed_ref)
        for gi, w in enumerate(POOL_WINDOWS):
            cs = slice(gi * POOL_GROUP, (gi + 1) * POOL_GROUP)
            win = hs[:, cs]
            for dlt in range(1, w):
                win = win + sp_ref[POOL_KEEP - dlt][:, cs]
            mixed_ref[0:DEC_BATCH, cs] = win * (1.0 / w) - hs[:, cs]

    mixed = mixed_ref[...].astype(BF16)
    outs = [jnp.dot(mixed[:, gi * POOL_GROUP:(gi + 1) * POOL_GROUP], pw_ref[gi], preferred_element_type=F32)
            for gi in range(len(POOL_WINDOWS))]
    x3 = x2 + jnp.concatenate(outs, -1) * ps_ref[...]
    x3_ref[...] = x3
    h2 = _rms(x3, gffn_ref[...])
    _route(h2, wr_ref, br_ref, tri_ref, cnt_ref, meta_ref, gate_ref, cnt_out_ref)


def _const_spec_p(a):
    return pl.BlockSpec(a.shape, lambda i, *_, n=a.ndim: (0,) * n)


def _pool_route(dest, x1, ys, gates0, g_mix, sp_t, pw_bf16, p_scale, g_ffn, wr, br, tri):
    def rows(shape):
        return pl.BlockSpec(shape, lambda i, *_: (i, 0))

    ext = pltpu.VMEM((POOL_HALO + TR, D_MODEL), F32)
    consts = [g_mix, sp_t, pw_bf16, p_scale, g_ffn, wr, br, tri]
    return pl.pallas_call(
        _pool_kernel,
        grid_spec=pltpu.PrefetchScalarGridSpec(
            num_scalar_prefetch=1, grid=(N_TILES,),
            in_specs=[rows((TR, D_MODEL)), pl.BlockSpec(memory_space=pl.ANY), rows((TR, 8))]
                     + [_const_spec_p(a) for a in consts],
            out_specs=[rows((TR, D_MODEL)),
                       pl.BlockSpec((None, 16, D_MODEL),
                                    lambda i, *_: (jnp.minimum(i // TILES_PER_SEQ, BATCH - 1), 0, 0)),
                       pl.BlockSpec((DEC_BATCH, D_MODEL), lambda i, *_: (0, 0)),
                       pl.BlockSpec((1, 8, TR), lambda i, *_: (i, 0, 0)), rows((TR, 8)),
                       pl.BlockSpec((1, 1, LANES), lambda i, *_: (i, 0, 0))],
            scratch_shapes=[pltpu.VMEM((1, LANES), F32), ext, ext, ext, ext, ext,
                            pltpu.VMEM((TR, D_MODEL), F32)] + _ExpertRows.scratch),
        out_shape=[jax.ShapeDtypeStruct((NT, D_MODEL), F32),
                   jax.ShapeDtypeStruct((BATCH, 16, D_MODEL), F32),
                   jax.ShapeDtypeStruct((DEC_BATCH, D_MODEL), F32)] + _ROUTE_OUT_SHAPES,
        compiler_params=_cparams(("arbitrary",)),
        name="pool_route",
    )(dest, x1, ys, gates0, *consts)


def _final_kernel(dest_ref, x_ref, ys_ref, gate_ref, g_ref, yp_ref, ysm_ref, ybuf, ysems):
    i = pl.program_id(0)
    y = _rms(_ExpertRows(dest_ref, ys_ref, ybuf, ysems).fetch_combine(x_ref[...], gate_ref), g_ref[...])

    @pl.when(i < SAMPLE_TILE)
    def _():
        yp_ref[...] = y

    @pl.when(i == SAMPLE_TILE)
    def _():
        ysm_ref[...] = y[0:DEC_BATCH]


def _final(dest, x3, ys, gates, g_final):
    return pl.pallas_call(
        _final_kernel,
        grid_spec=pltpu.PrefetchScalarGridSpec(
            num_scalar_prefetch=1, grid=(N_TILES,),
            in_specs=[pl.BlockSpec((TR, D_MODEL), lambda i, *_: (i, 0)), pl.BlockSpec(memory_space=pl.ANY),
                      pl.BlockSpec((TR, 8), lambda i, *_: (i, 0)), _const_spec_p(g_final)],
            out_specs=[pl.BlockSpec((TR, D_MODEL), lambda i, *_: (jnp.minimum(i, SAMPLE_TILE - 1), 0)),
                       pl.BlockSpec((DEC_BATCH, D_MODEL), lambda i, *_: (0, 0))],
            scratch_shapes=_ExpertRows.scratch),
        out_shape=[jax.ShapeDtypeStruct((NP, D_MODEL), F32), jax.ShapeDtypeStruct((DEC_BATCH, D_MODEL), F32)],
        compiler_params=_cparams(("arbitrary",)),
        name="final_norm",
    )(dest, x3, ys, gates, g_final)


def kernel(x_prompt, x_sample, cache_win_k, cache_win_v, state_hgrn, state_pool, rel_bias, norm_mix, norm_ffn,
           norm_final, w_in, w_out, hgrn_lb, hgrn_gnorm, pool_w, pool_scale, moe_wg_group, moe_bg_group,
           moe_wg_exp, moe_bg_exp, moe_w1, moe_w3, moe_w2):
    xp = x_prompt.reshape(NP, D_MODEL)
    xs_pad = jnp.pad(x_sample.reshape(DEC_BATCH, D_MODEL), ((0, TR - DEC_BATCH), (0, 0)))
    lb = jnp.cumsum(jax.nn.softmax(hgrn_lb.astype(F32), axis=0), axis=0)[0:1]
    gnorm = hgrn_gnorm[0:1].astype(F32)

    q, k, v, hq, hf, hi, hg, k_win, v_win = _inproj(xp, xs_pad, norm_mix[0:1], w_in[0].astype(BF16))
    oa = _attention_prompt(q, k, v, _attn_bias_tables(rel_bias))
    ob, s_prompt = _hgrn_prompt(hq, hf, hi, hg, lb, gnorm)
    oa_s, ob_s, s_sample = _sample_mixers(
        q, k, v, hq, hf, hi, hg,
        cache_win_k[0], cache_win_v[0], state_hgrn[0], _sample_bias_tables(rel_bias), lb, gnorm)
    wr, br, tri = _router_operands(moe_wg_group[0], moe_bg_group[0], moe_wg_exp[0], moe_bg_exp[0])
    x1, meta, gates0, counts = _outproj_route(xp, xs_pad, oa, oa_s, ob, ob_s, w_out[0].astype(BF16),
                                              norm_ffn[0:1], wr, br, tri)
    ys, dest = _moe(x1, norm_ffn[0:1], meta, counts, 0, moe_w1, moe_w3, moe_w2)

    wr, br, tri = _router_operands(moe_wg_group[1], moe_bg_group[1], moe_wg_exp[1], moe_bg_exp[1])
    sp_t = jnp.transpose(state_pool[0], (1, 0, 2))
    x3, hkeep, hsample, meta, gates1, counts = _pool_route(
        dest, x1, ys, gates0, norm_mix[1:2], sp_t, pool_w[0].astype(BF16), pool_scale[0:1], norm_ffn[1:2],
        wr, br, tri)
    ys, dest = _moe(x3, norm_ffn[1:2], meta, counts, 1, moe_w1, moe_w3, moe_w2)
    y_prompt, y_sample = _final(dest, x3, ys, gates1, norm_final.reshape(1, D_MODEL))

    k_s = k[NP:NV].reshape(1, DEC_BATCH, 1, N_HEADS, D_HEAD)
    v_s = v[NP:NV].reshape(1, DEC_BATCH, 1, N_HEADS, D_HEAD)
    return (y_prompt.reshape(BATCH, SEQ, D_MODEL), y_sample.reshape(DEC_BATCH, 1, D_MODEL),
            k_win.reshape(1, BATCH, W_MAX, N_HEADS, D_HEAD), v_win.reshape(1, BATCH, W_MAX, N_HEADS, D_HEAD),
            s_prompt[None], hkeep[:, 16 - POOL_KEEP:][None],
            k_s, v_s, s_sample[None], hsample.reshape(1, DEC_BATCH, 1, D_MODEL))
```

```python
import functools

import numpy as np
import jax
import jax.numpy as jnp
from jax import lax
from jax.experimental import pallas as pl
from jax.experimental.pallas import tpu as pltpu

F32 = jnp.float32
BF16 = jnp.bfloat16
I32 = jnp.int32

D_MODEL = 1024
BATCH = 4
SEQ = 4096
DEC_BATCH = 32
PAST_LEN = 8192
W_MAX = 2048
N_HEADS = 4
D_HEAD = 128
A_WIDTH = N_HEADS * D_HEAD
N_PROJ = 7 * A_WIDTH
DILATED = ((128, 1), (512, 4), (2048, 16))
NUM_BUCKETS = 32
MAX_DISTANCE = 2048
POOL_WINDOWS = (2, 4, 8, 16)
POOL_GROUP = 256
POOL_KEEP = 15
N_GROUPS = 4
EXPERTS_PER_GROUP = 8
N_EXPERTS = 32
D_EXPERT = 512
EPS = 1e-6
NEG = -1e30

LANES = 128
SUBLANES = 8
ROW_VREGS = D_MODEL // LANES
TR = 256
NP = BATCH * SEQ
NV = NP + DEC_BATCH
NT = NP + TR
N_TILES = NT // TR
SAMPLE_TILE = NP // TR
TILES_PER_SEQ = SEQ // TR
TM = 256
N_SLOTS_BLK = (2 * NV + N_EXPERTS * (TM - 1) + TM - 1) // TM
N_SLOTS = N_SLOTS_BLK * TM
QB = 128
ATTN_UNROLL = 8
ATTN_PERIOD = 3 * QB
ROW_DMA_UNROLL = 8
CH = 128
HGRN_HEADS_PER_STEP = 2
N_LEVELS = 7
POOL_HALO = 32
VMEM_LIMIT = 56 * 1024 * 1024


def _cparams(sem=None, vmem=VMEM_LIMIT):
    kw = dict(vmem_limit_bytes=vmem)
    if sem is not None:
        kw["dimension_semantics"] = sem
    return pltpu.CompilerParams(**kw)


def _rms(x, g):
    return x * lax.rsqrt(jnp.mean(x * x, -1, keepdims=True) + EPS) * g


def _sigmoid(x):
    return 1.0 / (1.0 + jnp.exp(-x))


def _silu(x):
    return x * _sigmoid(x)


def _token_tile(i, xp_ref, xs_ref):
    return jnp.where(i == SAMPLE_TILE, xs_ref[...], xp_ref[...])


_PROMPT_ROWS = pl.BlockSpec((TR, D_MODEL), lambda i: (jnp.minimum(i, SAMPLE_TILE - 1), 0))
_SAMPLE_ROWS = pl.BlockSpec((TR, D_MODEL), lambda i: (0, 0))
WIN_TILES = W_MAX // TR


def _window_block(i):
    seq = jnp.minimum(i // TILES_PER_SEQ, BATCH - 1)
    j = jnp.clip(i % TILES_PER_SEQ - (TILES_PER_SEQ - WIN_TILES), 0, WIN_TILES - 1)
    return jnp.where(i >= SAMPLE_TILE, BATCH * WIN_TILES - 1, seq * WIN_TILES + j)


def _inproj_kernel(xp_ref, xs_ref, g_ref, w_ref, *out_refs):
    i = pl.program_id(0)
    h = _rms(_token_tile(i, xp_ref, xs_ref), g_ref[...])
    p = jnp.dot(h.astype(BF16), w_ref[...], preferred_element_type=F32)
    for n, o_ref in enumerate(out_refs[:7]):
        o_ref[...] = p[:, n * A_WIDTH:(n + 1) * A_WIDTH]

    @pl.when(jnp.logical_and(i < SAMPLE_TILE, i % TILES_PER_SEQ >= TILES_PER_SEQ - WIN_TILES))
    def _():
        for n, o_ref in ((1, out_refs[7]), (2, out_refs[8])):
            for h_i in range(N_HEADS):
                c0 = n * A_WIDTH + h_i * D_HEAD
                o_ref[pl.ds(h_i, TR, stride=N_HEADS), :] = p[:, c0:c0 + D_HEAD]


def _inproj(xp, xs_pad, g, w_bf16):
    out_sds = ([jax.ShapeDtypeStruct((NT, A_WIDTH), F32)] * 7
               + [jax.ShapeDtypeStruct((BATCH * W_MAX * N_HEADS, D_HEAD), F32)] * 2)
    win = pl.BlockSpec((TR * N_HEADS, D_HEAD), lambda i: (_window_block(i), 0))
    return pl.pallas_call(
        _inproj_kernel,
        grid=(N_TILES,),
        in_specs=[_PROMPT_ROWS, _SAMPLE_ROWS,
                  pl.BlockSpec((1, D_MODEL), lambda i: (0, 0)),
                  pl.BlockSpec((D_MODEL, N_PROJ), lambda i: (0, 0))],
        out_specs=[pl.BlockSpec((TR, A_WIDTH), lambda i: (i, 0))] * 7 + [win, win],
        out_shape=out_sds,
        compiler_params=_cparams(("arbitrary",)),
        name="inproj",
    )(xp, xs_pad, g, w_bf16)


def _t5_bucket(dist):
    max_exact = NUM_BUCKETS // 2
    d = np.asarray(dist)
    large = max_exact + np.floor(np.log(np.maximum(d, 1) / max_exact)
                                 / np.log(MAX_DISTANCE / max_exact) * (NUM_BUCKETS - max_exact)).astype(np.int32)
    large = np.minimum(large, NUM_BUCKETS - 1)
    return np.where(d < max_exact, d, large).astype(np.int32)


def _attn_bias_tables(rel_bias):
    period = ATTN_PERIOD
    m = np.arange(period)
    u = np.where(m < 2 * QB, m, m - period)
    pick = np.zeros((len(DILATED), 2, period, NUM_BUCKETS), np.float32)
    mask = np.zeros((len(DILATED), 2, period, 1), np.float32)
    for bi, (w, d) in enumerate(DILATED):
        nk = w // d
        for vi, off in enumerate((0, QB)):
            j = off - u
            ok = (j >= 0) & (j <= nk)
            pick[bi, vi, m[ok], _t5_bucket(d * j[ok])] = 1.0
            mask[bi, vi, ~ok, 0] = NEG
    vec = jnp.einsum("bvmk,kh->hbvm", pick, rel_bias.astype(F32), precision=lax.Precision.HIGHEST)
    vec = vec + jnp.transpose(mask, (3, 0, 1, 2))
    return vec[:, :, :, None, :]


def _attn_kernel(q_ref, k_ref, v_ref, vec_ref, o_ref,
                 qd, kd, vd, ud, md, sd, u_acc, m_acc, s_acc, bias_ref):
    scale = D_HEAD ** -0.5
    for bi in range(len(DILATED)):
        for vi in range(2):
            rows = jnp.broadcast_to(vec_ref[bi, vi], (QB, ATTN_PERIOD))
            bias_ref[bi, vi] = pltpu.roll(rows, 0, 1, stride=1, stride_axis=0)[:, :2 * QB]

    def block_stats(bi, t, nb):
        has_prev = jnp.minimum(t % nb, 1)
        q0 = pl.multiple_of(t * QB, QB)
        k0 = pl.multiple_of((t - has_prev) * QB, QB)
        qb = qd[pl.ds(q0, QB), :]
        kb = kd[pl.ds(k0, 2 * QB), :]
        vb = vd[pl.ds(k0, 2 * QB), :]
        s = lax.dot_general(qb, kb, (((1,), (1,)), ((), ())), preferred_element_type=F32)
        s = s + bias_ref[bi, has_prev]
        mb = jnp.max(s, -1, keepdims=True)
        p = jnp.exp(s - mb)
        sb = jnp.sum(p, -1, keepdims=True)
        u = jnp.dot(p.astype(BF16), vb, preferred_element_type=F32)
        return q0, mb, sb, u

    for bi in (2, 1, 0):
        d = DILATED[bi][1]
        cl = SEQ // d
        nb = cl // QB
        for r in range(d):
            src = pl.ds(r, cl, stride=d) if d > 1 else pl.ds(0, cl)
            dst = pl.ds(r * cl, cl)
            qd[dst, :] = (q_ref[src, :] * scale).astype(BF16)
            kd[dst, :] = k_ref[src, :].astype(BF16)
            vd[dst, :] = v_ref[src, :].astype(BF16)

        if d > 1:
            def body(g, carry, bi=bi, nb=nb):
                for un in range(ATTN_UNROLL):
                    q0, mb, sb, u = block_stats(bi, g * ATTN_UNROLL + un, nb)
                    ud[pl.ds(q0, QB), :] = u
                    md[pl.ds(q0, QB), :] = jnp.broadcast_to(mb, (QB, D_HEAD))
                    sd[pl.ds(q0, QB), :] = jnp.broadcast_to(sb, (QB, D_HEAD))
                return carry
            lax.fori_loop(0, SEQ // QB // ATTN_UNROLL, body, 0)
            for r in range(d):
                pos = pl.ds(r, cl, stride=d)
                cls = pl.ds(r * cl, cl)
                if bi == 2:
                    u_acc[pos, :] = ud[cls, :]
                    m_acc[pos, :] = md[cls, :]
                    s_acc[pos, :] = sd[cls, :]
                else:
                    m_old = m_acc[pos, :]
                    m_blk = md[cls, :]
                    m_new = jnp.maximum(m_old, m_blk)
                    a = jnp.exp(m_old - m_new)
                    b = jnp.exp(m_blk - m_new)
                    u_acc[pos, :] = a * u_acc[pos, :] + b * ud[cls, :]
                    s_acc[pos, :] = a * s_acc[pos, :] + b * sd[cls, :]
                    m_acc[pos, :] = m_new
        else:
            def body(g, carry, bi=bi, nb=nb):
                for un in range(ATTN_UNROLL):
                    q0, mb, sb, u = block_stats(bi, g * ATTN_UNROLL + un, nb)
                    rows = pl.ds(q0, QB)
                    m_old = m_acc[rows, :]
                    m_new = jnp.maximum(m_old, mb)
                    a = jnp.exp(m_old - m_new)
                    b = jnp.exp(mb - m_new)
                    num = a * u_acc[rows, :] + b * u
                    den = a * s_acc[rows, :] + b * sb
                    o_ref[rows, :] = (num / den).astype(o_ref.dtype)
                return carry
            lax.fori_loop(0, SEQ // QB // ATTN_UNROLL, body, 0)


def _attention_prompt(q, k, v, bias_tabs):
    blk = pl.BlockSpec((SEQ, D_HEAD), lambda b, h: (b, h))
    return pl.pallas_call(
        _attn_kernel,
        grid=(BATCH, N_HEADS),
        in_specs=[blk, blk, blk,
                  pl.BlockSpec((None, 3, 2, 1, ATTN_PERIOD), lambda b, h: (h, 0, 0, 0, 0))],
        out_specs=blk,
        out_shape=jax.ShapeDtypeStruct((NP, A_WIDTH), BF16),
        scratch_shapes=[pltpu.VMEM((SEQ, D_HEAD), BF16)] * 3
                       + [pltpu.VMEM((SEQ, D_HEAD), F32)] * 6
                       + [pltpu.VMEM((len(DILATED), 2, QB, 2 * QB), F32)],
        compiler_params=_cparams(("arbitrary", "arbitrary")),
        name="attn_prompt",
    )(q, k, v, bias_tabs)


def _hgrn_tables():
    t = np.arange(CH)
    u = np.arange(CH)
    sums_q = np.zeros((1 + N_LEVELS, CH, CH), np.float32)
    sums_k = np.zeros((2 + N_LEVELS, CH, CH), np.float32)
    sums_q[0] = (u[None, :] <= t[:, None])
    sums_k[0] = (u[None, :] > t[:, None])
    sums_k[1 + N_LEVELS] = 1.0
    pair = np.zeros((N_LEVELS, CH, CH), np.float32)
    for l in range(N_LEVELS):
        h = CH >> (l + 1)
        is_q = (t // h) % 2 == 1
        half_start = (t // h) * h
        half_end = half_start + h
        sel_q = (u[None, :] >= half_start[:, None]) & (u[None, :] <= t[:, None])
        sel_k = (u[None, :] > t[:, None]) & (u[None, :] < half_end[:, None])
        sums_q[1 + l] = sel_q & is_q[:, None]
        sums_k[1 + l] = sel_k & ~is_q[:, None]
        same = (t[:, None] // (2 * h)) == (t[None, :] // (2 * h))
        pair[l] = same & is_q[:, None] & (~is_q)[None, :]
    sums_kt = np.transpose(sums_k, (2, 0, 1)).reshape(CH, (2 + N_LEVELS) * CH)
    sums_q = sums_q.reshape((1 + N_LEVELS) * CH, CH)
    return (jnp.asarray(np.concatenate([sums_q, sums_q], 1), BF16),
            jnp.asarray(np.concatenate([sums_kt, sums_kt], 0), BF16),
            jnp.asarray(pair), jnp.asarray(np.eye(CH, dtype=np.float32)))


def _split_bf16(x):
    hi = x.astype(BF16)
    return hi, (x - hi.astype(F32)).astype(BF16)


def _hgrn_kernel(q_ref, f_ref, i_ref, g_ref, lb_ref, gn_ref, sq_ref, skt_ref, pair_ref, eye_ref, o_ref, s_ref):
    gn = gn_ref[...]

    def chunk(c, st, hh):
        rows = pl.ds(pl.multiple_of(c * CH, CH), CH)
        cols = slice(hh * D_HEAD, (hh + 1) * D_HEAD)
        lb = lb_ref[:, cols]
        q = _silu(q_ref[rows, cols])
        f = lb + (1.0 - lb) * _sigmoid(f_ref[rows, cols])
        lf = jnp.log(f)
        k = 1.0 - f
        v16 = i_ref[rows, cols].astype(BF16)
        kt = k.T
        exq = jnp.exp(jnp.dot(sq_ref[...], jnp.concatenate(_split_bf16(lf), 0),
                              preferred_element_type=F32))
        exk = jnp.exp(jnp.dot(jnp.concatenate(_split_bf16(lf.T), 1), skt_ref[...],
                              preferred_element_type=F32))
        inter = jnp.dot((q * exq[0:CH]).astype(BF16), st.astype(BF16), preferred_element_type=F32)
        sc = eye_ref[...] * jnp.sum(q * k, -1, keepdims=True)
        for l in range(N_LEVELS):
            ql = (q * exq[(1 + l) * CH:(2 + l) * CH]).astype(BF16)
            klt = (kt * exk[:, (1 + l) * CH:(2 + l) * CH]).astype(BF16)
            sc = sc + pair_ref[l] * jnp.dot(ql, klt, preferred_element_type=F32)
        o = inter + jnp.dot(sc.astype(BF16), v16, preferred_element_type=F32)
        st_new = (st * exk[:, (1 + N_LEVELS) * CH:]
                  + jnp.dot((kt * exk[:, 0:CH]).astype(BF16), v16, preferred_element_type=F32))
        o = _rms(o, gn) * _silu(g_ref[rows, cols])
        o_ref[rows, cols] = o.astype(o_ref.dtype)
        return st_new

    def step(c, states):
        return tuple(chunk(c, st, hh) for hh, st in enumerate(states))

    zero = jnp.zeros((D_HEAD, D_HEAD), F32)
    states = lax.fori_loop(0, SEQ // CH, step, (zero,) * HGRN_HEADS_PER_STEP)
    for hh, st in enumerate(states):
        s_ref[hh] = st


def _hgrn_prompt(hq, hf, hi, hg, lb, gnorm):
    tables = _hgrn_tables()
    width = HGRN_HEADS_PER_STEP * D_HEAD
    blk = pl.BlockSpec((SEQ, width), lambda b, h: (b, h))

    def full(a):
        return pl.BlockSpec(a.shape, lambda b, h, n=a.ndim: (0,) * n)

    return pl.pallas_call(
        _hgrn_kernel,
        grid=(BATCH, N_HEADS // HGRN_HEADS_PER_STEP),
        in_specs=[blk, blk, blk, blk,
                  pl.BlockSpec((1, width), lambda b, h: (0, h)),
                  pl.BlockSpec((1, D_HEAD), lambda b, h: (0, 0))] + [full(a) for a in tables],
        out_specs=[blk, pl.BlockSpec((None, HGRN_HEADS_PER_STEP, D_HEAD, D_HEAD), lambda b, h: (b, h, 0, 0))],
        out_shape=[jax.ShapeDtypeStruct((NP, A_WIDTH), BF16),
                   jax.ShapeDtypeStruct((BATCH, N_HEADS, D_HEAD, D_HEAD), F32)],
        compiler_params=_cparams(("arbitrary", "arbitrary")),
        name="hgrn_prompt",
    )(hq, hf, hi, hg, lb, gnorm, *tables)


def _sample_bias_tables(rel_bias):
    j = np.concatenate([QB - np.arange(QB), [0]])
    pick = np.zeros((len(DILATED), QB + 1, NUM_BUCKETS), np.float32)
    for bi, (w, d) in enumerate(DILATED):
        pick[bi, np.arange(QB + 1), _t5_bucket(d * j)] = 1.0
    return jnp.einsum("bjk,kh->bhj", pick, rel_bias.astype(F32), precision=lax.Precision.HIGHEST)[..., None]


def _bf16_round(x):
    return x.astype(BF16).astype(F32)


def _col(row, eye):
    return jnp.sum(eye * row, -1, keepdims=True)


NEAR_POS = 4 * QB
FAR_STEPS = (W_MAX - NEAR_POS) // 16


def _cached_rows(near_ref, far_ref, d, h):
    def near(first_pos, n, step):
        return near_ref[pl.ds((first_pos - (W_MAX - NEAR_POS)) * N_HEADS + h, n, stride=step * N_HEADS), :]

    if QB * d <= NEAR_POS:
        return near(W_MAX - QB * d, QB, d)
    return jnp.concatenate([far_ref[:, h, :], near(W_MAX - NEAR_POS, QB - FAR_STEPS, d)], 0)


def _sample_kernel(q_ref, k_ref, v_ref, hq_ref, hf_ref, hi_ref, hg_ref, kn_ref, kf_ref, vn_ref, vf_ref, s0_ref,
                   bias_ref, lb_ref, gn_ref, eye_ref, oa_ref, ob_ref, s_ref, oa_acc, ob_acc):
    b = pl.program_id(0)
    row = pl.ds(b, 1)
    scale = D_HEAD ** -0.5
    q = _bf16_round(q_ref[row, :] * scale)
    kn = _bf16_round(k_ref[row, :])
    vn = _bf16_round(v_ref[row, :])
    eye = eye_ref[...]

    stats = []
    for bi, (w, d) in enumerate(DILATED):
        per_head = []
        for h in range(N_HEADS):
            hs = slice(h * D_HEAD, (h + 1) * D_HEAD)
            kb = _bf16_round(_cached_rows(kn_ref, kf_ref, d, h))
            vb = _bf16_round(_cached_rows(vn_ref, vf_ref, d, h))
            s = jnp.sum(kb * q[:, hs], -1, keepdims=True) + bias_ref[bi, h, 0:QB]
            s0 = jnp.sum(q[:, hs] * kn[:, hs], -1, keepdims=True) + bias_ref[bi, h, QB:QB + 1]
            m = jnp.maximum(jnp.max(s, 0, keepdims=True), s0)
            p = jnp.exp(s - m)
            p0 = jnp.exp(s0 - m)
            ssum = jnp.sum(p, 0, keepdims=True) + p0
            u = jnp.sum(_bf16_round(p) * vb, 0, keepdims=True) + _bf16_round(p0) * vn[:, hs]
            per_head.append((m, ssum, u))
        stats.append(per_head)
    outs = []
    for h in range(N_HEADS):
        m_all = functools.reduce(jnp.maximum, [stats[bi][h][0] for bi in range(3)])
        num = 0.0
        den = 0.0
        for bi in range(3):
            m, ssum, u = stats[bi][h]
            c = jnp.exp(m - m_all)
            num = num + c * u
            den = den + c * ssum
        outs.append(num / den)
    oa_acc[row, :] = jnp.concatenate(outs, -1)

    qh = _silu(hq_ref[row, :])
    lb = lb_ref[...]
    f = lb + (1.0 - lb) * _sigmoid(hf_ref[row, :])
    vi = hi_ref[row, :]
    gate = _silu(hg_ref[row, :])
    gn = gn_ref[...]
    obs = []
    for h in range(N_HEADS):
        hs = slice(h * D_HEAD, (h + 1) * D_HEAD)
        f_col = _col(f[:, hs], eye)
        q_col = _col(qh[:, hs], eye)
        s_old = s0_ref[h]
        s_ref[h] = f_col * s_old + (1.0 - f_col) * vi[:, hs]
        inter = jnp.sum(_bf16_round(q_col * f_col) * _bf16_round(s_old), 0, keepdims=True)
        qk = jnp.sum(qh[:, hs] * (1.0 - f[:, hs]), -1, keepdims=True)
        o = inter + qk * vi[:, hs]
        obs.append(_rms(o, gn) * gate[:, hs])
    ob_acc[row, :] = jnp.concatenate(obs, -1)

    @pl.when(b == DEC_BATCH - 1)
    def _():
        pad = jnp.zeros((TR - DEC_BATCH, A_WIDTH), F32)
        oa_ref[...] = jnp.concatenate([oa_acc[...], pad], 0).astype(oa_ref.dtype)
        ob_ref[...] = jnp.concatenate([ob_acc[...], pad], 0).astype(ob_ref.dtype)


def _sample_mixers(q, k, v, hq, hf, hi, hg, cache_k, cache_v, state, bias_s, lb, gnorm):
    tile = pl.BlockSpec((TR, A_WIDTH), lambda b: (SAMPLE_TILE, 0))
    out_tile = pl.BlockSpec((TR, A_WIDTH), lambda b: (0, 0))
    near = pl.BlockSpec((None, NEAR_POS * N_HEADS, D_HEAD), lambda b: (b, W_MAX // NEAR_POS - 1, 0))
    far = pl.BlockSpec((None, FAR_STEPS, SUBLANES, D_HEAD), lambda b: (b, 0, 0, 0))
    rows = (DEC_BATCH, W_MAX * N_HEADS, D_HEAD)
    groups = (DEC_BATCH, W_MAX // 16, 16 * N_HEADS, D_HEAD)
    eye = jnp.eye(D_HEAD, dtype=F32)

    return pl.pallas_call(
        _sample_kernel,
        grid=(DEC_BATCH,),
        in_specs=[tile, tile, tile, tile, tile, tile, tile, near, far, near, far,
                  pl.BlockSpec((None, N_HEADS, D_HEAD, D_HEAD), lambda b: (b, 0, 0, 0)),
                  pl.BlockSpec(bias_s.shape, lambda b: (0, 0, 0, 0)),
                  pl.BlockSpec((1, A_WIDTH), lambda b: (0, 0)),
                  pl.BlockSpec((1, D_HEAD), lambda b: (0, 0)),
                  pl.BlockSpec((D_HEAD, D_HEAD), lambda b: (0, 0))],
        out_specs=[out_tile, out_tile,
                   pl.BlockSpec((None, N_HEADS, D_HEAD, D_HEAD), lambda b: (b, 0, 0, 0))],
        out_shape=[jax.ShapeDtypeStruct((TR, A_WIDTH), BF16), jax.ShapeDtypeStruct((TR, A_WIDTH), BF16),
                   jax.ShapeDtypeStruct((DEC_BATCH, N_HEADS, D_HEAD, D_HEAD), F32)],
        scratch_shapes=[pltpu.VMEM((DEC_BATCH, A_WIDTH), F32)] * 2,
        compiler_params=_cparams(("arbitrary",)),
        name="sample_mixers",
    )(q, k, v, hq, hf, hi, hg, cache_k.reshape(rows), cache_k.reshape(groups),
      cache_v.reshape(rows), cache_v.reshape(groups), state, bias_s, lb, gnorm, eye)


def _store_rows_as_tiles(ref, val):
    n = val.shape[0]
    for j in range(ROW_VREGS):
        ref[pl.ds(j, n, stride=ROW_VREGS), :] = val[:, j * LANES:(j + 1) * LANES]


def _load_rows_from_tiles(ref, n):
    return jnp.concatenate([ref[pl.ds(j, n, stride=ROW_VREGS), :] for j in range(ROW_VREGS)], -1)


def _route(h2, wr_ref, br_ref, tri_ref, cnt_ref, meta_ref, gate_ref, cnt_out_ref):
    i = pl.program_id(0)

    @pl.when(i == 0)
    def _():
        cnt_ref[...] = jnp.zeros_like(cnt_ref)

    logits = jnp.dot(h2.astype(BF16), wr_ref[...], preferred_element_type=F32) + br_ref[...]
    lane = lax.broadcasted_iota(I32, (TR, LANES), 1).astype(F32)
    big = float(1 << 20)
    is_g = lane < N_GROUPS
    gl = jnp.where(is_g, logits, NEG)
    gmax = jnp.max(gl, -1, keepdims=True)
    gsel = jnp.min(jnp.where(gl == gmax, lane, big), -1, keepdims=True)
    pg = 1.0 / jnp.sum(jnp.where(is_g, jnp.exp(gl - gmax), 0.0), -1, keepdims=True)
    lo = N_GROUPS + EXPERTS_PER_GROUP * gsel
    in_grp = jnp.logical_and(lane >= lo, lane < lo + EXPERTS_PER_GROUP)
    el = jnp.where(in_grp, logits, NEG)
    m1 = jnp.max(el, -1, keepdims=True)
    i1 = jnp.min(jnp.where(el == m1, lane, big), -1, keepdims=True)
    el2 = jnp.where(lane == i1, NEG, el)
    m2 = jnp.max(el2, -1, keepdims=True)
    i2 = jnp.min(jnp.where(el2 == m2, lane, big), -1, keepdims=True)
    r = jnp.exp(m2 - m1)
    g1 = pg / (1.0 + r)
    g2 = pg * r / (1.0 + r)
    e1 = i1 - N_GROUPS
    e2 = i2 - N_GROUPS

    tok = i * TR + lax.broadcasted_iota(I32, (TR, 1), 0)
    valid = tok < NV
    oh1 = jnp.logical_and(lane == e1, valid)
    oh2 = jnp.logical_and(lane == e2, valid)
    oh = jnp.where(jnp.logical_or(oh1, oh2), 1.0, 0.0)
    before = jnp.dot(tri_ref[...], oh.astype(BF16), preferred_element_type=F32) + cnt_ref[...]
    rank1 = jnp.sum(jnp.where(oh1, before, 0.0), -1, keepdims=True)
    rank2 = jnp.sum(jnp.where(oh2, before, 0.0), -1, keepdims=True)
    tile_cnt = jnp.sum(oh, 0, keepdims=True)
    cnt_ref[...] = cnt_ref[...] + tile_cnt

    eye = jnp.where(lax.broadcasted_iota(I32, (TR, TR), 0) == lax.broadcasted_iota(I32, (TR, TR), 1), 1.0, 0.0)
    rows = [jnp.sum(eye * col, 0, keepdims=True) for col in (e1, e2, rank1, rank2)]
    meta_ref[0] = jnp.concatenate(rows + [jnp.zeros((4, TR), F32)], 0).astype(I32)
    gates = jnp.where(lane == 0, g1, jnp.where(lane == 1, g2, 0.0))
    gates = jnp.where(valid, gates, 0.0)
    gate_ref[...] = gates[:, 0:8]
    cnt_out_ref[0] = tile_cnt


def _router_operands(wg_group, bg_group, wg_exp, bg_exp):
    wr = jnp.zeros((D_MODEL, LANES), F32)
    wr = wr.at[:, 0:N_GROUPS].set(wg_group.astype(F32)).at[:, N_GROUPS:N_GROUPS + N_EXPERTS].set(wg_exp.astype(F32))
    br = jnp.zeros((1, LANES), F32)
    br = br.at[0, 0:N_GROUPS].set(bg_group.astype(F32)).at[0, N_GROUPS:N_GROUPS + N_EXPERTS].set(bg_exp.astype(F32))
    tri = jnp.asarray(np.tril(np.ones((TR, TR), np.float32), -1), BF16)
    return wr.astype(BF16), br, tri


_ROUTE_OUT_SPECS = [pl.BlockSpec((1, 8, TR), lambda i: (i, 0, 0)),
                    pl.BlockSpec((TR, 8), lambda i: (i, 0)),
                    pl.BlockSpec((1, 1, LANES), lambda i: (i, 0, 0))]
_ROUTE_OUT_SHAPES = [jax.ShapeDtypeStruct((N_TILES, 8, TR), I32), jax.ShapeDtypeStruct((NT, 8), F32),
                     jax.ShapeDtypeStruct((N_TILES, 1, LANES), F32)]


def _const_spec(a):
    return pl.BlockSpec(a.shape, lambda i, n=a.ndim: (0,) * n)


def _outproj_kernel(xp_ref, xs_ref, oa_ref, oas_ref, ob_ref, obs_ref, w_ref, g_ref, wr_ref, br_ref, tri_ref,
                    x1_ref, meta_ref, gate_ref, cnt_out_ref, cnt_ref):
    i = pl.program_id(0)
    x = _token_tile(i, xp_ref, xs_ref)
    mixed = jnp.concatenate([_token_tile(i, oa_ref, oas_ref), _token_tile(i, ob_ref, obs_ref)], -1)
    x1 = x + jnp.dot(mixed, w_ref[...], preferred_element_type=F32)
    x1_ref[...] = x1
    h2 = _rms(x1, g_ref[...])
    _route(h2, wr_ref, br_ref, tri_ref, cnt_ref, meta_ref, gate_ref, cnt_out_ref)


def _outproj_route(xp, xs_pad, oa, oa_s, ob, ob_s, w_out_bf16, g_ffn, wr, br, tri):
    row = pl.BlockSpec((TR, D_MODEL), lambda i: (i, 0))
    half = pl.BlockSpec((TR, A_WIDTH), lambda i: (jnp.minimum(i, SAMPLE_TILE - 1), 0))
    half_s = pl.BlockSpec((TR, A_WIDTH), lambda i: (0, 0))
    return pl.pallas_call(
        _outproj_kernel,
        grid=(N_TILES,),
        in_specs=[_PROMPT_ROWS, _SAMPLE_ROWS, half, half_s, half, half_s,
                  _const_spec(w_out_bf16), _const_spec(g_ffn),
                  _const_spec(wr), _const_spec(br), _const_spec(tri)],
        out_specs=[row] + _ROUTE_OUT_SPECS,
        out_shape=[jax.ShapeDtypeStruct((NT, D_MODEL), F32)] + _ROUTE_OUT_SHAPES,
        scratch_shapes=[pltpu.VMEM((1, LANES), F32)],
        compiler_params=_cparams(("arbitrary",)),
        name="outproj_route",
    )(xp, xs_pad, oa, oa_s, ob, ob_s, w_out_bf16, g_ffn, wr, br, tri)


def _plan(meta, tile_counts):
    tcnt = tile_counts[:, 0, :N_EXPERTS].astype(I32)
    cnt = jnp.sum(tcnt, 0)
    nblk = (cnt + TM - 1) // TM
    blk_end = jnp.cumsum(nblk)
    pstart = (blk_end - nblk) * TM
    n_used = blk_end[-1]
    blk = jnp.minimum(jnp.arange(N_SLOTS_BLK, dtype=I32), n_used - 1)
    blk_exp = jnp.sum((blk[:, None] >= blk_end[None, :]).astype(I32), -1)
    blk_exp = jnp.minimum(blk_exp, N_EXPERTS - 1)
    eid = meta[:, 0:2, :]
    experts = jnp.arange(N_EXPERTS, dtype=I32)[:, None, None, None]
    start_of = jnp.sum(jnp.where(eid[None] == experts, pstart[:, None, None, None], 0), 0)
    tok = (jnp.arange(N_TILES, dtype=I32) * TR)[:, None, None] + jnp.arange(TR, dtype=I32)
    dest = jnp.where(tok < NV, start_of + meta[:, 2:4, :], 0).astype(I32).reshape(-1)
    pad_lo = pstart + cnt
    pad_hi = pstart + nblk * TM
    return dest, blk_exp, n_used.reshape(1).astype(I32), pad_lo.astype(I32), pad_hi.astype(I32)


def _row(ref, r):
    return ref.at[pl.ds(pl.multiple_of(r * ROW_VREGS, ROW_VREGS), ROW_VREGS), :]


def _dest_index(tile, kk, t):
    return (2 * tile + kk) * TR + t


PACK_ROWS = D_MODEL // (2 * LANES)
U32 = jnp.uint32


def _pack_bf16_rows(h):
    bits = pltpu.bitcast(h.astype(BF16).astype(F32), U32)
    half = D_MODEL // 2
    return [bits[:, j * LANES:(j + 1) * LANES]
            | lax.shift_right_logical(bits[:, half + j * LANES:half + (j + 1) * LANES], U32(16))
            for j in range(PACK_ROWS)]


def _unpack_bf16_rows(words):
    hi = [pltpu.bitcast(w & U32(0xFFFF0000), F32) for w in words]
    lo = [pltpu.bitcast(lax.shift_left(w, U32(16)), F32) for w in words]
    return jnp.concatenate(hi + lo, -1).astype(BF16)


def _dispatch_kernel(dest_ref, plo_ref, phi_ref, nused_ref, x_ref, g_ref, xs_ref, stage, zero_ref, sems, zsem):
    i = pl.program_id(0)
    last = pl.num_programs(0) - 1
    slot = i % 2
    for j, w in enumerate(_pack_bf16_rows(_rms(x_ref[...], g_ref[...]))):
        stage[slot, :, j, :] = w

    def wait_tile(n_tok, s):
        for _ in range(2):
            pltpu.make_async_copy(stage.at[s, pl.ds(0, n_tok)], xs_ref.at[pl.ds(0, n_tok)], sems.at[s]).wait()

    def scatter(n_tok):
        def issue(t, carry):
            for kk in range(2):
                pltpu.make_async_copy(stage.at[slot, t], xs_ref.at[dest_ref[_dest_index(i, kk, t)]],
                                      sems.at[slot]).start(priority=kk)
            return carry
        lax.fori_loop(0, n_tok, issue, 0)

    @pl.when(i < last)
    def _():
        scatter(TR)

    @pl.when(i > 0)
    def _():
        wait_tile(TR, 1 - slot)

    @pl.when(i == last)
    def _():
        n_last = NV - (N_TILES - 1) * TR
        scatter(n_last)
        wait_tile(n_last, slot)
        zero_ref[...] = jnp.zeros_like(zero_ref)

        zero_row = zero_ref.at[0]

        def per_expert(e, carry):
            def fill(p, c2):
                pltpu.make_async_copy(zero_row, xs_ref.at[p], zsem).start()
                return c2
            lax.fori_loop(plo_ref[e], phi_ref[e], fill, 0)

            def fill_wait(p, c2):
                pltpu.make_async_copy(zero_row, xs_ref.at[0], zsem).wait()
                return c2
            lax.fori_loop(plo_ref[e], phi_ref[e], fill_wait, 0)
            return carry
        lax.fori_loop(0, N_EXPERTS, per_expert, 0)

        def block_of(nb):
            return xs_ref.at[pl.ds(pl.multiple_of(nb * TM, TM), TM)]

        def fill_blk(nb, carry):
            pltpu.make_async_copy(zero_ref, block_of(nb), zsem).start()
            return carry
        lax.fori_loop(nused_ref[0], N_SLOTS_BLK, fill_blk, 0)

        def fill_blk_wait(nb, carry):
            pltpu.make_async_copy(zero_ref, block_of(0), zsem).wait()
            return carry
        lax.fori_loop(nused_ref[0], N_SLOTS_BLK, fill_blk_wait, 0)


def _dispatch(x, g_ffn, dest, pad_lo, pad_hi, n_used):
    return pl.pallas_call(
        _dispatch_kernel,
        grid_spec=pltpu.PrefetchScalarGridSpec(
            num_scalar_prefetch=4, grid=(N_TILES,),
            in_specs=[pl.BlockSpec((TR, D_MODEL), lambda i, *_: (i, 0)),
                      pl.BlockSpec((1, D_MODEL), lambda i, *_: (0, 0))],
            out_specs=pl.BlockSpec(memory_space=pl.ANY),
            scratch_shapes=[pltpu.VMEM((2, TR, PACK_ROWS, LANES), U32),
                            pltpu.VMEM((TM, PACK_ROWS, LANES), U32),
                            pltpu.SemaphoreType.DMA((2,)), pltpu.SemaphoreType.DMA(())]),
        out_shape=jax.ShapeDtypeStruct((N_SLOTS, PACK_ROWS, LANES), U32),
        compiler_params=_cparams(("arbitrary",)),
        name="dispatch",
    )(dest, pad_lo, pad_hi, n_used, x, g_ffn)


class _ExpertRows:
    scratch = [pltpu.VMEM((2, 2 * TR * ROW_VREGS, LANES), F32), pltpu.SemaphoreType.DMA((2,))]

    def __init__(self, dest_ref, ys_ref, buf, sems):
        self.dest_ref, self.ys_ref, self.buf, self.sems = dest_ref, ys_ref, buf, sems

    def _copy(self, src_row, slot, dst_row):
        dst = self.buf.at[slot, pl.ds(pl.multiple_of(dst_row * ROW_VREGS, ROW_VREGS), ROW_VREGS), :]
        return pltpu.make_async_copy(_row(self.ys_ref, src_row), dst, self.sems.at[slot])

    def start(self, tile):
        slot = tile % 2

        def issue(g, carry):
            for un in range(ROW_DMA_UNROLL):
                t = g * ROW_DMA_UNROLL + un
                for kk in range(2):
                    self._copy(self.dest_ref[_dest_index(tile, kk, t)], slot, kk * TR + t).start(priority=kk)
            return carry
        lax.fori_loop(0, TR // ROW_DMA_UNROLL, issue, 0)

    def wait(self, tile):
        slot = tile % 2
        pltpu.make_async_copy(self.ys_ref.at[pl.ds(0, 2 * TR * ROW_VREGS), :], self.buf.at[slot],
                              self.sems.at[slot]).wait()

    def combine(self, tile, x, gate_ref):
        rows = self.buf.at[tile % 2]
        g = gate_ref[...]
        for kk in range(2):
            y = jnp.concatenate([rows[pl.ds(kk * TR * ROW_VREGS + j, TR, stride=ROW_VREGS), :]
                                 for j in range(ROW_VREGS)], -1)
            x = x + g[:, kk:kk + 1] * y
        return x

    def fetch_combine(self, x, gate_ref):
        i = pl.program_id(0)

        @pl.when(i == 0)
        def _():
            self.start(i)

        @pl.when(i + 1 < pl.num_programs(0))
        def _():
            self.start(i + 1)

        self.wait(i)
        return self.combine(i, x, gate_ref)


def _ffn_kernel(be_ref, nused_ref, xs_ref, w1_ref, w3_ref, w2_ref, y_ref, w1b, w3b, w2b):
    nb = pl.program_id(0)

    @pl.when(nb < nused_ref[0])
    def _():
        prev = be_ref[jnp.maximum(nb - 1, 0)]
        fresh = jnp.logical_or(nb == 0, be_ref[nb] != prev)

        @pl.when(fresh)
        def _():
            w1b[...] = w1_ref[...].astype(BF16)
            w3b[...] = w3_ref[...].astype(BF16)
            w2b[...] = w2_ref[...].astype(BF16)

        x = _unpack_bf16_rows([xs_ref[:, j, :] for j in range(PACK_ROWS)])
        a = jnp.dot(x, w1b[...], preferred_element_type=F32)
        b = jnp.dot(x, w3b[...], preferred_element_type=F32)
        mid = (_silu(a) * b).astype(BF16)
        y = jnp.dot(mid, w2b[...], preferred_element_type=F32)
        _store_rows_as_tiles(y_ref, y)

    @pl.when(nb >= nused_ref[0])
    def _():
        y_ref[...] = jnp.zeros_like(y_ref)


def _experts(xs, blk_exp, n_used, layer, w1, w3, w2):
    rows = pl.BlockSpec((TM * ROW_VREGS, LANES), lambda nb, be, nu: (nb, 0))
    rows_in = pl.BlockSpec((TM, PACK_ROWS, LANES), lambda nb, be, nu: (nb, 0, 0))

    def wspec(a, b):
        return pl.BlockSpec((None, None, a, b), lambda nb, be, nu: (layer, be[nb], 0, 0))

    return pl.pallas_call(
        _ffn_kernel,
        grid_spec=pltpu.PrefetchScalarGridSpec(
            num_scalar_prefetch=2, grid=(N_SLOTS_BLK,),
            in_specs=[rows_in, wspec(D_MODEL, D_EXPERT), wspec(D_MODEL, D_EXPERT), wspec(D_EXPERT, D_MODEL)],
            out_specs=rows,
            scratch_shapes=[pltpu.VMEM((D_MODEL, D_EXPERT), BF16), pltpu.VMEM((D_MODEL, D_EXPERT), BF16),
                            pltpu.VMEM((D_EXPERT, D_MODEL), BF16)]),
        out_shape=jax.ShapeDtypeStruct((N_SLOTS * ROW_VREGS, LANES), F32),
        compiler_params=_cparams(("arbitrary",)),
        name="experts",
    )(blk_exp, n_used, xs, w1, w3, w2)


def _moe(x, g_ffn, meta, counts, layer, w1, w3, w2):
    dest, blk_exp, n_used, pad_lo, pad_hi = _plan(meta, counts)
    xs = _dispatch(x, g_ffn, dest, pad_lo, pad_hi, n_used)
    return _experts(xs, blk_exp, n_used, layer, w1, w3, w2), dest


def _pool_kernel(dest_ref, x1_ref, ys_ref, gate0_ref, gmix_ref, sp_ref, pw_ref, ps_ref, gffn_ref,
                 wr_ref, br_ref, tri_ref,
                 x3_ref, hkeep_ref, hs_ref, meta_ref, gate_ref, cnt_out_ref,
                 cnt_ref, ext_ref, e1_ref, e2_ref, e3_ref, e4_ref, mixed_ref, ybuf, ysems):
    i = pl.program_id(0)
    x2 = _ExpertRows(dest_ref, ys_ref, ybuf, ysems).fetch_combine(x1_ref[...], gate0_ref)
    h = _rms(x2, gmix_ref[...])
    H = POOL_HALO
    lvl_refs = (e1_ref, e2_ref, e3_ref, e4_ref)

    @pl.when(i < SAMPLE_TILE)
    def _():
        hkeep_ref[...] = h[TR - 16:TR]
        @pl.when(i % TILES_PER_SEQ == 0)
        def _():
            ext_ref[0:H, :] = jnp.zeros((H, D_MODEL), F32)
        ext_ref[H:H + TR, :] = h
        src = ext_ref
        for lv in range(4):
            sh = 1 << lv
            lo = 8 * (lv + 1)
            c0 = POOL_GROUP * lv
            dst = lvl_refs[lv]
            dst[lo:H + TR, c0:] = src[lo:H + TR, c0:] + src[lo - sh:H + TR - sh, c0:]
            src = dst
        pos = (i % TILES_PER_SEQ) * TR + lax.broadcasted_iota(I32, (TR, 1), 0)
        for gi, w in enumerate(POOL_WINDOWS):
            cs = slice(gi * POOL_GROUP, (gi + 1) * POOL_GROUP)
            inv = 1.0 / jnp.minimum(w, pos + 1).astype(F32)
            mixed_ref[:, cs] = lvl_refs[gi][H:H + TR, cs] * inv - h[:, cs]
        ext_ref[0:H, :] = h[TR - H:TR]

    @pl.when(i == SAMPLE_TILE)
    def _():
        hs = h[0:DEC_BATCH]
        hs_ref[...] = hs
        mixed_ref[...] = jnp.zeros_like(mixed_ref)
        for gi, w in enumerate(POOL_WINDOWS):
            cs = slice(gi * POOL_GROUP, (gi + 1) * POOL_GROUP)
            win = hs[:, cs]
            for dlt in range(1, w):
                win = win + sp_ref[POOL_KEEP - dlt][:, cs]
            mixed_ref[0:DEC_BATCH, cs] = win * (1.0 / w) - hs[:, cs]

    mixed = mixed_ref[...].astype(BF16)
    outs = [jnp.dot(mixed[:, gi * POOL_GROUP:(gi + 1) * POOL_GROUP], pw_ref[gi], preferred_element_type=F32)
            for gi in range(len(POOL_WINDOWS))]
    x3 = x2 + jnp.concatenate(outs, -1) * ps_ref[...]
    x3_ref[...] = x3
    h2 = _rms(x3, gffn_ref[...])
    _route(h2, wr_ref, br_ref, tri_ref, cnt_ref, meta_ref, gate_ref, cnt_out_ref)


def _const_spec_p(a):
    return pl.BlockSpec(a.shape, lambda i, *_, n=a.ndim: (0,) * n)


def _pool_route(dest, x1, ys, gates0, g_mix, sp_t, pw_bf16, p_scale, g_ffn, wr, br, tri):
    def rows(shape):
        return pl.BlockSpec(shape, lambda i, *_: (i, 0))

    ext = pltpu.VMEM((POOL_HALO + TR, D_MODEL), F32)
    consts = [g_mix, sp_t, pw_bf16, p_scale, g_ffn, wr, br, tri]
    return pl.pallas_call(
        _pool_kernel,
        grid_spec=pltpu.PrefetchScalarGridSpec(
            num_scalar_prefetch=1, grid=(N_TILES,),
            in_specs=[rows((TR, D_MODEL)), pl.BlockSpec(memory_space=pl.ANY), rows((TR, 8))]
                     + [_const_spec_p(a) for a in consts],
            out_specs=[rows((TR, D_MODEL)),
                       pl.BlockSpec((None, 16, D_MODEL),
                                    lambda i, *_: (jnp.minimum(i // TILES_PER_SEQ, BATCH - 1), 0, 0)),
                       pl.BlockSpec((DEC_BATCH, D_MODEL), lambda i, *_: (0, 0)),
                       pl.BlockSpec((1, 8, TR), lambda i, *_: (i, 0, 0)), rows((TR, 8)),
                       pl.BlockSpec((1, 1, LANES), lambda i, *_: (i, 0, 0))],
            scratch_shapes=[pltpu.VMEM((1, LANES), F32), ext, ext, ext, ext, ext,
                            pltpu.VMEM((TR, D_MODEL), F32)] + _ExpertRows.scratch),
        out_shape=[jax.ShapeDtypeStruct((NT, D_MODEL), F32),
                   jax.ShapeDtypeStruct((BATCH, 16, D_MODEL), F32),
                   jax.ShapeDtypeStruct((DEC_BATCH, D_MODEL), F32)] + _ROUTE_OUT_SHAPES,
        compiler_params=_cparams(("arbitrary",)),
        name="pool_route",
    )(dest, x1, ys, gates0, *consts)


def _final_kernel(dest_ref, x_ref, ys_ref, gate_ref, g_ref, yp_ref, ysm_ref, ybuf, ysems):
    i = pl.program_id(0)
    y = _rms(_ExpertRows(dest_ref, ys_ref, ybuf, ysems).fetch_combine(x_ref[...], gate_ref), g_ref[...])

    @pl.when(i < SAMPLE_TILE)
    def _():
        yp_ref[...] = y

    @pl.when(i == SAMPLE_TILE)
    def _():
        ysm_ref[...] = y[0:DEC_BATCH]


def _final(dest, x3, ys, gates, g_final):
    return pl.pallas_call(
        _final_kernel,
        grid_spec=pltpu.PrefetchScalarGridSpec(
            num_scalar_prefetch=1, grid=(N_TILES,),
            in_specs=[pl.BlockSpec((TR, D_MODEL), lambda i, *_: (i, 0)), pl.BlockSpec(memory_space=pl.ANY),
                      pl.BlockSpec((TR, 8), lambda i, *_: (i, 0)), _const_spec_p(g_final)],
            out_specs=[pl.BlockSpec((TR, D_MODEL), lambda i, *_: (jnp.minimum(i, SAMPLE_TILE - 1), 0)),
                       pl.BlockSpec((DEC_BATCH, D_MODEL), lambda i, *_: (0, 0))],
            scratch_shapes=_ExpertRows.scratch),
        out_shape=[jax.ShapeDtypeStruct((NP, D_MODEL), F32), jax.ShapeDtypeStruct((DEC_BATCH, D_MODEL), F32)],
        compiler_params=_cparams(("arbitrary",)),
        name="final_norm",
    )(dest, x3, ys, gates, g_final)


def kernel(x_prompt, x_sample, cache_win_k, cache_win_v, state_hgrn, state_pool, rel_bias, norm_mix, norm_ffn,
           norm_final, w_in, w_out, hgrn_lb, hgrn_gnorm, pool_w, pool_scale, moe_wg_group, moe_bg_group,
           moe_wg_exp, moe_bg_exp, moe_w1, moe_w3, moe_w2):
    xp = x_prompt.reshape(NP, D_MODEL)
    xs_pad = jnp.pad(x_sample.reshape(DEC_BATCH, D_MODEL), ((0, TR - DEC_BATCH), (0, 0)))
    lb = jnp.cumsum(jax.nn.softmax(hgrn_lb.astype(F32), axis=0), axis=0)[0:1]
    gnorm = hgrn_gnorm[0:1].astype(F32)

    q, k, v, hq, hf, hi, hg, k_win, v_win = _inproj(xp, xs_pad, norm_mix[0:1], w_in[0].astype(BF16))
    oa = _attention_prompt(q, k, v, _attn_bias_tables(rel_bias))
    ob, s_prompt = _hgrn_prompt(hq, hf, hi, hg, lb, gnorm)
    oa_s, ob_s, s_sample = _sample_mixers(
        q, k, v, hq, hf, hi, hg,
        cache_win_k[0], cache_win_v[0], state_hgrn[0], _sample_bias_tables(rel_bias), lb, gnorm)
    wr, br, tri = _router_operands(moe_wg_group[0], moe_bg_group[0], moe_wg_exp[0], moe_bg_exp[0])
    x1, meta, gates0, counts = _outproj_route(xp, xs_pad, oa, oa_s, ob, ob_s, w_out[0].astype(BF16),
                                              norm_ffn[0:1], wr, br, tri)
    ys, dest = _moe(x1, norm_ffn[0:1], meta, counts, 0, moe_w1, moe_w3, moe_w2)

    wr, br, tri = _router_operands(moe_wg_group[1], moe_bg_group[1], moe_wg_exp[1], moe_bg_exp[1])
    sp_t = jnp.transpose(state_pool[0], (1, 0, 2))
    x3, hkeep, hsample, meta, gates1, counts = _pool_route(
        dest, x1, ys, gates0, norm_mix[1:2], sp_t, pool_w[0].astype(BF16), pool_scale[0:1], norm_ffn[1:2],
        wr, br, tri)
    ys, dest = _moe(x3, norm_ffn[1:2], meta, counts, 1, moe_w1, moe_w3, moe_w2)
    y_prompt, y_sample = _final(dest, x3, ys, gates1, norm_final.reshape(1, D_MODEL))

    k_s = k[NP:NV].reshape(1, DEC_BATCH, 1, N_HEADS, D_HEAD)
    v_s = v[NP:NV].reshape(1, DEC_BATCH, 1, N_HEADS, D_HEAD)
    return (y_prompt.reshape(BATCH, SEQ, D_MODEL), y_sample.reshape(DEC_BATCH, 1, D_MODEL),
            k_win.reshape(1, BATCH, W_MAX, N_HEADS, D_HEAD), v_win.reshape(1, BATCH, W_MAX, N_HEADS, D_HEAD),
            s_prompt[None], hkeep[:, 16 - POOL_KEEP:][None],
            k_s, v_s, s_sample[None], hsample.reshape(1, DEC_BATCH, 1, D_MODEL))
```

```python
import functools

import numpy as np
import jax
import jax.numpy as jnp
from jax import lax
from jax.experimental import pallas as pl
from jax.experimental.pallas import tpu as pltpu

F32 = jnp.float32
BF16 = jnp.bfloat16
I32 = jnp.int32

D_MODEL = 1024
BATCH = 4
SEQ = 4096
DEC_BATCH = 32
PAST_LEN = 8192
W_MAX = 2048
N_HEADS = 4
D_HEAD = 128
A_WIDTH = N_HEADS * D_HEAD
N_PROJ = 7 * A_WIDTH
DILATED = ((128, 1), (512, 4), (2048, 16))
NUM_BUCKETS = 32
MAX_DISTANCE = 2048
POOL_WINDOWS = (2, 4, 8, 16)
POOL_GROUP = 256
POOL_KEEP = 15
N_GROUPS = 4
EXPERTS_PER_GROUP = 8
N_EXPERTS = 32
D_EXPERT = 512
EPS = 1e-6
NEG = -1e30

LANES = 128
SUBLANES = 8
ROW_VREGS = D_MODEL // LANES
TR = 256
NP = BATCH * SEQ
NV = NP + DEC_BATCH
NT = NP + TR
N_TILES = NT // TR
SAMPLE_TILE = NP // TR
TILES_PER_SEQ = SEQ // TR
TM = 256
N_SLOTS_BLK = (2 * NV + N_EXPERTS * (TM - 1) + TM - 1) // TM
N_SLOTS = N_SLOTS_BLK * TM
QB = 128
ATTN_UNROLL = 8
ATTN_PERIOD = 3 * QB
ROW_DMA_UNROLL = 8
CH = 128
HGRN_HEADS_PER_STEP = 2
N_LEVELS = 7
POOL_HALO = 32
VMEM_LIMIT = 56 * 1024 * 1024


def _cparams(sem=None, vmem=VMEM_LIMIT):
    kw = dict(vmem_limit_bytes=vmem)
    if sem is not None:
        kw["dimension_semantics"] = sem
    return pltpu.CompilerParams(**kw)


def _rms(x, g):
    return x * lax.rsqrt(jnp.mean(x * x, -1, keepdims=True) + EPS) * g


def _sigmoid(x):
    return 1.0 / (1.0 + jnp.exp(-x))


def _silu(x):
    return x * _sigmoid(x)


def _token_tile(i, xp_ref, xs_ref):
    return jnp.where(i == SAMPLE_TILE, xs_ref[...], xp_ref[...])


_PROMPT_ROWS = pl.BlockSpec((TR, D_MODEL), lambda i: (jnp.minimum(i, SAMPLE_TILE - 1), 0))
_SAMPLE_ROWS = pl.BlockSpec((TR, D_MODEL), lambda i: (0, 0))
WIN_TILES = W_MAX // TR


def _window_block(i):
    seq = jnp.minimum(i // TILES_PER_SEQ, BATCH - 1)
    j = jnp.clip(i % TILES_PER_SEQ - (TILES_PER_SEQ - WIN_TILES), 0, WIN_TILES - 1)
    return jnp.where(i >= SAMPLE_TILE, BATCH * WIN_TILES - 1, seq * WIN_TILES + j)


def _inproj_kernel(xp_ref, xs_ref, g_ref, w_ref, *out_refs):
    i = pl.program_id(0)
    h = _rms(_token_tile(i, xp_ref, xs_ref), g_ref[...])
    p = jnp.dot(h.astype(BF16), w_ref[...], preferred_element_type=F32)
    for n, o_ref in enumerate(out_refs[:7]):
        o_ref[...] = p[:, n * A_WIDTH:(n + 1) * A_WIDTH]

    @pl.when(jnp.logical_and(i < SAMPLE_TILE, i % TILES_PER_SEQ >= TILES_PER_SEQ - WIN_TILES))
    def _():
        for n, o_ref in ((1, out_refs[7]), (2, out_refs[8])):
            for h_i in range(N_HEADS):
                c0 = n * A_WIDTH + h_i * D_HEAD
                o_ref[pl.ds(h_i, TR, stride=N_HEADS), :] = p[:, c0:c0 + D_HEAD]


def _inproj(xp, xs_pad, g, w_bf16):
    out_sds = ([jax.ShapeDtypeStruct((NT, A_WIDTH), F32)] * 7
               + [jax.ShapeDtypeStruct((BATCH * W_MAX * N_HEADS, D_HEAD), F32)] * 2)
    win = pl.BlockSpec((TR * N_HEADS, D_HEAD), lambda i: (_window_block(i), 0))
    return pl.pallas_call(
        _inproj_kernel,
        grid=(N_TILES,),
        in_specs=[_PROMPT_ROWS, _SAMPLE_ROWS,
                  pl.BlockSpec((1, D_MODEL), lambda i: (0, 0)),
                  pl.BlockSpec((D_MODEL, N_PROJ), lambda i: (0, 0))],
        out_specs=[pl.BlockSpec((TR, A_WIDTH), lambda i: (i, 0))] * 7 + [win, win],
        out_shape=out_sds,
        compiler_params=_cparams(("arbitrary",)),
        name="inproj",
    )(xp, xs_pad, g, w_bf16)


def _t5_bucket(dist):
    max_exact = NUM_BUCKETS // 2
    d = np.asarray(dist)
    large = max_exact + np.floor(np.log(np.maximum(d, 1) / max_exact)
                                 / np.log(MAX_DISTANCE / max_exact) * (NUM_BUCKETS - max_exact)).astype(np.int32)
    large = np.minimum(large, NUM_BUCKETS - 1)
    return np.where(d < max_exact, d, large).astype(np.int32)


def _attn_bias_tables(rel_bias):
    period = ATTN_PERIOD
    m = np.arange(period)
    u = np.where(m < 2 * QB, m, m - period)
    pick = np.zeros((len(DILATED), 2, period, NUM_BUCKETS), np.float32)
    mask = np.zeros((len(DILATED), 2, period, 1), np.float32)
    for bi, (w, d) in enumerate(DILATED):
        nk = w // d
        for vi, off in enumerate((0, QB)):
            j = off - u
            ok = (j >= 0) & (j <= nk)
            pick[bi, vi, m[ok], _t5_bucket(d * j[ok])] = 1.0
            mask[bi, vi, ~ok, 0] = NEG
    vec = jnp.einsum("bvmk,kh->hbvm", pick, rel_bias.astype(F32), precision=lax.Precision.HIGHEST)
    vec = vec + jnp.transpose(mask, (3, 0, 1, 2))
    return vec[:, :, :, None, :]


def _attn_kernel(q_ref, k_ref, v_ref, vec_ref, o_ref,
                 qd, kd, vd, ud, md, sd, u_acc, m_acc, s_acc, bias_ref):
    scale = D_HEAD ** -0.5
    for bi in range(len(DILATED)):
        for vi in range(2):
            rows = jnp.broadcast_to(vec_ref[bi, vi], (QB, ATTN_PERIOD))
            bias_ref[bi, vi] = pltpu.roll(rows, 0, 1, stride=1, stride_axis=0)[:, :2 * QB]

    def block_stats(bi, t, nb):
        has_prev = jnp.minimum(t % nb, 1)
        q0 = pl.multiple_of(t * QB, QB)
        k0 = pl.multiple_of((t - has_prev) * QB, QB)
        qb = qd[pl.ds(q0, QB), :]
        kb = kd[pl.ds(k0, 2 * QB), :]
        vb = vd[pl.ds(k0, 2 * QB), :]
        s = lax.dot_general(qb, kb, (((1,), (1,)), ((), ())), preferred_element_type=F32)
        s = s + bias_ref[bi, has_prev]
        mb = jnp.max(s, -1, keepdims=True)
        p = jnp.exp(s - mb)
        sb = jnp.sum(p, -1, keepdims=True)
        u = jnp.dot(p.astype(BF16), vb, preferred_element_type=F32)
        return q0, mb, sb, u

    for bi in (2, 1, 0):
        d = DILATED[bi][1]
        cl = SEQ // d
        nb = cl // QB
        for r in range(d):
            src = pl.ds(r, cl, stride=d) if d > 1 else pl.ds(0, cl)
            dst = pl.ds(r * cl, cl)
            qd[dst, :] = (q_ref[src, :] * scale).astype(BF16)
            kd[dst, :] = k_ref[src, :].astype(BF16)
            vd[dst, :] = v_ref[src, :].astype(BF16)

        if d > 1:
            def body(g, carry, bi=bi, nb=nb):
                for un in range(ATTN_UNROLL):
                    q0, mb, sb, u = block_stats(bi, g * ATTN_UNROLL + un, nb)
                    ud[pl.ds(q0, QB), :] = u
                    md[pl.ds(q0, QB), :] = jnp.broadcast_to(mb, (QB, D_HEAD))
                    sd[pl.ds(q0, QB), :] = jnp.broadcast_to(sb, (QB, D_HEAD))
                return carry
            lax.fori_loop(0, SEQ // QB // ATTN_UNROLL, body, 0)
            for r in range(d):
                pos = pl.ds(r, cl, stride=d)
                cls = pl.ds(r * cl, cl)
                if bi == 2:
                    u_acc[pos, :] = ud[cls, :]
                    m_acc[pos, :] = md[cls, :]
                    s_acc[pos, :] = sd[cls, :]
                else:
                    m_old = m_acc[pos, :]
                    m_blk = md[cls, :]
                    m_new = jnp.maximum(m_old, m_blk)
                    a = jnp.exp(m_old - m_new)
                    b = jnp.exp(m_blk - m_new)
                    u_acc[pos, :] = a * u_acc[pos, :] + b * ud[cls, :]
                    s_acc[pos, :] = a * s_acc[pos, :] + b * sd[cls, :]
                    m_acc[pos, :] = m_new
        else:
            def body(g, carry, bi=bi, nb=nb):
                for un in range(ATTN_UNROLL):
                    q0, mb, sb, u = block_stats(bi, g * ATTN_UNROLL + un, nb)
                    rows = pl.ds(q0, QB)
                    m_old = m_acc[rows, :]
                    m_new = jnp.maximum(m_old, mb)
                    a = jnp.exp(m_old - m_new)
                    b = jnp.exp(mb - m_new)
                    num = a * u_acc[rows, :] + b * u
                    den = a * s_acc[rows, :] + b * sb
                    o_ref[rows, :] = (num / den).astype(o_ref.dtype)
                return carry
            lax.fori_loop(0, SEQ // QB // ATTN_UNROLL, body, 0)


def _attention_prompt(q, k, v, bias_tabs):
    blk = pl.BlockSpec((SEQ, D_HEAD), lambda b, h: (b, h))
    return pl.pallas_call(
        _attn_kernel,
        grid=(BATCH, N_HEADS),
        in_specs=[blk, blk, blk,
                  pl.BlockSpec((None, 3, 2, 1, ATTN_PERIOD), lambda b, h: (h, 0, 0, 0, 0))],
        out_specs=blk,
        out_shape=jax.ShapeDtypeStruct((NP, A_WIDTH), BF16),
        scratch_shapes=[pltpu.VMEM((SEQ, D_HEAD), BF16)] * 3
                       + [pltpu.VMEM((SEQ, D_HEAD), F32)] * 6
                       + [pltpu.VMEM((len(DILATED), 2, QB, 2 * QB), F32)],
        compiler_params=_cparams(("arbitrary", "arbitrary")),
        name="attn_prompt",
    )(q, k, v, bias_tabs)


def _hgrn_tables():
    t = np.arange(CH)
    u = np.arange(CH)
    sums_q = np.zeros((1 + N_LEVELS, CH, CH), np.float32)
    sums_k = np.zeros((2 + N_LEVELS, CH, CH), np.float32)
    sums_q[0] = (u[None, :] <= t[:, None])
    sums_k[0] = (u[None, :] > t[:, None])
    sums_k[1 + N_LEVELS] = 1.0
    pair = np.zeros((N_LEVELS, CH, CH), np.float32)
    for l in range(N_LEVELS):
        h = CH >> (l + 1)
        is_q = (t // h) % 2 == 1
        half_start = (t // h) * h
        half_end = half_start + h
        sel_q = (u[None, :] >= half_start[:, None]) & (u[None, :] <= t[:, None])
        sel_k = (u[None, :] > t[:, None]) & (u[None, :] < half_end[:, None])
        sums_q[1 + l] = sel_q & is_q[:, None]
        sums_k[1 + l] = sel_k & ~is_q[:, None]
        same = (t[:, None] // (2 * h)) == (t[None, :] // (2 * h))
        pair[l] = same & is_q[:, None] & (~is_q)[None, :]
    sums_kt = np.transpose(sums_k, (2, 0, 1)).reshape(CH, (2 + N_LEVELS) * CH)
    sums_q = sums_q.reshape((1 + N_LEVELS) * CH, CH)
    return (jnp.asarray(np.concatenate([sums_q, sums_q], 1), BF16),
            jnp.asarray(np.concatenate([sums_kt, sums_kt], 0), BF16),
            jnp.asarray(pair), jnp.asarray(np.eye(CH, dtype=np.float32)))


def _split_bf16(x):
    hi = x.astype(BF16)
    return hi, (x - hi.astype(F32)).astype(BF16)


def _hgrn_kernel(q_ref, f_ref, i_ref, g_ref, lb_ref, gn_ref, sq_ref, skt_ref, pair_ref, eye_ref, o_ref, s_ref):
    gn = gn_ref[...]

    def chunk(c, st, hh):
        rows = pl.ds(pl.multiple_of(c * CH, CH), CH)
        cols = slice(hh * D_HEAD, (hh + 1) * D_HEAD)
        lb = lb_ref[:, cols]
        q = _silu(q_ref[rows, cols])
        f = lb + (1.0 - lb) * _sigmoid(f_ref[rows, cols])
        lf = jnp.log(f)
        k = 1.0 - f
        v16 = i_ref[rows, cols].astype(BF16)
        kt = k.T
        exq = jnp.exp(jnp.dot(sq_ref[...], jnp.concatenate(_split_bf16(lf), 0),
                              preferred_element_type=F32))
        exk = jnp.exp(jnp.dot(jnp.concatenate(_split_bf16(lf.T), 1), skt_ref[...],
                              preferred_element_type=F32))
        inter = jnp.dot((q * exq[0:CH]).astype(BF16), st.astype(BF16), preferred_element_type=F32)
        sc = eye_ref[...] * jnp.sum(q * k, -1, keepdims=True)
        for l in range(N_LEVELS):
            ql = (q * exq[(1 + l) * CH:(2 + l) * CH]).astype(BF16)
            klt = (kt * exk[:, (1 + l) * CH:(2 + l) * CH]).astype(BF16)
            sc = sc + pair_ref[l] * jnp.dot(ql, klt, preferred_element_type=F32)
        o = inter + jnp.dot(sc.astype(BF16), v16, preferred_element_type=F32)
        st_new = (st * exk[:, (1 + N_LEVELS) * CH:]
                  + jnp.dot((kt * exk[:, 0:CH]).astype(BF16), v16, preferred_element_type=F32))
        o = _rms(o, gn) * _silu(g_ref[rows, cols])
        o_ref[rows, cols] = o.astype(o_ref.dtype)
        return st_new

    def step(c, states):
        return tuple(chunk(c, st, hh) for hh, st in enumerate(states))

    zero = jnp.zeros((D_HEAD, D_HEAD), F32)
    states = lax.fori_loop(0, SEQ // CH, step, (zero,) * HGRN_HEADS_PER_STEP)
    for hh, st in enumerate(states):
        s_ref[hh] = st


def _hgrn_prompt(hq, hf, hi, hg, lb, gnorm):
    tables = _hgrn_tables()
    width = HGRN_HEADS_PER_STEP * D_HEAD
    blk = pl.BlockSpec((SEQ, width), lambda b, h: (b, h))

    def full(a):
        return pl.BlockSpec(a.shape, lambda b, h, n=a.ndim: (0,) * n)

    return pl.pallas_call(
        _hgrn_kernel,
        grid=(BATCH, N_HEADS // HGRN_HEADS_PER_STEP),
        in_specs=[blk, blk, blk, blk,
                  pl.BlockSpec((1, width), lambda b, h: (0, h)),
                  pl.BlockSpec((1, D_HEAD), lambda b, h: (0, 0))] + [full(a) for a in tables],
        out_specs=[blk, pl.BlockSpec((None, HGRN_HEADS_PER_STEP, D_HEAD, D_HEAD), lambda b, h: (b, h, 0, 0))],
        out_shape=[jax.ShapeDtypeStruct((NP, A_WIDTH), BF16),
                   jax.ShapeDtypeStruct((BATCH, N_HEADS, D_HEAD, D_HEAD), F32)],
        compiler_params=_cparams(("arbitrary", "arbitrary")),
        name="hgrn_prompt",
    )(hq, hf, hi, hg, lb, gnorm, *tables)


def _sample_bias_tables(rel_bias):
    j = np.concatenate([QB - np.arange(QB), [0]])
    pick = np.zeros((len(DILATED), QB + 1, NUM_BUCKETS), np.float32)
    for bi, (w, d) in enumerate(DILATED):
        pick[bi, np.arange(QB + 1), _t5_bucket(d * j)] = 1.0
    return jnp.einsum("bjk,kh->bhj", pick, rel_bias.astype(F32), precision=lax.Precision.HIGHEST)[..., None]


def _bf16_round(x):
    return x.astype(BF16).astype(F32)


def _col(row, eye):
    return jnp.sum(eye * row, -1, keepdims=True)


NEAR_POS = 4 * QB
FAR_STEPS = (W_MAX - NEAR_POS) // 16


def _cached_rows(near_ref, far_ref, d, h):
    def near(first_pos, n, step):
        return near_ref[pl.ds((first_pos - (W_MAX - NEAR_POS)) * N_HEADS + h, n, stride=step * N_HEADS), :]

    if QB * d <= NEAR_POS:
        return near(W_MAX - QB * d, QB, d)
    return jnp.concatenate([far_ref[:, h, :], near(W_MAX - NEAR_POS, QB - FAR_STEPS, d)], 0)


def _sample_kernel(q_ref, k_ref, v_ref, hq_ref, hf_ref, hi_ref, hg_ref, kn_ref, kf_ref, vn_ref, vf_ref, s0_ref,
                   bias_ref, lb_ref, gn_ref, eye_ref, oa_ref, ob_ref, s_ref, oa_acc, ob_acc):
    b = pl.program_id(0)
    row = pl.ds(b, 1)
    scale = D_HEAD ** -0.5
    q = _bf16_round(q_ref[row, :] * scale)
    kn = _bf16_round(k_ref[row, :])
    vn = _bf16_round(v_ref[row, :])
    eye = eye_ref[...]

    stats = []
    for bi, (w, d) in enumerate(DILATED):
        per_head = []
        for h in range(N_HEADS):
            hs = slice(h * D_HEAD, (h + 1) * D_HEAD)
            kb = _bf16_round(_cached_rows(kn_ref, kf_ref, d, h))
            vb = _bf16_round(_cached_rows(vn_ref, vf_ref, d, h))
            s = jnp.sum(kb * q[:, hs], -1, keepdims=True) + bias_ref[bi, h, 0:QB]
            s0 = jnp.sum(q[:, hs] * kn[:, hs], -1, keepdims=True) + bias_ref[bi, h, QB:QB + 1]
            m = jnp.maximum(jnp.max(s, 0, keepdims=True), s0)
            p = jnp.exp(s - m)
            p0 = jnp.exp(s0 - m)
            ssum = jnp.sum(p, 0, keepdims=True) + p0
            u = jnp.sum(_bf16_round(p) * vb, 0, keepdims=True) + _bf16_round(p0) * vn[:, hs]
            per_head.append((m, ssum, u))
        stats.append(per_head)
    outs = []
    for h in range(N_HEADS):
        m_all = functools.reduce(jnp.maximum, [stats[bi][h][0] for bi in range(3)])
        num = 0.0
        den = 0.0
        for bi in range(3):
            m, ssum, u = stats[bi][h]
            c = jnp.exp(m - m_all)
            num = num + c * u
            den = den + c * ssum
        outs.append(num / den)
    oa_acc[row, :] = jnp.concatenate(outs, -1)

    qh = _silu(hq_ref[row, :])
    lb = lb_ref[...]
    f = lb + (1.0 - lb) * _sigmoid(hf_ref[row, :])
    vi = hi_ref[row, :]
    gate = _silu(hg_ref[row, :])
    gn = gn_ref[...]
    obs = []
    for h in range(N_HEADS):
        hs = slice(h * D_HEAD, (h + 1) * D_HEAD)
        f_col = _col(f[:, hs], eye)
        q_col = _col(qh[:, hs], eye)
        s_old = s0_ref[h]
        s_ref[h] = f_col * s_old + (1.0 - f_col) * vi[:, hs]
        inter = jnp.sum(_bf16_round(q_col * f_col) * _bf16_round(s_old), 0, keepdims=True)
        qk = jnp.sum(qh[:, hs] * (1.0 - f[:, hs]), -1, keepdims=True)
        o = inter + qk * vi[:, hs]
        obs.append(_rms(o, gn) * gate[:, hs])
    ob_acc[row, :] = jnp.concatenate(obs, -1)

    @pl.when(b == DEC_BATCH - 1)
    def _():
        pad = jnp.zeros((TR - DEC_BATCH, A_WIDTH), F32)
        oa_ref[...] = jnp.concatenate([oa_acc[...], pad], 0).astype(oa_ref.dtype)
        ob_ref[...] = jnp.concatenate([ob_acc[...], pad], 0).astype(ob_ref.dtype)


def _sample_mixers(q, k, v, hq, hf, hi, hg, cache_k, cache_v, state, bias_s, lb, gnorm):
    tile = pl.BlockSpec((TR, A_WIDTH), lambda b: (SAMPLE_TILE, 0))
    out_tile = pl.BlockSpec((TR, A_WIDTH), lambda b: (0, 0))
    near = pl.BlockSpec((None, NEAR_POS * N_HEADS, D_HEAD), lambda b: (b, W_MAX // NEAR_POS - 1, 0))
    far = pl.BlockSpec((None, FAR_STEPS, SUBLANES, D_HEAD), lambda b: (b, 0, 0, 0))
    rows = (DEC_BATCH, W_MAX * N_HEADS, D_HEAD)
    groups = (DEC_BATCH, W_MAX // 16, 16 * N_HEADS, D_HEAD)
    eye = jnp.eye(D_HEAD, dtype=F32)

    return pl.pallas_call(
        _sample_kernel,
        grid=(DEC_BATCH,),
        in_specs=[tile, tile, tile, tile, tile, tile, tile, near, far, near, far,
                  pl.BlockSpec((None, N_HEADS, D_HEAD, D_HEAD), lambda b: (b, 0, 0, 0)),
                  pl.BlockSpec(bias_s.shape, lambda b: (0, 0, 0, 0)),
                  pl.BlockSpec((1, A_WIDTH), lambda b: (0, 0)),
                  pl.BlockSpec((1, D_HEAD), lambda b: (0, 0)),
                  pl.BlockSpec((D_HEAD, D_HEAD), lambda b: (0, 0))],
        out_specs=[out_tile, out_tile,
                   pl.BlockSpec((None, N_HEADS, D_HEAD, D_HEAD), lambda b: (b, 0, 0, 0))],
        out_shape=[jax.ShapeDtypeStruct((TR, A_WIDTH), BF16), jax.ShapeDtypeStruct((TR, A_WIDTH), BF16),
                   jax.ShapeDtypeStruct((DEC_BATCH, N_HEADS, D_HEAD, D_HEAD), F32)],
        scratch_shapes=[pltpu.VMEM((DEC_BATCH, A_WIDTH), F32)] * 2,
        compiler_params=_cparams(("arbitrary",)),
        name="sample_mixers",
    )(q, k, v, hq, hf, hi, hg, cache_k.reshape(rows), cache_k.reshape(groups),
      cache_v.reshape(rows), cache_v.reshape(groups), state, bias_s, lb, gnorm, eye)


def _store_rows_as_tiles(ref, val):
    n = val.shape[0]
    for j in range(ROW_VREGS):
        ref[pl.ds(j, n, stride=ROW_VREGS), :] = val[:, j * LANES:(j + 1) * LANES]


def _load_rows_from_tiles(ref, n):
    return jnp.concatenate([ref[pl.ds(j, n, stride=ROW_VREGS), :] for j in range(ROW_VREGS)], -1)


def _route(h2, wr_ref, br_ref, tri_ref, cnt_ref, meta_ref, gate_ref, cnt_out_ref):
    i = pl.program_id(0)

    @pl.when(i == 0)
    def _():
        cnt_ref[...] = jnp.zeros_like(cnt_ref)

    logits = jnp.dot(h2.astype(BF16), wr_ref[...], preferred_element_type=F32) + br_ref[...]
    lane = lax.broadcasted_iota(I32, (TR, LANES), 1).astype(F32)
    big = float(1 << 20)
    is_g = lane < N_GROUPS
    gl = jnp.where(is_g, logits, NEG)
    gmax = jnp.max(gl, -1, keepdims=True)
    gsel = jnp.min(jnp.where(gl == gmax, lane, big), -1, keepdims=True)
    pg = 1.0 / jnp.sum(jnp.where(is_g, jnp.exp(gl - gmax), 0.0), -1, keepdims=True)
    lo = N_GROUPS + EXPERTS_PER_GROUP * gsel
    in_grp = jnp.logical_and(lane >= lo, lane < lo + EXPERTS_PER_GROUP)
    el = jnp.where(in_grp, logits, NEG)
    m1 = jnp.max(el, -1, keepdims=True)
    i1 = jnp.min(jnp.where(el == m1, lane, big), -1, keepdims=True)
    el2 = jnp.where(lane == i1, NEG, el)
    m2 = jnp.max(el2, -1, keepdims=True)
    i2 = jnp.min(jnp.where(el2 == m2, lane, big), -1, keepdims=True)
    r = jnp.exp(m2 - m1)
    g1 = pg / (1.0 + r)
    g2 = pg * r / (1.0 + r)
    e1 = i1 - N_GROUPS
    e2 = i2 - N_GROUPS

    tok = i * TR + lax.broadcasted_iota(I32, (TR, 1), 0)
    valid = tok < NV
    oh1 = jnp.logical_and(lane == e1, valid)
    oh2 = jnp.logical_and(lane == e2, valid)
    oh = jnp.where(jnp.logical_or(oh1, oh2), 1.0, 0.0)
    before = jnp.dot(tri_ref[...], oh.astype(BF16), preferred_element_type=F32) + cnt_ref[...]
    rank1 = jnp.sum(jnp.where(oh1, before, 0.0), -1, keepdims=True)
    rank2 = jnp.sum(jnp.where(oh2, before, 0.0), -1, keepdims=True)
    tile_cnt = jnp.sum(oh, 0, keepdims=True)
    cnt_ref[...] = cnt_ref[...] + tile_cnt

    eye = jnp.where(lax.broadcasted_iota(I32, (TR, TR), 0) == lax.broadcasted_iota(I32, (TR, TR), 1), 1.0, 0.0)
    rows = [jnp.sum(eye * col, 0, keepdims=True) for col in (e1, e2, rank1, rank2)]
    meta_ref[0] = jnp.concatenate(rows + [jnp.zeros((4, TR), F32)], 0).astype(I32)
    gates = jnp.where(lane == 0, g1, jnp.where(lane == 1, g2, 0.0))
    gates = jnp.where(valid, gates, 0.0)
    gate_ref[...] = gates[:, 0:8]
    cnt_out_ref[0] = tile_cnt


def _router_operands(wg_group, bg_group, wg_exp, bg_exp):
    wr = jnp.zeros((D_MODEL, LANES), F32)
    wr = wr.at[:, 0:N_GROUPS].set(wg_group.astype(F32)).at[:, N_GROUPS:N_GROUPS + N_EXPERTS].set(wg_exp.astype(F32))
    br = jnp.zeros((1, LANES), F32)
    br = br.at[0, 0:N_GROUPS].set(bg_group.astype(F32)).at[0, N_GROUPS:N_GROUPS + N_EXPERTS].set(bg_exp.astype(F32))
    tri = jnp.asarray(np.tril(np.ones((TR, TR), np.float32), -1), BF16)
    return wr.astype(BF16), br, tri


_ROUTE_OUT_SPECS = [pl.BlockSpec((1, 8, TR), lambda i: (i, 0, 0)),
                    pl.BlockSpec((TR, 8), lambda i: (i, 0)),
                    pl.BlockSpec((1, 1, LANES), lambda i: (i, 0, 0))]
_ROUTE_OUT_SHAPES = [jax.ShapeDtypeStruct((N_TILES, 8, TR), I32), jax.ShapeDtypeStruct((NT, 8), F32),
                     jax.ShapeDtypeStruct((N_TILES, 1, LANES), F32)]


_TOKEN_TILES = jax.ShapeDtypeStruct((NT * ROW_VREGS, LANES), F32)


def _const_spec(a):
    return pl.BlockSpec(a.shape, lambda i, n=a.ndim: (0,) * n)


def _outproj_kernel(xp_ref, xs_ref, oa_ref, oas_ref, ob_ref, obs_ref, w_ref, g_ref, wr_ref, br_ref, tri_ref,
                    x1_ref, meta_ref, gate_ref, cnt_out_ref, cnt_ref):
    i = pl.program_id(0)
    x = _token_tile(i, xp_ref, xs_ref)
    mixed = jnp.concatenate([_token_tile(i, oa_ref, oas_ref), _token_tile(i, ob_ref, obs_ref)], -1)
    x1 = x + jnp.dot(mixed, w_ref[...], preferred_element_type=F32)
    _store_rows_as_tiles(x1_ref, x1)
    h2 = _rms(x1, g_ref[...])
    _route(h2, wr_ref, br_ref, tri_ref, cnt_ref, meta_ref, gate_ref, cnt_out_ref)


def _outproj_route(xp, xs_pad, oa, oa_s, ob, ob_s, w_out_bf16, g_ffn, wr, br, tri):
    row = pl.BlockSpec((TR * ROW_VREGS, LANES), lambda i: (i, 0))
    half = pl.BlockSpec((TR, A_WIDTH), lambda i: (jnp.minimum(i, SAMPLE_TILE - 1), 0))
    half_s = pl.BlockSpec((TR, A_WIDTH), lambda i: (0, 0))
    return pl.pallas_call(
        _outproj_kernel,
        grid=(N_TILES,),
        in_specs=[_PROMPT_ROWS, _SAMPLE_ROWS, half, half_s, half, half_s,
                  _const_spec(w_out_bf16), _const_spec(g_ffn),
                  _const_spec(wr), _const_spec(br), _const_spec(tri)],
        out_specs=[row] + _ROUTE_OUT_SPECS,
        out_shape=[_TOKEN_TILES] + _ROUTE_OUT_SHAPES,
        scratch_shapes=[pltpu.VMEM((1, LANES), F32)],
        compiler_params=_cparams(("arbitrary",)),
        name="outproj_route",
    )(xp, xs_pad, oa, oa_s, ob, ob_s, w_out_bf16, g_ffn, wr, br, tri)


def _plan(meta, tile_counts):
    tcnt = tile_counts[:, 0, :N_EXPERTS].astype(I32)
    cnt = jnp.sum(tcnt, 0)
    nblk = (cnt + TM - 1) // TM
    blk_end = jnp.cumsum(nblk)
    pstart = (blk_end - nblk) * TM
    n_used = blk_end[-1]
    blk = jnp.minimum(jnp.arange(N_SLOTS_BLK, dtype=I32), n_used - 1)
    blk_exp = jnp.sum((blk[:, None] >= blk_end[None, :]).astype(I32), -1)
    blk_exp = jnp.minimum(blk_exp, N_EXPERTS - 1)
    eid = meta[:, 0:2, :]
    experts = jnp.arange(N_EXPERTS, dtype=I32)[:, None, None, None]
    start_of = jnp.sum(jnp.where(eid[None] == experts, pstart[:, None, None, None], 0), 0)
    tok = (jnp.arange(N_TILES, dtype=I32) * TR)[:, None, None] + jnp.arange(TR, dtype=I32)
    dest = jnp.where(tok < NV, start_of + meta[:, 2:4, :], 0).astype(I32).reshape(-1)
    pad_lo = pstart + cnt
    pad_hi = pstart + nblk * TM
    return dest, blk_exp, n_used.reshape(1).astype(I32), pad_lo.astype(I32), pad_hi.astype(I32)


def _row(ref, r):
    return ref.at[pl.ds(pl.multiple_of(r * ROW_VREGS, ROW_VREGS), ROW_VREGS), :]


def _dest_index(tile, kk, t):
    return (2 * tile + kk) * TR + t


ZERO_TOKEN = NT - 1


def _invert_kernel(dest_ref, plo_ref, phi_ref, nused_ref, src_ref):
    def fill(lo, hi):
        def body(p, carry):
            src_ref[p] = ZERO_TOKEN
            return carry
        lax.fori_loop(lo, hi, body, 0)

    def per_expert(e, carry):
        fill(plo_ref[e], phi_ref[e])
        return carry
    lax.fori_loop(0, N_EXPERTS, per_expert, 0)
    fill(nused_ref[0] * TM, N_SLOTS)

    def tokens(tile, n_tok):
        for kk in range(2):
            def body(g, carry, kk=kk):
                for un in range(ROW_DMA_UNROLL):
                    t = g * ROW_DMA_UNROLL + un
                    src_ref[dest_ref[_dest_index(tile, kk, t)]] = tile * TR + t
                return carry
            lax.fori_loop(0, n_tok // ROW_DMA_UNROLL, body, 0)

    def per_tile(tile, carry):
        tokens(tile, TR)
        return carry
    lax.fori_loop(0, N_TILES - 1, per_tile, 0)
    tokens(N_TILES - 1, NV - (N_TILES - 1) * TR)


def _invert(dest, pad_lo, pad_hi, n_used):
    return pl.pallas_call(
        _invert_kernel,
        grid_spec=pltpu.PrefetchScalarGridSpec(
            num_scalar_prefetch=4, grid=(1,), in_specs=[],
            out_specs=pl.BlockSpec(memory_space=pltpu.SMEM)),
        out_shape=jax.ShapeDtypeStruct((N_SLOTS,), I32),
        compiler_params=_cparams(("arbitrary",)),
        name="invert_plan",
    )(dest, pad_lo, pad_hi, n_used)


class _ExpertRows:
    scratch = [pltpu.VMEM((2, 2 * TR * ROW_VREGS, LANES), F32), pltpu.SemaphoreType.DMA((2,))]

    def __init__(self, dest_ref, ys_ref, buf, sems):
        self.dest_ref, self.ys_ref, self.buf, self.sems = dest_ref, ys_ref, buf, sems

    def _copy(self, src_row, slot, dst_row):
        dst = self.buf.at[slot, pl.ds(pl.multiple_of(dst_row * ROW_VREGS, ROW_VREGS), ROW_VREGS), :]
        return pltpu.make_async_copy(_row(self.ys_ref, src_row), dst, self.sems.at[slot])

    def start(self, tile):
        slot = tile % 2

        def issue(g, carry):
            for un in range(ROW_DMA_UNROLL):
                t = g * ROW_DMA_UNROLL + un
                for kk in range(2):
                    self._copy(self.dest_ref[_dest_index(tile, kk, t)], slot, kk * TR + t).start(priority=kk)
            return carry
        lax.fori_loop(0, TR // ROW_DMA_UNROLL, issue, 0)

    def wait(self, tile):
        slot = tile % 2
        pltpu.make_async_copy(self.ys_ref.at[pl.ds(0, 2 * TR * ROW_VREGS), :], self.buf.at[slot],
                              self.sems.at[slot]).wait()

    def combine(self, tile, x, gate_ref):
        rows = self.buf.at[tile % 2]
        g = gate_ref[...]
        for kk in range(2):
            y = jnp.concatenate([rows[pl.ds(kk * TR * ROW_VREGS + j, TR, stride=ROW_VREGS), :]
                                 for j in range(ROW_VREGS)], -1)
            x = x + g[:, kk:kk + 1] * y
        return x

    def fetch_combine(self, x, gate_ref):
        i = pl.program_id(0)

        @pl.when(i == 0)
        def _():
            self.start(i)

        @pl.when(i + 1 < pl.num_programs(0))
        def _():
            self.start(i + 1)

        self.wait(i)
        return self.combine(i, x, gate_ref)


def _ffn_kernel(be_ref, nused_ref, src_ref, x_ref, g_ref, w1_ref, w3_ref, w2_ref, y_ref, w1b, w3b, w2b, xbuf, sems):
    nb = pl.program_id(0)
    n_used = nused_ref[0]
    slot = nb % 2

    def gather(blk, sl):
        def issue(g, carry):
            for un in range(ROW_DMA_UNROLL):
                r = g * ROW_DMA_UNROLL + un
                dst = xbuf.at[sl, pl.ds(pl.multiple_of(r * ROW_VREGS, ROW_VREGS), ROW_VREGS), :]
                pltpu.make_async_copy(_row(x_ref, src_ref[blk * TM + r]), dst, sems.at[sl]).start(priority=un % 2)
            return carry
        lax.fori_loop(0, TM // ROW_DMA_UNROLL, issue, 0)

    def wait_rows(sl):
        pltpu.make_async_copy(x_ref.at[pl.ds(0, TM * ROW_VREGS), :], xbuf.at[sl], sems.at[sl]).wait()

    @pl.when(nb == 0)
    def _():
        gather(nb, slot)

    @pl.when(jnp.logical_and(nb < n_used, nb + 1 < N_SLOTS_BLK))
    def _():
        gather(nb + 1, 1 - slot)

    @pl.when(nb <= n_used)
    def _():
        wait_rows(slot)

    @pl.when(nb < n_used)
    def _():
        prev = be_ref[jnp.maximum(nb - 1, 0)]
        fresh = jnp.logical_or(nb == 0, be_ref[nb] != prev)

        @pl.when(fresh)
        def _():
            w1b[...] = w1_ref[...].astype(BF16)
            w3b[...] = w3_ref[...].astype(BF16)
            w2b[...] = w2_ref[...].astype(BF16)

        x = _rms(_load_rows_from_tiles(xbuf.at[slot], TM), g_ref[...]).astype(BF16)
        a = jnp.dot(x, w1b[...], preferred_element_type=F32)
        b = jnp.dot(x, w3b[...], preferred_element_type=F32)
        mid = (_silu(a) * b).astype(BF16)
        y = jnp.dot(mid, w2b[...], preferred_element_type=F32)
        _store_rows_as_tiles(y_ref, y)

    @pl.when(nb >= n_used)
    def _():
        y_ref[...] = jnp.zeros_like(y_ref)


def _experts(x_tiles, g_ffn, src, blk_exp, n_used, layer, w1, w3, w2):
    rows = pl.BlockSpec((TM * ROW_VREGS, LANES), lambda nb, *_: (nb, 0))

    def wspec(a, b):
        return pl.BlockSpec((None, None, a, b), lambda nb, be, *_: (layer, be[nb], 0, 0))

    return pl.pallas_call(
        _ffn_kernel,
        grid_spec=pltpu.PrefetchScalarGridSpec(
            num_scalar_prefetch=3, grid=(N_SLOTS_BLK,),
            in_specs=[pl.BlockSpec(memory_space=pl.ANY), pl.BlockSpec((1, D_MODEL), lambda nb, *_: (0, 0)),
                      wspec(D_MODEL, D_EXPERT), wspec(D_MODEL, D_EXPERT), wspec(D_EXPERT, D_MODEL)],
            out_specs=rows,
            scratch_shapes=[pltpu.VMEM((D_MODEL, D_EXPERT), BF16), pltpu.VMEM((D_MODEL, D_EXPERT), BF16),
                            pltpu.VMEM((D_EXPERT, D_MODEL), BF16),
                            pltpu.VMEM((2, TM * ROW_VREGS, LANES), F32), pltpu.SemaphoreType.DMA((2,))]),
        out_shape=jax.ShapeDtypeStruct((N_SLOTS * ROW_VREGS, LANES), F32),
        compiler_params=_cparams(("arbitrary",)),
        name="experts",
    )(blk_exp, n_used, src, x_tiles, g_ffn, w1, w3, w2)


def _moe(x_tiles, g_ffn, meta, counts, layer, w1, w3, w2):
    dest, blk_exp, n_used, pad_lo, pad_hi = _plan(meta, counts)
    src = _invert(dest, pad_lo, pad_hi, n_used)
    return _experts(x_tiles, g_ffn, src, blk_exp, n_used, layer, w1, w3, w2), dest


def _pool_kernel(dest_ref, x1_ref, ys_ref, gate0_ref, gmix_ref, sp_ref, pw_ref, ps_ref, gffn_ref,
                 wr_ref, br_ref, tri_ref,
                 x3_ref, hkeep_ref, hs_ref, meta_ref, gate_ref, cnt_out_ref,
                 cnt_ref, ext_ref, e1_ref, e2_ref, e3_ref, e4_ref, mixed_ref, ybuf, ysems):
    i = pl.program_id(0)
    x2 = _ExpertRows(dest_ref, ys_ref, ybuf, ysems).fetch_combine(_load_rows_from_tiles(x1_ref, TR), gate0_ref)
    h = _rms(x2, gmix_ref[...])
    H = POOL_HALO
    lvl_refs = (e1_ref, e2_ref, e3_ref, e4_ref)

    @pl.when(i < SAMPLE_TILE)
    def _():
        hkeep_ref[...] = h[TR - 16:TR]
        @pl.when(i % TILES_PER_SEQ == 0)
        def _():
            ext_ref[0:H, :] = jnp.zeros((H, D_MODEL), F32)
        ext_ref[H:H + TR, :] = h
        src = ext_ref
        for lv in range(4):
            sh = 1 << lv
            lo = 8 * (lv + 1)
            c0 = POOL_GROUP * lv
            dst = lvl_refs[lv]
            dst[lo:H + TR, c0:] = src[lo:H + TR, c0:] + src[lo - sh:H + TR - sh, c0:]
            src = dst
        pos = (i % TILES_PER_SEQ) * TR + lax.broadcasted_iota(I32, (TR, 1), 0)
        for gi, w in enumerate(POOL_WINDOWS):
            cs = slice(gi * POOL_GROUP, (gi + 1) * POOL_GROUP)
            inv = 1.0 / jnp.minimum(w, pos + 1).astype(F32)
            mixed_ref[:, cs] = lvl_refs[gi][H:H + TR, cs] * inv - h[:, cs]
        ext_ref[0:H, :] = h[TR - H:TR]

    @pl.when(i == SAMPLE_TILE)
    def _():
        hs = h[0:DEC_BATCH]
        hs_ref[...] = hs
        mixed_ref[...] = jnp.zeros_like(mixed_ref)
        for gi, w in enumerate(POOL_WINDOWS):
            cs = slice(gi * POOL_GROUP, (gi + 1) * POOL_GROUP)
            win = hs[:, cs]
            for dlt in range(1, w):
                win = win + sp_ref[POOL_KEEP - dlt][:, cs]
            mixed_ref[0:DEC_BATCH, cs] = win * (1.0 / w) - hs[:, cs]

    mixed = mixed_ref[...].astype(BF16)
    outs = [jnp.dot(mixed[:, gi * POOL_GROUP:(gi + 1) * POOL_GROUP], pw_ref[gi], preferred_element_type=F32)
            for gi in range(len(POOL_WINDOWS))]
    x3 = x2 + jnp.concatenate(outs, -1) * ps_ref[...]
    _store_rows_as_tiles(x3_ref, x3)
    h2 = _rms(x3, gffn_ref[...])
    _route(h2, wr_ref, br_ref, tri_ref, cnt_ref, meta_ref, gate_ref, cnt_out_ref)


def _const_spec_p(a):
    return pl.BlockSpec(a.shape, lambda i, *_, n=a.ndim: (0,) * n)


def _pool_route(dest, x1, ys, gates0, g_mix, sp_t, pw_bf16, p_scale, g_ffn, wr, br, tri):
    def rows(shape):
        return pl.BlockSpec(shape, lambda i, *_: (i, 0))

    ext = pltpu.VMEM((POOL_HALO + TR, D_MODEL), F32)
    consts = [g_mix, sp_t, pw_bf16, p_scale, g_ffn, wr, br, tri]
    return pl.pallas_call(
        _pool_kernel,
        grid_spec=pltpu.PrefetchScalarGridSpec(
            num_scalar_prefetch=1, grid=(N_TILES,),
            in_specs=[rows((TR * ROW_VREGS, LANES)), pl.BlockSpec(memory_space=pl.ANY), rows((TR, 8))]
                     + [_const_spec_p(a) for a in consts],
            out_specs=[rows((TR * ROW_VREGS, LANES)),
                       pl.BlockSpec((None, 16, D_MODEL),
                                    lambda i, *_: (jnp.minimum(i // TILES_PER_SEQ, BATCH - 1), 0, 0)),
                       pl.BlockSpec((DEC_BATCH, D_MODEL), lambda i, *_: (0, 0)),
                       pl.BlockSpec((1, 8, TR), lambda i, *_: (i, 0, 0)), rows((TR, 8)),
                       pl.BlockSpec((1, 1, LANES), lambda i, *_: (i, 0, 0))],
            scratch_shapes=[pltpu.VMEM((1, LANES), F32), ext, ext, ext, ext, ext,
                            pltpu.VMEM((TR, D_MODEL), F32)] + _ExpertRows.scratch),
        out_shape=[_TOKEN_TILES,
                   jax.ShapeDtypeStruct((BATCH, 16, D_MODEL), F32),
                   jax.ShapeDtypeStruct((DEC_BATCH, D_MODEL), F32)] + _ROUTE_OUT_SHAPES,
        compiler_params=_cparams(("arbitrary",)),
        name="pool_route",
    )(dest, x1, ys, gates0, *consts)


def _final_kernel(dest_ref, x_ref, ys_ref, gate_ref, g_ref, yp_ref, ysm_ref, ybuf, ysems):
    i = pl.program_id(0)
    x = _load_rows_from_tiles(x_ref, TR)
    y = _rms(_ExpertRows(dest_ref, ys_ref, ybuf, ysems).fetch_combine(x, gate_ref), g_ref[...])

    @pl.when(i < SAMPLE_TILE)
    def _():
        yp_ref[...] = y

    @pl.when(i == SAMPLE_TILE)
    def _():
        ysm_ref[...] = y[0:DEC_BATCH]


def _final(dest, x3, ys, gates, g_final):
    return pl.pallas_call(
        _final_kernel,
        grid_spec=pltpu.PrefetchScalarGridSpec(
            num_scalar_prefetch=1, grid=(N_TILES,),
            in_specs=[pl.BlockSpec((TR * ROW_VREGS, LANES), lambda i, *_: (i, 0)), pl.BlockSpec(memory_space=pl.ANY),
                      pl.BlockSpec((TR, 8), lambda i, *_: (i, 0)), _const_spec_p(g_final)],
            out_specs=[pl.BlockSpec((TR, D_MODEL), lambda i, *_: (jnp.minimum(i, SAMPLE_TILE - 1), 0)),
                       pl.BlockSpec((DEC_BATCH, D_MODEL), lambda i, *_: (0, 0))],
            scratch_shapes=_ExpertRows.scratch),
        out_shape=[jax.ShapeDtypeStruct((NP, D_MODEL), F32), jax.ShapeDtypeStruct((DEC_BATCH, D_MODEL), F32)],
        compiler_params=_cparams(("arbitrary",)),
        name="final_norm",
    )(dest, x3, ys, gates, g_final)


def kernel(x_prompt, x_sample, cache_win_k, cache_win_v, state_hgrn, state_pool, rel_bias, norm_mix, norm_ffn,
           norm_final, w_in, w_out, hgrn_lb, hgrn_gnorm, pool_w, pool_scale, moe_wg_group, moe_bg_group,
           moe_wg_exp, moe_bg_exp, moe_w1, moe_w3, moe_w2):
    xp = x_prompt.reshape(NP, D_MODEL)
    xs_pad = jnp.pad(x_sample.reshape(DEC_BATCH, D_MODEL), ((0, TR - DEC_BATCH), (0, 0)))
    lb = jnp.cumsum(jax.nn.softmax(hgrn_lb.astype(F32), axis=0), axis=0)[0:1]
    gnorm = hgrn_gnorm[0:1].astype(F32)

    q, k, v, hq, hf, hi, hg, k_win, v_win = _inproj(xp, xs_pad, norm_mix[0:1], w_in[0].astype(BF16))
    oa = _attention_prompt(q, k, v, _attn_bias_tables(rel_bias))
    ob, s_prompt = _hgrn_prompt(hq, hf, hi, hg, lb, gnorm)
    oa_s, ob_s, s_sample = _sample_mixers(
        q, k, v, hq, hf, hi, hg,
        cache_win_k[0], cache_win_v[0], state_hgrn[0], _sample_bias_tables(rel_bias), lb, gnorm)
    wr, br, tri = _router_operands(moe_wg_group[0], moe_bg_group[0], moe_wg_exp[0], moe_bg_exp[0])
    x1, meta, gates0, counts = _outproj_route(xp, xs_pad, oa, oa_s, ob, ob_s, w_out[0].astype(BF16),
                                              norm_ffn[0:1], wr, br, tri)
    ys, dest = _moe(x1, norm_ffn[0:1], meta, counts, 0, moe_w1, moe_w3, moe_w2)

    wr, br, tri = _router_operands(moe_wg_group[1], moe_bg_group[1], moe_wg_exp[1], moe_bg_exp[1])
    sp_t = jnp.transpose(state_pool[0], (1, 0, 2))
    x3, hkeep, hsample, meta, gates1, counts = _pool_route(
        dest, x1, ys, gates0, norm_mix[1:2], sp_t, pool_w[0].astype(BF16), pool_scale[0:1], norm_ffn[1:2],
        wr, br, tri)
    ys, dest = _moe(x3, norm_ffn[1:2], meta, counts, 1, moe_w1, moe_w3, moe_w2)
    y_prompt, y_sample = _final(dest, x3, ys, gates1, norm_final.reshape(1, D_MODEL))

    k_s = k[NP:NV].reshape(1, DEC_BATCH, 1, N_HEADS, D_HEAD)
    v_s = v[NP:NV].reshape(1, DEC_BATCH, 1, N_HEADS, D_HEAD)
    return (y_prompt.reshape(BATCH, SEQ, D_MODEL), y_sample.reshape(DEC_BATCH, 1, D_MODEL),
            k_win.reshape(1, BATCH, W_MAX, N_HEADS, D_HEAD), v_win.reshape(1, BATCH, W_MAX, N_HEADS, D_HEAD),
            s_prompt[None], hkeep[:, 16 - POOL_KEEP:][None],
            k_s, v_s, s_sample[None], hsample.reshape(1, DEC_BATCH, 1, D_MODEL))
```

```python
import functools

import numpy as np
import jax
import jax.numpy as jnp
from jax import lax
from jax.experimental import pallas as pl
from jax.experimental.pallas import tpu as pltpu

F32 = jnp.float32
BF16 = jnp.bfloat16
I32 = jnp.int32

D_MODEL = 1024
BATCH = 4
SEQ = 4096
DEC_BATCH = 32
PAST_LEN = 8192
W_MAX = 2048
N_HEADS = 4
D_HEAD = 128
A_WIDTH = N_HEADS * D_HEAD
N_PROJ = 7 * A_WIDTH
DILATED = ((128, 1), (512, 4), (2048, 16))
NUM_BUCKETS = 32
MAX_DISTANCE = 2048
POOL_WINDOWS = (2, 4, 8, 16)
POOL_GROUP = 256
POOL_KEEP = 15
N_GROUPS = 4
EXPERTS_PER_GROUP = 8
N_EXPERTS = 32
D_EXPERT = 512
EPS = 1e-6
NEG = -1e30

LANES = 128
SUBLANES = 8
ROW_VREGS = D_MODEL // LANES
TR = 256
NP = BATCH * SEQ
NV = NP + DEC_BATCH
NT = NP + TR
N_TILES = NT // TR
SAMPLE_TILE = NP // TR
TILES_PER_SEQ = SEQ // TR
TM = 256
N_SLOTS_BLK = (2 * NV + N_EXPERTS * (TM - 1) + TM - 1) // TM
N_SLOTS = N_SLOTS_BLK * TM
QB = 128
ATTN_UNROLL = 8
ATTN_PERIOD = 3 * QB
ROW_DMA_UNROLL = 8
CH = 128
HGRN_HEADS_PER_STEP = 2
N_LEVELS = 7
POOL_HALO = 32
VMEM_LIMIT = 56 * 1024 * 1024


def _cparams(sem=None, vmem=VMEM_LIMIT):
    kw = dict(vmem_limit_bytes=vmem)
    if sem is not None:
        kw["dimension_semantics"] = sem
    return pltpu.CompilerParams(**kw)


def _rms(x, g):
    return x * lax.rsqrt(jnp.mean(x * x, -1, keepdims=True) + EPS) * g


def _sigmoid(x):
    return 1.0 / (1.0 + jnp.exp(-x))


def _silu(x):
    return x * _sigmoid(x)


def _token_tile(i, xp_ref, xs_ref):
    return jnp.where(i == SAMPLE_TILE, xs_ref[...], xp_ref[...])


_PROMPT_ROWS = pl.BlockSpec((TR, D_MODEL), lambda i: (jnp.minimum(i, SAMPLE_TILE - 1), 0))
_SAMPLE_ROWS = pl.BlockSpec((TR, D_MODEL), lambda i: (0, 0))
WIN_TILES = W_MAX // TR


def _window_block(i):
    seq = jnp.minimum(i // TILES_PER_SEQ, BATCH - 1)
    j = jnp.clip(i % TILES_PER_SEQ - (TILES_PER_SEQ - WIN_TILES), 0, WIN_TILES - 1)
    return jnp.where(i >= SAMPLE_TILE, BATCH * WIN_TILES - 1, seq * WIN_TILES + j)


def _inproj_kernel(xp_ref, xs_ref, g_ref, w_ref, *out_refs):
    i = pl.program_id(0)
    h = _rms(_token_tile(i, xp_ref, xs_ref), g_ref[...])
    p = jnp.dot(h.astype(BF16), w_ref[...], preferred_element_type=F32)
    for n, o_ref in enumerate(out_refs[:7]):
        o_ref[...] = p[:, n * A_WIDTH:(n + 1) * A_WIDTH]

    @pl.when(jnp.logical_and(i < SAMPLE_TILE, i % TILES_PER_SEQ >= TILES_PER_SEQ - WIN_TILES))
    def _():
        for n, o_ref in ((1, out_refs[7]), (2, out_refs[8])):
            for h_i in range(N_HEADS):
                c0 = n * A_WIDTH + h_i * D_HEAD
                o_ref[pl.ds(h_i, TR, stride=N_HEADS), :] = p[:, c0:c0 + D_HEAD]


def _inproj(xp, xs_pad, g, w_bf16):
    out_sds = ([jax.ShapeDtypeStruct((NT, A_WIDTH), F32)] * 7
               + [jax.ShapeDtypeStruct((BATCH * W_MAX * N_HEADS, D_HEAD), F32)] * 2)
    win = pl.BlockSpec((TR * N_HEADS, D_HEAD), lambda i: (_window_block(i), 0))
    return pl.pallas_call(
        _inproj_kernel,
        grid=(N_TILES,),
        in_specs=[_PROMPT_ROWS, _SAMPLE_ROWS,
                  pl.BlockSpec((1, D_MODEL), lambda i: (0, 0)),
                  pl.BlockSpec((D_MODEL, N_PROJ), lambda i: (0, 0))],
        out_specs=[pl.BlockSpec((TR, A_WIDTH), lambda i: (i, 0))] * 7 + [win, win],
        out_shape=out_sds,
        compiler_params=_cparams(("arbitrary",)),
        name="inproj",
    )(xp, xs_pad, g, w_bf16)


def _t5_bucket(dist):
    max_exact = NUM_BUCKETS // 2
    d = np.asarray(dist)
    large = max_exact + np.floor(np.log(np.maximum(d, 1) / max_exact)
                                 / np.log(MAX_DISTANCE / max_exact) * (NUM_BUCKETS - max_exact)).astype(np.int32)
    large = np.minimum(large, NUM_BUCKETS - 1)
    return np.where(d < max_exact, d, large).astype(np.int32)


def _attn_bias_tables(rel_bias):
    period = ATTN_PERIOD
    m = np.arange(period)
    u = np.where(m < 2 * QB, m, m - period)
    pick = np.zeros((len(DILATED), 2, period, NUM_BUCKETS), np.float32)
    mask = np.zeros((len(DILATED), 2, period, 1), np.float32)
    for bi, (w, d) in enumerate(DILATED):
        nk = w // d
        for vi, off in enumerate((0, QB)):
            j = off - u
            ok = (j >= 0) & (j <= nk)
            pick[bi, vi, m[ok], _t5_bucket(d * j[ok])] = 1.0
            mask[bi, vi, ~ok, 0] = NEG
    vec = jnp.einsum("bvmk,kh->hbvm", pick, rel_bias.astype(F32), precision=lax.Precision.HIGHEST)
    vec = vec + jnp.transpose(mask, (3, 0, 1, 2))
    return vec[:, :, :, None, :]


def _attn_kernel(q_ref, k_ref, v_ref, vec_ref, o_ref,
                 q4, k4, v4, qd, kd, vd, ud, md, sd, u_acc, m_acc, s_acc, bias_ref):
    scale = D_HEAD ** -0.5
    c4 = SEQ // 4
    c16 = SEQ // 16
    for bi in range(len(DILATED)):
        for vi in range(2):
            rows = jnp.broadcast_to(vec_ref[bi, vi], (QB, ATTN_PERIOD))
            bias_ref[bi, vi] = pltpu.roll(rows, 0, 1, stride=1, stride_axis=0)[:, :2 * QB]

    def block_stats(bi, t, nb):
        has_prev = jnp.minimum(t % nb, 1)
        q0 = pl.multiple_of(t * QB, QB)
        k0 = pl.multiple_of((t - has_prev) * QB, QB)
        qb = qd[pl.ds(q0, QB), :]
        kb = kd[pl.ds(k0, 2 * QB), :]
        vb = vd[pl.ds(k0, 2 * QB), :]
        s = lax.dot_general(qb, kb, (((1,), (1,)), ((), ())), preferred_element_type=F32)
        s = s + bias_ref[bi, has_prev]
        mb = jnp.max(s, -1, keepdims=True)
        p = jnp.exp(s - mb)
        sb = jnp.sum(p, -1, keepdims=True)
        u = jnp.dot(p.astype(BF16), vb, preferred_element_type=F32)
        return q0, mb, sb, u

    def run_blocks(bi, nb, consume):
        def body(g, carry):
            for un in range(ATTN_UNROLL):
                q0, mb, sb, u = block_stats(bi, g * ATTN_UNROLL + un, nb)
                consume(pl.ds(q0, QB), mb, sb, u)
            return carry
        lax.fori_loop(0, SEQ // QB // ATTN_UNROLL, body, 0)

    def merged(rows, m_ref, s_ref, u_ref, mb, sb, u):
        m_old = m_ref[rows, :]
        m_new = jnp.maximum(m_old, mb)
        a = jnp.exp(m_old - m_new)
        b = jnp.exp(mb - m_new)
        return m_new, a * s_ref[rows, :] + b * sb, a * u_ref[rows, :] + b * u

    for r in range(4):
        src, dst = pl.ds(r, c4, stride=4), pl.ds(r * c4, c4)
        q4[dst, :] = q_ref[src, :] * scale
        k4[dst, :] = k_ref[src, :]
        v4[dst, :] = v_ref[src, :]

    for r in range(4):
        for j in range(4):
            src, dst = pl.ds(r * c4 + j, c16, stride=4), pl.ds((r + 4 * j) * c16, c16)
            qd[dst, :] = q4[src, :].astype(BF16)
            kd[dst, :] = k4[src, :].astype(BF16)
            vd[dst, :] = v4[src, :].astype(BF16)

    def keep16(rows, mb, sb, u):
        ud[rows, :] = u
        md[rows, :] = jnp.broadcast_to(mb, (QB, D_HEAD))
        sd[rows, :] = jnp.broadcast_to(sb, (QB, D_HEAD))
    run_blocks(2, c16 // QB, keep16)
    for r in range(4):
        for j in range(4):
            src, dst = pl.ds((r + 4 * j) * c16, c16), pl.ds(r * c4 + j, c16, stride=4)
            u_acc[dst, :] = ud[src, :]
            m_acc[dst, :] = md[src, :]
            s_acc[dst, :] = sd[src, :]

    qd[...] = q4[...].astype(BF16)
    kd[...] = k4[...].astype(BF16)
    vd[...] = v4[...].astype(BF16)

    def merge4(rows, mb, sb, u):
        m_acc[rows, :], s_acc[rows, :], u_acc[rows, :] = merged(rows, m_acc, s_acc, u_acc, mb, sb, u)
    run_blocks(1, c4 // QB, merge4)

    for r in range(4):
        src, dst = pl.ds(r * c4, c4), pl.ds(r, c4, stride=4)
        ud[dst, :] = u_acc[src, :]
        md[dst, :] = m_acc[src, :]
        sd[dst, :] = s_acc[src, :]
    qd[...] = (q_ref[...] * scale).astype(BF16)
    kd[...] = k_ref[...].astype(BF16)
    vd[...] = v_ref[...].astype(BF16)

    def finish(rows, mb, sb, u):
        _, den, num = merged(rows, md, sd, ud, mb, sb, u)
        o_ref[rows, :] = (num / den).astype(o_ref.dtype)
    run_blocks(0, SEQ // QB, finish)


def _attention_prompt(q, k, v, bias_tabs):
    blk = pl.BlockSpec((SEQ, D_HEAD), lambda b, h: (b, h))
    return pl.pallas_call(
        _attn_kernel,
        grid=(BATCH, N_HEADS),
        in_specs=[blk, blk, blk,
                  pl.BlockSpec((None, 3, 2, 1, ATTN_PERIOD), lambda b, h: (h, 0, 0, 0, 0))],
        out_specs=blk,
        out_shape=jax.ShapeDtypeStruct((NP, A_WIDTH), BF16),
        scratch_shapes=[pltpu.VMEM((SEQ, D_HEAD), F32)] * 3
                       + [pltpu.VMEM((SEQ, D_HEAD), BF16)] * 3
                       + [pltpu.VMEM((SEQ, D_HEAD), F32)] * 6
                       + [pltpu.VMEM((len(DILATED), 2, QB, 2 * QB), F32)],
        compiler_params=_cparams(("arbitrary", "arbitrary")),
        name="attn_prompt",
    )(q, k, v, bias_tabs)


def _hgrn_tables():
    t = np.arange(CH)
    u = np.arange(CH)
    sums_q = np.zeros((1 + N_LEVELS, CH, CH), np.float32)
    sums_k = np.zeros((2 + N_LEVELS, CH, CH), np.float32)
    sums_q[0] = (u[None, :] <= t[:, None])
    sums_k[0] = (u[None, :] > t[:, None])
    sums_k[1 + N_LEVELS] = 1.0
    pair = np.zeros((N_LEVELS, CH, CH), np.float32)
    for l in range(N_LEVELS):
        h = CH >> (l + 1)
        is_q = (t // h) % 2 == 1
        half_start = (t // h) * h
        half_end = half_start + h
        sel_q = (u[None, :] >= half_start[:, None]) & (u[None, :] <= t[:, None])
        sel_k = (u[None, :] > t[:, None]) & (u[None, :] < half_end[:, None])
        sums_q[1 + l] = sel_q & is_q[:, None]
        sums_k[1 + l] = sel_k & ~is_q[:, None]
        same = (t[:, None] // (2 * h)) == (t[None, :] // (2 * h))
        pair[l] = same & is_q[:, None] & (~is_q)[None, :]
    sums_kt = np.transpose(sums_k, (2, 0, 1)).reshape(CH, (2 + N_LEVELS) * CH)
    sums_q = sums_q.reshape((1 + N_LEVELS) * CH, CH)
    return (jnp.asarray(np.concatenate([sums_q, sums_q], 1), BF16),
            jnp.asarray(np.concatenate([sums_kt, sums_kt], 0), BF16),
            jnp.asarray(pair), jnp.asarray(np.eye(CH, dtype=np.float32)))


def _split_bf16(x):
    hi = x.astype(BF16)
    return hi, (x - hi.astype(F32)).astype(BF16)


def _hgrn_kernel(q_ref, f_ref, i_ref, g_ref, lb_ref, gn_ref, sq_ref, skt_ref, pair_ref, eye_ref, o_ref, s_ref):
    gn = gn_ref[...]

    def chunk(c, st, hh):
        rows = pl.ds(pl.multiple_of(c * CH, CH), CH)
        cols = slice(hh * D_HEAD, (hh + 1) * D_HEAD)
        lb = lb_ref[:, cols]
        q = _silu(q_ref[rows, cols])
        f = lb + (1.0 - lb) * _sigmoid(f_ref[rows, cols])
        lf = jnp.log(f)
        k = 1.0 - f
        v16 = i_ref[rows, cols].astype(BF16)
        kt = k.T
        exq = jnp.exp(jnp.dot(sq_ref[...], jnp.concatenate(_split_bf16(lf), 0),
                              preferred_element_type=F32))
        exk = jnp.exp(jnp.dot(jnp.concatenate(_split_bf16(lf.T), 1), skt_ref[...],
                              preferred_element_type=F32))
        inter = jnp.dot((q * exq[0:CH]).astype(BF16), st.astype(BF16), preferred_element_type=F32)
        sc = eye_ref[...] * jnp.sum(q * k, -1, keepdims=True)
        for l in range(N_LEVELS):
            ql = (q * exq[(1 + l) * CH:(2 + l) * CH]).astype(BF16)
            klt = (kt * exk[:, (1 + l) * CH:(2 + l) * CH]).astype(BF16)
            sc = sc + pair_ref[l] * jnp.dot(ql, klt, preferred_element_type=F32)
        o = inter + jnp.dot(sc.astype(BF16), v16, preferred_element_type=F32)
        st_new = (st * exk[:, (1 + N_LEVELS) * CH:]
                  + jnp.dot((kt * exk[:, 0:CH]).astype(BF16), v16, preferred_element_type=F32))
        o = _rms(o, gn) * _silu(g_ref[rows, cols])
        o_ref[rows, cols] = o.astype(o_ref.dtype)
        return st_new

    def step(c, states):
        return tuple(chunk(c, st, hh) for hh, st in enumerate(states))

    zero = jnp.zeros((D_HEAD, D_HEAD), F32)
    states = lax.fori_loop(0, SEQ // CH, step, (zero,) * HGRN_HEADS_PER_STEP)
    for hh, st in enumerate(states):
        s_ref[hh] = st


def _hgrn_prompt(hq, hf, hi, hg, lb, gnorm):
    tables = _hgrn_tables()
    width = HGRN_HEADS_PER_STEP * D_HEAD
    blk = pl.BlockSpec((SEQ, width), lambda b, h: (b, h))

    def full(a):
        return pl.BlockSpec(a.shape, lambda b, h, n=a.ndim: (0,) * n)

    return pl.pallas_call(
        _hgrn_kernel,
        grid=(BATCH, N_HEADS // HGRN_HEADS_PER_STEP),
        in_specs=[blk, blk, blk, blk,
                  pl.BlockSpec((1, width), lambda b, h: (0, h)),
                  pl.BlockSpec((1, D_HEAD), lambda b, h: (0, 0))] + [full(a) for a in tables],
        out_specs=[blk, pl.BlockSpec((None, HGRN_HEADS_PER_STEP, D_HEAD, D_HEAD), lambda b, h: (b, h, 0, 0))],
        out_shape=[jax.ShapeDtypeStruct((NP, A_WIDTH), BF16),
                   jax.ShapeDtypeStruct((BATCH, N_HEADS, D_HEAD, D_HEAD), F32)],
        compiler_params=_cparams(("arbitrary", "arbitrary")),
        name="hgrn_prompt",
    )(hq, hf, hi, hg, lb, gnorm, *tables)


def _sample_bias_tables(rel_bias):
    j = np.concatenate([QB - np.arange(QB), [0]])
    pick = np.zeros((len(DILATED), QB + 1, NUM_BUCKETS), np.float32)
    for bi, (w, d) in enumerate(DILATED):
        pick[bi, np.arange(QB + 1), _t5_bucket(d * j)] = 1.0
    return jnp.einsum("bjk,kh->bhj", pick, rel_bias.astype(F32), precision=lax.Precision.HIGHEST)[..., None]


def _bf16_round(x):
    return x.astype(BF16).astype(F32)


def _col(row, eye):
    return jnp.sum(eye * row, -1, keepdims=True)


NEAR_POS = 4 * QB
FAR_STEPS = (W_MAX - NEAR_POS) // 16


def _cached_rows(near_ref, far_ref, d, h):
    def near(first_pos, n, step):
        return near_ref[pl.ds((first_pos - (W_MAX - NEAR_POS)) * N_HEADS + h, n, stride=step * N_HEADS), :]

    if QB * d <= NEAR_POS:
        return near(W_MAX - QB * d, QB, d)
    return jnp.concatenate([far_ref[:, h, :], near(W_MAX - NEAR_POS, QB - FAR_STEPS, d)], 0)


def _sample_kernel(q_ref, k_ref, v_ref, hq_ref, hf_ref, hi_ref, hg_ref, kn_ref, kf_ref, vn_ref, vf_ref, s0_ref,
                   bias_ref, lb_ref, gn_ref, eye_ref, oa_ref, ob_ref, s_ref, oa_acc, ob_acc):
    b = pl.program_id(0)
    row = pl.ds(b, 1)
    scale = D_HEAD ** -0.5
    q = _bf16_round(q_ref[row, :] * scale)
    kn = _bf16_round(k_ref[row, :])
    vn = _bf16_round(v_ref[row, :])
    eye = eye_ref[...]

    stats = []
    for bi, (w, d) in enumerate(DILATED):
        per_head = []
        for h in range(N_HEADS):
            hs = slice(h * D_HEAD, (h + 1) * D_HEAD)
            kb = _bf16_round(_cached_rows(kn_ref, kf_ref, d, h))
            vb = _bf16_round(_cached_rows(vn_ref, vf_ref, d, h))
            s = jnp.sum(kb * q[:, hs], -1, keepdims=True) + bias_ref[bi, h, 0:QB]
            s0 = jnp.sum(q[:, hs] * kn[:, hs], -1, keepdims=True) + bias_ref[bi, h, QB:QB + 1]
            m = jnp.maximum(jnp.max(s, 0, keepdims=True), s0)
            p = jnp.exp(s - m)
            p0 = jnp.exp(s0 - m)
            ssum = jnp.sum(p, 0, keepdims=True) + p0
            u = jnp.sum(_bf16_round(p) * vb, 0, keepdims=True) + _bf16_round(p0) * vn[:, hs]
            per_head.append((m, ssum, u))
        stats.append(per_head)
    outs = []
    for h in range(N_HEADS):
        m_all = functools.reduce(jnp.maximum, [stats[bi][h][0] for bi in range(3)])
        num = 0.0
        den = 0.0
        for bi in range(3):
            m, ssum, u = stats[bi][h]
            c = jnp.exp(m - m_all)
            num = num + c * u
            den = den + c * ssum
        outs.append(num / den)
    oa_acc[row, :] = jnp.concatenate(outs, -1)

    qh = _silu(hq_ref[row, :])
    lb = lb_ref[...]
    f = lb + (1.0 - lb) * _sigmoid(hf_ref[row, :])
    vi = hi_ref[row, :]
    gate = _silu(hg_ref[row, :])
    gn = gn_ref[...]
    obs = []
    for h in range(N_HEADS):
        hs = slice(h * D_HEAD, (h + 1) * D_HEAD)
        f_col = _col(f[:, hs], eye)
        q_col = _col(qh[:, hs], eye)
        s_old = s0_ref[h]
        s_ref[h] = f_col * s_old + (1.0 - f_col) * vi[:, hs]
        inter = jnp.sum(_bf16_round(q_col * f_col) * _bf16_round(s_old), 0, keepdims=True)
        qk = jnp.sum(qh[:, hs] * (1.0 - f[:, hs]), -1, keepdims=True)
        o = inter + qk * vi[:, hs]
        obs.append(_rms(o, gn) * gate[:, hs])
    ob_acc[row, :] = jnp.concatenate(obs, -1)

    @pl.when(b == DEC_BATCH - 1)
    def _():
        pad = jnp.zeros((TR - DEC_BATCH, A_WIDTH), F32)
        oa_ref[...] = jnp.concatenate([oa_acc[...], pad], 0).astype(oa_ref.dtype)
        ob_ref[...] = jnp.concatenate([ob_acc[...], pad], 0).astype(ob_ref.dtype)


def _sample_mixers(q, k, v, hq, hf, hi, hg, cache_k, cache_v, state, bias_s, lb, gnorm):
    tile = pl.BlockSpec((TR, A_WIDTH), lambda b: (SAMPLE_TILE, 0))
    out_tile = pl.BlockSpec((TR, A_WIDTH), lambda b: (0, 0))
    near = pl.BlockSpec((None, NEAR_POS * N_HEADS, D_HEAD), lambda b: (b, W_MAX // NEAR_POS - 1, 0))
    far = pl.BlockSpec((None, FAR_STEPS, SUBLANES, D_HEAD), lambda b: (b, 0, 0, 0))
    rows = (DEC_BATCH, W_MAX * N_HEADS, D_HEAD)
    groups = (DEC_BATCH, W_MAX // 16, 16 * N_HEADS, D_HEAD)
    eye = jnp.eye(D_HEAD, dtype=F32)

    return pl.pallas_call(
        _sample_kernel,
        grid=(DEC_BATCH,),
        in_specs=[tile, tile, tile, tile, tile, tile, tile, near, far, near, far,
                  pl.BlockSpec((None, N_HEADS, D_HEAD, D_HEAD), lambda b: (b, 0, 0, 0)),
                  pl.BlockSpec(bias_s.shape, lambda b: (0, 0, 0, 0)),
                  pl.BlockSpec((1, A_WIDTH), lambda b: (0, 0)),
                  pl.BlockSpec((1, D_HEAD), lambda b: (0, 0)),
                  pl.BlockSpec((D_HEAD, D_HEAD), lambda b: (0, 0))],
        out_specs=[out_tile, out_tile,
                   pl.BlockSpec((None, N_HEADS, D_HEAD, D_HEAD), lambda b: (b, 0, 0, 0))],
        out_shape=[jax.ShapeDtypeStruct((TR, A_WIDTH), BF16), jax.ShapeDtypeStruct((TR, A_WIDTH), BF16),
                   jax.ShapeDtypeStruct((DEC_BATCH, N_HEADS, D_HEAD, D_HEAD), F32)],
        scratch_shapes=[pltpu.VMEM((DEC_BATCH, A_WIDTH), F32)] * 2,
        compiler_params=_cparams(("arbitrary",)),
        name="sample_mixers",
    )(q, k, v, hq, hf, hi, hg, cache_k.reshape(rows), cache_k.reshape(groups),
      cache_v.reshape(rows), cache_v.reshape(groups), state, bias_s, lb, gnorm, eye)


def _store_rows_as_tiles(ref, val):
    n = val.shape[0]
    for j in range(ROW_VREGS):
        ref[pl.ds(j, n, stride=ROW_VREGS), :] = val[:, j * LANES:(j + 1) * LANES]


def _load_rows_from_tiles(ref, n):
    return jnp.concatenate([ref[pl.ds(j, n, stride=ROW_VREGS), :] for j in range(ROW_VREGS)], -1)


def _route(h2, wr_ref, br_ref, tri_ref, cnt_ref, meta_ref, gate_ref, cnt_out_ref):
    i = pl.program_id(0)

    @pl.when(i == 0)
    def _():
        cnt_ref[...] = jnp.zeros_like(cnt_ref)

    logits = jnp.dot(h2.astype(BF16), wr_ref[...], preferred_element_type=F32) + br_ref[...]
    lane = lax.broadcasted_iota(I32, (TR, LANES), 1).astype(F32)
    big = float(1 << 20)
    is_g = lane < N_GROUPS
    gl = jnp.where(is_g, logits, NEG)
    gmax = jnp.max(gl, -1, keepdims=True)
    gsel = jnp.min(jnp.where(gl == gmax, lane, big), -1, keepdims=True)
    pg = 1.0 / jnp.sum(jnp.where(is_g, jnp.exp(gl - gmax), 0.0), -1, keepdims=True)
    lo = N_GROUPS + EXPERTS_PER_GROUP * gsel
    in_grp = jnp.logical_and(lane >= lo, lane < lo + EXPERTS_PER_GROUP)
    el = jnp.where(in_grp, logits, NEG)
    m1 = jnp.max(el, -1, keepdims=True)
    i1 = jnp.min(jnp.where(el == m1, lane, big), -1, keepdims=True)
    el2 = jnp.where(lane == i1, NEG, el)
    m2 = jnp.max(el2, -1, keepdims=True)
    i2 = jnp.min(jnp.where(el2 == m2, lane, big), -1, keepdims=True)
    r = jnp.exp(m2 - m1)
    g1 = pg / (1.0 + r)
    g2 = pg * r / (1.0 + r)
    e1 = i1 - N_GROUPS
    e2 = i2 - N_GROUPS

    tok = i * TR + lax.broadcasted_iota(I32, (TR, 1), 0)
    valid = tok < NV
    oh1 = jnp.logical_and(lane == e1, valid)
    oh2 = jnp.logical_and(lane == e2, valid)
    oh = jnp.where(jnp.logical_or(oh1, oh2), 1.0, 0.0)
    before = jnp.dot(tri_ref[...], oh.astype(BF16), preferred_element_type=F32) + cnt_ref[...]
    rank1 = jnp.sum(jnp.where(oh1, before, 0.0), -1, keepdims=True)
    rank2 = jnp.sum(jnp.where(oh2, before, 0.0), -1, keepdims=True)
    tile_cnt = jnp.sum(oh, 0, keepdims=True)
    cnt_ref[...] = cnt_ref[...] + tile_cnt

    eye = jnp.where(lax.broadcasted_iota(I32, (TR, TR), 0) == lax.broadcasted_iota(I32, (TR, TR), 1), 1.0, 0.0)
    rows = [jnp.sum(eye * col, 0, keepdims=True) for col in (e1, e2, rank1, rank2)]
    meta_ref[0] = jnp.concatenate(rows + [jnp.zeros((4, TR), F32)], 0).astype(I32)
    gates = jnp.where(lane == 0, g1, jnp.where(lane == 1, g2, 0.0))
    gates = jnp.where(valid, gates, 0.0)
    gate_ref[...] = gates[:, 0:8]
    cnt_out_ref[0] = tile_cnt


def _router_operands(wg_group, bg_group, wg_exp, bg_exp):
    wr = jnp.zeros((D_MODEL, LANES), F32)
    wr = wr.at[:, 0:N_GROUPS].set(wg_group.astype(F32)).at[:, N_GROUPS:N_GROUPS + N_EXPERTS].set(wg_exp.astype(F32))
    br = jnp.zeros((1, LANES), F32)
    br = br.at[0, 0:N_GROUPS].set(bg_group.astype(F32)).at[0, N_GROUPS:N_GROUPS + N_EXPERTS].set(bg_exp.astype(F32))
    tri = jnp.asarray(np.tril(np.ones((TR, TR), np.float32), -1), BF16)
    return wr.astype(BF16), br, tri


_ROUTE_OUT_SPECS = [pl.BlockSpec((1, 8, TR), lambda i: (i, 0, 0)),
                    pl.BlockSpec((TR, 8), lambda i: (i, 0)),
                    pl.BlockSpec((1, 1, LANES), lambda i: (i, 0, 0))]
_ROUTE_OUT_SHAPES = [jax.ShapeDtypeStruct((N_TILES, 8, TR), I32), jax.ShapeDtypeStruct((NT, 8), F32),
                     jax.ShapeDtypeStruct((N_TILES, 1, LANES), F32)]


def _const_spec(a):
    return pl.BlockSpec(a.shape, lambda i, n=a.ndim: (0,) * n)


def _outproj_kernel(xp_ref, xs_ref, oa_ref, oas_ref, ob_ref, obs_ref, w_ref, g_ref, wr_ref, br_ref, tri_ref,
                    x1_ref, meta_ref, gate_ref, cnt_out_ref, cnt_ref):
    i = pl.program_id(0)
    x = _token_tile(i, xp_ref, xs_ref)
    mixed = jnp.concatenate([_token_tile(i, oa_ref, oas_ref), _token_tile(i, ob_ref, obs_ref)], -1)
    x1 = x + jnp.dot(mixed, w_ref[...], preferred_element_type=F32)
    x1_ref[...] = x1
    h2 = _rms(x1, g_ref[...])
    _route(h2, wr_ref, br_ref, tri_ref, cnt_ref, meta_ref, gate_ref, cnt_out_ref)


def _outproj_route(xp, xs_pad, oa, oa_s, ob, ob_s, w_out_bf16, g_ffn, wr, br, tri):
    row = pl.BlockSpec((TR, D_MODEL), lambda i: (i, 0))
    half = pl.BlockSpec((TR, A_WIDTH), lambda i: (jnp.minimum(i, SAMPLE_TILE - 1), 0))
    half_s = pl.BlockSpec((TR, A_WIDTH), lambda i: (0, 0))
    return pl.pallas_call(
        _outproj_kernel,
        grid=(N_TILES,),
        in_specs=[_PROMPT_ROWS, _SAMPLE_ROWS, half, half_s, half, half_s,
                  _const_spec(w_out_bf16), _const_spec(g_ffn),
                  _const_spec(wr), _const_spec(br), _const_spec(tri)],
        out_specs=[row] + _ROUTE_OUT_SPECS,
        out_shape=[jax.ShapeDtypeStruct((NT, D_MODEL), F32)] + _ROUTE_OUT_SHAPES,
        scratch_shapes=[pltpu.VMEM((1, LANES), F32)],
        compiler_params=_cparams(("arbitrary",)),
        name="outproj_route",
    )(xp, xs_pad, oa, oa_s, ob, ob_s, w_out_bf16, g_ffn, wr, br, tri)


def _plan(meta, tile_counts):
    tcnt = tile_counts[:, 0, :N_EXPERTS].astype(I32)
    cnt = jnp.sum(tcnt, 0)
    nblk = (cnt + TM - 1) // TM
    blk_end = jnp.cumsum(nblk)
    pstart = (blk_end - nblk) * TM
    n_used = blk_end[-1]
    blk = jnp.minimum(jnp.arange(N_SLOTS_BLK, dtype=I32), n_used - 1)
    blk_exp = jnp.sum((blk[:, None] >= blk_end[None, :]).astype(I32), -1)
    blk_exp = jnp.minimum(blk_exp, N_EXPERTS - 1)
    eid = meta[:, 0:2, :]
    experts = jnp.arange(N_EXPERTS, dtype=I32)[:, None, None, None]
    start_of = jnp.sum(jnp.where(eid[None] == experts, pstart[:, None, None, None], 0), 0)
    tok = (jnp.arange(N_TILES, dtype=I32) * TR)[:, None, None] + jnp.arange(TR, dtype=I32)
    dest = jnp.where(tok < NV, start_of + meta[:, 2:4, :], 0).astype(I32).reshape(-1)
    pad_lo = pstart + cnt
    pad_hi = pstart + nblk * TM
    return dest, blk_exp, n_used.reshape(1).astype(I32), pad_lo.astype(I32), pad_hi.astype(I32)


def _row(ref, r):
    return ref.at[pl.ds(pl.multiple_of(r * ROW_VREGS, ROW_VREGS), ROW_VREGS), :]


def _dest_index(tile, kk, t):
    return (2 * tile + kk) * TR + t


def _dispatch_kernel(dest_ref, plo_ref, phi_ref, nused_ref, x_ref, g_ref, xs_ref, stage, zero_ref, sems, zsem):
    i = pl.program_id(0)
    last = pl.num_programs(0) - 1
    slot = i % 2
    _store_rows_as_tiles(stage.at[slot], _rms(x_ref[...], g_ref[...]))

    def wait_tile(n_tok, s):
        for _ in range(2):
            pltpu.make_async_copy(stage.at[s, pl.ds(0, n_tok * ROW_VREGS), :],
                                  xs_ref.at[pl.ds(0, n_tok * ROW_VREGS), :], sems.at[s]).wait()

    def scatter(n_tok):
        def issue(t, carry):
            src = stage.at[slot, pl.ds(pl.multiple_of(t * ROW_VREGS, ROW_VREGS), ROW_VREGS), :]
            for kk in range(2):
                pltpu.make_async_copy(src, _row(xs_ref, dest_ref[_dest_index(i, kk, t)]),
                                      sems.at[slot]).start(priority=kk)
            return carry
        lax.fori_loop(0, n_tok, issue, 0)

    @pl.when(i < last)
    def _():
        scatter(TR)

    @pl.when(i > 0)
    def _():
        wait_tile(TR, 1 - slot)

    @pl.when(i == last)
    def _():
        n_last = NV - (N_TILES - 1) * TR
        scatter(n_last)
        wait_tile(n_last, slot)
        zero_ref[...] = jnp.zeros_like(zero_ref)

        zero_row = zero_ref.at[pl.ds(0, ROW_VREGS), :]

        def per_expert(e, carry):
            def fill(p, c2):
                pltpu.make_async_copy(zero_row, _row(xs_ref, p), zsem).start()
                return c2
            lax.fori_loop(plo_ref[e], phi_ref[e], fill, 0)

            def fill_wait(p, c2):
                pltpu.make_async_copy(zero_row, _row(xs_ref, 0), zsem).wait()
                return c2
            lax.fori_loop(plo_ref[e], phi_ref[e], fill_wait, 0)
            return carry
        lax.fori_loop(0, N_EXPERTS, per_expert, 0)

        def block_of(nb):
            return xs_ref.at[pl.ds(pl.multiple_of(nb * (TM * ROW_VREGS), TM * ROW_VREGS), TM * ROW_VREGS), :]

        def fill_blk(nb, carry):
            pltpu.make_async_copy(zero_ref, block_of(nb), zsem).start()
            return carry
        lax.fori_loop(nused_ref[0], N_SLOTS_BLK, fill_blk, 0)

        def fill_blk_wait(nb, carry):
            pltpu.make_async_copy(zero_ref, block_of(0), zsem).wait()
            return carry
        lax.fori_loop(nused_ref[0], N_SLOTS_BLK, fill_blk_wait, 0)


def _dispatch(x, g_ffn, dest, pad_lo, pad_hi, n_used):
    return pl.pallas_call(
        _dispatch_kernel,
        grid_spec=pltpu.PrefetchScalarGridSpec(
            num_scalar_prefetch=4, grid=(N_TILES,),
            in_specs=[pl.BlockSpec((TR, D_MODEL), lambda i, *_: (i, 0)),
                      pl.BlockSpec((1, D_MODEL), lambda i, *_: (0, 0))],
            out_specs=pl.BlockSpec(memory_space=pl.ANY),
            scratch_shapes=[pltpu.VMEM((2, TR * ROW_VREGS, LANES), F32),
                            pltpu.VMEM((TM * ROW_VREGS, LANES), F32),
                            pltpu.SemaphoreType.DMA((2,)), pltpu.SemaphoreType.DMA(())]),
        out_shape=jax.ShapeDtypeStruct((N_SLOTS * ROW_VREGS, LANES), F32),
        compiler_params=_cparams(("arbitrary",)),
        name="dispatch",
    )(dest, pad_lo, pad_hi, n_used, x, g_ffn)


class _ExpertRows:
    scratch = [pltpu.VMEM((2, 2 * TR * ROW_VREGS, LANES), F32), pltpu.SemaphoreType.DMA((2,))]

    def __init__(self, dest_ref, ys_ref, buf, sems):
        self.dest_ref, self.ys_ref, self.buf, self.sems = dest_ref, ys_ref, buf, sems

    def _copy(self, src_row, slot, dst_row):
        dst = self.buf.at[slot, pl.ds(pl.multiple_of(dst_row * ROW_VREGS, ROW_VREGS), ROW_VREGS), :]
        return pltpu.make_async_copy(_row(self.ys_ref, src_row), dst, self.sems.at[slot])

    def start(self, tile):
        slot = tile % 2

        def issue(g, carry):
            for un in range(ROW_DMA_UNROLL):
                t = g * ROW_DMA_UNROLL + un
                for kk in range(2):
                    self._copy(self.dest_ref[_dest_index(tile, kk, t)], slot, kk * TR + t).start(priority=kk)
            return carry
        lax.fori_loop(0, TR // ROW_DMA_UNROLL, issue, 0)

    def wait(self, tile):
        slot = tile % 2
        pltpu.make_async_copy(self.ys_ref.at[pl.ds(0, 2 * TR * ROW_VREGS), :], self.buf.at[slot],
                              self.sems.at[slot]).wait()

    def combine(self, tile, x, gate_ref):
        rows = self.buf.at[tile % 2]
        g = gate_ref[...]
        for kk in range(2):
            y = jnp.concatenate([rows[pl.ds(kk * TR * ROW_VREGS + j, TR, stride=ROW_VREGS), :]
                                 for j in range(ROW_VREGS)], -1)
            x = x + g[:, kk:kk + 1] * y
        return x

    def fetch_combine(self, x, gate_ref):
        i = pl.program_id(0)

        @pl.when(i == 0)
        def _():
            self.start(i)

        @pl.when(i + 1 < pl.num_programs(0))
        def _():
            self.start(i + 1)

        self.wait(i)
        return self.combine(i, x, gate_ref)


def _ffn_kernel(be_ref, nused_ref, xs_ref, w1_ref, w3_ref, w2_ref, y_ref, w1b, w3b, w2b):
    nb = pl.program_id(0)

    @pl.when(nb < nused_ref[0])
    def _():
        prev = be_ref[jnp.maximum(nb - 1, 0)]
        fresh = jnp.logical_or(nb == 0, be_ref[nb] != prev)

        @pl.when(fresh)
        def _():
            w1b[...] = w1_ref[...].astype(BF16)
            w3b[...] = w3_ref[...].astype(BF16)
            w2b[...] = w2_ref[...].astype(BF16)

        x = _load_rows_from_tiles(xs_ref, TM).astype(BF16)
        a = jnp.dot(x, w1b[...], preferred_element_type=F32)
        b = jnp.dot(x, w3b[...], preferred_element_type=F32)
        mid = (_silu(a) * b).astype(BF16)
        y = jnp.dot(mid, w2b[...], preferred_element_type=F32)
        _store_rows_as_tiles(y_ref, y)

    @pl.when(nb >= nused_ref[0])
    def _():
        y_ref[...] = jnp.zeros_like(y_ref)


def _experts(xs, blk_exp, n_used, layer, w1, w3, w2):
    rows = pl.BlockSpec((TM * ROW_VREGS, LANES), lambda nb, be, nu: (nb, 0))
    rows_in = rows

    def wspec(a, b):
        return pl.BlockSpec((None, None, a, b), lambda nb, be, nu: (layer, be[nb], 0, 0))

    return pl.pallas_call(
        _ffn_kernel,
        grid_spec=pltpu.PrefetchScalarGridSpec(
            num_scalar_prefetch=2, grid=(N_SLOTS_BLK,),
            in_specs=[rows_in, wspec(D_MODEL, D_EXPERT), wspec(D_MODEL, D_EXPERT), wspec(D_EXPERT, D_MODEL)],
            out_specs=rows,
            scratch_shapes=[pltpu.VMEM((D_MODEL, D_EXPERT), BF16), pltpu.VMEM((D_MODEL, D_EXPERT), BF16),
                            pltpu.VMEM((D_EXPERT, D_MODEL), BF16)]),
        out_shape=jax.ShapeDtypeStruct((N_SLOTS * ROW_VREGS, LANES), F32),
        compiler_params=_cparams(("arbitrary",)),
        name="experts",
    )(blk_exp, n_used, xs, w1, w3, w2)


def _moe(x, g_ffn, meta, counts, layer, w1, w3, w2):
    dest, blk_exp, n_used, pad_lo, pad_hi = _plan(meta, counts)
    xs = _dispatch(x, g_ffn, dest, pad_lo, pad_hi, n_used)
    return _experts(xs, blk_exp, n_used, layer, w1, w3, w2), dest


def _pool_kernel(dest_ref, x1_ref, ys_ref, gate0_ref, gmix_ref, sp_ref, pw_ref, ps_ref, gffn_ref,
                 wr_ref, br_ref, tri_ref,
                 x3_ref, hkeep_ref, hs_ref, meta_ref, gate_ref, cnt_out_ref,
                 cnt_ref, ext_ref, e1_ref, e2_ref, e3_ref, e4_ref, mixed_ref, ybuf, ysems):
    i = pl.program_id(0)
    x2 = _ExpertRows(dest_ref, ys_ref, ybuf, ysems).fetch_combine(x1_ref[...], gate0_ref)
    h = _rms(x2, gmix_ref[...])
    H = POOL_HALO
    lvl_refs = (e1_ref, e2_ref, e3_ref, e4_ref)

    @pl.when(i < SAMPLE_TILE)
    def _():
        hkeep_ref[...] = h[TR - 16:TR]
        @pl.when(i % TILES_PER_SEQ == 0)
        def _():
            ext_ref[0:H, :] = jnp.zeros((H, D_MODEL), F32)
        ext_ref[H:H + TR, :] = h
        src = ext_ref
        for lv in range(4):
            sh = 1 << lv
            lo = 8 * (lv + 1)
            c0 = POOL_GROUP * lv
            dst = lvl_refs[lv]
            dst[lo:H + TR, c0:] = src[lo:H + TR, c0:] + src[lo - sh:H + TR - sh, c0:]
            src = dst
        pos = (i % TILES_PER_SEQ) * TR + lax.broadcasted_iota(I32, (TR, 1), 0)
        for gi, w in enumerate(POOL_WINDOWS):
            cs = slice(gi * POOL_GROUP, (gi + 1) * POOL_GROUP)
            inv = 1.0 / jnp.minimum(w, pos + 1).astype(F32)
            mixed_ref[:, cs] = lvl_refs[gi][H:H + TR, cs] * inv - h[:, cs]
        ext_ref[0:H, :] = h[TR - H:TR]

    @pl.when(i == SAMPLE_TILE)
    def _():
        hs = h[0:DEC_BATCH]
        hs_ref[...] = hs
        mixed_ref[...] = jnp.zeros_like(mixed_ref)
        for gi, w in enumerate(POOL_WINDOWS):
            cs = slice(gi * POOL_GROUP, (gi + 1) * POOL_GROUP)
            win = hs[:, cs]
            for dlt in range(1, w):
                win = win + sp_ref[POOL_KEEP - dlt][:, cs]
            mixed_ref[0:DEC_BATCH, cs] = win * (1.0 / w) - hs[:, cs]

    mixed = mixed_ref[...].astype(BF16)
    outs = [jnp.dot(mixed[:, gi * POOL_GROUP:(gi + 1) * POOL_GROUP], pw_ref[gi], preferred_element_type=F32)
            for gi in range(len(POOL_WINDOWS))]
    x3 = x2 + jnp.concatenate(outs, -1) * ps_ref[...]
    x3_ref[...] = x3
    h2 = _rms(x3, gffn_ref[...])
    _route(h2, wr_ref, br_ref, tri_ref, cnt_ref, meta_ref, gate_ref, cnt_out_ref)


def _const_spec_p(a):
    return pl.BlockSpec(a.shape, lambda i, *_, n=a.ndim: (0,) * n)


def _pool_route(dest, x1, ys, gates0, g_mix, sp_t, pw_bf16, p_scale, g_ffn, wr, br, tri):
    def rows(shape):
        return pl.BlockSpec(shape, lambda i, *_: (i, 0))

    ext = pltpu.VMEM((POOL_HALO + TR, D_MODEL), F32)
    consts = [g_mix, sp_t, pw_bf16, p_scale, g_ffn, wr, br, tri]
    return pl.pallas_call(
        _pool_kernel,
        grid_spec=pltpu.PrefetchScalarGridSpec(
            num_scalar_prefetch=1, grid=(N_TILES,),
            in_specs=[rows((TR, D_MODEL)), pl.BlockSpec(memory_space=pl.ANY), rows((TR, 8))]
                     + [_const_spec_p(a) for a in consts],
            out_specs=[rows((TR, D_MODEL)),
                       pl.BlockSpec((None, 16, D_MODEL),
                                    lambda i, *_: (jnp.minimum(i // TILES_PER_SEQ, BATCH - 1), 0, 0)),
                       pl.BlockSpec((DEC_BATCH, D_MODEL), lambda i, *_: (0, 0)),
                       pl.BlockSpec((1, 8, TR), lambda i, *_: (i, 0, 0)), rows((TR, 8)),
                       pl.BlockSpec((1, 1, LANES), lambda i, *_: (i, 0, 0))],
            scratch_shapes=[pltpu.VMEM((1, LANES), F32), ext, ext, ext, ext, ext,
                            pltpu.VMEM((TR, D_MODEL), F32)] + _ExpertRows.scratch),
        out_shape=[jax.ShapeDtypeStruct((NT, D_MODEL), F32),
                   jax.ShapeDtypeStruct((BATCH, 16, D_MODEL), F32),
                   jax.ShapeDtypeStruct((DEC_BATCH, D_MODEL), F32)] + _ROUTE_OUT_SHAPES,
        compiler_params=_cparams(("arbitrary",)),
        name="pool_route",
    )(dest, x1, ys, gates0, *consts)


def _final_kernel(dest_ref, x_ref, ys_ref, gate_ref, g_ref, yp_ref, ysm_ref, ybuf, ysems):
    i = pl.program_id(0)
    y = _rms(_ExpertRows(dest_ref, ys_ref, ybuf, ysems).fetch_combine(x_ref[...], gate_ref), g_ref[...])

    @pl.when(i < SAMPLE_TILE)
    def _():
        yp_ref[...] = y

    @pl.when(i == SAMPLE_TILE)
    def _():
        ysm_ref[...] = y[0:DEC_BATCH]


def _final(dest, x3, ys, gates, g_final):
    return pl.pallas_call(
        _final_kernel,
        grid_spec=pltpu.PrefetchScalarGridSpec(
            num_scalar_prefetch=1, grid=(N_TILES,),
            in_specs=[pl.BlockSpec((TR, D_MODEL), lambda i, *_: (i, 0)), pl.BlockSpec(memory_space=pl.ANY),
                      pl.BlockSpec((TR, 8), lambda i, *_: (i, 0)), _const_spec_p(g_final)],
            out_specs=[pl.BlockSpec((TR, D_MODEL), lambda i, *_: (jnp.minimum(i, SAMPLE_TILE - 1), 0)),
                       pl.BlockSpec((DEC_BATCH, D_MODEL), lambda i, *_: (0, 0))],
            scratch_shapes=_ExpertRows.scratch),
        out_shape=[jax.ShapeDtypeStruct((NP, D_MODEL), F32), jax.ShapeDtypeStruct((DEC_BATCH, D_MODEL), F32)],
        compiler_params=_cparams(("arbitrary",)),
        name="final_norm",
    )(dest, x3, ys, gates, g_final)


def kernel(x_prompt, x_sample, cache_win_k, cache_win_v, state_hgrn, state_pool, rel_bias, norm_mix, norm_ffn,
           norm_final, w_in, w_out, hgrn_lb, hgrn_gnorm, pool_w, pool_scale, moe_wg_group, moe_bg_group,
           moe_wg_exp, moe_bg_exp, moe_w1, moe_w3, moe_w2):
    xp = x_prompt.reshape(NP, D_MODEL)
    xs_pad = jnp.pad(x_sample.reshape(DEC_BATCH, D_MODEL), ((0, TR - DEC_BATCH), (0, 0)))
    lb = jnp.cumsum(jax.nn.softmax(hgrn_lb.astype(F32), axis=0), axis=0)[0:1]
    gnorm = hgrn_gnorm[0:1].astype(F32)

    q, k, v, hq, hf, hi, hg, k_win, v_win = _inproj(xp, xs_pad, norm_mix[0:1], w_in[0].astype(BF16))
    oa = _attention_prompt(q, k, v, _attn_bias_tables(rel_bias))
    ob, s_prompt = _hgrn_prompt(hq, hf, hi, hg, lb, gnorm)
    oa_s, ob_s, s_sample = _sample_mixers(
        q, k, v, hq, hf, hi, hg,
        cache_win_k[0], cache_win_v[0], state_hgrn[0], _sample_bias_tables(rel_bias), lb, gnorm)
    wr, br, tri = _router_operands(moe_wg_group[0], moe_bg_group[0], moe_wg_exp[0], moe_bg_exp[0])
    x1, meta, gates0, counts = _outproj_route(xp, xs_pad, oa, oa_s, ob, ob_s, w_out[0].astype(BF16),
                                              norm_ffn[0:1], wr, br, tri)
    ys, dest = _moe(x1, norm_ffn[0:1], meta, counts, 0, moe_w1, moe_w3, moe_w2)

    wr, br, tri = _router_operands(moe_wg_group[1], moe_bg_group[1], moe_wg_exp[1], moe_bg_exp[1])
    sp_t = jnp.transpose(state_pool[0], (1, 0, 2))
    x3, hkeep, hsample, meta, gates1, counts = _pool_route(
        dest, x1, ys, gates0, norm_mix[1:2], sp_t, pool_w[0].astype(BF16), pool_scale[0:1], norm_ffn[1:2],
        wr, br, tri)
    ys, dest = _moe(x3, norm_ffn[1:2], meta, counts, 1, moe_w1, moe_w3, moe_w2)
    y_prompt, y_sample = _final(dest, x3, ys, gates1, norm_final.reshape(1, D_MODEL))

    k_s = k[NP:NV].reshape(1, DEC_BATCH, 1, N_HEADS, D_HEAD)
    v_s = v[NP:NV].reshape(1, DEC_BATCH, 1, N_HEADS, D_HEAD)
    return (y_prompt.reshape(BATCH, SEQ, D_MODEL), y_sample.reshape(DEC_BATCH, 1, D_MODEL),
            k_win.reshape(1, BATCH, W_MAX, N_HEADS, D_HEAD), v_win.reshape(1, BATCH, W_MAX, N_HEADS, D_HEAD),
            s_prompt[None], hkeep[:, 16 - POOL_KEEP:][None],
            k_s, v_s, s_sample[None], hsample.reshape(1, DEC_BATCH, 1, D_MODEL))
```

```python
import functools

import numpy as np
import jax
import jax.numpy as jnp
from jax import lax
from jax.experimental import pallas as pl
from jax.experimental.pallas import tpu as pltpu

F32 = jnp.float32
BF16 = jnp.bfloat16
I32 = jnp.int32

D_MODEL = 1024
BATCH = 4
SEQ = 4096
DEC_BATCH = 32
PAST_LEN = 8192
W_MAX = 2048
N_HEADS = 4
D_HEAD = 128
A_WIDTH = N_HEADS * D_HEAD
N_PROJ = 7 * A_WIDTH
DILATED = ((128, 1), (512, 4), (2048, 16))
NUM_BUCKETS = 32
MAX_DISTANCE = 2048
POOL_WINDOWS = (2, 4, 8, 16)
POOL_GROUP = 256
POOL_KEEP = 15
N_GROUPS = 4
EXPERTS_PER_GROUP = 8
N_EXPERTS = 32
D_EXPERT = 512
EPS = 1e-6
NEG = -1e30

LANES = 128
SUBLANES = 8
ROW_VREGS = D_MODEL // LANES
TR = 256
NP = BATCH * SEQ
NV = NP + DEC_BATCH
NT = NP + TR
N_TILES = NT // TR
SAMPLE_TILE = NP // TR
TILES_PER_SEQ = SEQ // TR
TM = 256
N_SLOTS_BLK = (2 * NV + N_EXPERTS * (TM - 1) + TM - 1) // TM
N_SLOTS = N_SLOTS_BLK * TM
QB = 128
ATTN_UNROLL = 8
ATTN_PERIOD = 3 * QB
ROW_DMA_UNROLL = 8
PAD_FILL_SIZES = (64, 8, 1)
CH = 128
HGRN_HEADS_PER_STEP = 2
N_LEVELS = 7
POOL_HALO = 32
VMEM_LIMIT = 56 * 1024 * 1024


def _cparams(sem=None, vmem=VMEM_LIMIT):
    kw = dict(vmem_limit_bytes=vmem)
    if sem is not None:
        kw["dimension_semantics"] = sem
    return pltpu.CompilerParams(**kw)


def _rms(x, g):
    return x * lax.rsqrt(jnp.mean(x * x, -1, keepdims=True) + EPS) * g


def _sigmoid(x):
    return 1.0 / (1.0 + jnp.exp(-x))


def _silu(x):
    return x * _sigmoid(x)


def _token_tile(i, xp_ref, xs_ref):
    return jnp.where(i == SAMPLE_TILE, xs_ref[...], xp_ref[...])


_PROMPT_ROWS = pl.BlockSpec((TR, D_MODEL), lambda i: (jnp.minimum(i, SAMPLE_TILE - 1), 0))
_SAMPLE_ROWS = pl.BlockSpec((TR, D_MODEL), lambda i: (0, 0))
WIN_TILES = W_MAX // TR


def _window_block(i):
    seq = jnp.minimum(i // TILES_PER_SEQ, BATCH - 1)
    j = jnp.clip(i % TILES_PER_SEQ - (TILES_PER_SEQ - WIN_TILES), 0, WIN_TILES - 1)
    return jnp.where(i >= SAMPLE_TILE, BATCH * WIN_TILES - 1, seq * WIN_TILES + j)


def _inproj_kernel(xp_ref, xs_ref, g_ref, w_ref, *out_refs):
    i = pl.program_id(0)
    h = _rms(_token_tile(i, xp_ref, xs_ref), g_ref[...])
    p = jnp.dot(h.astype(BF16), w_ref[...], preferred_element_type=F32)
    for n, o_ref in enumerate(out_refs[:7]):
        o_ref[...] = p[:, n * A_WIDTH:(n + 1) * A_WIDTH]

    @pl.when(jnp.logical_and(i < SAMPLE_TILE, i % TILES_PER_SEQ >= TILES_PER_SEQ - WIN_TILES))
    def _():
        for n, o_ref in ((1, out_refs[7]), (2, out_refs[8])):
            for h_i in range(N_HEADS):
                c0 = n * A_WIDTH + h_i * D_HEAD
                o_ref[pl.ds(h_i, TR, stride=N_HEADS), :] = p[:, c0:c0 + D_HEAD]


def _inproj(xp, xs_pad, g, w_bf16):
    out_sds = ([jax.ShapeDtypeStruct((NT, A_WIDTH), F32)] * 7
               + [jax.ShapeDtypeStruct((BATCH * W_MAX * N_HEADS, D_HEAD), F32)] * 2)
    win = pl.BlockSpec((TR * N_HEADS, D_HEAD), lambda i: (_window_block(i), 0))
    return pl.pallas_call(
        _inproj_kernel,
        grid=(N_TILES,),
        in_specs=[_PROMPT_ROWS, _SAMPLE_ROWS,
                  pl.BlockSpec((1, D_MODEL), lambda i: (0, 0)),
                  pl.BlockSpec((D_MODEL, N_PROJ), lambda i: (0, 0))],
        out_specs=[pl.BlockSpec((TR, A_WIDTH), lambda i: (i, 0))] * 7 + [win, win],
        out_shape=out_sds,
        compiler_params=_cparams(("arbitrary",)),
        name="inproj",
    )(xp, xs_pad, g, w_bf16)


def _t5_bucket(dist):
    max_exact = NUM_BUCKETS // 2
    d = np.asarray(dist)
    large = max_exact + np.floor(np.log(np.maximum(d, 1) / max_exact)
                                 / np.log(MAX_DISTANCE / max_exact) * (NUM_BUCKETS - max_exact)).astype(np.int32)
    large = np.minimum(large, NUM_BUCKETS - 1)
    return np.where(d < max_exact, d, large).astype(np.int32)


def _attn_bias_tables(rel_bias):
    period = ATTN_PERIOD
    m = np.arange(period)
    u = np.where(m < 2 * QB, m, m - period)
    pick = np.zeros((len(DILATED), 2, period, NUM_BUCKETS), np.float32)
    mask = np.zeros((len(DILATED), 2, period, 1), np.float32)
    for bi, (w, d) in enumerate(DILATED):
        nk = w // d
        for vi, off in enumerate((0, QB)):
            j = off - u
            ok = (j >= 0) & (j <= nk)
            pick[bi, vi, m[ok], _t5_bucket(d * j[ok])] = 1.0
            mask[bi, vi, ~ok, 0] = NEG
    vec = jnp.einsum("bvmk,kh->hbvm", pick, rel_bias.astype(F32), precision=lax.Precision.HIGHEST)
    vec = vec + jnp.transpose(mask, (3, 0, 1, 2))
    return vec[:, :, :, None, :]


def _attn_kernel(q_ref, k_ref, v_ref, vec_ref, o_ref,
                 q4, k4, v4, qd, kd, vd, ud, md, sd, u_acc, m_acc, s_acc, bias_ref):
    scale = D_HEAD ** -0.5
    c4 = SEQ // 4
    c16 = SEQ // 16
    for bi in range(len(DILATED)):
        for vi in range(2):
            rows = jnp.broadcast_to(vec_ref[bi, vi], (QB, ATTN_PERIOD))
            bias_ref[bi, vi] = pltpu.roll(rows, 0, 1, stride=1, stride_axis=0)[:, :2 * QB]

    def block_stats(bi, t, nb):
        has_prev = jnp.minimum(t % nb, 1)
        q0 = pl.multiple_of(t * QB, QB)
        k0 = pl.multiple_of((t - has_prev) * QB, QB)
        qb = qd[pl.ds(q0, QB), :]
        kb = kd[pl.ds(k0, 2 * QB), :]
        vb = vd[pl.ds(k0, 2 * QB), :]
        s = lax.dot_general(qb, kb, (((1,), (1,)), ((), ())), preferred_element_type=F32)
        s = s + bias_ref[bi, has_prev]
        mb = jnp.max(s, -1, keepdims=True)
        p = jnp.exp(s - mb)
        sb = jnp.sum(p, -1, keepdims=True)
        u = jnp.dot(p.astype(BF16), vb, preferred_element_type=F32)
        return q0, mb, sb, u

    def run_blocks(bi, nb, consume):
        def body(g, carry):
            for un in range(ATTN_UNROLL):
                q0, mb, sb, u = block_stats(bi, g * ATTN_UNROLL + un, nb)
                consume(pl.ds(q0, QB), mb, sb, u)
            return carry
        lax.fori_loop(0, SEQ // QB // ATTN_UNROLL, body, 0)

    def merged(rows, m_ref, s_ref, u_ref, mb, sb, u):
        m_old = m_ref[rows, :]
        m_new = jnp.maximum(m_old, mb)
        a = jnp.exp(m_old - m_new)
        b = jnp.exp(mb - m_new)
        return m_new, a * s_ref[rows, :] + b * sb, a * u_ref[rows, :] + b * u

    for r in range(4):
        src, dst = pl.ds(r, c4, stride=4), pl.ds(r * c4, c4)
        q4[dst, :] = q_ref[src, :] * scale
        k4[dst, :] = k_ref[src, :]
        v4[dst, :] = v_ref[src, :]

    for r in range(4):
        for j in range(4):
            src, dst = pl.ds(r * c4 + j, c16, stride=4), pl.ds((r + 4 * j) * c16, c16)
            qd[dst, :] = q4[src, :].astype(BF16)
            kd[dst, :] = k4[src, :].astype(BF16)
            vd[dst, :] = v4[src, :].astype(BF16)

    def keep16(rows, mb, sb, u):
        ud[rows, :] = u
        md[rows, :] = jnp.broadcast_to(mb, (QB, D_HEAD))
        sd[rows, :] = jnp.broadcast_to(sb, (QB, D_HEAD))
    run_blocks(2, c16 // QB, keep16)
    for r in range(4):
        for j in range(4):
            src, dst = pl.ds((r + 4 * j) * c16, c16), pl.ds(r * c4 + j, c16, stride=4)
            u_acc[dst, :] = ud[src, :]
            m_acc[dst, :] = md[src, :]
            s_acc[dst, :] = sd[src, :]

    qd[...] = q4[...].astype(BF16)
    kd[...] = k4[...].astype(BF16)
    vd[...] = v4[...].astype(BF16)

    def merge4(rows, mb, sb, u):
        m_acc[rows, :], s_acc[rows, :], u_acc[rows, :] = merged(rows, m_acc, s_acc, u_acc, mb, sb, u)
    run_blocks(1, c4 // QB, merge4)

    for r in range(4):
        src, dst = pl.ds(r * c4, c4), pl.ds(r, c4, stride=4)
        ud[dst, :] = u_acc[src, :]
        md[dst, :] = m_acc[src, :]
        sd[dst, :] = s_acc[src, :]
    qd[...] = (q_ref[...] * scale).astype(BF16)
    kd[...] = k_ref[...].astype(BF16)
    vd[...] = v_ref[...].astype(BF16)

    def finish(rows, mb, sb, u):
        _, den, num = merged(rows, md, sd, ud, mb, sb, u)
        o_ref[rows, :] = (num / den).astype(o_ref.dtype)
    run_blocks(0, SEQ // QB, finish)


def _attention_prompt(q, k, v, bias_tabs):
    blk = pl.BlockSpec((SEQ, D_HEAD), lambda b, h: (b, h))
    return pl.pallas_call(
        _attn_kernel,
        grid=(BATCH, N_HEADS),
        in_specs=[blk, blk, blk,
                  pl.BlockSpec((None, 3, 2, 1, ATTN_PERIOD), lambda b, h: (h, 0, 0, 0, 0))],
        out_specs=blk,
        out_shape=jax.ShapeDtypeStruct((NP, A_WIDTH), BF16),
        scratch_shapes=[pltpu.VMEM((SEQ, D_HEAD), F32)] * 3
                       + [pltpu.VMEM((SEQ, D_HEAD), BF16)] * 3
                       + [pltpu.VMEM((SEQ, D_HEAD), F32)] * 6
                       + [pltpu.VMEM((len(DILATED), 2, QB, 2 * QB), F32)],
        compiler_params=_cparams(("arbitrary", "arbitrary")),
        name="attn_prompt",
    )(q, k, v, bias_tabs)


def _hgrn_tables():
    t = np.arange(CH)
    u = np.arange(CH)
    sums_q = np.zeros((1 + N_LEVELS, CH, CH), np.float32)
    sums_k = np.zeros((2 + N_LEVELS, CH, CH), np.float32)
    sums_q[0] = (u[None, :] <= t[:, None])
    sums_k[0] = (u[None, :] > t[:, None])
    sums_k[1 + N_LEVELS] = 1.0
    pair = np.zeros((N_LEVELS, CH, CH), np.float32)
    for l in range(N_LEVELS):
        h = CH >> (l + 1)
        is_q = (t // h) % 2 == 1
        half_start = (t // h) * h
        half_end = half_start + h
        sel_q = (u[None, :] >= half_start[:, None]) & (u[None, :] <= t[:, None])
        sel_k = (u[None, :] > t[:, None]) & (u[None, :] < half_end[:, None])
        sums_q[1 + l] = sel_q & is_q[:, None]
        sums_k[1 + l] = sel_k & ~is_q[:, None]
        same = (t[:, None] // (2 * h)) == (t[None, :] // (2 * h))
        pair[l] = same & is_q[:, None] & (~is_q)[None, :]
    sums_kt = np.transpose(sums_k, (2, 0, 1)).reshape(CH, (2 + N_LEVELS) * CH)
    sums_q = sums_q.reshape((1 + N_LEVELS) * CH, CH)
    return (jnp.asarray(np.concatenate([sums_q, sums_q], 1), BF16),
            jnp.asarray(np.concatenate([sums_kt, sums_kt], 0), BF16),
            jnp.asarray(pair), jnp.asarray(np.eye(CH, dtype=np.float32)))


def _split_bf16(x):
    hi = x.astype(BF16)
    return hi, (x - hi.astype(F32)).astype(BF16)


def _hgrn_kernel(q_ref, f_ref, i_ref, g_ref, lb_ref, gn_ref, sq_ref, skt_ref, pair_ref, eye_ref, o_ref, s_ref):
    gn = gn_ref[...]

    def chunk(c, st, hh):
        rows = pl.ds(pl.multiple_of(c * CH, CH), CH)
        cols = slice(hh * D_HEAD, (hh + 1) * D_HEAD)
        lb = lb_ref[:, cols]
        q = _silu(q_ref[rows, cols])
        f = lb + (1.0 - lb) * _sigmoid(f_ref[rows, cols])
        lf = jnp.log(f)
        k = 1.0 - f
        v16 = i_ref[rows, cols].astype(BF16)
        kt = k.T
        exq = jnp.exp(jnp.dot(sq_ref[...], jnp.concatenate(_split_bf16(lf), 0),
                              preferred_element_type=F32))
        exk = jnp.exp(jnp.dot(jnp.concatenate(_split_bf16(lf.T), 1), skt_ref[...],
                              preferred_element_type=F32))
        inter = jnp.dot((q * exq[0:CH]).astype(BF16), st.astype(BF16), preferred_element_type=F32)
        sc = eye_ref[...] * jnp.sum(q * k, -1, keepdims=True)
        for l in range(N_LEVELS):
            ql = (q * exq[(1 + l) * CH:(2 + l) * CH]).astype(BF16)
            klt = (kt * exk[:, (1 + l) * CH:(2 + l) * CH]).astype(BF16)
            sc = sc + pair_ref[l] * jnp.dot(ql, klt, preferred_element_type=F32)
        o = inter + jnp.dot(sc.astype(BF16), v16, preferred_element_type=F32)
        st_new = (st * exk[:, (1 + N_LEVELS) * CH:]
                  + jnp.dot((kt * exk[:, 0:CH]).astype(BF16), v16, preferred_element_type=F32))
        o = _rms(o, gn) * _silu(g_ref[rows, cols])
        o_ref[rows, cols] = o.astype(o_ref.dtype)
        return st_new

    def step(c, states):
        return tuple(chunk(c, st, hh) for hh, st in enumerate(states))

    zero = jnp.zeros((D_HEAD, D_HEAD), F32)
    states = lax.fori_loop(0, SEQ // CH, step, (zero,) * HGRN_HEADS_PER_STEP)
    for hh, st in enumerate(states):
        s_ref[hh] = st


def _hgrn_prompt(hq, hf, hi, hg, lb, gnorm):
    tables = _hgrn_tables()
    width = HGRN_HEADS_PER_STEP * D_HEAD
    blk = pl.BlockSpec((SEQ, width), lambda b, h: (b, h))

    def full(a):
        return pl.BlockSpec(a.shape, lambda b, h, n=a.ndim: (0,) * n)

    return pl.pallas_call(
        _hgrn_kernel,
        grid=(BATCH, N_HEADS // HGRN_HEADS_PER_STEP),
        in_specs=[blk, blk, blk, blk,
                  pl.BlockSpec((1, width), lambda b, h: (0, h)),
                  pl.BlockSpec((1, D_HEAD), lambda b, h: (0, 0))] + [full(a) for a in tables],
        out_specs=[blk, pl.BlockSpec((None, HGRN_HEADS_PER_STEP, D_HEAD, D_HEAD), lambda b, h: (b, h, 0, 0))],
        out_shape=[jax.ShapeDtypeStruct((NP, A_WIDTH), BF16),
                   jax.ShapeDtypeStruct((BATCH, N_HEADS, D_HEAD, D_HEAD), F32)],
        compiler_params=_cparams(("arbitrary", "arbitrary")),
        name="hgrn_prompt",
    )(hq, hf, hi, hg, lb, gnorm, *tables)


def _sample_bias_tables(rel_bias):
    j = np.concatenate([QB - np.arange(QB), [0]])
    pick = np.zeros((len(DILATED), QB + 1, NUM_BUCKETS), np.float32)
    for bi, (w, d) in enumerate(DILATED):
        pick[bi, np.arange(QB + 1), _t5_bucket(d * j)] = 1.0
    return jnp.einsum("bjk,kh->bhj", pick, rel_bias.astype(F32), precision=lax.Precision.HIGHEST)[..., None]


def _bf16_round(x):
    return x.astype(BF16).astype(F32)


def _col(row, eye):
    return jnp.sum(eye * row, -1, keepdims=True)


NEAR_POS = 4 * QB
FAR_STEPS = (W_MAX - NEAR_POS) // 16


def _cached_rows(near_ref, far_ref, d, h):
    def near(first_pos, n, step):
        return near_ref[pl.ds((first_pos - (W_MAX - NEAR_POS)) * N_HEADS + h, n, stride=step * N_HEADS), :]

    if QB * d <= NEAR_POS:
        return near(W_MAX - QB * d, QB, d)
    return jnp.concatenate([far_ref[:, h, :], near(W_MAX - NEAR_POS, QB - FAR_STEPS, d)], 0)


def _sample_kernel(q_ref, k_ref, v_ref, hq_ref, hf_ref, hi_ref, hg_ref, kn_ref, kf_ref, vn_ref, vf_ref, s0_ref,
                   bias_ref, lb_ref, gn_ref, eye_ref, oa_ref, ob_ref, s_ref, oa_acc, ob_acc):
    b = pl.program_id(0)
    row = pl.ds(b, 1)
    scale = D_HEAD ** -0.5
    q = _bf16_round(q_ref[row, :] * scale)
    kn = _bf16_round(k_ref[row, :])
    vn = _bf16_round(v_ref[row, :])
    eye = eye_ref[...]

    stats = []
    for bi, (w, d) in enumerate(DILATED):
        per_head = []
        for h in range(N_HEADS):
            hs = slice(h * D_HEAD, (h + 1) * D_HEAD)
            kb = _bf16_round(_cached_rows(kn_ref, kf_ref, d, h))
            vb = _bf16_round(_cached_rows(vn_ref, vf_ref, d, h))
            s = jnp.sum(kb * q[:, hs], -1, keepdims=True) + bias_ref[bi, h, 0:QB]
            s0 = jnp.sum(q[:, hs] * kn[:, hs], -1, keepdims=True) + bias_ref[bi, h, QB:QB + 1]
            m = jnp.maximum(jnp.max(s, 0, keepdims=True), s0)
            p = jnp.exp(s - m)
            p0 = jnp.exp(s0 - m)
            ssum = jnp.sum(p, 0, keepdims=True) + p0
            u = jnp.sum(_bf16_round(p) * vb, 0, keepdims=True) + _bf16_round(p0) * vn[:, hs]
            per_head.append((m, ssum, u))
        stats.append(per_head)
    outs = []
    for h in range(N_HEADS):
        m_all = functools.reduce(jnp.maximum, [stats[bi][h][0] for bi in range(3)])
        num = 0.0
        den = 0.0
        for bi in range(3):
            m, ssum, u = stats[bi][h]
            c = jnp.exp(m - m_all)
            num = num + c * u
            den = den + c * ssum
        outs.append(num / den)
    oa_acc[row, :] = jnp.concatenate(outs, -1)

    qh = _silu(hq_ref[row, :])
    lb = lb_ref[...]
    f = lb + (1.0 - lb) * _sigmoid(hf_ref[row, :])
    vi = hi_ref[row, :]
    gate = _silu(hg_ref[row, :])
    gn = gn_ref[...]
    obs = []
    for h in range(N_HEADS):
        hs = slice(h * D_HEAD, (h + 1) * D_HEAD)
        f_col = _col(f[:, hs], eye)
        q_col = _col(qh[:, hs], eye)
        s_old = s0_ref[h]
        s_ref[h] = f_col * s_old + (1.0 - f_col) * vi[:, hs]
        inter = jnp.sum(_bf16_round(q_col * f_col) * _bf16_round(s_old), 0, keepdims=True)
        qk = jnp.sum(qh[:, hs] * (1.0 - f[:, hs]), -1, keepdims=True)
        o = inter + qk * vi[:, hs]
        obs.append(_rms(o, gn) * gate[:, hs])
    ob_acc[row, :] = jnp.concatenate(obs, -1)

    @pl.when(b == DEC_BATCH - 1)
    def _():
        pad = jnp.zeros((TR - DEC_BATCH, A_WIDTH), F32)
        oa_ref[...] = jnp.concatenate([oa_acc[...], pad], 0).astype(oa_ref.dtype)
        ob_ref[...] = jnp.concatenate([ob_acc[...], pad], 0).astype(ob_ref.dtype)


def _sample_mixers(q, k, v, hq, hf, hi, hg, cache_k, cache_v, state, bias_s, lb, gnorm):
    tile = pl.BlockSpec((TR, A_WIDTH), lambda b: (SAMPLE_TILE, 0))
    out_tile = pl.BlockSpec((TR, A_WIDTH), lambda b: (0, 0))
    near = pl.BlockSpec((None, NEAR_POS * N_HEADS, D_HEAD), lambda b: (b, W_MAX // NEAR_POS - 1, 0))
    far = pl.BlockSpec((None, FAR_STEPS, SUBLANES, D_HEAD), lambda b: (b, 0, 0, 0))
    rows = (DEC_BATCH, W_MAX * N_HEADS, D_HEAD)
    groups = (DEC_BATCH, W_MAX // 16, 16 * N_HEADS, D_HEAD)
    eye = jnp.eye(D_HEAD, dtype=F32)

    return pl.pallas_call(
        _sample_kernel,
        grid=(DEC_BATCH,),
        in_specs=[tile, tile, tile, tile, tile, tile, tile, near, far, near, far,
                  pl.BlockSpec((None, N_HEADS, D_HEAD, D_HEAD), lambda b: (b, 0, 0, 0)),
                  pl.BlockSpec(bias_s.shape, lambda b: (0, 0, 0, 0)),
                  pl.BlockSpec((1, A_WIDTH), lambda b: (0, 0)),
                  pl.BlockSpec((1, D_HEAD), lambda b: (0, 0)),
                  pl.BlockSpec((D_HEAD, D_HEAD), lambda b: (0, 0))],
        out_specs=[out_tile, out_tile,
                   pl.BlockSpec((None, N_HEADS, D_HEAD, D_HEAD), lambda b: (b, 0, 0, 0))],
        out_shape=[jax.ShapeDtypeStruct((TR, A_WIDTH), BF16), jax.ShapeDtypeStruct((TR, A_WIDTH), BF16),
                   jax.ShapeDtypeStruct((DEC_BATCH, N_HEADS, D_HEAD, D_HEAD), F32)],
        scratch_shapes=[pltpu.VMEM((DEC_BATCH, A_WIDTH), F32)] * 2,
        compiler_params=_cparams(("arbitrary",)),
        name="sample_mixers",
    )(q, k, v, hq, hf, hi, hg, cache_k.reshape(rows), cache_k.reshape(groups),
      cache_v.reshape(rows), cache_v.reshape(groups), state, bias_s, lb, gnorm, eye)


def _store_rows_as_tiles(ref, val):
    n = val.shape[0]
    for j in range(ROW_VREGS):
        ref[pl.ds(j, n, stride=ROW_VREGS), :] = val[:, j * LANES:(j + 1) * LANES]


def _load_rows_from_tiles(ref, n):
    return jnp.concatenate([ref[pl.ds(j, n, stride=ROW_VREGS), :] for j in range(ROW_VREGS)], -1)


def _route(h2, wr_ref, br_ref, tri_ref, cnt_ref, meta_ref, gate_ref, cnt_out_ref):
    i = pl.program_id(0)

    @pl.when(i == 0)
    def _():
        cnt_ref[...] = jnp.zeros_like(cnt_ref)

    logits = jnp.dot(h2.astype(BF16), wr_ref[...], preferred_element_type=F32) + br_ref[...]
    lane = lax.broadcasted_iota(I32, (TR, LANES), 1).astype(F32)
    big = float(1 << 20)
    is_g = lane < N_GROUPS
    gl = jnp.where(is_g, logits, NEG)
    gmax = jnp.max(gl, -1, keepdims=True)
    gsel = jnp.min(jnp.where(gl == gmax, lane, big), -1, keepdims=True)
    pg = 1.0 / jnp.sum(jnp.where(is_g, jnp.exp(gl - gmax), 0.0), -1, keepdims=True)
    lo = N_GROUPS + EXPERTS_PER_GROUP * gsel
    in_grp = jnp.logical_and(lane >= lo, lane < lo + EXPERTS_PER_GROUP)
    el = jnp.where(in_grp, logits, NEG)
    m1 = jnp.max(el, -1, keepdims=True)
    i1 = jnp.min(jnp.where(el == m1, lane, big), -1, keepdims=True)
    el2 = jnp.where(lane == i1, NEG, el)
    m2 = jnp.max(el2, -1, keepdims=True)
    i2 = jnp.min(jnp.where(el2 == m2, lane, big), -1, keepdims=True)
    r = jnp.exp(m2 - m1)
    g1 = pg / (1.0 + r)
    g2 = pg * r / (1.0 + r)
    e1 = i1 - N_GROUPS
    e2 = i2 - N_GROUPS

    tok = i * TR + lax.broadcasted_iota(I32, (TR, 1), 0)
    valid = tok < NV
    oh1 = jnp.logical_and(lane == e1, valid)
    oh2 = jnp.logical_and(lane == e2, valid)
    oh = jnp.where(jnp.logical_or(oh1, oh2), 1.0, 0.0)
    before = jnp.dot(tri_ref[...], oh.astype(BF16), preferred_element_type=F32) + cnt_ref[...]
    rank1 = jnp.sum(jnp.where(oh1, before, 0.0), -1, keepdims=True)
    rank2 = jnp.sum(jnp.where(oh2, before, 0.0), -1, keepdims=True)
    tile_cnt = jnp.sum(oh, 0, keepdims=True)
    cnt_ref[...] = cnt_ref[...] + tile_cnt

    eye = jnp.where(lax.broadcasted_iota(I32, (TR, TR), 0) == lax.broadcasted_iota(I32, (TR, TR), 1), 1.0, 0.0)
    rows = [jnp.sum(eye * col, 0, keepdims=True) for col in (e1, e2, rank1, rank2)]
    meta_ref[0] = jnp.concatenate(rows + [jnp.zeros((4, TR), F32)], 0).astype(I32)
    gates = jnp.where(lane == 0, g1, jnp.where(lane == 1, g2, 0.0))
    gates = jnp.where(valid, gates, 0.0)
    gate_ref[...] = gates[:, 0:8]
    cnt_out_ref[0] = tile_cnt


def _router_operands(wg_group, bg_group, wg_exp, bg_exp):
    wr = jnp.zeros((D_MODEL, LANES), F32)
    wr = wr.at[:, 0:N_GROUPS].set(wg_group.astype(F32)).at[:, N_GROUPS:N_GROUPS + N_EXPERTS].set(wg_exp.astype(F32))
    br = jnp.zeros((1, LANES), F32)
    br = br.at[0, 0:N_GROUPS].set(bg_group.astype(F32)).at[0, N_GROUPS:N_GROUPS + N_EXPERTS].set(bg_exp.astype(F32))
    tri = jnp.asarray(np.tril(np.ones((TR, TR), np.float32), -1), BF16)
    return wr.astype(BF16), br, tri


_ROUTE_OUT_SPECS = [pl.BlockSpec((1, 8, TR), lambda i: (i, 0, 0)),
                    pl.BlockSpec((TR, 8), lambda i: (i, 0)),
                    pl.BlockSpec((1, 1, LANES), lambda i: (i, 0, 0))]
_ROUTE_OUT_SHAPES = [jax.ShapeDtypeStruct((N_TILES, 8, TR), I32), jax.ShapeDtypeStruct((NT, 8), F32),
                     jax.ShapeDtypeStruct((N_TILES, 1, LANES), F32)]


def _const_spec(a):
    return pl.BlockSpec(a.shape, lambda i, n=a.ndim: (0,) * n)


def _outproj_kernel(xp_ref, xs_ref, oa_ref, oas_ref, ob_ref, obs_ref, w_ref, g_ref, wr_ref, br_ref, tri_ref,
                    x1_ref, meta_ref, gate_ref, cnt_out_ref, cnt_ref):
    i = pl.program_id(0)
    x = _token_tile(i, xp_ref, xs_ref)
    mixed = jnp.concatenate([_token_tile(i, oa_ref, oas_ref), _token_tile(i, ob_ref, obs_ref)], -1)
    x1 = x + jnp.dot(mixed, w_ref[...], preferred_element_type=F32)
    x1_ref[...] = x1
    h2 = _rms(x1, g_ref[...])
    _route(h2, wr_ref, br_ref, tri_ref, cnt_ref, meta_ref, gate_ref, cnt_out_ref)


def _outproj_route(xp, xs_pad, oa, oa_s, ob, ob_s, w_out_bf16, g_ffn, wr, br, tri):
    row = pl.BlockSpec((TR, D_MODEL), lambda i: (i, 0))
    half = pl.BlockSpec((TR, A_WIDTH), lambda i: (jnp.minimum(i, SAMPLE_TILE - 1), 0))
    half_s = pl.BlockSpec((TR, A_WIDTH), lambda i: (0, 0))
    return pl.pallas_call(
        _outproj_kernel,
        grid=(N_TILES,),
        in_specs=[_PROMPT_ROWS, _SAMPLE_ROWS, half, half_s, half, half_s,
                  _const_spec(w_out_bf16), _const_spec(g_ffn),
                  _const_spec(wr), _const_spec(br), _const_spec(tri)],
        out_specs=[row] + _ROUTE_OUT_SPECS,
        out_shape=[jax.ShapeDtypeStruct((NT, D_MODEL), F32)] + _ROUTE_OUT_SHAPES,
        scratch_shapes=[pltpu.VMEM((1, LANES), F32)],
        compiler_params=_cparams(("arbitrary",)),
        name="outproj_route",
    )(xp, xs_pad, oa, oa_s, ob, ob_s, w_out_bf16, g_ffn, wr, br, tri)


def _plan(meta, tile_counts):
    tcnt = tile_counts[:, 0, :N_EXPERTS].astype(I32)
    cnt = jnp.sum(tcnt, 0)
    nblk = (cnt + TM - 1) // TM
    blk_end = jnp.cumsum(nblk)
    pstart = (blk_end - nblk) * TM
    n_used = blk_end[-1]
    blk = jnp.minimum(jnp.arange(N_SLOTS_BLK, dtype=I32), n_used - 1)
    blk_exp = jnp.sum((blk[:, None] >= blk_end[None, :]).astype(I32), -1)
    blk_exp = jnp.minimum(blk_exp, N_EXPERTS - 1)
    eid = meta[:, 0:2, :]
    experts = jnp.arange(N_EXPERTS, dtype=I32)[:, None, None, None]
    start_of = jnp.sum(jnp.where(eid[None] == experts, pstart[:, None, None, None], 0), 0)
    tok = (jnp.arange(N_TILES, dtype=I32) * TR)[:, None, None] + jnp.arange(TR, dtype=I32)
    dest = jnp.where(tok < NV, start_of + meta[:, 2:4, :], 0).astype(I32).reshape(-1)
    pad_lo = pstart + cnt
    pad_hi = pstart + nblk * TM
    return dest, blk_exp, n_used.reshape(1).astype(I32), pad_lo.astype(I32), pad_hi.astype(I32)


def _row(ref, r):
    return ref.at[pl.ds(pl.multiple_of(r * ROW_VREGS, ROW_VREGS), ROW_VREGS), :]


def _dest_index(tile, kk, t):
    return (2 * tile + kk) * TR + t


def _dispatch_kernel(dest_ref, plo_ref, phi_ref, nused_ref, x_ref, g_ref, xs_ref, stage, zero_ref, sems, zsem):
    i = pl.program_id(0)
    last = pl.num_programs(0) - 1
    slot = i % 2
    _store_rows_as_tiles(stage.at[slot], _rms(x_ref[...], g_ref[...]))

    def wait_tile(n_tok, s):
        for _ in range(2):
            pltpu.make_async_copy(stage.at[s, pl.ds(0, n_tok * ROW_VREGS), :],
                                  xs_ref.at[pl.ds(0, n_tok * ROW_VREGS), :], sems.at[s]).wait()

    def scatter(n_tok):
        def issue(t, carry):
            src = stage.at[slot, pl.ds(pl.multiple_of(t * ROW_VREGS, ROW_VREGS), ROW_VREGS), :]
            for kk in range(2):
                pltpu.make_async_copy(src, _row(xs_ref, dest_ref[_dest_index(i, kk, t)]),
                                      sems.at[slot]).start(priority=kk)
            return carry
        lax.fori_loop(0, n_tok, issue, 0)

    @pl.when(i < last)
    def _():
        scatter(TR)

    @pl.when(i > 0)
    def _():
        wait_tile(TR, 1 - slot)

    @pl.when(i == last)
    def _():
        n_last = NV - (N_TILES - 1) * TR
        scatter(n_last)
        wait_tile(n_last, slot)
        zero_ref[...] = jnp.zeros_like(zero_ref)

        def zero_copy(first_row, n_rows):
            dst = xs_ref.at[pl.ds(pl.multiple_of(first_row * ROW_VREGS, ROW_VREGS), n_rows * ROW_VREGS), :]
            return pltpu.make_async_copy(zero_ref.at[pl.ds(0, n_rows * ROW_VREGS), :], dst, zsem)

        def per_expert(e, carry):
            row, left = plo_ref[e], phi_ref[e] - plo_ref[e]
            for size in PAD_FILL_SIZES:
                shift = size.bit_length() - 1
                cnt = lax.shift_right_logical(left, shift)

                def fill(j, c2, row=row, size=size):
                    zero_copy(row + j * size, size).start()
                    return c2
                lax.fori_loop(0, cnt, fill, 0)

                def fill_wait(j, c2, size=size):
                    zero_copy(0, size).wait()
                    return c2
                lax.fori_loop(0, cnt, fill_wait, 0)
                done = lax.shift_left(cnt, shift)
                row, left = row + done, left - done
            return carry
        lax.fori_loop(0, N_EXPERTS, per_expert, 0)

        def block_of(nb):
            return xs_ref.at[pl.ds(pl.multiple_of(nb * (TM * ROW_VREGS), TM * ROW_VREGS), TM * ROW_VREGS), :]

        def fill_blk(nb, carry):
            pltpu.make_async_copy(zero_ref, block_of(nb), zsem).start()
            return carry
        lax.fori_loop(nused_ref[0], N_SLOTS_BLK, fill_blk, 0)

        def fill_blk_wait(nb, carry):
            pltpu.make_async_copy(zero_ref, block_of(0), zsem).wait()
            return carry
        lax.fori_loop(nused_ref[0], N_SLOTS_BLK, fill_blk_wait, 0)


def _dispatch(x, g_ffn, dest, pad_lo, pad_hi, n_used):
    return pl.pallas_call(
        _dispatch_kernel,
        grid_spec=pltpu.PrefetchScalarGridSpec(
            num_scalar_prefetch=4, grid=(N_TILES,),
            in_specs=[pl.BlockSpec((TR, D_MODEL), lambda i, *_: (i, 0)),
                      pl.BlockSpec((1, D_MODEL), lambda i, *_: (0, 0))],
            out_specs=pl.BlockSpec(memory_space=pl.ANY),
            scratch_shapes=[pltpu.VMEM((2, TR * ROW_VREGS, LANES), F32),
                            pltpu.VMEM((TM * ROW_VREGS, LANES), F32),
                            pltpu.SemaphoreType.DMA((2,)), pltpu.SemaphoreType.DMA(())]),
        out_shape=jax.ShapeDtypeStruct((N_SLOTS * ROW_VREGS, LANES), F32),
        compiler_params=_cparams(("arbitrary",)),
        name="dispatch",
    )(dest, pad_lo, pad_hi, n_used, x, g_ffn)


class _ExpertRows:
    scratch = [pltpu.VMEM((2, 2 * TR * ROW_VREGS, LANES), F32), pltpu.SemaphoreType.DMA((2,))]

    def __init__(self, dest_ref, ys_ref, buf, sems):
        self.dest_ref, self.ys_ref, self.buf, self.sems = dest_ref, ys_ref, buf, sems

    def _copy(self, src_row, slot, dst_row):
        dst = self.buf.at[slot, pl.ds(pl.multiple_of(dst_row * ROW_VREGS, ROW_VREGS), ROW_VREGS), :]
        return pltpu.make_async_copy(_row(self.ys_ref, src_row), dst, self.sems.at[slot])

    def start(self, tile):
        slot = tile % 2

        def issue(g, carry):
            for un in range(ROW_DMA_UNROLL):
                t = g * ROW_DMA_UNROLL + un
                for kk in range(2):
                    self._copy(self.dest_ref[_dest_index(tile, kk, t)], slot, kk * TR + t).start(priority=kk)
            return carry
        lax.fori_loop(0, TR // ROW_DMA_UNROLL, issue, 0)

    def wait(self, tile):
        slot = tile % 2
        pltpu.make_async_copy(self.ys_ref.at[pl.ds(0, 2 * TR * ROW_VREGS), :], self.buf.at[slot],
                              self.sems.at[slot]).wait()

    def combine(self, tile, x, gate_ref):
        rows = self.buf.at[tile % 2]
        g = gate_ref[...]
        for kk in range(2):
            y = jnp.concatenate([rows[pl.ds(kk * TR * ROW_VREGS + j, TR, stride=ROW_VREGS), :]
                                 for j in range(ROW_VREGS)], -1)
            x = x + g[:, kk:kk + 1] * y
        return x

    def fetch_combine(self, x, gate_ref):
        i = pl.program_id(0)

        @pl.when(i == 0)
        def _():
            self.start(i)

        @pl.when(i + 1 < pl.num_programs(0))
        def _():
            self.start(i + 1)

        self.wait(i)
        return self.combine(i, x, gate_ref)


def _ffn_kernel(be_ref, nused_ref, xs_ref, w1_ref, w3_ref, w2_ref, y_ref, w1b, w3b, w2b):
    nb = pl.program_id(0)

    @pl.when(nb < nused_ref[0])
    def _():
        prev = be_ref[jnp.maximum(nb - 1, 0)]
        fresh = jnp.logical_or(nb == 0, be_ref[nb] != prev)

        @pl.when(fresh)
        def _():
            w1b[...] = w1_ref[...].astype(BF16)
            w3b[...] = w3_ref[...].astype(BF16)
            w2b[...] = w2_ref[...].astype(BF16)

        x = _load_rows_from_tiles(xs_ref, TM).astype(BF16)
        a = jnp.dot(x, w1b[...], preferred_element_type=F32)
        b = jnp.dot(x, w3b[...], preferred_element_type=F32)
        mid = (_silu(a) * b).astype(BF16)
        y = jnp.dot(mid, w2b[...], preferred_element_type=F32)
        _store_rows_as_tiles(y_ref, y)

    @pl.when(nb >= nused_ref[0])
    def _():
        y_ref[...] = jnp.zeros_like(y_ref)


def _experts(xs, blk_exp, n_used, layer, w1, w3, w2):
    rows = pl.BlockSpec((TM * ROW_VREGS, LANES), lambda nb, be, nu: (nb, 0))
    rows_in = rows

    def wspec(a, b):
        return pl.BlockSpec((None, None, a, b), lambda nb, be, nu: (layer, be[nb], 0, 0))

    return pl.pallas_call(
        _ffn_kernel,
        grid_spec=pltpu.PrefetchScalarGridSpec(
            num_scalar_prefetch=2, grid=(N_SLOTS_BLK,),
            in_specs=[rows_in, wspec(D_MODEL, D_EXPERT), wspec(D_MODEL, D_EXPERT), wspec(D_EXPERT, D_MODEL)],
            out_specs=rows,
            scratch_shapes=[pltpu.VMEM((D_MODEL, D_EXPERT), BF16), pltpu.VMEM((D_MODEL, D_EXPERT), BF16),
                            pltpu.VMEM((D_EXPERT, D_MODEL), BF16)]),
        out_shape=jax.ShapeDtypeStruct((N_SLOTS * ROW_VREGS, LANES), F32),
        compiler_params=_cparams(("arbitrary",)),
        name="experts",
    )(blk_exp, n_used, xs, w1, w3, w2)


def _moe(x, g_ffn, meta, counts, layer, w1, w3, w2):
    dest, blk_exp, n_used, pad_lo, pad_hi = _plan(meta, counts)
    xs = _dispatch(x, g_ffn, dest, pad_lo, pad_hi, n_used)
    return _experts(xs, blk_exp, n_used, layer, w1, w3, w2), dest


def _pool_kernel(dest_ref, x1_ref, ys_ref, gate0_ref, gmix_ref, sp_ref, pw_ref, ps_ref, gffn_ref,
                 wr_ref, br_ref, tri_ref,
                 x3_ref, hkeep_ref, hs_ref, meta_ref, gate_ref, cnt_out_ref,
                 cnt_ref, ext_ref, e1_ref, e2_ref, e3_ref, e4_ref, mixed_ref, ybuf, ysems):
    i = pl.program_id(0)
    x2 = _ExpertRows(dest_ref, ys_ref, ybuf, ysems).fetch_combine(x1_ref[...], gate0_ref)
    h = _rms(x2, gmix_ref[...])
    H = POOL_HALO
    lvl_refs = (e1_ref, e2_ref, e3_ref, e4_ref)

    @pl.when(i < SAMPLE_TILE)
    def _():
        hkeep_ref[...] = h[TR - 16:TR]
        @pl.when(i % TILES_PER_SEQ == 0)
        def _():
            ext_ref[0:H, :] = jnp.zeros((H, D_MODEL), F32)
        ext_ref[H:H + TR, :] = h
        src = ext_ref
        for lv in range(4):
            sh = 1 << lv
            lo = 8 * (lv + 1)
            c0 = POOL_GROUP * lv
            dst = lvl_refs[lv]
            dst[lo:H + TR, c0:] = src[lo:H + TR, c0:] + src[lo - sh:H + TR - sh, c0:]
            src = dst
        pos = (i % TILES_PER_SEQ) * TR + lax.broadcasted_iota(I32, (TR, 1), 0)
        for gi, w in enumerate(POOL_WINDOWS):
            cs = slice(gi * POOL_GROUP, (gi + 1) * POOL_GROUP)
            inv = 1.0 / jnp.minimum(w, pos + 1).astype(F32)
            mixed_ref[:, cs] = lvl_refs[gi][H:H + TR, cs] * inv - h[:, cs]
        ext_ref[0:H, :] = h[TR - H:TR]

    @pl.when(i == SAMPLE_TILE)
    def _():
        hs = h[0:DEC_BATCH]
        hs_ref[...] = hs
        mixed_ref[...] = jnp.zeros_like(mixed_ref)
        for gi, w in enumerate(POOL_WINDOWS):
            cs = slice(gi * POOL_GROUP, (gi + 1) * POOL_GROUP)
            win = hs[:, cs]
            for dlt in range(1, w):
                win = win + sp_ref[POOL_KEEP - dlt][:, cs]
            mixed_ref[0:DEC_BATCH, cs] = win * (1.0 / w) - hs[:, cs]

    mixed = mixed_ref[...].astype(BF16)
    outs = [jnp.dot(mixed[:, gi * POOL_GROUP:(gi + 1) * POOL_GROUP], pw_ref[gi], preferred_element_type=F32)
            for gi in range(len(POOL_WINDOWS))]
    x3 = x2 + jnp.concatenate(outs, -1) * ps_ref[...]
    x3_ref[...] = x3
    h2 = _rms(x3, gffn_ref[...])
    _route(h2, wr_ref, br_ref, tri_ref, cnt_ref, meta_ref, gate_ref, cnt_out_ref)


def _const_spec_p(a):
    return pl.BlockSpec(a.shape, lambda i, *_, n=a.ndim: (0,) * n)


def _pool_route(dest, x1, ys, gates0, g_mix, sp_t, pw_bf16, p_scale, g_ffn, wr, br, tri):
    def rows(shape):
        return pl.BlockSpec(shape, lambda i, *_: (i, 0))

    ext = pltpu.VMEM((POOL_HALO + TR, D_MODEL), F32)
    consts = [g_mix, sp_t, pw_bf16, p_scale, g_ffn, wr, br, tri]
    return pl.pallas_call(
        _pool_kernel,
        grid_spec=pltpu.PrefetchScalarGridSpec(
            num_scalar_prefetch=1, grid=(N_TILES,),
            in_specs=[rows((TR, D_MODEL)), pl.BlockSpec(memory_space=pl.ANY), rows((TR, 8))]
                     + [_const_spec_p(a) for a in consts],
            out_specs=[rows((TR, D_MODEL)),
                       pl.BlockSpec((None, 16, D_MODEL),
                                    lambda i, *_: (jnp.minimum(i // TILES_PER_SEQ, BATCH - 1), 0, 0)),
                       pl.BlockSpec((DEC_BATCH, D_MODEL), lambda i, *_: (0, 0)),
                       pl.BlockSpec((1, 8, TR), lambda i, *_: (i, 0, 0)), rows((TR, 8)),
                       pl.BlockSpec((1, 1, LANES), lambda i, *_: (i, 0, 0))],
            scratch_shapes=[pltpu.VMEM((1, LANES), F32), ext, ext, ext, ext, ext,
                            pltpu.VMEM((TR, D_MODEL), F32)] + _ExpertRows.scratch),
        out_shape=[jax.ShapeDtypeStruct((NT, D_MODEL), F32),
                   jax.ShapeDtypeStruct((BATCH, 16, D_MODEL), F32),
                   jax.ShapeDtypeStruct((DEC_BATCH, D_MODEL), F32)] + _ROUTE_OUT_SHAPES,
        compiler_params=_cparams(("arbitrary",)),
        name="pool_route",
    )(dest, x1, ys, gates0, *consts)


def _final_kernel(dest_ref, x_ref, ys_ref, gate_ref, g_ref, yp_ref, ysm_ref, ybuf, ysems):
    i = pl.program_id(0)
    y = _rms(_ExpertRows(dest_ref, ys_ref, ybuf, ysems).fetch_combine(x_ref[...], gate_ref), g_ref[...])

    @pl.when(i < SAMPLE_TILE)
    def _():
        yp_ref[...] = y

    @pl.when(i == SAMPLE_TILE)
    def _():
        ysm_ref[...] = y[0:DEC_BATCH]


def _final(dest, x3, ys, gates, g_final):
    return pl.pallas_call(
        _final_kernel,
        grid_spec=pltpu.PrefetchScalarGridSpec(
            num_scalar_prefetch=1, grid=(N_TILES,),
            in_specs=[pl.BlockSpec((TR, D_MODEL), lambda i, *_: (i, 0)), pl.BlockSpec(memory_space=pl.ANY),
                      pl.BlockSpec((TR, 8), lambda i, *_: (i, 0)), _const_spec_p(g_final)],
            out_specs=[pl.BlockSpec((TR, D_MODEL), lambda i, *_: (jnp.minimum(i, SAMPLE_TILE - 1), 0)),
                       pl.BlockSpec((DEC_BATCH, D_MODEL), lambda i, *_: (0, 0))],
            scratch_shapes=_ExpertRows.scratch),
        out_shape=[jax.ShapeDtypeStruct((NP, D_MODEL), F32), jax.ShapeDtypeStruct((DEC_BATCH, D_MODEL), F32)],
        compiler_params=_cparams(("arbitrary",)),
        name="final_norm",
    )(dest, x3, ys, gates, g_final)


def kernel(x_prompt, x_sample, cache_win_k, cache_win_v, state_hgrn, state_pool, rel_bias, norm_mix, norm_ffn,
           norm_final, w_in, w_out, hgrn_lb, hgrn_gnorm, pool_w, pool_scale, moe_wg_group, moe_bg_group,
           moe_wg_exp, moe_bg_exp, moe_w1, moe_w3, moe_w2):
    xp = x_prompt.reshape(NP, D_MODEL)
    xs_pad = jnp.pad(x_sample.reshape(DEC_BATCH, D_MODEL), ((0, TR - DEC_BATCH), (0, 0)))
    lb = jnp.cumsum(jax.nn.softmax(hgrn_lb.astype(F32), axis=0), axis=0)[0:1]
    gnorm = hgrn_gnorm[0:1].astype(F32)

    q, k, v, hq, hf, hi, hg, k_win, v_win = _inproj(xp, xs_pad, norm_mix[0:1], w_in[0].astype(BF16))
    oa = _attention_prompt(q, k, v, _attn_bias_tables(rel_bias))
    ob, s_prompt = _hgrn_prompt(hq, hf, hi, hg, lb, gnorm)
    oa_s, ob_s, s_sample = _sample_mixers(
        q, k, v, hq, hf, hi, hg,
        cache_win_k[0], cache_win_v[0], state_hgrn[0], _sample_bias_tables(rel_bias), lb, gnorm)
    wr, br, tri = _router_operands(moe_wg_group[0], moe_bg_group[0], moe_wg_exp[0], moe_bg_exp[0])
    x1, meta, gates0, counts = _outproj_route(xp, xs_pad, oa, oa_s, ob, ob_s, w_out[0].astype(BF16),
                                              norm_ffn[0:1], wr, br, tri)
    ys, dest = _moe(x1, norm_ffn[0:1], meta, counts, 0, moe_w1, moe_w3, moe_w2)

    wr, br, tri = _router_operands(moe_wg_group[1], moe_bg_group[1], moe_wg_exp[1], moe_bg_exp[1])
    sp_t = jnp.transpose(state_pool[0], (1, 0, 2))
    x3, hkeep, hsample, meta, gates1, counts = _pool_route(
        dest, x1, ys, gates0, norm_mix[1:2], sp_t, pool_w[0].astype(BF16), pool_scale[0:1], norm_ffn[1:2],
        wr, br, tri)
    ys, dest = _moe(x3, norm_ffn[1:2], meta, counts, 1, moe_w1, moe_w3, moe_w2)
    y_prompt, y_sample = _final(dest, x3, ys, gates1, norm_final.reshape(1, D_MODEL))

    k_s = k[NP:NV].reshape(1, DEC_BATCH, 1, N_HEADS, D_HEAD)
    v_s = v[NP:NV].reshape(1, DEC_BATCH, 1, N_HEADS, D_HEAD)
    return (y_prompt.reshape(BATCH, SEQ, D_MODEL), y_sample.reshape(DEC_BATCH, 1, D_MODEL),
            k_win.reshape(1, BATCH, W_MAX, N_HEADS, D_HEAD), v_win.reshape(1, BATCH, W_MAX, N_HEADS, D_HEAD),
            s_prompt[None], hkeep[:, 16 - POOL_KEEP:][None],
            k_s, v_s, s_sample[None], hsample.reshape(1, DEC_BATCH, 1, D_MODEL))
```

```python
import functools

import numpy as np
import jax
import jax.numpy as jnp
from jax import lax
from jax.experimental import pallas as pl
from jax.experimental.pallas import tpu as pltpu

F32 = jnp.float32
BF16 = jnp.bfloat16
I32 = jnp.int32

D_MODEL = 1024
BATCH = 4
SEQ = 4096
DEC_BATCH = 32
PAST_LEN = 8192
W_MAX = 2048
N_HEADS = 4
D_HEAD = 128
A_WIDTH = N_HEADS * D_HEAD
N_PROJ = 7 * A_WIDTH
DILATED = ((128, 1), (512, 4), (2048, 16))
NUM_BUCKETS = 32
MAX_DISTANCE = 2048
POOL_WINDOWS = (2, 4, 8, 16)
POOL_GROUP = 256
POOL_KEEP = 15
N_GROUPS = 4
EXPERTS_PER_GROUP = 8
N_EXPERTS = 32
D_EXPERT = 512
EPS = 1e-6
NEG = -1e30

LANES = 128
SUBLANES = 8
ROW_VREGS = D_MODEL // LANES
TR = 256
NP = BATCH * SEQ
NV = NP + DEC_BATCH
NT = NP + TR
N_TILES = NT // TR
SAMPLE_TILE = NP // TR
TILES_PER_SEQ = SEQ // TR
TM = 512
N_SLOTS_BLK = (2 * NV + N_EXPERTS * (TM - 1) + TM - 1) // TM
N_SLOTS = N_SLOTS_BLK * TM
QB = 128
ATTN_UNROLL = 8
ATTN_PERIOD = 3 * QB
ROW_DMA_UNROLL = 8
PAD_FILL_SIZES = (64, 8, 1)
CH = 128
HGRN_HEADS_PER_STEP = 2
N_LEVELS = 7
POOL_HALO = 32
VMEM_LIMIT = 56 * 1024 * 1024


def _cparams(sem=None, vmem=VMEM_LIMIT):
    kw = dict(vmem_limit_bytes=vmem)
    if sem is not None:
        kw["dimension_semantics"] = sem
    return pltpu.CompilerParams(**kw)


def _rms(x, g):
    return x * lax.rsqrt(jnp.mean(x * x, -1, keepdims=True) + EPS) * g


def _sigmoid(x):
    return 1.0 / (1.0 + jnp.exp(-x))


def _silu(x):
    return x * _sigmoid(x)


def _token_tile(i, xp_ref, xs_ref):
    return jnp.where(i == SAMPLE_TILE, xs_ref[...], xp_ref[...])


_PROMPT_ROWS = pl.BlockSpec((TR, D_MODEL), lambda i: (jnp.minimum(i, SAMPLE_TILE - 1), 0))
_SAMPLE_ROWS = pl.BlockSpec((TR, D_MODEL), lambda i: (0, 0))
WIN_TILES = W_MAX // TR


def _window_block(i):
    seq = jnp.minimum(i // TILES_PER_SEQ, BATCH - 1)
    j = jnp.clip(i % TILES_PER_SEQ - (TILES_PER_SEQ - WIN_TILES), 0, WIN_TILES - 1)
    return jnp.where(i >= SAMPLE_TILE, BATCH * WIN_TILES - 1, seq * WIN_TILES + j)


def _inproj_kernel(xp_ref, xs_ref, g_ref, w_ref, *out_refs):
    i = pl.program_id(0)
    h = _rms(_token_tile(i, xp_ref, xs_ref), g_ref[...])
    p = jnp.dot(h.astype(BF16), w_ref[...], preferred_element_type=F32)
    for n, o_ref in enumerate(out_refs[:7]):
        o_ref[...] = p[:, n * A_WIDTH:(n + 1) * A_WIDTH]

    @pl.when(jnp.logical_and(i < SAMPLE_TILE, i % TILES_PER_SEQ >= TILES_PER_SEQ - WIN_TILES))
    def _():
        for n, o_ref in ((1, out_refs[7]), (2, out_refs[8])):
            for h_i in range(N_HEADS):
                c0 = n * A_WIDTH + h_i * D_HEAD
                o_ref[pl.ds(h_i, TR, stride=N_HEADS), :] = p[:, c0:c0 + D_HEAD]


def _inproj(xp, xs_pad, g, w_bf16):
    out_sds = ([jax.ShapeDtypeStruct((NT, A_WIDTH), F32)] * 7
               + [jax.ShapeDtypeStruct((BATCH * W_MAX * N_HEADS, D_HEAD), F32)] * 2)
    win = pl.BlockSpec((TR * N_HEADS, D_HEAD), lambda i: (_window_block(i), 0))
    return pl.pallas_call(
        _inproj_kernel,
        grid=(N_TILES,),
        in_specs=[_PROMPT_ROWS, _SAMPLE_ROWS,
                  pl.BlockSpec((1, D_MODEL), lambda i: (0, 0)),
                  pl.BlockSpec((D_MODEL, N_PROJ), lambda i: (0, 0))],
        out_specs=[pl.BlockSpec((TR, A_WIDTH), lambda i: (i, 0))] * 7 + [win, win],
        out_shape=out_sds,
        compiler_params=_cparams(("arbitrary",)),
        name="inproj",
    )(xp, xs_pad, g, w_bf16)


def _t5_bucket(dist):
    max_exact = NUM_BUCKETS // 2
    d = np.asarray(dist)
    large = max_exact + np.floor(np.log(np.maximum(d, 1) / max_exact)
                                 / np.log(MAX_DISTANCE / max_exact) * (NUM_BUCKETS - max_exact)).astype(np.int32)
    large = np.minimum(large, NUM_BUCKETS - 1)
    return np.where(d < max_exact, d, large).astype(np.int32)


def _attn_bias_tables(rel_bias):
    period = ATTN_PERIOD
    m = np.arange(period)
    u = np.where(m < 2 * QB, m, m - period)
    pick = np.zeros((len(DILATED), 2, period, NUM_BUCKETS), np.float32)
    mask = np.zeros((len(DILATED), 2, period, 1), np.float32)
    for bi, (w, d) in enumerate(DILATED):
        nk = w // d
        for vi, off in enumerate((0, QB)):
            j = off - u
            ok = (j >= 0) & (j <= nk)
            pick[bi, vi, m[ok], _t5_bucket(d * j[ok])] = 1.0
            mask[bi, vi, ~ok, 0] = NEG
    vec = jnp.einsum("bvmk,kh->hbvm", pick, rel_bias.astype(F32), precision=lax.Precision.HIGHEST)
    vec = vec + jnp.transpose(mask, (3, 0, 1, 2))
    return vec[:, :, :, None, :]


def _attn_kernel(q_ref, k_ref, v_ref, vec_ref, o_ref,
                 q4, k4, v4, qd, kd, vd, ud, md, sd, u_acc, m_acc, s_acc, bias_ref):
    scale = D_HEAD ** -0.5
    c4 = SEQ // 4
    c16 = SEQ // 16
    for bi in range(len(DILATED)):
        for vi in range(2):
            rows = jnp.broadcast_to(vec_ref[bi, vi], (QB, ATTN_PERIOD))
            bias_ref[bi, vi] = pltpu.roll(rows, 0, 1, stride=1, stride_axis=0)[:, :2 * QB]

    def block_stats(bi, t, nb):
        has_prev = jnp.minimum(t % nb, 1)
        q0 = pl.multiple_of(t * QB, QB)
        k0 = pl.multiple_of((t - has_prev) * QB, QB)
        qb = qd[pl.ds(q0, QB), :]
        kb = kd[pl.ds(k0, 2 * QB), :]
        vb = vd[pl.ds(k0, 2 * QB), :]
        s = lax.dot_general(qb, kb, (((1,), (1,)), ((), ())), preferred_element_type=F32)
        s = s + bias_ref[bi, has_prev]
        mb = jnp.max(s, -1, keepdims=True)
        p = jnp.exp(s - mb)
        sb = jnp.sum(p, -1, keepdims=True)
        u = jnp.dot(p.astype(BF16), vb, preferred_element_type=F32)
        return q0, mb, sb, u

    def run_blocks(bi, nb, consume):
        def body(g, carry):
            for un in range(ATTN_UNROLL):
                q0, mb, sb, u = block_stats(bi, g * ATTN_UNROLL + un, nb)
                consume(pl.ds(q0, QB), mb, sb, u)
            return carry
        lax.fori_loop(0, SEQ // QB // ATTN_UNROLL, body, 0)

    def merged(rows, m_ref, s_ref, u_ref, mb, sb, u):
        m_old = m_ref[rows, :]
        m_new = jnp.maximum(m_old, mb)
        a = jnp.exp(m_old - m_new)
        b = jnp.exp(mb - m_new)
        return m_new, a * s_ref[rows, :] + b * sb, a * u_ref[rows, :] + b * u

    for r in range(4):
        src, dst = pl.ds(r, c4, stride=4), pl.ds(r * c4, c4)
        q4[dst, :] = q_ref[src, :] * scale
        k4[dst, :] = k_ref[src, :]
        v4[dst, :] = v_ref[src, :]

    for r in range(4):
        for j in range(4):
            src, dst = pl.ds(r * c4 + j, c16, stride=4), pl.ds((r + 4 * j) * c16, c16)
            qd[dst, :] = q4[src, :].astype(BF16)
            kd[dst, :] = k4[src, :].astype(BF16)
            vd[dst, :] = v4[src, :].astype(BF16)

    def keep16(rows, mb, sb, u):
        ud[rows, :] = u
        md[rows, :] = jnp.broadcast_to(mb, (QB, D_HEAD))
        sd[rows, :] = jnp.broadcast_to(sb, (QB, D_HEAD))
    run_blocks(2, c16 // QB, keep16)
    for r in range(4):
        for j in range(4):
            src, dst = pl.ds((r + 4 * j) * c16, c16), pl.ds(r * c4 + j, c16, stride=4)
            u_acc[dst, :] = ud[src, :]
            m_acc[dst, :] = md[src, :]
            s_acc[dst, :] = sd[src, :]

    qd[...] = q4[...].astype(BF16)
    kd[...] = k4[...].astype(BF16)
    vd[...] = v4[...].astype(BF16)

    def merge4(rows, mb, sb, u):
        m_acc[rows, :], s_acc[rows, :], u_acc[rows, :] = merged(rows, m_acc, s_acc, u_acc, mb, sb, u)
    run_blocks(1, c4 // QB, merge4)

    for r in range(4):
        src, dst = pl.ds(r * c4, c4), pl.ds(r, c4, stride=4)
        ud[dst, :] = u_acc[src, :]
        md[dst, :] = m_acc[src, :]
        sd[dst, :] = s_acc[src, :]
    qd[...] = (q_ref[...] * scale).astype(BF16)
    kd[...] = k_ref[...].astype(BF16)
    vd[...] = v_ref[...].astype(BF16)

    def finish(rows, mb, sb, u):
        _, den, num = merged(rows, md, sd, ud, mb, sb, u)
        o_ref[rows, :] = (num / den).astype(o_ref.dtype)
    run_blocks(0, SEQ // QB, finish)


def _attention_prompt(q, k, v, bias_tabs):
    blk = pl.BlockSpec((SEQ, D_HEAD), lambda b, h: (b, h))
    return pl.pallas_call(
        _attn_kernel,
        grid=(BATCH, N_HEADS),
        in_specs=[blk, blk, blk,
                  pl.BlockSpec((None, 3, 2, 1, ATTN_PERIOD), lambda b, h: (h, 0, 0, 0, 0))],
        out_specs=blk,
        out_shape=jax.ShapeDtypeStruct((NP, A_WIDTH), BF16),
        scratch_shapes=[pltpu.VMEM((SEQ, D_HEAD), F32)] * 3
                       + [pltpu.VMEM((SEQ, D_HEAD), BF16)] * 3
                       + [pltpu.VMEM((SEQ, D_HEAD), F32)] * 6
                       + [pltpu.VMEM((len(DILATED), 2, QB, 2 * QB), F32)],
        compiler_params=_cparams(("arbitrary", "arbitrary")),
        name="attn_prompt",
    )(q, k, v, bias_tabs)


def _hgrn_tables():
    t = np.arange(CH)
    u = np.arange(CH)
    sums_q = np.zeros((1 + N_LEVELS, CH, CH), np.float32)
    sums_k = np.zeros((2 + N_LEVELS, CH, CH), np.float32)
    sums_q[0] = (u[None, :] <= t[:, None])
    sums_k[0] = (u[None, :] > t[:, None])
    sums_k[1 + N_LEVELS] = 1.0
    pair = np.zeros((N_LEVELS, CH, CH), np.float32)
    for l in range(N_LEVELS):
        h = CH >> (l + 1)
        is_q = (t // h) % 2 == 1
        half_start = (t // h) * h
        half_end = half_start + h
        sel_q = (u[None, :] >= half_start[:, None]) & (u[None, :] <= t[:, None])
        sel_k = (u[None, :] > t[:, None]) & (u[None, :] < half_end[:, None])
        sums_q[1 + l] = sel_q & is_q[:, None]
        sums_k[1 + l] = sel_k & ~is_q[:, None]
        same = (t[:, None] // (2 * h)) == (t[None, :] // (2 * h))
        pair[l] = same & is_q[:, None] & (~is_q)[None, :]
    sums_kt = np.transpose(sums_k, (2, 0, 1)).reshape(CH, (2 + N_LEVELS) * CH)
    sums_q = sums_q.reshape((1 + N_LEVELS) * CH, CH)
    return (jnp.asarray(np.concatenate([sums_q, sums_q], 1), BF16),
            jnp.asarray(np.concatenate([sums_kt, sums_kt], 0), BF16),
            jnp.asarray(pair), jnp.asarray(np.eye(CH, dtype=np.float32)))


def _split_bf16(x):
    hi = x.astype(BF16)
    return hi, (x - hi.astype(F32)).astype(BF16)


def _hgrn_kernel(q_ref, f_ref, i_ref, g_ref, lb_ref, gn_ref, sq_ref, skt_ref, pair_ref, eye_ref, o_ref, s_ref):
    gn = gn_ref[...]

    def chunk(c, st, hh):
        rows = pl.ds(pl.multiple_of(c * CH, CH), CH)
        cols = slice(hh * D_HEAD, (hh + 1) * D_HEAD)
        lb = lb_ref[:, cols]
        q = _silu(q_ref[rows, cols])
        f = lb + (1.0 - lb) * _sigmoid(f_ref[rows, cols])
        lf = jnp.log(f)
        k = 1.0 - f
        v16 = i_ref[rows, cols].astype(BF16)
        kt = k.T
        exq = jnp.exp(jnp.dot(sq_ref[...], jnp.concatenate(_split_bf16(lf), 0),
                              preferred_element_type=F32))
        exk = jnp.exp(jnp.dot(jnp.concatenate(_split_bf16(lf.T), 1), skt_ref[...],
                              preferred_element_type=F32))
        inter = jnp.dot((q * exq[0:CH]).astype(BF16), st.astype(BF16), preferred_element_type=F32)
        sc = eye_ref[...] * jnp.sum(q * k, -1, keepdims=True)
        for l in range(N_LEVELS):
            ql = (q * exq[(1 + l) * CH:(2 + l) * CH]).astype(BF16)
            klt = (kt * exk[:, (1 + l) * CH:(2 + l) * CH]).astype(BF16)
            sc = sc + pair_ref[l] * jnp.dot(ql, klt, preferred_element_type=F32)
        o = inter + jnp.dot(sc.astype(BF16), v16, preferred_element_type=F32)
        st_new = (st * exk[:, (1 + N_LEVELS) * CH:]
                  + jnp.dot((kt * exk[:, 0:CH]).astype(BF16), v16, preferred_element_type=F32))
        o = _rms(o, gn) * _silu(g_ref[rows, cols])
        o_ref[rows, cols] = o.astype(o_ref.dtype)
        return st_new

    def step(c, states):
        return tuple(chunk(c, st, hh) for hh, st in enumerate(states))

    zero = jnp.zeros((D_HEAD, D_HEAD), F32)
    states = lax.fori_loop(0, SEQ // CH, step, (zero,) * HGRN_HEADS_PER_STEP)
    for hh, st in enumerate(states):
        s_ref[hh] = st


def _hgrn_prompt(hq, hf, hi, hg, lb, gnorm):
    tables = _hgrn_tables()
    width = HGRN_HEADS_PER_STEP * D_HEAD
    blk = pl.BlockSpec((SEQ, width), lambda b, h: (b, h))

    def full(a):
        return pl.BlockSpec(a.shape, lambda b, h, n=a.ndim: (0,) * n)

    return pl.pallas_call(
        _hgrn_kernel,
        grid=(BATCH, N_HEADS // HGRN_HEADS_PER_STEP),
        in_specs=[blk, blk, blk, blk,
                  pl.BlockSpec((1, width), lambda b, h: (0, h)),
                  pl.BlockSpec((1, D_HEAD), lambda b, h: (0, 0))] + [full(a) for a in tables],
        out_specs=[blk, pl.BlockSpec((None, HGRN_HEADS_PER_STEP, D_HEAD, D_HEAD), lambda b, h: (b, h, 0, 0))],
        out_shape=[jax.ShapeDtypeStruct((NP, A_WIDTH), BF16),
                   jax.ShapeDtypeStruct((BATCH, N_HEADS, D_HEAD, D_HEAD), F32)],
        compiler_params=_cparams(("arbitrary", "arbitrary")),
        name="hgrn_prompt",
    )(hq, hf, hi, hg, lb, gnorm, *tables)


def _sample_bias_tables(rel_bias):
    j = np.concatenate([QB - np.arange(QB), [0]])
    pick = np.zeros((len(DILATED), QB + 1, NUM_BUCKETS), np.float32)
    for bi, (w, d) in enumerate(DILATED):
        pick[bi, np.arange(QB + 1), _t5_bucket(d * j)] = 1.0
    return jnp.einsum("bjk,kh->bhj", pick, rel_bias.astype(F32), precision=lax.Precision.HIGHEST)[..., None]


def _bf16_round(x):
    return x.astype(BF16).astype(F32)


def _col(row, eye):
    return jnp.sum(eye * row, -1, keepdims=True)


NEAR_POS = 4 * QB
FAR_STEPS = (W_MAX - NEAR_POS) // 16


def _cached_rows(near_ref, far_ref, d, h):
    def near(first_pos, n, step):
        return near_ref[pl.ds((first_pos - (W_MAX - NEAR_POS)) * N_HEADS + h, n, stride=step * N_HEADS), :]

    if QB * d <= NEAR_POS:
        return near(W_MAX - QB * d, QB, d)
    return jnp.concatenate([far_ref[:, h, :], near(W_MAX - NEAR_POS, QB - FAR_STEPS, d)], 0)


def _sample_kernel(q_ref, k_ref, v_ref, hq_ref, hf_ref, hi_ref, hg_ref, kn_ref, kf_ref, vn_ref, vf_ref, s0_ref,
                   bias_ref, lb_ref, gn_ref, eye_ref, oa_ref, ob_ref, s_ref, oa_acc, ob_acc):
    b = pl.program_id(0)
    row = pl.ds(b, 1)
    scale = D_HEAD ** -0.5
    q = _bf16_round(q_ref[row, :] * scale)
    kn = _bf16_round(k_ref[row, :])
    vn = _bf16_round(v_ref[row, :])
    eye = eye_ref[...]

    stats = []
    for bi, (w, d) in enumerate(DILATED):
        per_head = []
        for h in range(N_HEADS):
            hs = slice(h * D_HEAD, (h + 1) * D_HEAD)
            kb = _bf16_round(_cached_rows(kn_ref, kf_ref, d, h))
            vb = _bf16_round(_cached_rows(vn_ref, vf_ref, d, h))
            s = jnp.sum(kb * q[:, hs], -1, keepdims=True) + bias_ref[bi, h, 0:QB]
            s0 = jnp.sum(q[:, hs] * kn[:, hs], -1, keepdims=True) + bias_ref[bi, h, QB:QB + 1]
            m = jnp.maximum(jnp.max(s, 0, keepdims=True), s0)
            p = jnp.exp(s - m)
            p0 = jnp.exp(s0 - m)
            ssum = jnp.sum(p, 0, keepdims=True) + p0
            u = jnp.sum(_bf16_round(p) * vb, 0, keepdims=True) + _bf16_round(p0) * vn[:, hs]
            per_head.append((m, ssum, u))
        stats.append(per_head)
    outs = []
    for h in range(N_HEADS):
        m_all = functools.reduce(jnp.maximum, [stats[bi][h][0] for bi in range(3)])
        num = 0.0
        den = 0.0
        for bi in range(3):
            m, ssum, u = stats[bi][h]
            c = jnp.exp(m - m_all)
            num = num + c * u
            den = den + c * ssum
        outs.append(num / den)
    oa_acc[row, :] = jnp.concatenate(outs, -1)

    qh = _silu(hq_ref[row, :])
    lb = lb_ref[...]
    f = lb + (1.0 - lb) * _sigmoid(hf_ref[row, :])
    vi = hi_ref[row, :]
    gate = _silu(hg_ref[row, :])
    gn = gn_ref[...]
    obs = []
    for h in range(N_HEADS):
        hs = slice(h * D_HEAD, (h + 1) * D_HEAD)
        f_col = _col(f[:, hs], eye)
        q_col = _col(qh[:, hs], eye)
        s_old = s0_ref[h]
        s_ref[h] = f_col * s_old + (1.0 - f_col) * vi[:, hs]
        inter = jnp.sum(_bf16_round(q_col * f_col) * _bf16_round(s_old), 0, keepdims=True)
        qk = jnp.sum(qh[:, hs] * (1.0 - f[:, hs]), -1, keepdims=True)
        o = inter + qk * vi[:, hs]
        obs.append(_rms(o, gn) * gate[:, hs])
    ob_acc[row, :] = jnp.concatenate(obs, -1)

    @pl.when(b == DEC_BATCH - 1)
    def _():
        pad = jnp.zeros((TR - DEC_BATCH, A_WIDTH), F32)
        oa_ref[...] = jnp.concatenate([oa_acc[...], pad], 0).astype(oa_ref.dtype)
        ob_ref[...] = jnp.concatenate([ob_acc[...], pad], 0).astype(ob_ref.dtype)


def _sample_mixers(q, k, v, hq, hf, hi, hg, cache_k, cache_v, state, bias_s, lb, gnorm):
    tile = pl.BlockSpec((TR, A_WIDTH), lambda b: (SAMPLE_TILE, 0))
    out_tile = pl.BlockSpec((TR, A_WIDTH), lambda b: (0, 0))
    near = pl.BlockSpec((None, NEAR_POS * N_HEADS, D_HEAD), lambda b: (b, W_MAX // NEAR_POS - 1, 0))
    far = pl.BlockSpec((None, FAR_STEPS, SUBLANES, D_HEAD), lambda b: (b, 0, 0, 0))
    rows = (DEC_BATCH, W_MAX * N_HEADS, D_HEAD)
    groups = (DEC_BATCH, W_MAX // 16, 16 * N_HEADS, D_HEAD)
    eye = jnp.eye(D_HEAD, dtype=F32)

    return pl.pallas_call(
        _sample_kernel,
        grid=(DEC_BATCH,),
        in_specs=[tile, tile, tile, tile, tile, tile, tile, near, far, near, far,
                  pl.BlockSpec((None, N_HEADS, D_HEAD, D_HEAD), lambda b: (b, 0, 0, 0)),
                  pl.BlockSpec(bias_s.shape, lambda b: (0, 0, 0, 0)),
                  pl.BlockSpec((1, A_WIDTH), lambda b: (0, 0)),
                  pl.BlockSpec((1, D_HEAD), lambda b: (0, 0)),
                  pl.BlockSpec((D_HEAD, D_HEAD), lambda b: (0, 0))],
        out_specs=[out_tile, out_tile,
                   pl.BlockSpec((None, N_HEADS, D_HEAD, D_HEAD), lambda b: (b, 0, 0, 0))],
        out_shape=[jax.ShapeDtypeStruct((TR, A_WIDTH), BF16), jax.ShapeDtypeStruct((TR, A_WIDTH), BF16),
                   jax.ShapeDtypeStruct((DEC_BATCH, N_HEADS, D_HEAD, D_HEAD), F32)],
        scratch_shapes=[pltpu.VMEM((DEC_BATCH, A_WIDTH), F32)] * 2,
        compiler_params=_cparams(("arbitrary",)),
        name="sample_mixers",
    )(q, k, v, hq, hf, hi, hg, cache_k.reshape(rows), cache_k.reshape(groups),
      cache_v.reshape(rows), cache_v.reshape(groups), state, bias_s, lb, gnorm, eye)


def _store_rows_as_tiles(ref, val):
    n = val.shape[0]
    for j in range(ROW_VREGS):
        ref[pl.ds(j, n, stride=ROW_VREGS), :] = val[:, j * LANES:(j + 1) * LANES]


def _load_rows_from_tiles(ref, n):
    return jnp.concatenate([ref[pl.ds(j, n, stride=ROW_VREGS), :] for j in range(ROW_VREGS)], -1)


def _route(h2, wr_ref, br_ref, tri_ref, cnt_ref, meta_ref, gate_ref, cnt_out_ref):
    i = pl.program_id(0)

    @pl.when(i == 0)
    def _():
        cnt_ref[...] = jnp.zeros_like(cnt_ref)

    logits = jnp.dot(h2.astype(BF16), wr_ref[...], preferred_element_type=F32) + br_ref[...]
    lane = lax.broadcasted_iota(I32, (TR, LANES), 1).astype(F32)
    big = float(1 << 20)
    is_g = lane < N_GROUPS
    gl = jnp.where(is_g, logits, NEG)
    gmax = jnp.max(gl, -1, keepdims=True)
    gsel = jnp.min(jnp.where(gl == gmax, lane, big), -1, keepdims=True)
    pg = 1.0 / jnp.sum(jnp.where(is_g, jnp.exp(gl - gmax), 0.0), -1, keepdims=True)
    lo = N_GROUPS + EXPERTS_PER_GROUP * gsel
    in_grp = jnp.logical_and(lane >= lo, lane < lo + EXPERTS_PER_GROUP)
    el = jnp.where(in_grp, logits, NEG)
    m1 = jnp.max(el, -1, keepdims=True)
    i1 = jnp.min(jnp.where(el == m1, lane, big), -1, keepdims=True)
    el2 = jnp.where(lane == i1, NEG, el)
    m2 = jnp.max(el2, -1, keepdims=True)
    i2 = jnp.min(jnp.where(el2 == m2, lane, big), -1, keepdims=True)
    r = jnp.exp(m2 - m1)
    g1 = pg / (1.0 + r)
    g2 = pg * r / (1.0 + r)
    e1 = i1 - N_GROUPS
    e2 = i2 - N_GROUPS

    tok = i * TR + lax.broadcasted_iota(I32, (TR, 1), 0)
    valid = tok < NV
    oh1 = jnp.logical_and(lane == e1, valid)
    oh2 = jnp.logical_and(lane == e2, valid)
    oh = jnp.where(jnp.logical_or(oh1, oh2), 1.0, 0.0)
    before = jnp.dot(tri_ref[...], oh.astype(BF16), preferred_element_type=F32) + cnt_ref[...]
    rank1 = jnp.sum(jnp.where(oh1, before, 0.0), -1, keepdims=True)
    rank2 = jnp.sum(jnp.where(oh2, before, 0.0), -1, keepdims=True)
    tile_cnt = jnp.sum(oh, 0, keepdims=True)
    cnt_ref[...] = cnt_ref[...] + tile_cnt

    eye = jnp.where(lax.broadcasted_iota(I32, (TR, TR), 0) == lax.broadcasted_iota(I32, (TR, TR), 1), 1.0, 0.0)
    rows = [jnp.sum(eye * col, 0, keepdims=True) for col in (e1, e2, rank1, rank2)]
    meta_ref[0] = jnp.concatenate(rows + [jnp.zeros((4, TR), F32)], 0).astype(I32)
    gates = jnp.where(lane == 0, g1, jnp.where(lane == 1, g2, 0.0))
    gates = jnp.where(valid, gates, 0.0)
    gate_ref[...] = gates[:, 0:8]
    cnt_out_ref[0] = tile_cnt


def _router_operands(wg_group, bg_group, wg_exp, bg_exp):
    wr = jnp.zeros((D_MODEL, LANES), F32)
    wr = wr.at[:, 0:N_GROUPS].set(wg_group.astype(F32)).at[:, N_GROUPS:N_GROUPS + N_EXPERTS].set(wg_exp.astype(F32))
    br = jnp.zeros((1, LANES), F32)
    br = br.at[0, 0:N_GROUPS].set(bg_group.astype(F32)).at[0, N_GROUPS:N_GROUPS + N_EXPERTS].set(bg_exp.astype(F32))
    tri = jnp.asarray(np.tril(np.ones((TR, TR), np.float32), -1), BF16)
    return wr.astype(BF16), br, tri


_ROUTE_OUT_SPECS = [pl.BlockSpec((1, 8, TR), lambda i: (i, 0, 0)),
                    pl.BlockSpec((TR, 8), lambda i: (i, 0)),
                    pl.BlockSpec((1, 1, LANES), lambda i: (i, 0, 0))]
_ROUTE_OUT_SHAPES = [jax.ShapeDtypeStruct((N_TILES, 8, TR), I32), jax.ShapeDtypeStruct((NT, 8), F32),
                     jax.ShapeDtypeStruct((N_TILES, 1, LANES), F32)]


def _const_spec(a):
    return pl.BlockSpec(a.shape, lambda i, n=a.ndim: (0,) * n)


def _outproj_kernel(xp_ref, xs_ref, oa_ref, oas_ref, ob_ref, obs_ref, w_ref, g_ref, wr_ref, br_ref, tri_ref,
                    x1_ref, meta_ref, gate_ref, cnt_out_ref, cnt_ref):
    i = pl.program_id(0)
    x = _token_tile(i, xp_ref, xs_ref)
    mixed = jnp.concatenate([_token_tile(i, oa_ref, oas_ref), _token_tile(i, ob_ref, obs_ref)], -1)
    x1 = x + jnp.dot(mixed, w_ref[...], preferred_element_type=F32)
    x1_ref[...] = x1
    h2 = _rms(x1, g_ref[...])
    _route(h2, wr_ref, br_ref, tri_ref, cnt_ref, meta_ref, gate_ref, cnt_out_ref)


def _outproj_route(xp, xs_pad, oa, oa_s, ob, ob_s, w_out_bf16, g_ffn, wr, br, tri):
    row = pl.BlockSpec((TR, D_MODEL), lambda i: (i, 0))
    half = pl.BlockSpec((TR, A_WIDTH), lambda i: (jnp.minimum(i, SAMPLE_TILE - 1), 0))
    half_s = pl.BlockSpec((TR, A_WIDTH), lambda i: (0, 0))
    return pl.pallas_call(
        _outproj_kernel,
        grid=(N_TILES,),
        in_specs=[_PROMPT_ROWS, _SAMPLE_ROWS, half, half_s, half, half_s,
                  _const_spec(w_out_bf16), _const_spec(g_ffn),
                  _const_spec(wr), _const_spec(br), _const_spec(tri)],
        out_specs=[row] + _ROUTE_OUT_SPECS,
        out_shape=[jax.ShapeDtypeStruct((NT, D_MODEL), F32)] + _ROUTE_OUT_SHAPES,
        scratch_shapes=[pltpu.VMEM((1, LANES), F32)],
        compiler_params=_cparams(("arbitrary",)),
        name="outproj_route",
    )(xp, xs_pad, oa, oa_s, ob, ob_s, w_out_bf16, g_ffn, wr, br, tri)


def _plan(meta, tile_counts):
    tcnt = tile_counts[:, 0, :N_EXPERTS].astype(I32)
    cnt = jnp.sum(tcnt, 0)
    nblk = (cnt + TM - 1) // TM
    blk_end = jnp.cumsum(nblk)
    pstart = (blk_end - nblk) * TM
    n_used = blk_end[-1]
    blk = jnp.minimum(jnp.arange(N_SLOTS_BLK, dtype=I32), n_used - 1)
    blk_exp = jnp.sum((blk[:, None] >= blk_end[None, :]).astype(I32), -1)
    blk_exp = jnp.minimum(blk_exp, N_EXPERTS - 1)
    eid = meta[:, 0:2, :]
    experts = jnp.arange(N_EXPERTS, dtype=I32)[:, None, None, None]
    start_of = jnp.sum(jnp.where(eid[None] == experts, pstart[:, None, None, None], 0), 0)
    tok = (jnp.arange(N_TILES, dtype=I32) * TR)[:, None, None] + jnp.arange(TR, dtype=I32)
    dest = jnp.where(tok < NV, start_of + meta[:, 2:4, :], 0).astype(I32).reshape(-1)
    pad_lo = pstart + cnt
    pad_hi = pstart + nblk * TM
    return dest, blk_exp, n_used.reshape(1).astype(I32), pad_lo.astype(I32), pad_hi.astype(I32)


def _row(ref, r):
    return ref.at[pl.ds(pl.multiple_of(r * ROW_VREGS, ROW_VREGS), ROW_VREGS), :]


def _dest_index(tile, kk, t):
    return (2 * tile + kk) * TR + t


def _dispatch_kernel(dest_ref, plo_ref, phi_ref, nused_ref, x_ref, g_ref, xs_ref, stage, zero_ref, sems, zsem):
    i = pl.program_id(0)
    last = pl.num_programs(0) - 1
    slot = i % 2
    _store_rows_as_tiles(stage.at[slot], _rms(x_ref[...], g_ref[...]))

    def wait_tile(n_tok, s):
        for _ in range(2):
            pltpu.make_async_copy(stage.at[s, pl.ds(0, n_tok * ROW_VREGS), :],
                                  xs_ref.at[pl.ds(0, n_tok * ROW_VREGS), :], sems.at[s]).wait()

    def scatter(n_tok):
        def issue(t, carry):
            src = stage.at[slot, pl.ds(pl.multiple_of(t * ROW_VREGS, ROW_VREGS), ROW_VREGS), :]
            for kk in range(2):
                pltpu.make_async_copy(src, _row(xs_ref, dest_ref[_dest_index(i, kk, t)]),
                                      sems.at[slot]).start(priority=kk)
            return carry
        lax.fori_loop(0, n_tok, issue, 0)

    @pl.when(i < last)
    def _():
        scatter(TR)

    @pl.when(i > 0)
    def _():
        wait_tile(TR, 1 - slot)

    @pl.when(i == last)
    def _():
        n_last = NV - (N_TILES - 1) * TR
        scatter(n_last)
        wait_tile(n_last, slot)
        zero_ref[...] = jnp.zeros_like(zero_ref)

        def zero_copy(first_row, n_rows):
            dst = xs_ref.at[pl.ds(pl.multiple_of(first_row * ROW_VREGS, ROW_VREGS), n_rows * ROW_VREGS), :]
            return pltpu.make_async_copy(zero_ref.at[pl.ds(0, n_rows * ROW_VREGS), :], dst, zsem)

        def per_expert(e, carry):
            row, left = plo_ref[e], phi_ref[e] - plo_ref[e]
            for size in PAD_FILL_SIZES:
                shift = size.bit_length() - 1
                cnt = lax.shift_right_logical(left, shift)

                def fill(j, c2, row=row, size=size):
                    zero_copy(row + j * size, size).start()
                    return c2
                lax.fori_loop(0, cnt, fill, 0)

                def fill_wait(j, c2, size=size):
                    zero_copy(0, size).wait()
                    return c2
                lax.fori_loop(0, cnt, fill_wait, 0)
                done = lax.shift_left(cnt, shift)
                row, left = row + done, left - done
            return carry
        lax.fori_loop(0, N_EXPERTS, per_expert, 0)

        def block_of(nb):
            return xs_ref.at[pl.ds(pl.multiple_of(nb * (TM * ROW_VREGS), TM * ROW_VREGS), TM * ROW_VREGS), :]

        def fill_blk(nb, carry):
            pltpu.make_async_copy(zero_ref, block_of(nb), zsem).start()
            return carry
        lax.fori_loop(nused_ref[0], N_SLOTS_BLK, fill_blk, 0)

        def fill_blk_wait(nb, carry):
            pltpu.make_async_copy(zero_ref, block_of(0), zsem).wait()
            return carry
        lax.fori_loop(nused_ref[0], N_SLOTS_BLK, fill_blk_wait, 0)


def _dispatch(x, g_ffn, dest, pad_lo, pad_hi, n_used):
    return pl.pallas_call(
        _dispatch_kernel,
        grid_spec=pltpu.PrefetchScalarGridSpec(
            num_scalar_prefetch=4, grid=(N_TILES,),
            in_specs=[pl.BlockSpec((TR, D_MODEL), lambda i, *_: (i, 0)),
                      pl.BlockSpec((1, D_MODEL), lambda i, *_: (0, 0))],
            out_specs=pl.BlockSpec(memory_space=pl.ANY),
            scratch_shapes=[pltpu.VMEM((2, TR * ROW_VREGS, LANES), F32),
                            pltpu.VMEM((TM * ROW_VREGS, LANES), F32),
                            pltpu.SemaphoreType.DMA((2,)), pltpu.SemaphoreType.DMA(())]),
        out_shape=jax.ShapeDtypeStruct((N_SLOTS * ROW_VREGS, LANES), F32),
        compiler_params=_cparams(("arbitrary",)),
        name="dispatch",
    )(dest, pad_lo, pad_hi, n_used, x, g_ffn)


class _ExpertRows:
    scratch = [pltpu.VMEM((2, 2 * TR * ROW_VREGS, LANES), F32), pltpu.SemaphoreType.DMA((2,))]

    def __init__(self, dest_ref, ys_ref, buf, sems):
        self.dest_ref, self.ys_ref, self.buf, self.sems = dest_ref, ys_ref, buf, sems

    def _copy(self, src_row, slot, dst_row):
        dst = self.buf.at[slot, pl.ds(pl.multiple_of(dst_row * ROW_VREGS, ROW_VREGS), ROW_VREGS), :]
        return pltpu.make_async_copy(_row(self.ys_ref, src_row), dst, self.sems.at[slot])

    def start(self, tile):
        slot = tile % 2

        def issue(g, carry):
            for un in range(ROW_DMA_UNROLL):
                t = g * ROW_DMA_UNROLL + un
                for kk in range(2):
                    self._copy(self.dest_ref[_dest_index(tile, kk, t)], slot, kk * TR + t).start(priority=kk)
            return carry
        lax.fori_loop(0, TR // ROW_DMA_UNROLL, issue, 0)

    def wait(self, tile):
        slot = tile % 2
        pltpu.make_async_copy(self.ys_ref.at[pl.ds(0, 2 * TR * ROW_VREGS), :], self.buf.at[slot],
                              self.sems.at[slot]).wait()

    def combine(self, tile, x, gate_ref):
        rows = self.buf.at[tile % 2]
        g = gate_ref[...]
        for kk in range(2):
            y = jnp.concatenate([rows[pl.ds(kk * TR * ROW_VREGS + j, TR, stride=ROW_VREGS), :]
                                 for j in range(ROW_VREGS)], -1)
            x = x + g[:, kk:kk + 1] * y
        return x

    def fetch_combine(self, x, gate_ref):
        i = pl.program_id(0)

        @pl.when(i == 0)
        def _():
            self.start(i)

        @pl.when(i + 1 < pl.num_programs(0))
        def _():
            self.start(i + 1)

        self.wait(i)
        return self.combine(i, x, gate_ref)


def _ffn_kernel(be_ref, nused_ref, xs_ref, w1_ref, w3_ref, w2_ref, y_ref, w1b, w3b, w2b):
    nb = pl.program_id(0)

    @pl.when(nb < nused_ref[0])
    def _():
        prev = be_ref[jnp.maximum(nb - 1, 0)]
        fresh = jnp.logical_or(nb == 0, be_ref[nb] != prev)

        @pl.when(fresh)
        def _():
            w1b[...] = w1_ref[...].astype(BF16)
            w3b[...] = w3_ref[...].astype(BF16)
            w2b[...] = w2_ref[...].astype(BF16)

        x = _load_rows_from_tiles(xs_ref, TM).astype(BF16)
        a = jnp.dot(x, w1b[...], preferred_element_type=F32)
        b = jnp.dot(x, w3b[...], preferred_element_type=F32)
        mid = (_silu(a) * b).astype(BF16)
        y = jnp.dot(mid, w2b[...], preferred_element_type=F32)
        _store_rows_as_tiles(y_ref, y)

    @pl.when(nb >= nused_ref[0])
    def _():
        y_ref[...] = jnp.zeros_like(y_ref)


def _experts(xs, blk_exp, n_used, layer, w1, w3, w2):
    rows = pl.BlockSpec((TM * ROW_VREGS, LANES), lambda nb, be, nu: (nb, 0))
    rows_in = rows

    def wspec(a, b):
        return pl.BlockSpec((None, None, a, b), lambda nb, be, nu: (layer, be[nb], 0, 0))

    return pl.pallas_call(
        _ffn_kernel,
        grid_spec=pltpu.PrefetchScalarGridSpec(
            num_scalar_prefetch=2, grid=(N_SLOTS_BLK,),
            in_specs=[rows_in, wspec(D_MODEL, D_EXPERT), wspec(D_MODEL, D_EXPERT), wspec(D_EXPERT, D_MODEL)],
            out_specs=rows,
            scratch_shapes=[pltpu.VMEM((D_MODEL, D_EXPERT), BF16), pltpu.VMEM((D_MODEL, D_EXPERT), BF16),
                            pltpu.VMEM((D_EXPERT, D_MODEL), BF16)]),
        out_shape=jax.ShapeDtypeStruct((N_SLOTS * ROW_VREGS, LANES), F32),
        compiler_params=_cparams(("arbitrary",)),
        name="experts",
    )(blk_exp, n_used, xs, w1, w3, w2)


def _moe(x, g_ffn, meta, counts, layer, w1, w3, w2):
    dest, blk_exp, n_used, pad_lo, pad_hi = _plan(meta, counts)
    xs = _dispatch(x, g_ffn, dest, pad_lo, pad_hi, n_used)
    return _experts(xs, blk_exp, n_used, layer, w1, w3, w2), dest


def _pool_kernel(dest_ref, x1_ref, ys_ref, gate0_ref, gmix_ref, sp_ref, pw_ref, ps_ref, gffn_ref,
                 wr_ref, br_ref, tri_ref,
                 x3_ref, hkeep_ref, hs_ref, meta_ref, gate_ref, cnt_out_ref,
                 cnt_ref, ext_ref, e1_ref, e2_ref, e3_ref, e4_ref, mixed_ref, ybuf, ysems):
    i = pl.program_id(0)
    x2 = _ExpertRows(dest_ref, ys_ref, ybuf, ysems).fetch_combine(x1_ref[...], gate0_ref)
    h = _rms(x2, gmix_ref[...])
    H = POOL_HALO
    lvl_refs = (e1_ref, e2_ref, e3_ref, e4_ref)

    @pl.when(i < SAMPLE_TILE)
    def _():
        hkeep_ref[...] = h[TR - 16:TR]
        @pl.when(i % TILES_PER_SEQ == 0)
        def _():
            ext_ref[0:H, :] = jnp.zeros((H, D_MODEL), F32)
        ext_ref[H:H + TR, :] = h
        src = ext_ref
        for lv in range(4):
            sh = 1 << lv
            lo = 8 * (lv + 1)
            c0 = POOL_GROUP * lv
            dst = lvl_refs[lv]
            dst[lo:H + TR, c0:] = src[lo:H + TR, c0:] + src[lo - sh:H + TR - sh, c0:]
            src = dst
        pos = (i % TILES_PER_SEQ) * TR + lax.broadcasted_iota(I32, (TR, 1), 0)
        for gi, w in enumerate(POOL_WINDOWS):
            cs = slice(gi * POOL_GROUP, (gi + 1) * POOL_GROUP)
            inv = 1.0 / jnp.minimum(w, pos + 1).astype(F32)
            mixed_ref[:, cs] = lvl_refs[gi][H:H + TR, cs] * inv - h[:, cs]
        ext_ref[0:H, :] = h[TR - H:TR]

    @pl.when(i == SAMPLE_TILE)
    def _():
        hs = h[0:DEC_BATCH]
        hs_ref[...] = hs
        mixed_ref[...] = jnp.zeros_like(mixed_ref)
        for gi, w in enumerate(POOL_WINDOWS):
            cs = slice(gi * POOL_GROUP, (gi + 1) * POOL_GROUP)
            win = hs[:, cs]
            for dlt in range(1, w):
                win = win + sp_ref[POOL_KEEP - dlt][:, cs]
            mixed_ref[0:DEC_BATCH, cs] = win * (1.0 / w) - hs[:, cs]

    mixed = mixed_ref[...].astype(BF16)
    outs = [jnp.dot(mixed[:, gi * POOL_GROUP:(gi + 1) * POOL_GROUP], pw_ref[gi], preferred_element_type=F32)
            for gi in range(len(POOL_WINDOWS))]
    x3 = x2 + jnp.concatenate(outs, -1) * ps_ref[...]
    x3_ref[...] = x3
    h2 = _rms(x3, gffn_ref[...])
    _route(h2, wr_ref, br_ref, tri_ref, cnt_ref, meta_ref, gate_ref, cnt_out_ref)


def _const_spec_p(a):
    return pl.BlockSpec(a.shape, lambda i, *_, n=a.ndim: (0,) * n)


def _pool_route(dest, x1, ys, gates0, g_mix, sp_t, pw_bf16, p_scale, g_ffn, wr, br, tri):
    def rows(shape):
        return pl.BlockSpec(shape, lambda i, *_: (i, 0))

    ext = pltpu.VMEM((POOL_HALO + TR, D_MODEL), F32)
    consts = [g_mix, sp_t, pw_bf16, p_scale, g_ffn, wr, br, tri]
    return pl.pallas_call(
        _pool_kernel,
        grid_spec=pltpu.PrefetchScalarGridSpec(
            num_scalar_prefetch=1, grid=(N_TILES,),
            in_specs=[rows((TR, D_MODEL)), pl.BlockSpec(memory_space=pl.ANY), rows((TR, 8))]
                     + [_const_spec_p(a) for a in consts],
            out_specs=[rows((TR, D_MODEL)),
                       pl.BlockSpec((None, 16, D_MODEL),
                                    lambda i, *_: (jnp.minimum(i // TILES_PER_SEQ, BATCH - 1), 0, 0)),
                       pl.BlockSpec((DEC_BATCH, D_MODEL), lambda i, *_: (0, 0)),
                       pl.BlockSpec((1, 8, TR), lambda i, *_: (i, 0, 0)), rows((TR, 8)),
                       pl.BlockSpec((1, 1, LANES), lambda i, *_: (i, 0, 0))],
            scratch_shapes=[pltpu.VMEM((1, LANES), F32), ext, ext, ext, ext, ext,
                            pltpu.VMEM((TR, D_MODEL), F32)] + _ExpertRows.scratch),
        out_shape=[jax.ShapeDtypeStruct((NT, D_MODEL), F32),
                   jax.ShapeDtypeStruct((BATCH, 16, D_MODEL), F32),
                   jax.ShapeDtypeStruct((DEC_BATCH, D_MODEL), F32)] + _ROUTE_OUT_SHAPES,
        compiler_params=_cparams(("arbitrary",)),
        name="pool_route",
    )(dest, x1, ys, gates0, *consts)


def _final_kernel(dest_ref, x_ref, ys_ref, gate_ref, g_ref, yp_ref, ysm_ref, ybuf, ysems):
    i = pl.program_id(0)
    y = _rms(_ExpertRows(dest_ref, ys_ref, ybuf, ysems).fetch_combine(x_ref[...], gate_ref), g_ref[...])

    @pl.when(i < SAMPLE_TILE)
    def _():
        yp_ref[...] = y

    @pl.when(i == SAMPLE_TILE)
    def _():
        ysm_ref[...] = y[0:DEC_BATCH]


def _final(dest, x3, ys, gates, g_final):
    return pl.pallas_call(
        _final_kernel,
        grid_spec=pltpu.PrefetchScalarGridSpec(
            num_scalar_prefetch=1, grid=(N_TILES,),
            in_specs=[pl.BlockSpec((TR, D_MODEL), lambda i, *_: (i, 0)), pl.BlockSpec(memory_space=pl.ANY),
                      pl.BlockSpec((TR, 8), lambda i, *_: (i, 0)), _const_spec_p(g_final)],
            out_specs=[pl.BlockSpec((TR, D_MODEL), lambda i, *_: (jnp.minimum(i, SAMPLE_TILE - 1), 0)),
                       pl.BlockSpec((DEC_BATCH, D_MODEL), lambda i, *_: (0, 0))],
            scratch_shapes=_ExpertRows.scratch),
        out_shape=[jax.ShapeDtypeStruct((NP, D_MODEL), F32), jax.ShapeDtypeStruct((DEC_BATCH, D_MODEL), F32)],
        compiler_params=_cparams(("arbitrary",)),
        name="final_norm",
    )(dest, x3, ys, gates, g_final)


def kernel(x_prompt, x_sample, cache_win_k, cache_win_v, state_hgrn, state_pool, rel_bias, norm_mix, norm_ffn,
           norm_final, w_in, w_out, hgrn_lb, hgrn_gnorm, pool_w, pool_scale, moe_wg_group, moe_bg_group,
           moe_wg_exp, moe_bg_exp, moe_w1, moe_w3, moe_w2):
    xp = x_prompt.reshape(NP, D_MODEL)
    xs_pad = jnp.pad(x_sample.reshape(DEC_BATCH, D_MODEL), ((0, TR - DEC_BATCH), (0, 0)))
    lb = jnp.cumsum(jax.nn.softmax(hgrn_lb.astype(F32), axis=0), axis=0)[0:1]
    gnorm = hgrn_gnorm[0:1].astype(F32)

    q, k, v, hq, hf, hi, hg, k_win, v_win = _inproj(xp, xs_pad, norm_mix[0:1], w_in[0].astype(BF16))
    oa = _attention_prompt(q, k, v, _attn_bias_tables(rel_bias))
    ob, s_prompt = _hgrn_prompt(hq, hf, hi, hg, lb, gnorm)
    oa_s, ob_s, s_sample = _sample_mixers(
        q, k, v, hq, hf, hi, hg,
        cache_win_k[0], cache_win_v[0], state_hgrn[0], _sample_bias_tables(rel_bias), lb, gnorm)
    wr, br, tri = _router_operands(moe_wg_group[0], moe_bg_group[0], moe_wg_exp[0], moe_bg_exp[0])
    x1, meta, gates0, counts = _outproj_route(xp, xs_pad, oa, oa_s, ob, ob_s, w_out[0].astype(BF16),
                                              norm_ffn[0:1], wr, br, tri)
    ys, dest = _moe(x1, norm_ffn[0:1], meta, counts, 0, moe_w1, moe_w3, moe_w2)

    wr, br, tri = _router_operands(moe_wg_group[1], moe_bg_group[1], moe_wg_exp[1], moe_bg_exp[1])
    sp_t = jnp.transpose(state_pool[0], (1, 0, 2))
    x3, hkeep, hsample, meta, gates1, counts = _pool_route(
        dest, x1, ys, gates0, norm_mix[1:2], sp_t, pool_w[0].astype(BF16), pool_scale[0:1], norm_ffn[1:2],
        wr, br, tri)
    ys, dest = _moe(x3, norm_ffn[1:2], meta, counts, 1, moe_w1, moe_w3, moe_w2)
    y_prompt, y_sample = _final(dest, x3, ys, gates1, norm_final.reshape(1, D_MODEL))

    k_s = k[NP:NV].reshape(1, DEC_BATCH, 1, N_HEADS, D_HEAD)
    v_s = v[NP:NV].reshape(1, DEC_BATCH, 1, N_HEADS, D_HEAD)
    return (y_prompt.reshape(BATCH, SEQ, D_MODEL), y_sample.reshape(DEC_BATCH, 1, D_MODEL),
            k_win.reshape(1, BATCH, W_MAX, N_HEADS, D_HEAD), v_win.reshape(1, BATCH, W_MAX, N_HEADS, D_HEAD),
            s_prompt[None], hkeep[:, 16 - POOL_KEEP:][None],
            k_s, v_s, s_sample[None], hsample.reshape(1, DEC_BATCH, 1, D_MODEL))
```

```python
import functools

import numpy as np
import jax
import jax.numpy as jnp
from jax import lax
from jax.experimental import pallas as pl
from jax.experimental.pallas import tpu as pltpu

F32 = jnp.float32
BF16 = jnp.bfloat16
I32 = jnp.int32

D_MODEL = 1024
BATCH = 4
SEQ = 4096
DEC_BATCH = 32
PAST_LEN = 8192
W_MAX = 2048
N_HEADS = 4
D_HEAD = 128
A_WIDTH = N_HEADS * D_HEAD
N_PROJ = 7 * A_WIDTH
DILATED = ((128, 1), (512, 4), (2048, 16))
NUM_BUCKETS = 32
MAX_DISTANCE = 2048
POOL_WINDOWS = (2, 4, 8, 16)
POOL_GROUP = 256
POOL_KEEP = 15
N_GROUPS = 4
EXPERTS_PER_GROUP = 8
N_EXPERTS = 32
D_EXPERT = 512
EPS = 1e-6
NEG = -1e30

LANES = 128
SUBLANES = 8
ROW_VREGS = D_MODEL // LANES
TR = 256
NP = BATCH * SEQ
NV = NP + DEC_BATCH
NT = NP + TR
N_TILES = NT // TR
SAMPLE_TILE = NP // TR
TILES_PER_SEQ = SEQ // TR
TM = 256
N_SLOTS_BLK = (2 * NV + N_EXPERTS * (TM - 1) + TM - 1) // TM
N_SLOTS = N_SLOTS_BLK * TM
QB = 128
ATTN_UNROLL = 16
ATTN_PERIOD = 3 * QB
ROW_DMA_UNROLL = 8
CH = 128
HGRN_HEADS_PER_STEP = 2
N_LEVELS = 7
POOL_HALO = 32
VMEM_LIMIT = 56 * 1024 * 1024


def _cparams(sem=None, vmem=VMEM_LIMIT):
    kw = dict(vmem_limit_bytes=vmem)
    if sem is not None:
        kw["dimension_semantics"] = sem
    return pltpu.CompilerParams(**kw)


def _rms(x, g):
    return x * lax.rsqrt(jnp.mean(x * x, -1, keepdims=True) + EPS) * g


def _sigmoid(x):
    return 1.0 / (1.0 + jnp.exp(-x))


def _silu(x):
    return x * _sigmoid(x)


def _token_tile(i, xp_ref, xs_ref):
    return jnp.where(i == SAMPLE_TILE, xs_ref[...], xp_ref[...])


_PROMPT_ROWS = pl.BlockSpec((TR, D_MODEL), lambda i: (jnp.minimum(i, SAMPLE_TILE - 1), 0))
_SAMPLE_ROWS = pl.BlockSpec((TR, D_MODEL), lambda i: (0, 0))
WIN_TILES = W_MAX // TR


def _window_block(i):
    seq = jnp.minimum(i // TILES_PER_SEQ, BATCH - 1)
    j = jnp.clip(i % TILES_PER_SEQ - (TILES_PER_SEQ - WIN_TILES), 0, WIN_TILES - 1)
    return jnp.where(i >= SAMPLE_TILE, BATCH * WIN_TILES - 1, seq * WIN_TILES + j)


def _inproj_kernel(xp_ref, xs_ref, g_ref, w_ref, *out_refs):
    i = pl.program_id(0)
    h = _rms(_token_tile(i, xp_ref, xs_ref), g_ref[...])
    p = jnp.dot(h.astype(BF16), w_ref[...], preferred_element_type=F32)
    for n, o_ref in enumerate(out_refs[:7]):
        o_ref[...] = p[:, n * A_WIDTH:(n + 1) * A_WIDTH]

    @pl.when(jnp.logical_and(i < SAMPLE_TILE, i % TILES_PER_SEQ >= TILES_PER_SEQ - WIN_TILES))
    def _():
        for n, o_ref in ((1, out_refs[7]), (2, out_refs[8])):
            for h_i in range(N_HEADS):
                c0 = n * A_WIDTH + h_i * D_HEAD
                o_ref[pl.ds(h_i, TR, stride=N_HEADS), :] = p[:, c0:c0 + D_HEAD]


def _inproj(xp, xs_pad, g, w_bf16):
    out_sds = ([jax.ShapeDtypeStruct((NT, A_WIDTH), F32)] * 7
               + [jax.ShapeDtypeStruct((BATCH * W_MAX * N_HEADS, D_HEAD), F32)] * 2)
    win = pl.BlockSpec((TR * N_HEADS, D_HEAD), lambda i: (_window_block(i), 0))
    return pl.pallas_call(
        _inproj_kernel,
        grid=(N_TILES,),
        in_specs=[_PROMPT_ROWS, _SAMPLE_ROWS,
                  pl.BlockSpec((1, D_MODEL), lambda i: (0, 0)),
                  pl.BlockSpec((D_MODEL, N_PROJ), lambda i: (0, 0))],
        out_specs=[pl.BlockSpec((TR, A_WIDTH), lambda i: (i, 0))] * 7 + [win, win],
        out_shape=out_sds,
        compiler_params=_cparams(("arbitrary",)),
        name="inproj",
    )(xp, xs_pad, g, w_bf16)


def _t5_bucket(dist):
    max_exact = NUM_BUCKETS // 2
    d = np.asarray(dist)
    large = max_exact + np.floor(np.log(np.maximum(d, 1) / max_exact)
                                 / np.log(MAX_DISTANCE / max_exact) * (NUM_BUCKETS - max_exact)).astype(np.int32)
    large = np.minimum(large, NUM_BUCKETS - 1)
    return np.where(d < max_exact, d, large).astype(np.int32)


def _attn_bias_tables(rel_bias):
    period = ATTN_PERIOD
    m = np.arange(period)
    u = np.where(m < 2 * QB, m, m - period)
    pick = np.zeros((len(DILATED), 2, period, NUM_BUCKETS), np.float32)
    mask = np.zeros((len(DILATED), 2, period, 1), np.float32)
    for bi, (w, d) in enumerate(DILATED):
        nk = w // d
        for vi, off in enumerate((0, QB)):
            j = off - u
            ok = (j >= 0) & (j <= nk)
            pick[bi, vi, m[ok], _t5_bucket(d * j[ok])] = 1.0
            mask[bi, vi, ~ok, 0] = NEG
    vec = jnp.einsum("bvmk,kh->hbvm", pick, rel_bias.astype(F32), precision=lax.Precision.HIGHEST)
    vec = vec + jnp.transpose(mask, (3, 0, 1, 2))
    return vec[:, :, :, None, :]


def _attn_kernel(q_ref, k_ref, v_ref, vec_ref, o_ref,
                 q4, k4, v4, qd, kd, vd, ud, md, sd, u_acc, m_acc, s_acc, bias_ref):
    scale = D_HEAD ** -0.5
    c4 = SEQ // 4
    c16 = SEQ // 16
    for bi in range(len(DILATED)):
        for vi in range(2):
            rows = jnp.broadcast_to(vec_ref[bi, vi], (QB, ATTN_PERIOD))
            bias_ref[bi, vi] = pltpu.roll(rows, 0, 1, stride=1, stride_axis=0)[:, :2 * QB]

    def block_stats(bi, t, nb):
        has_prev = jnp.minimum(t % nb, 1)
        q0 = pl.multiple_of(t * QB, QB)
        k0 = pl.multiple_of((t - has_prev) * QB, QB)
        qb = qd[pl.ds(q0, QB), :]
        kb = kd[pl.ds(k0, 2 * QB), :]
        vb = vd[pl.ds(k0, 2 * QB), :]
        s = lax.dot_general(qb, kb, (((1,), (1,)), ((), ())), preferred_element_type=F32)
        s = s + bias_ref[bi, has_prev]
        mb = jnp.max(s, -1, keepdims=True)
        p = jnp.exp(s - mb)
        sb = jnp.sum(p, -1, keepdims=True)
        u = jnp.dot(p.astype(BF16), vb, preferred_element_type=F32)
        return q0, mb, sb, u

    def run_blocks(bi, nb, consume):
        def body(g, carry):
            for un in range(ATTN_UNROLL):
                q0, mb, sb, u = block_stats(bi, g * ATTN_UNROLL + un, nb)
                consume(pl.ds(q0, QB), mb, sb, u)
            return carry
        lax.fori_loop(0, SEQ // QB // ATTN_UNROLL, body, 0)

    def merged(rows, m_ref, s_ref, u_ref, mb, sb, u):
        m_old = m_ref[rows, :]
        m_new = jnp.maximum(m_old, mb)
        a = jnp.exp(m_old - m_new)
        b = jnp.exp(mb - m_new)
        return m_new, a * s_ref[rows, :] + b * sb, a * u_ref[rows, :] + b * u

    for r in range(4):
        src, dst = pl.ds(r, c4, stride=4), pl.ds(r * c4, c4)
        q4[dst, :] = q_ref[src, :] * scale
        k4[dst, :] = k_ref[src, :]
        v4[dst, :] = v_ref[src, :]

    for r in range(4):
        for j in range(4):
            src, dst = pl.ds(r * c4 + j, c16, stride=4), pl.ds((r + 4 * j) * c16, c16)
            qd[dst, :] = q4[src, :].astype(BF16)
            kd[dst, :] = k4[src, :].astype(BF16)
            vd[dst, :] = v4[src, :].astype(BF16)

    def keep16(rows, mb, sb, u):
        ud[rows, :] = u
        md[rows, :] = jnp.broadcast_to(mb, (QB, D_HEAD))
        sd[rows, :] = jnp.broadcast_to(sb, (QB, D_HEAD))
    run_blocks(2, c16 // QB, keep16)
    for r in range(4):
        for j in range(4):
            src, dst = pl.ds((r + 4 * j) * c16, c16), pl.ds(r * c4 + j, c16, stride=4)
            u_acc[dst, :] = ud[src, :]
            m_acc[dst, :] = md[src, :]
            s_acc[dst, :] = sd[src, :]

    qd[...] = q4[...].astype(BF16)
    kd[...] = k4[...].astype(BF16)
    vd[...] = v4[...].astype(BF16)

    def merge4(rows, mb, sb, u):
        m_acc[rows, :], s_acc[rows, :], u_acc[rows, :] = merged(rows, m_acc, s_acc, u_acc, mb, sb, u)
    run_blocks(1, c4 // QB, merge4)

    for r in range(4):
        src, dst = pl.ds(r * c4, c4), pl.ds(r, c4, stride=4)
        ud[dst, :] = u_acc[src, :]
        md[dst, :] = m_acc[src, :]
        sd[dst, :] = s_acc[src, :]
    qd[...] = (q_ref[...] * scale).astype(BF16)
    kd[...] = k_ref[...].astype(BF16)
    vd[...] = v_ref[...].astype(BF16)

    def finish(rows, mb, sb, u):
        _, den, num = merged(rows, md, sd, ud, mb, sb, u)
        o_ref[rows, :] = (num / den).astype(o_ref.dtype)
    run_blocks(0, SEQ // QB, finish)


def _attention_prompt(q, k, v, bias_tabs):
    blk = pl.BlockSpec((SEQ, D_HEAD), lambda b, h: (b, h))
    return pl.pallas_call(
        _attn_kernel,
        grid=(BATCH, N_HEADS),
        in_specs=[blk, blk, blk,
                  pl.BlockSpec((None, 3, 2, 1, ATTN_PERIOD), lambda b, h: (h, 0, 0, 0, 0))],
        out_specs=blk,
        out_shape=jax.ShapeDtypeStruct((NP, A_WIDTH), BF16),
        scratch_shapes=[pltpu.VMEM((SEQ, D_HEAD), F32)] * 3
                       + [pltpu.VMEM((SEQ, D_HEAD), BF16)] * 3
                       + [pltpu.VMEM((SEQ, D_HEAD), F32)] * 6
                       + [pltpu.VMEM((len(DILATED), 2, QB, 2 * QB), F32)],
        compiler_params=_cparams(("arbitrary", "arbitrary")),
        name="attn_prompt",
    )(q, k, v, bias_tabs)


def _hgrn_tables():
    t = np.arange(CH)
    u = np.arange(CH)
    sums_q = np.zeros((1 + N_LEVELS, CH, CH), np.float32)
    sums_k = np.zeros((2 + N_LEVELS, CH, CH), np.float32)
    sums_q[0] = (u[None, :] <= t[:, None])
    sums_k[0] = (u[None, :] > t[:, None])
    sums_k[1 + N_LEVELS] = 1.0
    pair = np.zeros((N_LEVELS, CH, CH), np.float32)
    for l in range(N_LEVELS):
        h = CH >> (l + 1)
        is_q = (t // h) % 2 == 1
        half_start = (t // h) * h
        half_end = half_start + h
        sel_q = (u[None, :] >= half_start[:, None]) & (u[None, :] <= t[:, None])
        sel_k = (u[None, :] > t[:, None]) & (u[None, :] < half_end[:, None])
        sums_q[1 + l] = sel_q & is_q[:, None]
        sums_k[1 + l] = sel_k & ~is_q[:, None]
        same = (t[:, None] // (2 * h)) == (t[None, :] // (2 * h))
        pair[l] = same & is_q[:, None] & (~is_q)[None, :]
    sums_kt = np.transpose(sums_k, (2, 0, 1)).reshape(CH, (2 + N_LEVELS) * CH)
    sums_q = sums_q.reshape((1 + N_LEVELS) * CH, CH)
    return (jnp.asarray(np.concatenate([sums_q, sums_q], 1), BF16),
            jnp.asarray(np.concatenate([sums_kt, sums_kt], 0), BF16),
            jnp.asarray(pair), jnp.asarray(np.eye(CH, dtype=np.float32)))


def _split_bf16(x):
    hi = x.astype(BF16)
    return hi, (x - hi.astype(F32)).astype(BF16)


def _hgrn_kernel(q_ref, f_ref, i_ref, g_ref, lb_ref, gn_ref, sq_ref, skt_ref, pair_ref, eye_ref, o_ref, s_ref):
    gn = gn_ref[...]

    def chunk(c, st, hh):
        rows = pl.ds(pl.multiple_of(c * CH, CH), CH)
        cols = slice(hh * D_HEAD, (hh + 1) * D_HEAD)
        lb = lb_ref[:, cols]
        q = _silu(q_ref[rows, cols])
        f = lb + (1.0 - lb) * _sigmoid(f_ref[rows, cols])
        lf = jnp.log(f)
        k = 1.0 - f
        v16 = i_ref[rows, cols].astype(BF16)
        kt = k.T
        exq = jnp.exp(jnp.dot(sq_ref[...], jnp.concatenate(_split_bf16(lf), 0),
                              preferred_element_type=F32))
        exk = jnp.exp(jnp.dot(jnp.concatenate(_split_bf16(lf.T), 1), skt_ref[...],
                              preferred_element_type=F32))
        inter = jnp.dot((q * exq[0:CH]).astype(BF16), st.astype(BF16), preferred_element_type=F32)
        sc = eye_ref[...] * jnp.sum(q * k, -1, keepdims=True)
        for l in range(N_LEVELS):
            ql = (q * exq[(1 + l) * CH:(2 + l) * CH]).astype(BF16)
            klt = (kt * exk[:, (1 + l) * CH:(2 + l) * CH]).astype(BF16)
            sc = sc + pair_ref[l] * jnp.dot(ql, klt, preferred_element_type=F32)
        o = inter + jnp.dot(sc.astype(BF16), v16, preferred_element_type=F32)
        st_new = (st * exk[:, (1 + N_LEVELS) * CH:]
                  + jnp.dot((kt * exk[:, 0:CH]).astype(BF16), v16, preferred_element_type=F32))
        o = _rms(o, gn) * _silu(g_ref[rows, cols])
        o_ref[rows, cols] = o.astype(o_ref.dtype)
        return st_new

    def step(c, states):
        return tuple(chunk(c, st, hh) for hh, st in enumerate(states))

    zero = jnp.zeros((D_HEAD, D_HEAD), F32)
    states = lax.fori_loop(0, SEQ // CH, step, (zero,) * HGRN_HEADS_PER_STEP)
    for hh, st in enumerate(states):
        s_ref[hh] = st


def _hgrn_prompt(hq, hf, hi, hg, lb, gnorm):
    tables = _hgrn_tables()
    width = HGRN_HEADS_PER_STEP * D_HEAD
    blk = pl.BlockSpec((SEQ, width), lambda b, h: (b, h))

    def full(a):
        return pl.BlockSpec(a.shape, lambda b, h, n=a.ndim: (0,) * n)

    return pl.pallas_call(
        _hgrn_kernel,
        grid=(BATCH, N_HEADS // HGRN_HEADS_PER_STEP),
        in_specs=[blk, blk, blk, blk,
                  pl.BlockSpec((1, width), lambda b, h: (0, h)),
                  pl.BlockSpec((1, D_HEAD), lambda b, h: (0, 0))] + [full(a) for a in tables],
        out_specs=[blk, pl.BlockSpec((None, HGRN_HEADS_PER_STEP, D_HEAD, D_HEAD), lambda b, h: (b, h, 0, 0))],
        out_shape=[jax.ShapeDtypeStruct((NP, A_WIDTH), BF16),
                   jax.ShapeDtypeStruct((BATCH, N_HEADS, D_HEAD, D_HEAD), F32)],
        compiler_params=_cparams(("arbitrary", "arbitrary")),
        name="hgrn_prompt",
    )(hq, hf, hi, hg, lb, gnorm, *tables)


def _sample_bias_tables(rel_bias):
    j = np.concatenate([QB - np.arange(QB), [0]])
    pick = np.zeros((len(DILATED), QB + 1, NUM_BUCKETS), np.float32)
    for bi, (w, d) in enumerate(DILATED):
        pick[bi, np.arange(QB + 1), _t5_bucket(d * j)] = 1.0
    return jnp.einsum("bjk,kh->bhj", pick, rel_bias.astype(F32), precision=lax.Precision.HIGHEST)[..., None]


def _bf16_round(x):
    return x.astype(BF16).astype(F32)


def _col(row, eye):
    return jnp.sum(eye * row, -1, keepdims=True)


NEAR_POS = 4 * QB
FAR_STEPS = (W_MAX - NEAR_POS) // 16


def _cached_rows(near_ref, far_ref, d, h):
    def near(first_pos, n, step):
        return near_ref[pl.ds((first_pos - (W_MAX - NEAR_POS)) * N_HEADS + h, n, stride=step * N_HEADS), :]

    if QB * d <= NEAR_POS:
        return near(W_MAX - QB * d, QB, d)
    return jnp.concatenate([far_ref[:, h, :], near(W_MAX - NEAR_POS, QB - FAR_STEPS, d)], 0)


def _sample_kernel(q_ref, k_ref, v_ref, hq_ref, hf_ref, hi_ref, hg_ref, kn_ref, kf_ref, vn_ref, vf_ref, s0_ref,
                   bias_ref, lb_ref, gn_ref, eye_ref, oa_ref, ob_ref, s_ref, oa_acc, ob_acc):
    b = pl.program_id(0)
    row = pl.ds(b, 1)
    scale = D_HEAD ** -0.5
    q = _bf16_round(q_ref[row, :] * scale)
    kn = _bf16_round(k_ref[row, :])
    vn = _bf16_round(v_ref[row, :])
    eye = eye_ref[...]

    stats = []
    for bi, (w, d) in enumerate(DILATED):
        per_head = []
        for h in range(N_HEADS):
            hs = slice(h * D_HEAD, (h + 1) * D_HEAD)
            kb = _bf16_round(_cached_rows(kn_ref, kf_ref, d, h))
            vb = _bf16_round(_cached_rows(vn_ref, vf_ref, d, h))
            s = jnp.sum(kb * q[:, hs], -1, keepdims=True) + bias_ref[bi, h, 0:QB]
            s0 = jnp.sum(q[:, hs] * kn[:, hs], -1, keepdims=True) + bias_ref[bi, h, QB:QB + 1]
            m = jnp.maximum(jnp.max(s, 0, keepdims=True), s0)
            p = jnp.exp(s - m)
            p0 = jnp.exp(s0 - m)
            ssum = jnp.sum(p, 0, keepdims=True) + p0
            u = jnp.sum(_bf16_round(p) * vb, 0, keepdims=True) + _bf16_round(p0) * vn[:, hs]
            per_head.append((m, ssum, u))
        stats.append(per_head)
    outs = []
    for h in range(N_HEADS):
        m_all = functools.reduce(jnp.maximum, [stats[bi][h][0] for bi in range(3)])
        num = 0.0
        den = 0.0
        for bi in range(3):
            m, ssum, u = stats[bi][h]
            c = jnp.exp(m - m_all)
            num = num + c * u
            den = den + c * ssum
        outs.append(num / den)
    oa_acc[row, :] = jnp.concatenate(outs, -1)

    qh = _silu(hq_ref[row, :])
    lb = lb_ref[...]
    f = lb + (1.0 - lb) * _sigmoid(hf_ref[row, :])
    vi = hi_ref[row, :]
    gate = _silu(hg_ref[row, :])
    gn = gn_ref[...]
    obs = []
    for h in range(N_HEADS):
        hs = slice(h * D_HEAD, (h + 1) * D_HEAD)
        f_col = _col(f[:, hs], eye)
        q_col = _col(qh[:, hs], eye)
        s_old = s0_ref[h]
        s_ref[h] = f_col * s_old + (1.0 - f_col) * vi[:, hs]
        inter = jnp.sum(_bf16_round(q_col * f_col) * _bf16_round(s_old), 0, keepdims=True)
        qk = jnp.sum(qh[:, hs] * (1.0 - f[:, hs]), -1, keepdims=True)
        o = inter + qk * vi[:, hs]
        obs.append(_rms(o, gn) * gate[:, hs])
    ob_acc[row, :] = jnp.concatenate(obs, -1)

    @pl.when(b == DEC_BATCH - 1)
    def _():
        pad = jnp.zeros((TR - DEC_BATCH, A_WIDTH), F32)
        oa_ref[...] = jnp.concatenate([oa_acc[...], pad], 0).astype(oa_ref.dtype)
        ob_ref[...] = jnp.concatenate([ob_acc[...], pad], 0).astype(ob_ref.dtype)


def _sample_mixers(q, k, v, hq, hf, hi, hg, cache_k, cache_v, state, bias_s, lb, gnorm):
    tile = pl.BlockSpec((TR, A_WIDTH), lambda b: (SAMPLE_TILE, 0))
    out_tile = pl.BlockSpec((TR, A_WIDTH), lambda b: (0, 0))
    near = pl.BlockSpec((None, NEAR_POS * N_HEADS, D_HEAD), lambda b: (b, W_MAX // NEAR_POS - 1, 0))
    far = pl.BlockSpec((None, FAR_STEPS, SUBLANES, D_HEAD), lambda b: (b, 0, 0, 0))
    rows = (DEC_BATCH, W_MAX * N_HEADS, D_HEAD)
    groups = (DEC_BATCH, W_MAX // 16, 16 * N_HEADS, D_HEAD)
    eye = jnp.eye(D_HEAD, dtype=F32)

    return pl.pallas_call(
        _sample_kernel,
        grid=(DEC_BATCH,),
        in_specs=[tile, tile, tile, tile, tile, tile, tile, near, far, near, far,
                  pl.BlockSpec((None, N_HEADS, D_HEAD, D_HEAD), lambda b: (b, 0, 0, 0)),
                  pl.BlockSpec(bias_s.shape, lambda b: (0, 0, 0, 0)),
                  pl.BlockSpec((1, A_WIDTH), lambda b: (0, 0)),
                  pl.BlockSpec((1, D_HEAD), lambda b: (0, 0)),
                  pl.BlockSpec((D_HEAD, D_HEAD), lambda b: (0, 0))],
        out_specs=[out_tile, out_tile,
                   pl.BlockSpec((None, N_HEADS, D_HEAD, D_HEAD), lambda b: (b, 0, 0, 0))],
        out_shape=[jax.ShapeDtypeStruct((TR, A_WIDTH), BF16), jax.ShapeDtypeStruct((TR, A_WIDTH), BF16),
                   jax.ShapeDtypeStruct((DEC_BATCH, N_HEADS, D_HEAD, D_HEAD), F32)],
        scratch_shapes=[pltpu.VMEM((DEC_BATCH, A_WIDTH), F32)] * 2,
        compiler_params=_cparams(("arbitrary",)),
        name="sample_mixers",
    )(q, k, v, hq, hf, hi, hg, cache_k.reshape(rows), cache_k.reshape(groups),
      cache_v.reshape(rows), cache_v.reshape(groups), state, bias_s, lb, gnorm, eye)


def _store_rows_as_tiles(ref, val):
    n = val.shape[0]
    for j in range(ROW_VREGS):
        ref[pl.ds(j, n, stride=ROW_VREGS), :] = val[:, j * LANES:(j + 1) * LANES]


def _load_rows_from_tiles(ref, n):
    return jnp.concatenate([ref[pl.ds(j, n, stride=ROW_VREGS), :] for j in range(ROW_VREGS)], -1)


def _route(h2, wr_ref, br_ref, tri_ref, cnt_ref, meta_ref, gate_ref, cnt_out_ref):
    i = pl.program_id(0)

    @pl.when(i == 0)
    def _():
        cnt_ref[...] = jnp.zeros_like(cnt_ref)

    logits = jnp.dot(h2.astype(BF16), wr_ref[...], preferred_element_type=F32) + br_ref[...]
    lane = lax.broadcasted_iota(I32, (TR, LANES), 1).astype(F32)
    big = float(1 << 20)
    is_g = lane < N_GROUPS
    gl = jnp.where(is_g, logits, NEG)
    gmax = jnp.max(gl, -1, keepdims=True)
    gsel = jnp.min(jnp.where(gl == gmax, lane, big), -1, keepdims=True)
    pg = 1.0 / jnp.sum(jnp.where(is_g, jnp.exp(gl - gmax), 0.0), -1, keepdims=True)
    lo = N_GROUPS + EXPERTS_PER_GROUP * gsel
    in_grp = jnp.logical_and(lane >= lo, lane < lo + EXPERTS_PER_GROUP)
    el = jnp.where(in_grp, logits, NEG)
    m1 = jnp.max(el, -1, keepdims=True)
    i1 = jnp.min(jnp.where(el == m1, lane, big), -1, keepdims=True)
    el2 = jnp.where(lane == i1, NEG, el)
    m2 = jnp.max(el2, -1, keepdims=True)
    i2 = jnp.min(jnp.where(el2 == m2, lane, big), -1, keepdims=True)
    r = jnp.exp(m2 - m1)
    g1 = pg / (1.0 + r)
    g2 = pg * r / (1.0 + r)
    e1 = i1 - N_GROUPS
    e2 = i2 - N_GROUPS

    tok = i * TR + lax.broadcasted_iota(I32, (TR, 1), 0)
    valid = tok < NV
    oh1 = jnp.logical_and(lane == e1, valid)
    oh2 = jnp.logical_and(lane == e2, valid)
    oh = jnp.where(jnp.logical_or(oh1, oh2), 1.0, 0.0)
    before = jnp.dot(tri_ref[...], oh.astype(BF16), preferred_element_type=F32) + cnt_ref[...]
    rank1 = jnp.sum(jnp.where(oh1, before, 0.0), -1, keepdims=True)
    rank2 = jnp.sum(jnp.where(oh2, before, 0.0), -1, keepdims=True)
    tile_cnt = jnp.sum(oh, 0, keepdims=True)
    cnt_ref[...] = cnt_ref[...] + tile_cnt

    eye = jnp.where(lax.broadcasted_iota(I32, (TR, TR), 0) == lax.broadcasted_iota(I32, (TR, TR), 1), 1.0, 0.0)
    rows = [jnp.sum(eye * col, 0, keepdims=True) for col in (e1, e2, rank1, rank2)]
    meta_ref[0] = jnp.concatenate(rows + [jnp.zeros((4, TR), F32)], 0).astype(I32)
    gates = jnp.where(lane == 0, g1, jnp.where(lane == 1, g2, 0.0))
    gates = jnp.where(valid, gates, 0.0)
    gate_ref[...] = gates[:, 0:8]
    cnt_out_ref[0] = tile_cnt


def _router_operands(wg_group, bg_group, wg_exp, bg_exp):
    wr = jnp.zeros((D_MODEL, LANES), F32)
    wr = wr.at[:, 0:N_GROUPS].set(wg_group.astype(F32)).at[:, N_GROUPS:N_GROUPS + N_EXPERTS].set(wg_exp.astype(F32))
    br = jnp.zeros((1, LANES), F32)
    br = br.at[0, 0:N_GROUPS].set(bg_group.astype(F32)).at[0, N_GROUPS:N_GROUPS + N_EXPERTS].set(bg_exp.astype(F32))
    tri = jnp.asarray(np.tril(np.ones((TR, TR), np.float32), -1), BF16)
    return wr.astype(BF16), br, tri


_ROUTE_OUT_SPECS = [pl.BlockSpec((1, 8, TR), lambda i: (i, 0, 0)),
                    pl.BlockSpec((TR, 8), lambda i: (i, 0)),
                    pl.BlockSpec((1, 1, LANES), lambda i: (i, 0, 0))]
_ROUTE_OUT_SHAPES = [jax.ShapeDtypeStruct((N_TILES, 8, TR), I32), jax.ShapeDtypeStruct((NT, 8), F32),
                     jax.ShapeDtypeStruct((N_TILES, 1, LANES), F32)]


def _const_spec(a):
    return pl.BlockSpec(a.shape, lambda i, n=a.ndim: (0,) * n)


def _outproj_kernel(xp_ref, xs_ref, oa_ref, oas_ref, ob_ref, obs_ref, w_ref, g_ref, wr_ref, br_ref, tri_ref,
                    x1_ref, meta_ref, gate_ref, cnt_out_ref, cnt_ref):
    i = pl.program_id(0)
    x = _token_tile(i, xp_ref, xs_ref)
    mixed = jnp.concatenate([_token_tile(i, oa_ref, oas_ref), _token_tile(i, ob_ref, obs_ref)], -1)
    x1 = x + jnp.dot(mixed, w_ref[...], preferred_element_type=F32)
    x1_ref[...] = x1
    h2 = _rms(x1, g_ref[...])
    _route(h2, wr_ref, br_ref, tri_ref, cnt_ref, meta_ref, gate_ref, cnt_out_ref)


def _outproj_route(xp, xs_pad, oa, oa_s, ob, ob_s, w_out_bf16, g_ffn, wr, br, tri):
    row = pl.BlockSpec((TR, D_MODEL), lambda i: (i, 0))
    half = pl.BlockSpec((TR, A_WIDTH), lambda i: (jnp.minimum(i, SAMPLE_TILE - 1), 0))
    half_s = pl.BlockSpec((TR, A_WIDTH), lambda i: (0, 0))
    return pl.pallas_call(
        _outproj_kernel,
        grid=(N_TILES,),
        in_specs=[_PROMPT_ROWS, _SAMPLE_ROWS, half, half_s, half, half_s,
                  _const_spec(w_out_bf16), _const_spec(g_ffn),
                  _const_spec(wr), _const_spec(br), _const_spec(tri)],
        out_specs=[row] + _ROUTE_OUT_SPECS,
        out_shape=[jax.ShapeDtypeStruct((NT, D_MODEL), F32)] + _ROUTE_OUT_SHAPES,
        scratch_shapes=[pltpu.VMEM((1, LANES), F32)],
        compiler_params=_cparams(("arbitrary",)),
        name="outproj_route",
    )(xp, xs_pad, oa, oa_s, ob, ob_s, w_out_bf16, g_ffn, wr, br, tri)


def _plan(meta, tile_counts):
    tcnt = tile_counts[:, 0, :N_EXPERTS].astype(I32)
    cnt = jnp.sum(tcnt, 0)
    nblk = (cnt + TM - 1) // TM
    blk_end = jnp.cumsum(nblk)
    pstart = (blk_end - nblk) * TM
    n_used = blk_end[-1]
    blk = jnp.minimum(jnp.arange(N_SLOTS_BLK, dtype=I32), n_used - 1)
    blk_exp = jnp.sum((blk[:, None] >= blk_end[None, :]).astype(I32), -1)
    blk_exp = jnp.minimum(blk_exp, N_EXPERTS - 1)
    eid = meta[:, 0:2, :]
    experts = jnp.arange(N_EXPERTS, dtype=I32)[:, None, None, None]
    start_of = jnp.sum(jnp.where(eid[None] == experts, pstart[:, None, None, None], 0), 0)
    tok = (jnp.arange(N_TILES, dtype=I32) * TR)[:, None, None] + jnp.arange(TR, dtype=I32)
    dest = jnp.where(tok < NV, start_of + meta[:, 2:4, :], 0).astype(I32).reshape(-1)
    pad_lo = pstart + cnt
    pad_hi = pstart + nblk * TM
    return dest, blk_exp, n_used.reshape(1).astype(I32), pad_lo.astype(I32), pad_hi.astype(I32)


def _row(ref, r):
    return ref.at[pl.ds(pl.multiple_of(r * ROW_VREGS, ROW_VREGS), ROW_VREGS), :]


def _dest_index(tile, kk, t):
    return (2 * tile + kk) * TR + t


def _dispatch_kernel(dest_ref, plo_ref, phi_ref, nused_ref, x_ref, g_ref, xs_ref, stage, zero_ref, sems, zsem):
    i = pl.program_id(0)
    last = pl.num_programs(0) - 1
    slot = i % 2
    _store_rows_as_tiles(stage.at[slot], _rms(x_ref[...], g_ref[...]))

    def wait_tile(n_tok, s):
        for _ in range(2):
            pltpu.make_async_copy(stage.at[s, pl.ds(0, n_tok * ROW_VREGS), :],
                                  xs_ref.at[pl.ds(0, n_tok * ROW_VREGS), :], sems.at[s]).wait()

    def scatter(n_tok):
        def issue(t, carry):
            src = stage.at[slot, pl.ds(pl.multiple_of(t * ROW_VREGS, ROW_VREGS), ROW_VREGS), :]
            for kk in range(2):
                pltpu.make_async_copy(src, _row(xs_ref, dest_ref[_dest_index(i, kk, t)]),
                                      sems.at[slot]).start(priority=kk)
            return carry
        lax.fori_loop(0, n_tok, issue, 0)

    @pl.when(i < last)
    def _():
        scatter(TR)

    @pl.when(i > 0)
    def _():
        wait_tile(TR, 1 - slot)

    @pl.when(i == last)
    def _():
        n_last = NV - (N_TILES - 1) * TR
        scatter(n_last)
        wait_tile(n_last, slot)
        zero_ref[...] = jnp.zeros_like(zero_ref)

        zero_row = zero_ref.at[pl.ds(0, ROW_VREGS), :]

        def per_expert(e, carry):
            def fill(p, c2):
                pltpu.make_async_copy(zero_row, _row(xs_ref, p), zsem).start()
                return c2
            lax.fori_loop(plo_ref[e], phi_ref[e], fill, 0)

            def fill_wait(p, c2):
                pltpu.make_async_copy(zero_row, _row(xs_ref, 0), zsem).wait()
                return c2
            lax.fori_loop(plo_ref[e], phi_ref[e], fill_wait, 0)
            return carry
        lax.fori_loop(0, N_EXPERTS, per_expert, 0)

        def block_of(nb):
            return xs_ref.at[pl.ds(pl.multiple_of(nb * (TM * ROW_VREGS), TM * ROW_VREGS), TM * ROW_VREGS), :]

        def fill_blk(nb, carry):
            pltpu.make_async_copy(zero_ref, block_of(nb), zsem).start()
            return carry
        lax.fori_loop(nused_ref[0], N_SLOTS_BLK, fill_blk, 0)

        def fill_blk_wait(nb, carry):
            pltpu.make_async_copy(zero_ref, block_of(0), zsem).wait()
            return carry
        lax.fori_loop(nused_ref[0], N_SLOTS_BLK, fill_blk_wait, 0)


def _dispatch(x, g_ffn, dest, pad_lo, pad_hi, n_used):
    return pl.pallas_call(
        _dispatch_kernel,
        grid_spec=pltpu.PrefetchScalarGridSpec(
            num_scalar_prefetch=4, grid=(N_TILES,),
            in_specs=[pl.BlockSpec((TR, D_MODEL), lambda i, *_: (i, 0)),
                      pl.BlockSpec((1, D_MODEL), lambda i, *_: (0, 0))],
            out_specs=pl.BlockSpec(memory_space=pl.ANY),
            scratch_shapes=[pltpu.VMEM((2, TR * ROW_VREGS, LANES), F32),
                            pltpu.VMEM((TM * ROW_VREGS, LANES), F32),
                            pltpu.SemaphoreType.DMA((2,)), pltpu.SemaphoreType.DMA(())]),
        out_shape=jax.ShapeDtypeStruct((N_SLOTS * ROW_VREGS, LANES), F32),
        compiler_params=_cparams(("arbitrary",)),
        name="dispatch",
    )(dest, pad_lo, pad_hi, n_used, x, g_ffn)


class _ExpertRows:
    scratch = [pltpu.VMEM((2, 2 * TR * ROW_VREGS, LANES), F32), pltpu.SemaphoreType.DMA((2,))]

    def __init__(self, dest_ref, ys_ref, buf, sems):
        self.dest_ref, self.ys_ref, self.buf, self.sems = dest_ref, ys_ref, buf, sems

    def _copy(self, src_row, slot, dst_row):
        dst = self.buf.at[slot, pl.ds(pl.multiple_of(dst_row * ROW_VREGS, ROW_VREGS), ROW_VREGS), :]
        return pltpu.make_async_copy(_row(self.ys_ref, src_row), dst, self.sems.at[slot])

    def start(self, tile):
        slot = tile % 2

        def issue(g, carry):
            for un in range(ROW_DMA_UNROLL):
                t = g * ROW_DMA_UNROLL + un
                for kk in range(2):
                    self._copy(self.dest_ref[_dest_index(tile, kk, t)], slot, kk * TR + t).start(priority=kk)
            return carry
        lax.fori_loop(0, TR // ROW_DMA_UNROLL, issue, 0)

    def wait(self, tile):
        slot = tile % 2
        pltpu.make_async_copy(self.ys_ref.at[pl.ds(0, 2 * TR * ROW_VREGS), :], self.buf.at[slot],
                              self.sems.at[slot]).wait()

    def combine(self, tile, x, gate_ref):
        rows = self.buf.at[tile % 2]
        g = gate_ref[...]
        for kk in range(2):
            y = jnp.concatenate([rows[pl.ds(kk * TR * ROW_VREGS + j, TR, stride=ROW_VREGS), :]
                                 for j in range(ROW_VREGS)], -1)
            x = x + g[:, kk:kk + 1] * y
        return x

    def fetch_combine(self, x, gate_ref):
        i = pl.program_id(0)

        @pl.when(i == 0)
        def _():
            self.start(i)

        @pl.when(i + 1 < pl.num_programs(0))
        def _():
            self.start(i + 1)

        self.wait(i)
        return self.combine(i, x, gate_ref)


def _ffn_kernel(be_ref, nused_ref, xs_ref, w1_ref, w3_ref, w2_ref, y_ref, w1b, w3b, w2b):
    nb = pl.program_id(0)

    @pl.when(nb < nused_ref[0])
    def _():
        prev = be_ref[jnp.maximum(nb - 1, 0)]
        fresh = jnp.logical_or(nb == 0, be_ref[nb] != prev)

        @pl.when(fresh)
        def _():
            w1b[...] = w1_ref[...].astype(BF16)
            w3b[...] = w3_ref[...].astype(BF16)
            w2b[...] = w2_ref[...].astype(BF16)

        x = _load_rows_from_tiles(xs_ref, TM).astype(BF16)
        a = jnp.dot(x, w1b[...], preferred_element_type=F32)
        b = jnp.dot(x, w3b[...], preferred_element_type=F32)
        mid = (_silu(a) * b).astype(BF16)
        y = jnp.dot(mid, w2b[...], preferred_element_type=F32)
        _store_rows_as_tiles(y_ref, y)

    @pl.when(nb >= nused_ref[0])
    def _():
        y_ref[...] = jnp.zeros_like(y_ref)


def _experts(xs, blk_exp, n_used, layer, w1, w3, w2):
    rows = pl.BlockSpec((TM * ROW_VREGS, LANES), lambda nb, be, nu: (nb, 0))
    rows_in = rows

    def wspec(a, b):
        return pl.BlockSpec((None, None, a, b), lambda nb, be, nu: (layer, be[nb], 0, 0))

    return pl.pallas_call(
        _ffn_kernel,
        grid_spec=pltpu.PrefetchScalarGridSpec(
            num_scalar_prefetch=2, grid=(N_SLOTS_BLK,),
            in_specs=[rows_in, wspec(D_MODEL, D_EXPERT), wspec(D_MODEL, D_EXPERT), wspec(D_EXPERT, D_MODEL)],
            out_specs=rows,
            scratch_shapes=[pltpu.VMEM((D_MODEL, D_EXPERT), BF16), pltpu.VMEM((D_MODEL, D_EXPERT), BF16),
                            pltpu.VMEM((D_EXPERT, D_MODEL), BF16)]),
        out_shape=jax.ShapeDtypeStruct((N_SLOTS * ROW_VREGS, LANES), F32),
        compiler_params=_cparams(("arbitrary",)),
        name="experts",
    )(blk_exp, n_used, xs, w1, w3, w2)


def _moe(x, g_ffn, meta, counts, layer, w1, w3, w2):
    dest, blk_exp, n_used, pad_lo, pad_hi = _plan(meta, counts)
    xs = _dispatch(x, g_ffn, dest, pad_lo, pad_hi, n_used)
    return _experts(xs, blk_exp, n_used, layer, w1, w3, w2), dest


def _pool_kernel(dest_ref, x1_ref, ys_ref, gate0_ref, gmix_ref, sp_ref, pw_ref, ps_ref, gffn_ref,
                 wr_ref, br_ref, tri_ref,
                 x3_ref, hkeep_ref, hs_ref, meta_ref, gate_ref, cnt_out_ref,
                 cnt_ref, ext_ref, e1_ref, e2_ref, e3_ref, e4_ref, mixed_ref, ybuf, ysems):
    i = pl.program_id(0)
    x2 = _ExpertRows(dest_ref, ys_ref, ybuf, ysems).fetch_combine(x1_ref[...], gate0_ref)
    h = _rms(x2, gmix_ref[...])
    H = POOL_HALO
    lvl_refs = (e1_ref, e2_ref, e3_ref, e4_ref)

    @pl.when(i < SAMPLE_TILE)
    def _():
        hkeep_ref[...] = h[TR - 16:TR]
        @pl.when(i % TILES_PER_SEQ == 0)
        def _():
            ext_ref[0:H, :] = jnp.zeros((H, D_MODEL), F32)
        ext_ref[H:H + TR, :] = h
        src = ext_ref
        for lv in range(4):
            sh = 1 << lv
            lo = 8 * (lv + 1)
            c0 = POOL_GROUP * lv
            dst = lvl_refs[lv]
            dst[lo:H + TR, c0:] = src[lo:H + TR, c0:] + src[lo - sh:H + TR - sh, c0:]
            src = dst
        pos = (i % TILES_PER_SEQ) * TR + lax.broadcasted_iota(I32, (TR, 1), 0)
        for gi, w in enumerate(POOL_WINDOWS):
            cs = slice(gi * POOL_GROUP, (gi + 1) * POOL_GROUP)
            inv = 1.0 / jnp.minimum(w, pos + 1).astype(F32)
            mixed_ref[:, cs] = lvl_refs[gi][H:H + TR, cs] * inv - h[:, cs]
        ext_ref[0:H, :] = h[TR - H:TR]

    @pl.when(i == SAMPLE_TILE)
    def _():
        hs = h[0:DEC_BATCH]
        hs_ref[...] = hs
        mixed_ref[...] = jnp.zeros_like(mixed_ref)
        for gi, w in enumerate(POOL_WINDOWS):
            cs = slice(gi * POOL_GROUP, (gi + 1) * POOL_GROUP)
            win = hs[:, cs]
            for dlt in range(1, w):
                win = win + sp_ref[POOL_KEEP - dlt][:, cs]
            mixed_ref[0:DEC_BATCH, cs] = win * (1.0 / w) - hs[:, cs]

    mixed = mixed_ref[...].astype(BF16)
    outs = [jnp.dot(mixed[:, gi * POOL_GROUP:(gi + 1) * POOL_GROUP], pw_ref[gi], preferred_element_type=F32)
            for gi in range(len(POOL_WINDOWS))]
    x3 = x2 + jnp.concatenate(outs, -1) * ps_ref[...]
    x3_ref[...] = x3
    h2 = _rms(x3, gffn_ref[...])
    _route(h2, wr_ref, br_ref, tri_ref, cnt_ref, meta_ref, gate_ref, cnt_out_ref)


def _const_spec_p(a):
    return pl.BlockSpec(a.shape, lambda i, *_, n=a.ndim: (0,) * n)


def _pool_route(dest, x1, ys, gates0, g_mix, sp_t, pw_bf16, p_scale, g_ffn, wr, br, tri):
    def rows(shape):
        return pl.BlockSpec(shape, lambda i, *_: (i, 0))

    ext = pltpu.VMEM((POOL_HALO + TR, D_MODEL), F32)
    consts = [g_mix, sp_t, pw_bf16, p_scale, g_ffn, wr, br, tri]
    return pl.pallas_call(
        _pool_kernel,
        grid_spec=pltpu.PrefetchScalarGridSpec(
            num_scalar_prefetch=1, grid=(N_TILES,),
            in_specs=[rows((TR, D_MODEL)), pl.BlockSpec(memory_space=pl.ANY), rows((TR, 8))]
                     + [_const_spec_p(a) for a in consts],
            out_specs=[rows((TR, D_MODEL)),
                       pl.BlockSpec((None, 16, D_MODEL),
                                    lambda i, *_: (jnp.minimum(i // TILES_PER_SEQ, BATCH - 1), 0, 0)),
                       pl.BlockSpec((DEC_BATCH, D_MODEL), lambda i, *_: (0, 0)),
                       pl.BlockSpec((1, 8, TR), lambda i, *_: (i, 0, 0)), rows((TR, 8)),
                       pl.BlockSpec((1, 1, LANES), lambda i, *_: (i, 0, 0))],
            scratch_shapes=[pltpu.VMEM((1, LANES), F32), ext, ext, ext, ext, ext,
                            pltpu.VMEM((TR, D_MODEL), F32)] + _ExpertRows.scratch),
        out_shape=[jax.ShapeDtypeStruct((NT, D_MODEL), F32),
                   jax.ShapeDtypeStruct((BATCH, 16, D_MODEL), F32),
                   jax.ShapeDtypeStruct((DEC_BATCH, D_MODEL), F32)] + _ROUTE_OUT_SHAPES,
        compiler_params=_cparams(("arbitrary",)),
        name="pool_route",
    )(dest, x1, ys, gates0, *consts)


def _final_kernel(dest_ref, x_ref, ys_ref, gate_ref, g_ref, yp_ref, ysm_ref, ybuf, ysems):
    i = pl.program_id(0)
    y = _rms(_ExpertRows(dest_ref, ys_ref, ybuf, ysems).fetch_combine(x_ref[...], gate_ref), g_ref[...])

    @pl.when(i < SAMPLE_TILE)
    def _():
        yp_ref[...] = y

    @pl.when(i == SAMPLE_TILE)
    def _():
        ysm_ref[...] = y[0:DEC_BATCH]


def _final(dest, x3, ys, gates, g_final):
    return pl.pallas_call(
        _final_kernel,
        grid_spec=pltpu.PrefetchScalarGridSpec(
            num_scalar_prefetch=1, grid=(N_TILES,),
            in_specs=[pl.BlockSpec((TR, D_MODEL), lambda i, *_: (i, 0)), pl.BlockSpec(memory_space=pl.ANY),
                      pl.BlockSpec((TR, 8), lambda i, *_: (i, 0)), _const_spec_p(g_final)],
            out_specs=[pl.BlockSpec((TR, D_MODEL), lambda i, *_: (jnp.minimum(i, SAMPLE_TILE - 1), 0)),
                       pl.BlockSpec((DEC_BATCH, D_MODEL), lambda i, *_: (0, 0))],
            scratch_shapes=_ExpertRows.scratch),
        out_shape=[jax.ShapeDtypeStruct((NP, D_MODEL), F32), jax.ShapeDtypeStruct((DEC_BATCH, D_MODEL), F32)],
        compiler_params=_cparams(("arbitrary",)),
        name="final_norm",
    )(dest, x3, ys, gates, g_final)


def kernel(x_prompt, x_sample, cache_win_k, cache_win_v, state_hgrn, state_pool, rel_bias, norm_mix, norm_ffn,
           norm_final, w_in, w_out, hgrn_lb, hgrn_gnorm, pool_w, pool_scale, moe_wg_group, moe_bg_group,
           moe_wg_exp, moe_bg_exp, moe_w1, moe_w3, moe_w2):
    xp = x_prompt.reshape(NP, D_MODEL)
    xs_pad = jnp.pad(x_sample.reshape(DEC_BATCH, D_MODEL), ((0, TR - DEC_BATCH), (0, 0)))
    lb = jnp.cumsum(jax.nn.softmax(hgrn_lb.astype(F32), axis=0), axis=0)[0:1]
    gnorm = hgrn_gnorm[0:1].astype(F32)

    q, k, v, hq, hf, hi, hg, k_win, v_win = _inproj(xp, xs_pad, norm_mix[0:1], w_in[0].astype(BF16))
    oa = _attention_prompt(q, k, v, _attn_bias_tables(rel_bias))
    ob, s_prompt = _hgrn_prompt(hq, hf, hi, hg, lb, gnorm)
    oa_s, ob_s, s_sample = _sample_mixers(
        q, k, v, hq, hf, hi, hg,
        cache_win_k[0], cache_win_v[0], state_hgrn[0], _sample_bias_tables(rel_bias), lb, gnorm)
    wr, br, tri = _router_operands(moe_wg_group[0], moe_bg_group[0], moe_wg_exp[0], moe_bg_exp[0])
    x1, meta, gates0, counts = _outproj_route(xp, xs_pad, oa, oa_s, ob, ob_s, w_out[0].astype(BF16),
                                              norm_ffn[0:1], wr, br, tri)
    ys, dest = _moe(x1, norm_ffn[0:1], meta, counts, 0, moe_w1, moe_w3, moe_w2)

    wr, br, tri = _router_operands(moe_wg_group[1], moe_bg_group[1], moe_wg_exp[1], moe_bg_exp[1])
    sp_t = jnp.transpose(state_pool[0], (1, 0, 2))
    x3, hkeep, hsample, meta, gates1, counts = _pool_route(
        dest, x1, ys, gates0, norm_mix[1:2], sp_t, pool_w[0].astype(BF16), pool_scale[0:1], norm_ffn[1:2],
        wr, br, tri)
    ys, dest = _moe(x3, norm_ffn[1:2], meta, counts, 1, moe_w1, moe_w3, moe_w2)
    y_prompt, y_sample = _final(dest, x3, ys, gates1, norm_final.reshape(1, D_MODEL))

    k_s = k[NP:NV].reshape(1, DEC_BATCH, 1, N_HEADS, D_HEAD)
    v_s = v[NP:NV].reshape(1, DEC_BATCH, 1, N_HEADS, D_HEAD)
    return (y_prompt.reshape(BATCH, SEQ, D_MODEL), y_sample.reshape(DEC_BATCH, 1, D_MODEL),
            k_win.reshape(1, BATCH, W_MAX, N_HEADS, D_HEAD), v_win.reshape(1, BATCH, W_MAX, N_HEADS, D_HEAD),
            s_prompt[None], hkeep[:, 16 - POOL_KEEP:][None],
            k_s, v_s, s_sample[None], hsample.reshape(1, DEC_BATCH, 1, D_MODEL))
```

```python
import functools

import numpy as np
import jax
import jax.numpy as jnp
from jax import lax
from jax.experimental import pallas as pl
from jax.experimental.pallas import tpu as pltpu

F32 = jnp.float32
BF16 = jnp.bfloat16
I32 = jnp.int32

D_MODEL = 1024
BATCH = 4
SEQ = 4096
DEC_BATCH = 32
PAST_LEN = 8192
W_MAX = 2048
N_HEADS = 4
D_HEAD = 128
A_WIDTH = N_HEADS * D_HEAD
N_PROJ = 7 * A_WIDTH
DILATED = ((128, 1), (512, 4), (2048, 16))
NUM_BUCKETS = 32
MAX_DISTANCE = 2048
POOL_WINDOWS = (2, 4, 8, 16)
POOL_GROUP = 256
POOL_KEEP = 15
N_GROUPS = 4
EXPERTS_PER_GROUP = 8
N_EXPERTS = 32
D_EXPERT = 512
EPS = 1e-6
NEG = -1e30

LANES = 128
SUBLANES = 8
ROW_VREGS = D_MODEL // LANES
TR = 256
NP = BATCH * SEQ
NV = NP + DEC_BATCH
NT = NP + TR
N_TILES = NT // TR
SAMPLE_TILE = NP // TR
TILES_PER_SEQ = SEQ // TR
TM = 256
N_SLOTS_BLK = (2 * NV + N_EXPERTS * (TM - 1) + TM - 1) // TM
N_SLOTS = N_SLOTS_BLK * TM
QB = 128
ATTN_UNROLL = 32
ATTN_PERIOD = 3 * QB
ROW_DMA_UNROLL = 8
CH = 128
HGRN_HEADS_PER_STEP = 2
N_LEVELS = 7
POOL_HALO = 32
VMEM_LIMIT = 56 * 1024 * 1024


def _cparams(sem=None, vmem=VMEM_LIMIT):
    kw = dict(vmem_limit_bytes=vmem)
    if sem is not None:
        kw["dimension_semantics"] = sem
    return pltpu.CompilerParams(**kw)


def _rms(x, g):
    return x * lax.rsqrt(jnp.mean(x * x, -1, keepdims=True) + EPS) * g


def _sigmoid(x):
    return 1.0 / (1.0 + jnp.exp(-x))


def _silu(x):
    return x * _sigmoid(x)


def _token_tile(i, xp_ref, xs_ref):
    return jnp.where(i == SAMPLE_TILE, xs_ref[...], xp_ref[...])


_PROMPT_ROWS = pl.BlockSpec((TR, D_MODEL), lambda i: (jnp.minimum(i, SAMPLE_TILE - 1), 0))
_SAMPLE_ROWS = pl.BlockSpec((TR, D_MODEL), lambda i: (0, 0))
WIN_TILES = W_MAX // TR


def _window_block(i):
    seq = jnp.minimum(i // TILES_PER_SEQ, BATCH - 1)
    j = jnp.clip(i % TILES_PER_SEQ - (TILES_PER_SEQ - WIN_TILES), 0, WIN_TILES - 1)
    return jnp.where(i >= SAMPLE_TILE, BATCH * WIN_TILES - 1, seq * WIN_TILES + j)


def _inproj_kernel(xp_ref, xs_ref, g_ref, w_ref, *out_refs):
    i = pl.program_id(0)
    h = _rms(_token_tile(i, xp_ref, xs_ref), g_ref[...])
    p = jnp.dot(h.astype(BF16), w_ref[...], preferred_element_type=F32)
    for n, o_ref in enumerate(out_refs[:7]):
        o_ref[...] = p[:, n * A_WIDTH:(n + 1) * A_WIDTH]

    @pl.when(jnp.logical_and(i < SAMPLE_TILE, i % TILES_PER_SEQ >= TILES_PER_SEQ - WIN_TILES))
    def _():
        for n, o_ref in ((1, out_refs[7]), (2, out_refs[8])):
            for h_i in range(N_HEADS):
                c0 = n * A_WIDTH + h_i * D_HEAD
                o_ref[pl.ds(h_i, TR, stride=N_HEADS), :] = p[:, c0:c0 + D_HEAD]


def _inproj(xp, xs_pad, g, w_bf16):
    out_sds = ([jax.ShapeDtypeStruct((NT, A_WIDTH), F32)] * 7
               + [jax.ShapeDtypeStruct((BATCH * W_MAX * N_HEADS, D_HEAD), F32)] * 2)
    win = pl.BlockSpec((TR * N_HEADS, D_HEAD), lambda i: (_window_block(i), 0))
    return pl.pallas_call(
        _inproj_kernel,
        grid=(N_TILES,),
        in_specs=[_PROMPT_ROWS, _SAMPLE_ROWS,
                  pl.BlockSpec((1, D_MODEL), lambda i: (0, 0)),
                  pl.BlockSpec((D_MODEL, N_PROJ), lambda i: (0, 0))],
        out_specs=[pl.BlockSpec((TR, A_WIDTH), lambda i: (i, 0))] * 7 + [win, win],
        out_shape=out_sds,
        compiler_params=_cparams(("arbitrary",)),
        name="inproj",
    )(xp, xs_pad, g, w_bf16)


def _t5_bucket(dist):
    max_exact = NUM_BUCKETS // 2
    d = np.asarray(dist)
    large = max_exact + np.floor(np.log(np.maximum(d, 1) / max_exact)
                                 / np.log(MAX_DISTANCE / max_exact) * (NUM_BUCKETS - max_exact)).astype(np.int32)
    large = np.minimum(large, NUM_BUCKETS - 1)
    return np.where(d < max_exact, d, large).astype(np.int32)


def _attn_bias_tables(rel_bias):
    period = ATTN_PERIOD
    m = np.arange(period)
    u = np.where(m < 2 * QB, m, m - period)
    pick = np.zeros((len(DILATED), 2, period, NUM_BUCKETS), np.float32)
    mask = np.zeros((len(DILATED), 2, period, 1), np.float32)
    for bi, (w, d) in enumerate(DILATED):
        nk = w // d
        for vi, off in enumerate((0, QB)):
            j = off - u
            ok = (j >= 0) & (j <= nk)
            pick[bi, vi, m[ok], _t5_bucket(d * j[ok])] = 1.0
            mask[bi, vi, ~ok, 0] = NEG
    vec = jnp.einsum("bvmk,kh->hbvm", pick, rel_bias.astype(F32), precision=lax.Precision.HIGHEST)
    vec = vec + jnp.transpose(mask, (3, 0, 1, 2))
    return vec[:, :, :, None, :]


def _attn_kernel(q_ref, k_ref, v_ref, vec_ref, o_ref,
                 q4, k4, v4, qd, kd, vd, ud, md, sd, u_acc, m_acc, s_acc, bias_ref):
    scale = D_HEAD ** -0.5
    c4 = SEQ // 4
    c16 = SEQ // 16
    for bi in range(len(DILATED)):
        for vi in range(2):
            rows = jnp.broadcast_to(vec_ref[bi, vi], (QB, ATTN_PERIOD))
            bias_ref[bi, vi] = pltpu.roll(rows, 0, 1, stride=1, stride_axis=0)[:, :2 * QB]

    def block_stats(bi, t, nb):
        has_prev = jnp.minimum(t % nb, 1)
        q0 = pl.multiple_of(t * QB, QB)
        k0 = pl.multiple_of((t - has_prev) * QB, QB)
        qb = qd[pl.ds(q0, QB), :]
        kb = kd[pl.ds(k0, 2 * QB), :]
        vb = vd[pl.ds(k0, 2 * QB), :]
        s = lax.dot_general(qb, kb, (((1,), (1,)), ((), ())), preferred_element_type=F32)
        s = s + bias_ref[bi, has_prev]
        mb = jnp.max(s, -1, keepdims=True)
        p = jnp.exp(s - mb)
        sb = jnp.sum(p, -1, keepdims=True)
        u = jnp.dot(p.astype(BF16), vb, preferred_element_type=F32)
        return q0, mb, sb, u

    def run_blocks(bi, nb, consume):
        def body(g, carry):
            for un in range(ATTN_UNROLL):
                q0, mb, sb, u = block_stats(bi, g * ATTN_UNROLL + un, nb)
                consume(pl.ds(q0, QB), mb, sb, u)
            return carry
        lax.fori_loop(0, SEQ // QB // ATTN_UNROLL, body, 0)

    def merged(rows, m_ref, s_ref, u_ref, mb, sb, u):
        m_old = m_ref[rows, :]
        m_new = jnp.maximum(m_old, mb)
        a = jnp.exp(m_old - m_new)
        b = jnp.exp(mb - m_new)
        return m_new, a * s_ref[rows, :] + b * sb, a * u_ref[rows, :] + b * u

    for r in range(4):
        src, dst = pl.ds(r, c4, stride=4), pl.ds(r * c4, c4)
        q4[dst, :] = q_ref[src, :] * scale
        k4[dst, :] = k_ref[src, :]
        v4[dst, :] = v_ref[src, :]

    for r in range(4):
        for j in range(4):
            src, dst = pl.ds(r * c4 + j, c16, stride=4), pl.ds((r + 4 * j) * c16, c16)
            qd[dst, :] = q4[src, :].astype(BF16)
            kd[dst, :] = k4[src, :].astype(BF16)
            vd[dst, :] = v4[src, :].astype(BF16)

    def keep16(rows, mb, sb, u):
        ud[rows, :] = u
        md[rows, :] = jnp.broadcast_to(mb, (QB, D_HEAD))
        sd[rows, :] = jnp.broadcast_to(sb, (QB, D_HEAD))
    run_blocks(2, c16 // QB, keep16)
    for r in range(4):
        for j in range(4):
            src, dst = pl.ds((r + 4 * j) * c16, c16), pl.ds(r * c4 + j, c16, stride=4)
            u_acc[dst, :] = ud[src, :]
            m_acc[dst, :] = md[src, :]
            s_acc[dst, :] = sd[src, :]

    qd[...] = q4[...].astype(BF16)
    kd[...] = k4[...].astype(BF16)
    vd[...] = v4[...].astype(BF16)

    def merge4(rows, mb, sb, u):
        m_acc[rows, :], s_acc[rows, :], u_acc[rows, :] = merged(rows, m_acc, s_acc, u_acc, mb, sb, u)
    run_blocks(1, c4 // QB, merge4)

    for r in range(4):
        src, dst = pl.ds(r * c4, c4), pl.ds(r, c4, stride=4)
        ud[dst, :] = u_acc[src, :]
        md[dst, :] = m_acc[src, :]
        sd[dst, :] = s_acc[src, :]
    qd[...] = (q_ref[...] * scale).astype(BF16)
    kd[...] = k_ref[...].astype(BF16)
    vd[...] = v_ref[...].astype(BF16)

    def finish(rows, mb, sb, u):
        _, den, num = merged(rows, md, sd, ud, mb, sb, u)
        o_ref[rows, :] = (num / den).astype(o_ref.dtype)
    run_blocks(0, SEQ // QB, finish)


def _attention_prompt(q, k, v, bias_tabs):
    blk = pl.BlockSpec((SEQ, D_HEAD), lambda b, h: (b, h))
    return pl.pallas_call(
        _attn_kernel,
        grid=(BATCH, N_HEADS),
        in_specs=[blk, blk, blk,
                  pl.BlockSpec((None, 3, 2, 1, ATTN_PERIOD), lambda b, h: (h, 0, 0, 0, 0))],
        out_specs=blk,
        out_shape=jax.ShapeDtypeStruct((NP, A_WIDTH), BF16),
        scratch_shapes=[pltpu.VMEM((SEQ, D_HEAD), F32)] * 3
                       + [pltpu.VMEM((SEQ, D_HEAD), BF16)] * 3
                       + [pltpu.VMEM((SEQ, D_HEAD), F32)] * 6
                       + [pltpu.VMEM((len(DILATED), 2, QB, 2 * QB), F32)],
        compiler_params=_cparams(("arbitrary", "arbitrary")),
        name="attn_prompt",
    )(q, k, v, bias_tabs)


def _hgrn_tables():
    t = np.arange(CH)
    u = np.arange(CH)
    sums_q = np.zeros((1 + N_LEVELS, CH, CH), np.float32)
    sums_k = np.zeros((2 + N_LEVELS, CH, CH), np.float32)
    sums_q[0] = (u[None, :] <= t[:, None])
    sums_k[0] = (u[None, :] > t[:, None])
    sums_k[1 + N_LEVELS] = 1.0
    pair = np.zeros((N_LEVELS, CH, CH), np.float32)
    for l in range(N_LEVELS):
        h = CH >> (l + 1)
        is_q = (t // h) % 2 == 1
        half_start = (t // h) * h
        half_end = half_start + h
        sel_q = (u[None, :] >= half_start[:, None]) & (u[None, :] <= t[:, None])
        sel_k = (u[None, :] > t[:, None]) & (u[None, :] < half_end[:, None])
        sums_q[1 + l] = sel_q & is_q[:, None]
        sums_k[1 + l] = sel_k & ~is_q[:, None]
        same = (t[:, None] // (2 * h)) == (t[None, :] // (2 * h))
        pair[l] = same & is_q[:, None] & (~is_q)[None, :]
    sums_kt = np.transpose(sums_k, (2, 0, 1)).reshape(CH, (2 + N_LEVELS) * CH)
    sums_q = sums_q.reshape((1 + N_LEVELS) * CH, CH)
    return (jnp.asarray(np.concatenate([sums_q, sums_q], 1), BF16),
            jnp.asarray(np.concatenate([sums_kt, sums_kt], 0), BF16),
            jnp.asarray(pair), jnp.asarray(np.eye(CH, dtype=np.float32)))


def _split_bf16(x):
    hi = x.astype(BF16)
    return hi, (x - hi.astype(F32)).astype(BF16)


def _hgrn_kernel(q_ref, f_ref, i_ref, g_ref, lb_ref, gn_ref, sq_ref, skt_ref, pair_ref, eye_ref, o_ref, s_ref):
    gn = gn_ref[...]

    def chunk(c, st, hh):
        rows = pl.ds(pl.multiple_of(c * CH, CH), CH)
        cols = slice(hh * D_HEAD, (hh + 1) * D_HEAD)
        lb = lb_ref[:, cols]
        q = _silu(q_ref[rows, cols])
        f = lb + (1.0 - lb) * _sigmoid(f_ref[rows, cols])
        lf = jnp.log(f)
        k = 1.0 - f
        v16 = i_ref[rows, cols].astype(BF16)
        kt = k.T
        exq = jnp.exp(jnp.dot(sq_ref[...], jnp.concatenate(_split_bf16(lf), 0),
                              preferred_element_type=F32))
        exk = jnp.exp(jnp.dot(jnp.concatenate(_split_bf16(lf.T), 1), skt_ref[...],
                              preferred_element_type=F32))
        inter = jnp.dot((q * exq[0:CH]).astype(BF16), st.astype(BF16), preferred_element_type=F32)
        sc = eye_ref[...] * jnp.sum(q * k, -1, keepdims=True)
        for l in range(N_LEVELS):
            ql = (q * exq[(1 + l) * CH:(2 + l) * CH]).astype(BF16)
            klt = (kt * exk[:, (1 + l) * CH:(2 + l) * CH]).astype(BF16)
            sc = sc + pair_ref[l] * jnp.dot(ql, klt, preferred_element_type=F32)
        o = inter + jnp.dot(sc.astype(BF16), v16, preferred_element_type=F32)
        st_new = (st * exk[:, (1 + N_LEVELS) * CH:]
                  + jnp.dot((kt * exk[:, 0:CH]).astype(BF16), v16, preferred_element_type=F32))
        o = _rms(o, gn) * _silu(g_ref[rows, cols])
        o_ref[rows, cols] = o.astype(o_ref.dtype)
        return st_new

    def step(c, states):
        return tuple(chunk(c, st, hh) for hh, st in enumerate(states))

    zero = jnp.zeros((D_HEAD, D_HEAD), F32)
    states = lax.fori_loop(0, SEQ // CH, step, (zero,) * HGRN_HEADS_PER_STEP)
    for hh, st in enumerate(states):
        s_ref[hh] = st


def _hgrn_prompt(hq, hf, hi, hg, lb, gnorm):
    tables = _hgrn_tables()
    width = HGRN_HEADS_PER_STEP * D_HEAD
    blk = pl.BlockSpec((SEQ, width), lambda b, h: (b, h))

    def full(a):
        return pl.BlockSpec(a.shape, lambda b, h, n=a.ndim: (0,) * n)

    return pl.pallas_call(
        _hgrn_kernel,
        grid=(BATCH, N_HEADS // HGRN_HEADS_PER_STEP),
        in_specs=[blk, blk, blk, blk,
                  pl.BlockSpec((1, width), lambda b, h: (0, h)),
                  pl.BlockSpec((1, D_HEAD), lambda b, h: (0, 0))] + [full(a) for a in tables],
        out_specs=[blk, pl.BlockSpec((None, HGRN_HEADS_PER_STEP, D_HEAD, D_HEAD), lambda b, h: (b, h, 0, 0))],
        out_shape=[jax.ShapeDtypeStruct((NP, A_WIDTH), BF16),
                   jax.ShapeDtypeStruct((BATCH, N_HEADS, D_HEAD, D_HEAD), F32)],
        compiler_params=_cparams(("arbitrary", "arbitrary")),
        name="hgrn_prompt",
    )(hq, hf, hi, hg, lb, gnorm, *tables)


def _sample_bias_tables(rel_bias):
    j = np.concatenate([QB - np.arange(QB), [0]])
    pick = np.zeros((len(DILATED), QB + 1, NUM_BUCKETS), np.float32)
    for bi, (w, d) in enumerate(DILATED):
        pick[bi, np.arange(QB + 1), _t5_bucket(d * j)] = 1.0
    return jnp.einsum("bjk,kh->bhj", pick, rel_bias.astype(F32), precision=lax.Precision.HIGHEST)[..., None]


def _bf16_round(x):
    return x.astype(BF16).astype(F32)


def _col(row, eye):
    return jnp.sum(eye * row, -1, keepdims=True)


NEAR_POS = 4 * QB
FAR_STEPS = (W_MAX - NEAR_POS) // 16


def _cached_rows(near_ref, far_ref, d, h):
    def near(first_pos, n, step):
        return near_ref[pl.ds((first_pos - (W_MAX - NEAR_POS)) * N_HEADS + h, n, stride=step * N_HEADS), :]

    if QB * d <= NEAR_POS:
        return near(W_MAX - QB * d, QB, d)
    return jnp.concatenate([far_ref[:, h, :], near(W_MAX - NEAR_POS, QB - FAR_STEPS, d)], 0)


def _sample_kernel(q_ref, k_ref, v_ref, hq_ref, hf_ref, hi_ref, hg_ref, kn_ref, kf_ref, vn_ref, vf_ref, s0_ref,
                   bias_ref, lb_ref, gn_ref, eye_ref, oa_ref, ob_ref, s_ref, oa_acc, ob_acc):
    b = pl.program_id(0)
    row = pl.ds(b, 1)
    scale = D_HEAD ** -0.5
    q = _bf16_round(q_ref[row, :] * scale)
    kn = _bf16_round(k_ref[row, :])
    vn = _bf16_round(v_ref[row, :])
    eye = eye_ref[...]

    stats = []
    for bi, (w, d) in enumerate(DILATED):
        per_head = []
        for h in range(N_HEADS):
            hs = slice(h * D_HEAD, (h + 1) * D_HEAD)
            kb = _bf16_round(_cached_rows(kn_ref, kf_ref, d, h))
            vb = _bf16_round(_cached_rows(vn_ref, vf_ref, d, h))
            s = jnp.sum(kb * q[:, hs], -1, keepdims=True) + bias_ref[bi, h, 0:QB]
            s0 = jnp.sum(q[:, hs] * kn[:, hs], -1, keepdims=True) + bias_ref[bi, h, QB:QB + 1]
            m = jnp.maximum(jnp.max(s, 0, keepdims=True), s0)
            p = jnp.exp(s - m)
            p0 = jnp.exp(s0 - m)
            ssum = jnp.sum(p, 0, keepdims=True) + p0
            u = jnp.sum(_bf16_round(p) * vb, 0, keepdims=True) + _bf16_round(p0) * vn[:, hs]
            per_head.append((m, ssum, u))
        stats.append(per_head)
    outs = []
    for h in range(N_HEADS):
        m_all = functools.reduce(jnp.maximum, [stats[bi][h][0] for bi in range(3)])
        num = 0.0
        den = 0.0
        for bi in range(3):
            m, ssum, u = stats[bi][h]
            c = jnp.exp(m - m_all)
            num = num + c * u
            den = den + c * ssum
        outs.append(num / den)
    oa_acc[row, :] = jnp.concatenate(outs, -1)

    qh = _silu(hq_ref[row, :])
    lb = lb_ref[...]
    f = lb + (1.0 - lb) * _sigmoid(hf_ref[row, :])
    vi = hi_ref[row, :]
    gate = _silu(hg_ref[row, :])
    gn = gn_ref[...]
    obs = []
    for h in range(N_HEADS):
        hs = slice(h * D_HEAD, (h + 1) * D_HEAD)
        f_col = _col(f[:, hs], eye)
        q_col = _col(qh[:, hs], eye)
        s_old = s0_ref[h]
        s_ref[h] = f_col * s_old + (1.0 - f_col) * vi[:, hs]
        inter = jnp.sum(_bf16_round(q_col * f_col) * _bf16_round(s_old), 0, keepdims=True)
        qk = jnp.sum(qh[:, hs] * (1.0 - f[:, hs]), -1, keepdims=True)
        o = inter + qk * vi[:, hs]
        obs.append(_rms(o, gn) * gate[:, hs])
    ob_acc[row, :] = jnp.concatenate(obs, -1)

    @pl.when(b == DEC_BATCH - 1)
    def _():
        pad = jnp.zeros((TR - DEC_BATCH, A_WIDTH), F32)
        oa_ref[...] = jnp.concatenate([oa_acc[...], pad], 0).astype(oa_ref.dtype)
        ob_ref[...] = jnp.concatenate([ob_acc[...], pad], 0).astype(ob_ref.dtype)


def _sample_mixers(q, k, v, hq, hf, hi, hg, cache_k, cache_v, state, bias_s, lb, gnorm):
    tile = pl.BlockSpec((TR, A_WIDTH), lambda b: (SAMPLE_TILE, 0))
    out_tile = pl.BlockSpec((TR, A_WIDTH), lambda b: (0, 0))
    near = pl.BlockSpec((None, NEAR_POS * N_HEADS, D_HEAD), lambda b: (b, W_MAX // NEAR_POS - 1, 0))
    far = pl.BlockSpec((None, FAR_STEPS, SUBLANES, D_HEAD), lambda b: (b, 0, 0, 0))
    rows = (DEC_BATCH, W_MAX * N_HEADS, D_HEAD)
    groups = (DEC_BATCH, W_MAX // 16, 16 * N_HEADS, D_HEAD)
    eye = jnp.eye(D_HEAD, dtype=F32)

    return pl.pallas_call(
        _sample_kernel,
        grid=(DEC_BATCH,),
        in_specs=[tile, tile, tile, tile, tile, tile, tile, near, far, near, far,
                  pl.BlockSpec((None, N_HEADS, D_HEAD, D_HEAD), lambda b: (b, 0, 0, 0)),
                  pl.BlockSpec(bias_s.shape, lambda b: (0, 0, 0, 0)),
                  pl.BlockSpec((1, A_WIDTH), lambda b: (0, 0)),
                  pl.BlockSpec((1, D_HEAD), lambda b: (0, 0)),
                  pl.BlockSpec((D_HEAD, D_HEAD), lambda b: (0, 0))],
        out_specs=[out_tile, out_tile,
                   pl.BlockSpec((None, N_HEADS, D_HEAD, D_HEAD), lambda b: (b, 0, 0, 0))],
        out_shape=[jax.ShapeDtypeStruct((TR, A_WIDTH), BF16), jax.ShapeDtypeStruct((TR, A_WIDTH), BF16),
                   jax.ShapeDtypeStruct((DEC_BATCH, N_HEADS, D_HEAD, D_HEAD), F32)],
        scratch_shapes=[pltpu.VMEM((DEC_BATCH, A_WIDTH), F32)] * 2,
        compiler_params=_cparams(("arbitrary",)),
        name="sample_mixers",
    )(q, k, v, hq, hf, hi, hg, cache_k.reshape(rows), cache_k.reshape(groups),
      cache_v.reshape(rows), cache_v.reshape(groups), state, bias_s, lb, gnorm, eye)


def _store_rows_as_tiles(ref, val):
    n = val.shape[0]
    for j in range(ROW_VREGS):
        ref[pl.ds(j, n, stride=ROW_VREGS), :] = val[:, j * LANES:(j + 1) * LANES]


def _load_rows_from_tiles(ref, n):
    return jnp.concatenate([ref[pl.ds(j, n, stride=ROW_VREGS), :] for j in range(ROW_VREGS)], -1)


def _route(h2, wr_ref, br_ref, tri_ref, cnt_ref, meta_ref, gate_ref, cnt_out_ref):
    i = pl.program_id(0)

    @pl.when(i == 0)
    def _():
        cnt_ref[...] = jnp.zeros_like(cnt_ref)

    logits = jnp.dot(h2.astype(BF16), wr_ref[...], preferred_element_type=F32) + br_ref[...]
    lane = lax.broadcasted_iota(I32, (TR, LANES), 1).astype(F32)
    big = float(1 << 20)
    is_g = lane < N_GROUPS
    gl = jnp.where(is_g, logits, NEG)
    gmax = jnp.max(gl, -1, keepdims=True)
    gsel = jnp.min(jnp.where(gl == gmax, lane, big), -1, keepdims=True)
    pg = 1.0 / jnp.sum(jnp.where(is_g, jnp.exp(gl - gmax), 0.0), -1, keepdims=True)
    lo = N_GROUPS + EXPERTS_PER_GROUP * gsel
    in_grp = jnp.logical_and(lane >= lo, lane < lo + EXPERTS_PER_GROUP)
    el = jnp.where(in_grp, logits, NEG)
    m1 = jnp.max(el, -1, keepdims=True)
    i1 = jnp.min(jnp.where(el == m1, lane, big), -1, keepdims=True)
    el2 = jnp.where(lane == i1, NEG, el)
    m2 = jnp.max(el2, -1, keepdims=True)
    i2 = jnp.min(jnp.where(el2 == m2, lane, big), -1, keepdims=True)
    r = jnp.exp(m2 - m1)
    g1 = pg / (1.0 + r)
    g2 = pg * r / (1.0 + r)
    e1 = i1 - N_GROUPS
    e2 = i2 - N_GROUPS

    tok = i * TR + lax.broadcasted_iota(I32, (TR, 1), 0)
    valid = tok < NV
    oh1 = jnp.logical_and(lane == e1, valid)
    oh2 = jnp.logical_and(lane == e2, valid)
    oh = jnp.where(jnp.logical_or(oh1, oh2), 1.0, 0.0)
    before = jnp.dot(tri_ref[...], oh.astype(BF16), preferred_element_type=F32) + cnt_ref[...]
    rank1 = jnp.sum(jnp.where(oh1, before, 0.0), -1, keepdims=True)
    rank2 = jnp.sum(jnp.where(oh2, before, 0.0), -1, keepdims=True)
    tile_cnt = jnp.sum(oh, 0, keepdims=True)
    cnt_ref[...] = cnt_ref[...] + tile_cnt

    eye = jnp.where(lax.broadcasted_iota(I32, (TR, TR), 0) == lax.broadcasted_iota(I32, (TR, TR), 1), 1.0, 0.0)
    rows = [jnp.sum(eye * col, 0, keepdims=True) for col in (e1, e2, rank1, rank2)]
    meta_ref[0] = jnp.concatenate(rows + [jnp.zeros((4, TR), F32)], 0).astype(I32)
    gates = jnp.where(lane == 0, g1, jnp.where(lane == 1, g2, 0.0))
    gates = jnp.where(valid, gates, 0.0)
    gate_ref[...] = gates[:, 0:8]
    cnt_out_ref[0] = tile_cnt


def _router_operands(wg_group, bg_group, wg_exp, bg_exp):
    wr = jnp.zeros((D_MODEL, LANES), F32)
    wr = wr.at[:, 0:N_GROUPS].set(wg_group.astype(F32)).at[:, N_GROUPS:N_GROUPS + N_EXPERTS].set(wg_exp.astype(F32))
    br = jnp.zeros((1, LANES), F32)
    br = br.at[0, 0:N_GROUPS].set(bg_group.astype(F32)).at[0, N_GROUPS:N_GROUPS + N_EXPERTS].set(bg_exp.astype(F32))
    tri = jnp.asarray(np.tril(np.ones((TR, TR), np.float32), -1), BF16)
    return wr.astype(BF16), br, tri


_ROUTE_OUT_SPECS = [pl.BlockSpec((1, 8, TR), lambda i: (i, 0, 0)),
                    pl.BlockSpec((TR, 8), lambda i: (i, 0)),
                    pl.BlockSpec((1, 1, LANES), lambda i: (i, 0, 0))]
_ROUTE_OUT_SHAPES = [jax.ShapeDtypeStruct((N_TILES, 8, TR), I32), jax.ShapeDtypeStruct((NT, 8), F32),
                     jax.ShapeDtypeStruct((N_TILES, 1, LANES), F32)]


def _const_spec(a):
    return pl.BlockSpec(a.shape, lambda i, n=a.ndim: (0,) * n)


def _outproj_kernel(xp_ref, xs_ref, oa_ref, oas_ref, ob_ref, obs_ref, w_ref, g_ref, wr_ref, br_ref, tri_ref,
                    x1_ref, meta_ref, gate_ref, cnt_out_ref, cnt_ref):
    i = pl.program_id(0)
    x = _token_tile(i, xp_ref, xs_ref)
    mixed = jnp.concatenate([_token_tile(i, oa_ref, oas_ref), _token_tile(i, ob_ref, obs_ref)], -1)
    x1 = x + jnp.dot(mixed, w_ref[...], preferred_element_type=F32)
    x1_ref[...] = x1
    h2 = _rms(x1, g_ref[...])
    _route(h2, wr_ref, br_ref, tri_ref, cnt_ref, meta_ref, gate_ref, cnt_out_ref)


def _outproj_route(xp, xs_pad, oa, oa_s, ob, ob_s, w_out_bf16, g_ffn, wr, br, tri):
    row = pl.BlockSpec((TR, D_MODEL), lambda i: (i, 0))
    half = pl.BlockSpec((TR, A_WIDTH), lambda i: (jnp.minimum(i, SAMPLE_TILE - 1), 0))
    half_s = pl.BlockSpec((TR, A_WIDTH), lambda i: (0, 0))
    return pl.pallas_call(
        _outproj_kernel,
        grid=(N_TILES,),
        in_specs=[_PROMPT_ROWS, _SAMPLE_ROWS, half, half_s, half, half_s,
                  _const_spec(w_out_bf16), _const_spec(g_ffn),
                  _const_spec(wr), _const_spec(br), _const_spec(tri)],
        out_specs=[row] + _ROUTE_OUT_SPECS,
        out_shape=[jax.ShapeDtypeStruct((NT, D_MODEL), F32)] + _ROUTE_OUT_SHAPES,
        scratch_shapes=[pltpu.VMEM((1, LANES), F32)],
        compiler_params=_cparams(("arbitrary",)),
        name="outproj_route",
    )(xp, xs_pad, oa, oa_s, ob, ob_s, w_out_bf16, g_ffn, wr, br, tri)


def _plan(meta, tile_counts):
    tcnt = tile_counts[:, 0, :N_EXPERTS].astype(I32)
    cnt = jnp.sum(tcnt, 0)
    nblk = (cnt + TM - 1) // TM
    blk_end = jnp.cumsum(nblk)
    pstart = (blk_end - nblk) * TM
    n_used = blk_end[-1]
    blk = jnp.minimum(jnp.arange(N_SLOTS_BLK, dtype=I32), n_used - 1)
    blk_exp = jnp.sum((blk[:, None] >= blk_end[None, :]).astype(I32), -1)
    blk_exp = jnp.minimum(blk_exp, N_EXPERTS - 1)
    eid = meta[:, 0:2, :]
    experts = jnp.arange(N_EXPERTS, dtype=I32)[:, None, None, None]
    start_of = jnp.sum(jnp.where(eid[None] == experts, pstart[:, None, None, None], 0), 0)
    tok = (jnp.arange(N_TILES, dtype=I32) * TR)[:, None, None] + jnp.arange(TR, dtype=I32)
    dest = jnp.where(tok < NV, start_of + meta[:, 2:4, :], 0).astype(I32).reshape(-1)
    pad_lo = pstart + cnt
    pad_hi = pstart + nblk * TM
    return dest, blk_exp, n_used.reshape(1).astype(I32), pad_lo.astype(I32), pad_hi.astype(I32)


def _row(ref, r):
    return ref.at[pl.ds(pl.multiple_of(r * ROW_VREGS, ROW_VREGS), ROW_VREGS), :]


def _dest_index(tile, kk, t):
    return (2 * tile + kk) * TR + t


def _dispatch_kernel(dest_ref, plo_ref, phi_ref, nused_ref, x_ref, g_ref, xs_ref, stage, zero_ref, sems, zsem):
    i = pl.program_id(0)
    last = pl.num_programs(0) - 1
    slot = i % 2
    _store_rows_as_tiles(stage.at[slot], _rms(x_ref[...], g_ref[...]))

    def wait_tile(n_tok, s):
        for _ in range(2):
            pltpu.make_async_copy(stage.at[s, pl.ds(0, n_tok * ROW_VREGS), :],
                                  xs_ref.at[pl.ds(0, n_tok * ROW_VREGS), :], sems.at[s]).wait()

    def scatter(n_tok):
        def issue(t, carry):
            src = stage.at[slot, pl.ds(pl.multiple_of(t * ROW_VREGS, ROW_VREGS), ROW_VREGS), :]
            for kk in range(2):
                pltpu.make_async_copy(src, _row(xs_ref, dest_ref[_dest_index(i, kk, t)]),
                                      sems.at[slot]).start(priority=kk)
            return carry
        lax.fori_loop(0, n_tok, issue, 0)

    @pl.when(i < last)
    def _():
        scatter(TR)

    @pl.when(i > 0)
    def _():
        wait_tile(TR, 1 - slot)

    @pl.when(i == last)
    def _():
        n_last = NV - (N_TILES - 1) * TR
        scatter(n_last)
        wait_tile(n_last, slot)
        zero_ref[...] = jnp.zeros_like(zero_ref)

        zero_row = zero_ref.at[pl.ds(0, ROW_VREGS), :]

        def per_expert(e, carry):
            def fill(p, c2):
                pltpu.make_async_copy(zero_row, _row(xs_ref, p), zsem).start()
                return c2
            lax.fori_loop(plo_ref[e], phi_ref[e], fill, 0)

            def fill_wait(p, c2):
                pltpu.make_async_copy(zero_row, _row(xs_ref, 0), zsem).wait()
                return c2
            lax.fori_loop(plo_ref[e], phi_ref[e], fill_wait, 0)
            return carry
        lax.fori_loop(0, N_EXPERTS, per_expert, 0)

        def block_of(nb):
            return xs_ref.at[pl.ds(pl.multiple_of(nb * (TM * ROW_VREGS), TM * ROW_VREGS), TM * ROW_VREGS), :]

        def fill_blk(nb, carry):
            pltpu.make_async_copy(zero_ref, block_of(nb), zsem).start()
            return carry
        lax.fori_loop(nused_ref[0], N_SLOTS_BLK, fill_blk, 0)

        def fill_blk_wait(nb, carry):
            pltpu.make_async_copy(zero_ref, block_of(0), zsem).wait()
            return carry
        lax.fori_loop(nused_ref[0], N_SLOTS_BLK, fill_blk_wait, 0)


def _dispatch(x, g_ffn, dest, pad_lo, pad_hi, n_used):
    return pl.pallas_call(
        _dispatch_kernel,
        grid_spec=pltpu.PrefetchScalarGridSpec(
            num_scalar_prefetch=4, grid=(N_TILES,),
            in_specs=[pl.BlockSpec((TR, D_MODEL), lambda i, *_: (i, 0)),
                      pl.BlockSpec((1, D_MODEL), lambda i, *_: (0, 0))],
            out_specs=pl.BlockSpec(memory_space=pl.ANY),
            scratch_shapes=[pltpu.VMEM((2, TR * ROW_VREGS, LANES), F32),
                            pltpu.VMEM((TM * ROW_VREGS, LANES), F32),
                            pltpu.SemaphoreType.DMA((2,)), pltpu.SemaphoreType.DMA(())]),
        out_shape=jax.ShapeDtypeStruct((N_SLOTS * ROW_VREGS, LANES), F32),
        compiler_params=_cparams(("arbitrary",)),
        name="dispatch",
    )(dest, pad_lo, pad_hi, n_used, x, g_ffn)


class _ExpertRows:
    scratch = [pltpu.VMEM((2, 2 * TR * ROW_VREGS, LANES), F32), pltpu.SemaphoreType.DMA((2,))]

    def __init__(self, dest_ref, ys_ref, buf, sems):
        self.dest_ref, self.ys_ref, self.buf, self.sems = dest_ref, ys_ref, buf, sems

    def _copy(self, src_row, slot, dst_row):
        dst = self.buf.at[slot, pl.ds(pl.multiple_of(dst_row * ROW_VREGS, ROW_VREGS), ROW_VREGS), :]
        return pltpu.make_async_copy(_row(self.ys_ref, src_row), dst, self.sems.at[slot])

    def start(self, tile):
        slot = tile % 2

        def issue(g, carry):
            for un in range(ROW_DMA_UNROLL):
                t = g * ROW_DMA_UNROLL + un
                for kk in range(2):
                    self._copy(self.dest_ref[_dest_index(tile, kk, t)], slot, kk * TR + t).start(priority=kk)
            return carry
        lax.fori_loop(0, TR // ROW_DMA_UNROLL, issue, 0)

    def wait(self, tile):
        slot = tile % 2
        pltpu.make_async_copy(self.ys_ref.at[pl.ds(0, 2 * TR * ROW_VREGS), :], self.buf.at[slot],
                              self.sems.at[slot]).wait()

    def combine(self, tile, x, gate_ref):
        rows = self.buf.at[tile % 2]
        g = gate_ref[...]
        for kk in range(2):
            y = jnp.concatenate([rows[pl.ds(kk * TR * ROW_VREGS + j, TR, stride=ROW_VREGS), :]
                                 for j in range(ROW_VREGS)], -1)
            x = x + g[:, kk:kk + 1] * y
        return x

    def fetch_combine(self, x, gate_ref):
        i = pl.program_id(0)

        @pl.when(i == 0)
        def _():
            self.start(i)

        @pl.when(i + 1 < pl.num_programs(0))
        def _():
            self.start(i + 1)

        self.wait(i)
        return self.combine(i, x, gate_ref)


def _ffn_kernel(be_ref, nused_ref, xs_ref, w1_ref, w3_ref, w2_ref, y_ref, w1b, w3b, w2b):
    nb = pl.program_id(0)

    @pl.when(nb < nused_ref[0])
    def _():
        prev = be_ref[jnp.maximum(nb - 1, 0)]
        fresh = jnp.logical_or(nb == 0, be_ref[nb] != prev)

        @pl.when(fresh)
        def _():
            w1b[...] = w1_ref[...].astype(BF16)
            w3b[...] = w3_ref[...].astype(BF16)
            w2b[...] = w2_ref[...].astype(BF16)

        x = _load_rows_from_tiles(xs_ref, TM).astype(BF16)
        a = jnp.dot(x, w1b[...], preferred_element_type=F32)
        b = jnp.dot(x, w3b[...], preferred_element_type=F32)
        mid = (_silu(a) * b).astype(BF16)
        y = jnp.dot(mid, w2b[...], preferred_element_type=F32)
        _store_rows_as_tiles(y_ref, y)

    @pl.when(nb >= nused_ref[0])
    def _():
        y_ref[...] = jnp.zeros_like(y_ref)


def _experts(xs, blk_exp, n_used, layer, w1, w3, w2):
    rows = pl.BlockSpec((TM * ROW_VREGS, LANES), lambda nb, be, nu: (nb, 0))
    rows_in = rows

    def wspec(a, b):
        return pl.BlockSpec((None, None, a, b), lambda nb, be, nu: (layer, be[nb], 0, 0))

    return pl.pallas_call(
        _ffn_kernel,
        grid_spec=pltpu.PrefetchScalarGridSpec(
            num_scalar_prefetch=2, grid=(N_SLOTS_BLK,),
            in_specs=[rows_in, wspec(D_MODEL, D_EXPERT), wspec(D_MODEL, D_EXPERT), wspec(D_EXPERT, D_MODEL)],
            out_specs=rows,
            scratch_shapes=[pltpu.VMEM((D_MODEL, D_EXPERT), BF16), pltpu.VMEM((D_MODEL, D_EXPERT), BF16),
                            pltpu.VMEM((D_EXPERT, D_MODEL), BF16)]),
        out_shape=jax.ShapeDtypeStruct((N_SLOTS * ROW_VREGS, LANES), F32),
        compiler_params=_cparams(("arbitrary",)),
        name="experts",
    )(blk_exp, n_used, xs, w1, w3, w2)


def _moe(x, g_ffn, meta, counts, layer, w1, w3, w2):
    dest, blk_exp, n_used, pad_lo, pad_hi = _plan(meta, counts)
    xs = _dispatch(x, g_ffn, dest, pad_lo, pad_hi, n_used)
    return _experts(xs, blk_exp, n_used, layer, w1, w3, w2), dest


def _pool_kernel(dest_ref, x1_ref, ys_ref, gate0_ref, gmix_ref, sp_ref, pw_ref, ps_ref, gffn_ref,
                 wr_ref, br_ref, tri_ref,
                 x3_ref, hkeep_ref, hs_ref, meta_ref, gate_ref, cnt_out_ref,
                 cnt_ref, ext_ref, e1_ref, e2_ref, e3_ref, e4_ref, mixed_ref, ybuf, ysems):
    i = pl.program_id(0)
    x2 = _ExpertRows(dest_ref, ys_ref, ybuf, ysems).fetch_combine(x1_ref[...], gate0_ref)
    h = _rms(x2, gmix_ref[...])
    H = POOL_HALO
    lvl_refs = (e1_ref, e2_ref, e3_ref, e4_ref)

    @pl.when(i < SAMPLE_TILE)
    def _():
        hkeep_ref[...] = h[TR - 16:TR]
        @pl.when(i % TILES_PER_SEQ == 0)
        def _():
            ext_ref[0:H, :] = jnp.zeros((H, D_MODEL), F32)
        ext_ref[H:H + TR, :] = h
        src = ext_ref
        for lv in range(4):
            sh = 1 << lv
            lo = 8 * (lv + 1)
            c0 = POOL_GROUP * lv
            dst = lvl_refs[lv]
            dst[lo:H + TR, c0:] = src[lo:H + TR, c0:] + src[lo - sh:H + TR - sh, c0:]
            src = dst
        pos = (i % TILES_PER_SEQ) * TR + lax.broadcasted_iota(I32, (TR, 1), 0)
        for gi, w in enumerate(POOL_WINDOWS):
            cs = slice(gi * POOL_GROUP, (gi + 1) * POOL_GROUP)
            inv = 1.0 / jnp.minimum(w, pos + 1).astype(F32)
            mixed_ref[:, cs] = lvl_refs[gi][H:H + TR, cs] * inv - h[:, cs]
        ext_ref[0:H, :] = h[TR - H:TR]

    @pl.when(i == SAMPLE_TILE)
    def _():
        hs = h[0:DEC_BATCH]
        hs_ref[...] = hs
        mixed_ref[...] = jnp.zeros_like(mixed_ref)
        for gi, w in enumerate(POOL_WINDOWS):
            cs = slice(gi * POOL_GROUP, (gi + 1) * POOL_GROUP)
            win = hs[:, cs]
            for dlt in range(1, w):
                win = win + sp_ref[POOL_KEEP - dlt][:, cs]
            mixed_ref[0:DEC_BATCH, cs] = win * (1.0 / w) - hs[:, cs]

    mixed = mixed_ref[...].astype(BF16)
    outs = [jnp.dot(mixed[:, gi * POOL_GROUP:(gi + 1) * POOL_GROUP], pw_ref[gi], preferred_element_type=F32)
            for gi in range(len(POOL_WINDOWS))]
    x3 = x2 + jnp.concatenate(outs, -1) * ps_ref[...]
    x3_ref[...] = x3
    h2 = _rms(x3, gffn_ref[...])
    _route(h2, wr_ref, br_ref, tri_ref, cnt_ref, meta_ref, gate_ref, cnt_out_ref)


def _const_spec_p(a):
    return pl.BlockSpec(a.shape, lambda i, *_, n=a.ndim: (0,) * n)


def _pool_route(dest, x1, ys, gates0, g_mix, sp_t, pw_bf16, p_scale, g_ffn, wr, br, tri):
    def rows(shape):
        return pl.BlockSpec(shape, lambda i, *_: (i, 0))

    ext = pltpu.VMEM((POOL_HALO + TR, D_MODEL), F32)
    consts = [g_mix, sp_t, pw_bf16, p_scale, g_ffn, wr, br, tri]
    return pl.pallas_call(
        _pool_kernel,
        grid_spec=pltpu.PrefetchScalarGridSpec(
            num_scalar_prefetch=1, grid=(N_TILES,),
            in_specs=[rows((TR, D_MODEL)), pl.BlockSpec(memory_space=pl.ANY), rows((TR, 8))]
                     + [_const_spec_p(a) for a in consts],
            out_specs=[rows((TR, D_MODEL)),
                       pl.BlockSpec((None, 16, D_MODEL),
                                    lambda i, *_: (jnp.minimum(i // TILES_PER_SEQ, BATCH - 1), 0, 0)),
                       pl.BlockSpec((DEC_BATCH, D_MODEL), lambda i, *_: (0, 0)),
                       pl.BlockSpec((1, 8, TR), lambda i, *_: (i, 0, 0)), rows((TR, 8)),
                       pl.BlockSpec((1, 1, LANES), lambda i, *_: (i, 0, 0))],
            scratch_shapes=[pltpu.VMEM((1, LANES), F32), ext, ext, ext, ext, ext,
                            pltpu.VMEM((TR, D_MODEL), F32)] + _ExpertRows.scratch),
        out_shape=[jax.ShapeDtypeStruct((NT, D_MODEL), F32),
                   jax.ShapeDtypeStruct((BATCH, 16, D_MODEL), F32),
                   jax.ShapeDtypeStruct((DEC_BATCH, D_MODEL), F32)] + _ROUTE_OUT_SHAPES,
        compiler_params=_cparams(("arbitrary",)),
        name="pool_route",
    )(dest, x1, ys, gates0, *consts)


def _final_kernel(dest_ref, x_ref, ys_ref, gate_ref, g_ref, yp_ref, ysm_ref, ybuf, ysems):
    i = pl.program_id(0)
    y = _rms(_ExpertRows(dest_ref, ys_ref, ybuf, ysems).fetch_combine(x_ref[...], gate_ref), g_ref[...])

    @pl.when(i < SAMPLE_TILE)
    def _():
        yp_ref[...] = y

    @pl.when(i == SAMPLE_TILE)
    def _():
        ysm_ref[...] = y[0:DEC_BATCH]


def _final(dest, x3, ys, gates, g_final):
    return pl.pallas_call(
        _final_kernel,
        grid_spec=pltpu.PrefetchScalarGridSpec(
            num_scalar_prefetch=1, grid=(N_TILES,),
            in_specs=[pl.BlockSpec((TR, D_MODEL), lambda i, *_: (i, 0)), pl.BlockSpec(memory_space=pl.ANY),
                      pl.BlockSpec((TR, 8), lambda i, *_: (i, 0)), _const_spec_p(g_final)],
            out_specs=[pl.BlockSpec((TR, D_MODEL), lambda i, *_: (jnp.minimum(i, SAMPLE_TILE - 1), 0)),
                       pl.BlockSpec((DEC_BATCH, D_MODEL), lambda i, *_: (0, 0))],
            scratch_shapes=_ExpertRows.scratch),
        out_shape=[jax.ShapeDtypeStruct((NP, D_MODEL), F32), jax.ShapeDtypeStruct((DEC_BATCH, D_MODEL), F32)],
        compiler_params=_cparams(("arbitrary",)),
        name="final_norm",
    )(dest, x3, ys, gates, g_final)


def kernel(x_prompt, x_sample, cache_win_k, cache_win_v, state_hgrn, state_pool, rel_bias, norm_mix, norm_ffn,
           norm_final, w_in, w_out, hgrn_lb, hgrn_gnorm, pool_w, pool_scale, moe_wg_group, moe_bg_group,
           moe_wg_exp, moe_bg_exp, moe_w1, moe_w3, moe_w2):
    xp = x_prompt.reshape(NP, D_MODEL)
    xs_pad = jnp.pad(x_sample.reshape(DEC_BATCH, D_MODEL), ((0, TR - DEC_BATCH), (0, 0)))
    lb = jnp.cumsum(jax.nn.softmax(hgrn_lb.astype(F32), axis=0), axis=0)[0:1]
    gnorm = hgrn_gnorm[0:1].astype(F32)

    q, k, v, hq, hf, hi, hg, k_win, v_win = _inproj(xp, xs_pad, norm_mix[0:1], w_in[0].astype(BF16))
    oa = _attention_prompt(q, k, v, _attn_bias_tables(rel_bias))
    ob, s_prompt = _hgrn_prompt(hq, hf, hi, hg, lb, gnorm)
    oa_s, ob_s, s_sample = _sample_mixers(
        q, k, v, hq, hf, hi, hg,
        cache_win_k[0], cache_win_v[0], state_hgrn[0], _sample_bias_tables(rel_bias), lb, gnorm)
    wr, br, tri = _router_operands(moe_wg_group[0], moe_bg_group[0], moe_wg_exp[0], moe_bg_exp[0])
    x1, meta, gates0, counts = _outproj_route(xp, xs_pad, oa, oa_s, ob, ob_s, w_out[0].astype(BF16),
                                              norm_ffn[0:1], wr, br, tri)
    ys, dest = _moe(x1, norm_ffn[0:1], meta, counts, 0, moe_w1, moe_w3, moe_w2)

    wr, br, tri = _router_operands(moe_wg_group[1], moe_bg_group[1], moe_wg_exp[1], moe_bg_exp[1])
    sp_t = jnp.transpose(state_pool[0], (1, 0, 2))
    x3, hkeep, hsample, meta, gates1, counts = _pool_route(
        dest, x1, ys, gates0, norm_mix[1:2], sp_t, pool_w[0].astype(BF16), pool_scale[0:1], norm_ffn[1:2],
        wr, br, tri)
    ys, dest = _moe(x3, norm_ffn[1:2], meta, counts, 1, moe_w1, moe_w3, moe_w2)
    y_prompt, y_sample = _final(dest, x3, ys, gates1, norm_final.reshape(1, D_MODEL))

    k_s = k[NP:NV].reshape(1, DEC_BATCH, 1, N_HEADS, D_HEAD)
    v_s = v[NP:NV].reshape(1, DEC_BATCH, 1, N_HEADS, D_HEAD)
    return (y_prompt.reshape(BATCH, SEQ, D_MODEL), y_sample.reshape(DEC_BATCH, 1, D_MODEL),
            k_win.reshape(1, BATCH, W_MAX, N_HEADS, D_HEAD), v_win.reshape(1, BATCH, W_MAX, N_HEADS, D_HEAD),
            s_prompt[None], hkeep[:, 16 - POOL_KEEP:][None],
            k_s, v_s, s_sample[None], hsample.reshape(1, DEC_BATCH, 1, D_MODEL))
```

```python
import functools

import numpy as np
import jax
import jax.numpy as jnp
from jax import lax
from jax.experimental import pallas as pl
from jax.experimental.pallas import tpu as pltpu

F32 = jnp.float32
BF16 = jnp.bfloat16
I32 = jnp.int32

D_MODEL = 1024
BATCH = 4
SEQ = 4096
DEC_BATCH = 32
PAST_LEN = 8192
W_MAX = 2048
N_HEADS = 4
D_HEAD = 128
A_WIDTH = N_HEADS * D_HEAD
N_PROJ = 7 * A_WIDTH
DILATED = ((128, 1), (512, 4), (2048, 16))
NUM_BUCKETS = 32
MAX_DISTANCE = 2048
POOL_WINDOWS = (2, 4, 8, 16)
POOL_GROUP = 256
POOL_KEEP = 15
N_GROUPS = 4
EXPERTS_PER_GROUP = 8
N_EXPERTS = 32
D_EXPERT = 512
EPS = 1e-6
NEG = -1e30

LANES = 128
SUBLANES = 8
ROW_VREGS = D_MODEL // LANES
TR = 256
NP = BATCH * SEQ
NV = NP + DEC_BATCH
NT = NP + TR
N_TILES = NT // TR
SAMPLE_TILE = NP // TR
TILES_PER_SEQ = SEQ // TR
TM = 256
N_SLOTS_BLK = (2 * NV + N_EXPERTS * (TM - 1) + TM - 1) // TM
N_SLOTS = N_SLOTS_BLK * TM
QB = 128
ATTN_UNROLL = 32
ATTN_PERIOD = 3 * QB
ROW_DMA_UNROLL = 8
CH = 128
HGRN_HEADS_PER_STEP = 2
HGRN_CHUNKS_PER_ITER = 4
N_LEVELS = 7
POOL_HALO = 32
VMEM_LIMIT = 56 * 1024 * 1024


def _cparams(sem=None, vmem=VMEM_LIMIT):
    kw = dict(vmem_limit_bytes=vmem)
    if sem is not None:
        kw["dimension_semantics"] = sem
    return pltpu.CompilerParams(**kw)


def _rms(x, g):
    return x * lax.rsqrt(jnp.mean(x * x, -1, keepdims=True) + EPS) * g


def _sigmoid(x):
    return 1.0 / (1.0 + jnp.exp(-x))


def _silu(x):
    return x * _sigmoid(x)


def _token_tile(i, xp_ref, xs_ref):
    return jnp.where(i == SAMPLE_TILE, xs_ref[...], xp_ref[...])


_PROMPT_ROWS = pl.BlockSpec((TR, D_MODEL), lambda i: (jnp.minimum(i, SAMPLE_TILE - 1), 0))
_SAMPLE_ROWS = pl.BlockSpec((TR, D_MODEL), lambda i: (0, 0))
WIN_TILES = W_MAX // TR


def _window_block(i):
    seq = jnp.minimum(i // TILES_PER_SEQ, BATCH - 1)
    j = jnp.clip(i % TILES_PER_SEQ - (TILES_PER_SEQ - WIN_TILES), 0, WIN_TILES - 1)
    return jnp.where(i >= SAMPLE_TILE, BATCH * WIN_TILES - 1, seq * WIN_TILES + j)


def _inproj_kernel(xp_ref, xs_ref, g_ref, w_ref, *out_refs):
    i = pl.program_id(0)
    h = _rms(_token_tile(i, xp_ref, xs_ref), g_ref[...])
    p = jnp.dot(h.astype(BF16), w_ref[...], preferred_element_type=F32)
    for n, o_ref in enumerate(out_refs[:7]):
        o_ref[...] = p[:, n * A_WIDTH:(n + 1) * A_WIDTH]

    @pl.when(jnp.logical_and(i < SAMPLE_TILE, i % TILES_PER_SEQ >= TILES_PER_SEQ - WIN_TILES))
    def _():
        for n, o_ref in ((1, out_refs[7]), (2, out_refs[8])):
            for h_i in range(N_HEADS):
                c0 = n * A_WIDTH + h_i * D_HEAD
                o_ref[pl.ds(h_i, TR, stride=N_HEADS), :] = p[:, c0:c0 + D_HEAD]


def _inproj(xp, xs_pad, g, w_bf16):
    out_sds = ([jax.ShapeDtypeStruct((NT, A_WIDTH), F32)] * 7
               + [jax.ShapeDtypeStruct((BATCH * W_MAX * N_HEADS, D_HEAD), F32)] * 2)
    win = pl.BlockSpec((TR * N_HEADS, D_HEAD), lambda i: (_window_block(i), 0))
    return pl.pallas_call(
        _inproj_kernel,
        grid=(N_TILES,),
        in_specs=[_PROMPT_ROWS, _SAMPLE_ROWS,
                  pl.BlockSpec((1, D_MODEL), lambda i: (0, 0)),
                  pl.BlockSpec((D_MODEL, N_PROJ), lambda i: (0, 0))],
        out_specs=[pl.BlockSpec((TR, A_WIDTH), lambda i: (i, 0))] * 7 + [win, win],
        out_shape=out_sds,
        compiler_params=_cparams(("arbitrary",)),
        name="inproj",
    )(xp, xs_pad, g, w_bf16)


def _t5_bucket(dist):
    max_exact = NUM_BUCKETS // 2
    d = np.asarray(dist)
    large = max_exact + np.floor(np.log(np.maximum(d, 1) / max_exact)
                                 / np.log(MAX_DISTANCE / max_exact) * (NUM_BUCKETS - max_exact)).astype(np.int32)
    large = np.minimum(large, NUM_BUCKETS - 1)
    return np.where(d < max_exact, d, large).astype(np.int32)


def _attn_bias_tables(rel_bias):
    period = ATTN_PERIOD
    m = np.arange(period)
    u = np.where(m < 2 * QB, m, m - period)
    pick = np.zeros((len(DILATED), 2, period, NUM_BUCKETS), np.float32)
    mask = np.zeros((len(DILATED), 2, period, 1), np.float32)
    for bi, (w, d) in enumerate(DILATED):
        nk = w // d
        for vi, off in enumerate((0, QB)):
            j = off - u
            ok = (j >= 0) & (j <= nk)
            pick[bi, vi, m[ok], _t5_bucket(d * j[ok])] = 1.0
            mask[bi, vi, ~ok, 0] = NEG
    vec = jnp.einsum("bvmk,kh->hbvm", pick, rel_bias.astype(F32), precision=lax.Precision.HIGHEST)
    vec = vec + jnp.transpose(mask, (3, 0, 1, 2))
    return vec[:, :, :, None, :]


def _attn_kernel(q_ref, k_ref, v_ref, vec_ref, o_ref,
                 q4, k4, v4, qd, kd, vd, ud, md, sd, u_acc, m_acc, s_acc, bias_ref):
    scale = D_HEAD ** -0.5
    c4 = SEQ // 4
    c16 = SEQ // 16
    for bi in range(len(DILATED)):
        for vi in range(2):
            rows = jnp.broadcast_to(vec_ref[bi, vi], (QB, ATTN_PERIOD))
            bias_ref[bi, vi] = pltpu.roll(rows, 0, 1, stride=1, stride_axis=0)[:, :2 * QB]

    def block_stats(bi, t, nb):
        has_prev = jnp.minimum(t % nb, 1)
        q0 = pl.multiple_of(t * QB, QB)
        k0 = pl.multiple_of((t - has_prev) * QB, QB)
        qb = qd[pl.ds(q0, QB), :]
        kb = kd[pl.ds(k0, 2 * QB), :]
        vb = vd[pl.ds(k0, 2 * QB), :]
        s = lax.dot_general(qb, kb, (((1,), (1,)), ((), ())), preferred_element_type=F32)
        s = s + bias_ref[bi, has_prev]
        mb = jnp.max(s, -1, keepdims=True)
        p = jnp.exp(s - mb)
        sb = jnp.sum(p, -1, keepdims=True)
        u = jnp.dot(p.astype(BF16), vb, preferred_element_type=F32)
        return q0, mb, sb, u

    def run_blocks(bi, nb, consume):
        def body(g, carry):
            for un in range(ATTN_UNROLL):
                q0, mb, sb, u = block_stats(bi, g * ATTN_UNROLL + un, nb)
                consume(pl.ds(q0, QB), mb, sb, u)
            return carry
        lax.fori_loop(0, SEQ // QB // ATTN_UNROLL, body, 0)

    def merged(rows, m_ref, s_ref, u_ref, mb, sb, u):
        m_old = m_ref[rows, :]
        m_new = jnp.maximum(m_old, mb)
        a = jnp.exp(m_old - m_new)
        b = jnp.exp(mb - m_new)
        return m_new, a * s_ref[rows, :] + b * sb, a * u_ref[rows, :] + b * u

    for r in range(4):
        src, dst = pl.ds(r, c4, stride=4), pl.ds(r * c4, c4)
        q4[dst, :] = q_ref[src, :] * scale
        k4[dst, :] = k_ref[src, :]
        v4[dst, :] = v_ref[src, :]

    for r in range(4):
        for j in range(4):
            src, dst = pl.ds(r * c4 + j, c16, stride=4), pl.ds((r + 4 * j) * c16, c16)
            qd[dst, :] = q4[src, :].astype(BF16)
            kd[dst, :] = k4[src, :].astype(BF16)
            vd[dst, :] = v4[src, :].astype(BF16)

    def keep16(rows, mb, sb, u):
        ud[rows, :] = u
        md[rows, :] = jnp.broadcast_to(mb, (QB, D_HEAD))
        sd[rows, :] = jnp.broadcast_to(sb, (QB, D_HEAD))
    run_blocks(2, c16 // QB, keep16)
    for r in range(4):
        for j in range(4):
            src, dst = pl.ds((r + 4 * j) * c16, c16), pl.ds(r * c4 + j, c16, stride=4)
            u_acc[dst, :] = ud[src, :]
            m_acc[dst, :] = md[src, :]
            s_acc[dst, :] = sd[src, :]

    qd[...] = q4[...].astype(BF16)
    kd[...] = k4[...].astype(BF16)
    vd[...] = v4[...].astype(BF16)

    def merge4(rows, mb, sb, u):
        m_acc[rows, :], s_acc[rows, :], u_acc[rows, :] = merged(rows, m_acc, s_acc, u_acc, mb, sb, u)
    run_blocks(1, c4 // QB, merge4)

    for r in range(4):
        src, dst = pl.ds(r * c4, c4), pl.ds(r, c4, stride=4)
        ud[dst, :] = u_acc[src, :]
        md[dst, :] = m_acc[src, :]
        sd[dst, :] = s_acc[src, :]
    qd[...] = (q_ref[...] * scale).astype(BF16)
    kd[...] = k_ref[...].astype(BF16)
    vd[...] = v_ref[...].astype(BF16)

    def finish(rows, mb, sb, u):
        _, den, num = merged(rows, md, sd, ud, mb, sb, u)
        o_ref[rows, :] = (num / den).astype(o_ref.dtype)
    run_blocks(0, SEQ // QB, finish)


def _attention_prompt(q, k, v, bias_tabs):
    blk = pl.BlockSpec((SEQ, D_HEAD), lambda b, h: (b, h))
    return pl.pallas_call(
        _attn_kernel,
        grid=(BATCH, N_HEADS),
        in_specs=[blk, blk, blk,
                  pl.BlockSpec((None, 3, 2, 1, ATTN_PERIOD), lambda b, h: (h, 0, 0, 0, 0))],
        out_specs=blk,
        out_shape=jax.ShapeDtypeStruct((NP, A_WIDTH), BF16),
        scratch_shapes=[pltpu.VMEM((SEQ, D_HEAD), F32)] * 3
                       + [pltpu.VMEM((SEQ, D_HEAD), BF16)] * 3
                       + [pltpu.VMEM((SEQ, D_HEAD), F32)] * 6
                       + [pltpu.VMEM((len(DILATED), 2, QB, 2 * QB), F32)],
        compiler_params=_cparams(("arbitrary", "arbitrary")),
        name="attn_prompt",
    )(q, k, v, bias_tabs)


def _hgrn_tables():
    t = np.arange(CH)
    u = np.arange(CH)
    sums_q = np.zeros((1 + N_LEVELS, CH, CH), np.float32)
    sums_k = np.zeros((2 + N_LEVELS, CH, CH), np.float32)
    sums_q[0] = (u[None, :] <= t[:, None])
    sums_k[0] = (u[None, :] > t[:, None])
    sums_k[1 + N_LEVELS] = 1.0
    pair = np.zeros((N_LEVELS, CH, CH), np.float32)
    for l in range(N_LEVELS):
        h = CH >> (l + 1)
        is_q = (t // h) % 2 == 1
        half_start = (t // h) * h
        half_end = half_start + h
        sel_q = (u[None, :] >= half_start[:, None]) & (u[None, :] <= t[:, None])
        sel_k = (u[None, :] > t[:, None]) & (u[None, :] < half_end[:, None])
        sums_q[1 + l] = sel_q & is_q[:, None]
        sums_k[1 + l] = sel_k & ~is_q[:, None]
        same = (t[:, None] // (2 * h)) == (t[None, :] // (2 * h))
        pair[l] = same & is_q[:, None] & (~is_q)[None, :]
    sums_kt = np.transpose(sums_k, (2, 0, 1)).reshape(CH, (2 + N_LEVELS) * CH)
    sums_q = sums_q.reshape((1 + N_LEVELS) * CH, CH)
    return (jnp.asarray(np.concatenate([sums_q, sums_q], 1), BF16),
            jnp.asarray(np.concatenate([sums_kt, sums_kt], 0), BF16),
            jnp.asarray(pair), jnp.asarray(np.eye(CH, dtype=np.float32)))


def _split_bf16(x):
    hi = x.astype(BF16)
    return hi, (x - hi.astype(F32)).astype(BF16)


def _hgrn_kernel(q_ref, f_ref, i_ref, g_ref, lb_ref, gn_ref, sq_ref, skt_ref, pair_ref, eye_ref, o_ref, s_ref):
    gn = gn_ref[...]

    def chunk(c, st, hh):
        rows = pl.ds(pl.multiple_of(c * CH, CH), CH)
        cols = slice(hh * D_HEAD, (hh + 1) * D_HEAD)
        lb = lb_ref[:, cols]
        q = _silu(q_ref[rows, cols])
        f = lb + (1.0 - lb) * _sigmoid(f_ref[rows, cols])
        lf = jnp.log(f)
        k = 1.0 - f
        v16 = i_ref[rows, cols].astype(BF16)
        kt = k.T
        exq = jnp.exp(jnp.dot(sq_ref[...], jnp.concatenate(_split_bf16(lf), 0),
                              preferred_element_type=F32))
        exk = jnp.exp(jnp.dot(jnp.concatenate(_split_bf16(lf.T), 1), skt_ref[...],
                              preferred_element_type=F32))
        inter = jnp.dot((q * exq[0:CH]).astype(BF16), st.astype(BF16), preferred_element_type=F32)
        sc = eye_ref[...] * jnp.sum(q * k, -1, keepdims=True)
        for l in range(N_LEVELS):
            ql = (q * exq[(1 + l) * CH:(2 + l) * CH]).astype(BF16)
            klt = (kt * exk[:, (1 + l) * CH:(2 + l) * CH]).astype(BF16)
            sc = sc + pair_ref[l] * jnp.dot(ql, klt, preferred_element_type=F32)
        o = inter + jnp.dot(sc.astype(BF16), v16, preferred_element_type=F32)
        st_new = (st * exk[:, (1 + N_LEVELS) * CH:]
                  + jnp.dot((kt * exk[:, 0:CH]).astype(BF16), v16, preferred_element_type=F32))
        o = _rms(o, gn) * _silu(g_ref[rows, cols])
        o_ref[rows, cols] = o.astype(o_ref.dtype)
        return st_new

    def step(c2, states):
        for un in range(HGRN_CHUNKS_PER_ITER):
            states = tuple(chunk(c2 * HGRN_CHUNKS_PER_ITER + un, st, hh) for hh, st in enumerate(states))
        return states

    zero = jnp.zeros((D_HEAD, D_HEAD), F32)
    states = lax.fori_loop(0, SEQ // CH // HGRN_CHUNKS_PER_ITER, step, (zero,) * HGRN_HEADS_PER_STEP)
    for hh, st in enumerate(states):
        s_ref[hh] = st


def _hgrn_prompt(hq, hf, hi, hg, lb, gnorm):
    tables = _hgrn_tables()
    width = HGRN_HEADS_PER_STEP * D_HEAD
    blk = pl.BlockSpec((SEQ, width), lambda b, h: (b, h))

    def full(a):
        return pl.BlockSpec(a.shape, lambda b, h, n=a.ndim: (0,) * n)

    return pl.pallas_call(
        _hgrn_kernel,
        grid=(BATCH, N_HEADS // HGRN_HEADS_PER_STEP),
        in_specs=[blk, blk, blk, blk,
                  pl.BlockSpec((1, width), lambda b, h: (0, h)),
                  pl.BlockSpec((1, D_HEAD), lambda b, h: (0, 0))] + [full(a) for a in tables],
        out_specs=[blk, pl.BlockSpec((None, HGRN_HEADS_PER_STEP, D_HEAD, D_HEAD), lambda b, h: (b, h, 0, 0))],
        out_shape=[jax.ShapeDtypeStruct((NP, A_WIDTH), BF16),
                   jax.ShapeDtypeStruct((BATCH, N_HEADS, D_HEAD, D_HEAD), F32)],
        compiler_params=_cparams(("arbitrary", "arbitrary")),
        name="hgrn_prompt",
    )(hq, hf, hi, hg, lb, gnorm, *tables)


def _sample_bias_tables(rel_bias):
    j = np.concatenate([QB - np.arange(QB), [0]])
    pick = np.zeros((len(DILATED), QB + 1, NUM_BUCKETS), np.float32)
    for bi, (w, d) in enumerate(DILATED):
        pick[bi, np.arange(QB + 1), _t5_bucket(d * j)] = 1.0
    return jnp.einsum("bjk,kh->bhj", pick, rel_bias.astype(F32), precision=lax.Precision.HIGHEST)[..., None]


def _bf16_round(x):
    return x.astype(BF16).astype(F32)


def _col(row, eye):
    return jnp.sum(eye * row, -1, keepdims=True)


NEAR_POS = 4 * QB
FAR_STEPS = (W_MAX - NEAR_POS) // 16


def _cached_rows(near_ref, far_ref, d, h):
    def near(first_pos, n, step):
        return near_ref[pl.ds((first_pos - (W_MAX - NEAR_POS)) * N_HEADS + h, n, stride=step * N_HEADS), :]

    if QB * d <= NEAR_POS:
        return near(W_MAX - QB * d, QB, d)
    return jnp.concatenate([far_ref[:, h, :], near(W_MAX - NEAR_POS, QB - FAR_STEPS, d)], 0)


def _sample_kernel(q_ref, k_ref, v_ref, hq_ref, hf_ref, hi_ref, hg_ref, kn_ref, kf_ref, vn_ref, vf_ref, s0_ref,
                   bias_ref, lb_ref, gn_ref, eye_ref, oa_ref, ob_ref, s_ref, oa_acc, ob_acc):
    b = pl.program_id(0)
    row = pl.ds(b, 1)
    scale = D_HEAD ** -0.5
    q = _bf16_round(q_ref[row, :] * scale)
    kn = _bf16_round(k_ref[row, :])
    vn = _bf16_round(v_ref[row, :])
    eye = eye_ref[...]

    stats = []
    for bi, (w, d) in enumerate(DILATED):
        per_head = []
        for h in range(N_HEADS):
            hs = slice(h * D_HEAD, (h + 1) * D_HEAD)
            kb = _bf16_round(_cached_rows(kn_ref, kf_ref, d, h))
            vb = _bf16_round(_cached_rows(vn_ref, vf_ref, d, h))
            s = jnp.sum(kb * q[:, hs], -1, keepdims=True) + bias_ref[bi, h, 0:QB]
            s0 = jnp.sum(q[:, hs] * kn[:, hs], -1, keepdims=True) + bias_ref[bi, h, QB:QB + 1]
            m = jnp.maximum(jnp.max(s, 0, keepdims=True), s0)
            p = jnp.exp(s - m)
            p0 = jnp.exp(s0 - m)
            ssum = jnp.sum(p, 0, keepdims=True) + p0
            u = jnp.sum(_bf16_round(p) * vb, 0, keepdims=True) + _bf16_round(p0) * vn[:, hs]
            per_head.append((m, ssum, u))
        stats.append(per_head)
    outs = []
    for h in range(N_HEADS):
        m_all = functools.reduce(jnp.maximum, [stats[bi][h][0] for bi in range(3)])
        num = 0.0
        den = 0.0
        for bi in range(3):
            m, ssum, u = stats[bi][h]
            c = jnp.exp(m - m_all)
            num = num + c * u
            den = den + c * ssum
        outs.append(num / den)
    oa_acc[row, :] = jnp.concatenate(outs, -1)

    qh = _silu(hq_ref[row, :])
    lb = lb_ref[...]
    f = lb + (1.0 - lb) * _sigmoid(hf_ref[row, :])
    vi = hi_ref[row, :]
    gate = _silu(hg_ref[row, :])
    gn = gn_ref[...]
    obs = []
    for h in range(N_HEADS):
        hs = slice(h * D_HEAD, (h + 1) * D_HEAD)
        f_col = _col(f[:, hs], eye)
        q_col = _col(qh[:, hs], eye)
        s_old = s0_ref[h]
        s_ref[h] = f_col * s_old + (1.0 - f_col) * vi[:, hs]
        inter = jnp.sum(_bf16_round(q_col * f_col) * _bf16_round(s_old), 0, keepdims=True)
        qk = jnp.sum(qh[:, hs] * (1.0 - f[:, hs]), -1, keepdims=True)
        o = inter + qk * vi[:, hs]
        obs.append(_rms(o, gn) * gate[:, hs])
    ob_acc[row, :] = jnp.concatenate(obs, -1)

    @pl.when(b == DEC_BATCH - 1)
    def _():
        pad = jnp.zeros((TR - DEC_BATCH, A_WIDTH), F32)
        oa_ref[...] = jnp.concatenate([oa_acc[...], pad], 0).astype(oa_ref.dtype)
        ob_ref[...] = jnp.concatenate([ob_acc[...], pad], 0).astype(ob_ref.dtype)


def _sample_mixers(q, k, v, hq, hf, hi, hg, cache_k, cache_v, state, bias_s, lb, gnorm):
    tile = pl.BlockSpec((TR, A_WIDTH), lambda b: (SAMPLE_TILE, 0))
    out_tile = pl.BlockSpec((TR, A_WIDTH), lambda b: (0, 0))
    near = pl.BlockSpec((None, NEAR_POS * N_HEADS, D_HEAD), lambda b: (b, W_MAX // NEAR_POS - 1, 0))
    far = pl.BlockSpec((None, FAR_STEPS, SUBLANES, D_HEAD), lambda b: (b, 0, 0, 0))
    rows = (DEC_BATCH, W_MAX * N_HEADS, D_HEAD)
    groups = (DEC_BATCH, W_MAX // 16, 16 * N_HEADS, D_HEAD)
    eye = jnp.eye(D_HEAD, dtype=F32)

    return pl.pallas_call(
        _sample_kernel,
        grid=(DEC_BATCH,),
        in_specs=[tile, tile, tile, tile, tile, tile, tile, near, far, near, far,
                  pl.BlockSpec((None, N_HEADS, D_HEAD, D_HEAD), lambda b: (b, 0, 0, 0)),
                  pl.BlockSpec(bias_s.shape, lambda b: (0, 0, 0, 0)),
                  pl.BlockSpec((1, A_WIDTH), lambda b: (0, 0)),
                  pl.BlockSpec((1, D_HEAD), lambda b: (0, 0)),
                  pl.BlockSpec((D_HEAD, D_HEAD), lambda b: (0, 0))],
        out_specs=[out_tile, out_tile,
                   pl.BlockSpec((None, N_HEADS, D_HEAD, D_HEAD), lambda b: (b, 0, 0, 0))],
        out_shape=[jax.ShapeDtypeStruct((TR, A_WIDTH), BF16), jax.ShapeDtypeStruct((TR, A_WIDTH), BF16),
                   jax.ShapeDtypeStruct((DEC_BATCH, N_HEADS, D_HEAD, D_HEAD), F32)],
        scratch_shapes=[pltpu.VMEM((DEC_BATCH, A_WIDTH), F32)] * 2,
        compiler_params=_cparams(("arbitrary",)),
        name="sample_mixers",
    )(q, k, v, hq, hf, hi, hg, cache_k.reshape(rows), cache_k.reshape(groups),
      cache_v.reshape(rows), cache_v.reshape(groups), state, bias_s, lb, gnorm, eye)


def _store_rows_as_tiles(ref, val):
    n = val.shape[0]
    for j in range(ROW_VREGS):
        ref[pl.ds(j, n, stride=ROW_VREGS), :] = val[:, j * LANES:(j + 1) * LANES]


def _load_rows_from_tiles(ref, n):
    return jnp.concatenate([ref[pl.ds(j, n, stride=ROW_VREGS), :] for j in range(ROW_VREGS)], -1)


def _route(h2, wr_ref, br_ref, tri_ref, cnt_ref, meta_ref, gate_ref, cnt_out_ref):
    i = pl.program_id(0)

    @pl.when(i == 0)
    def _():
        cnt_ref[...] = jnp.zeros_like(cnt_ref)

    logits = jnp.dot(h2.astype(BF16), wr_ref[...], preferred_element_type=F32) + br_ref[...]
    lane = lax.broadcasted_iota(I32, (TR, LANES), 1).astype(F32)
    big = float(1 << 20)
    is_g = lane < N_GROUPS
    gl = jnp.where(is_g, logits, NEG)
    gmax = jnp.max(gl, -1, keepdims=True)
    gsel = jnp.min(jnp.where(gl == gmax, lane, big), -1, keepdims=True)
    pg = 1.0 / jnp.sum(jnp.where(is_g, jnp.exp(gl - gmax), 0.0), -1, keepdims=True)
    lo = N_GROUPS + EXPERTS_PER_GROUP * gsel
    in_grp = jnp.logical_and(lane >= lo, lane < lo + EXPERTS_PER_GROUP)
    el = jnp.where(in_grp, logits, NEG)
    m1 = jnp.max(el, -1, keepdims=True)
    i1 = jnp.min(jnp.where(el == m1, lane, big), -1, keepdims=True)
    el2 = jnp.where(lane == i1, NEG, el)
    m2 = jnp.max(el2, -1, keepdims=True)
    i2 = jnp.min(jnp.where(el2 == m2, lane, big), -1, keepdims=True)
    r = jnp.exp(m2 - m1)
    g1 = pg / (1.0 + r)
    g2 = pg * r / (1.0 + r)
    e1 = i1 - N_GROUPS
    e2 = i2 - N_GROUPS

    tok = i * TR + lax.broadcasted_iota(I32, (TR, 1), 0)
    valid = tok < NV
    oh1 = jnp.logical_and(lane == e1, valid)
    oh2 = jnp.logical_and(lane == e2, valid)
    oh = jnp.where(jnp.logical_or(oh1, oh2), 1.0, 0.0)
    before = jnp.dot(tri_ref[...], oh.astype(BF16), preferred_element_type=F32) + cnt_ref[...]
    rank1 = jnp.sum(jnp.where(oh1, before, 0.0), -1, keepdims=True)
    rank2 = jnp.sum(jnp.where(oh2, before, 0.0), -1, keepdims=True)
    tile_cnt = jnp.sum(oh, 0, keepdims=True)
    cnt_ref[...] = cnt_ref[...] + tile_cnt

    eye = jnp.where(lax.broadcasted_iota(I32, (TR, TR), 0) == lax.broadcasted_iota(I32, (TR, TR), 1), 1.0, 0.0)
    rows = [jnp.sum(eye * col, 0, keepdims=True) for col in (e1, e2, rank1, rank2)]
    meta_ref[0] = jnp.concatenate(rows + [jnp.zeros((4, TR), F32)], 0).astype(I32)
    gates = jnp.where(lane == 0, g1, jnp.where(lane == 1, g2, 0.0))
    gates = jnp.where(valid, gates, 0.0)
    gate_ref[...] = gates[:, 0:8]
    cnt_out_ref[0] = tile_cnt


def _router_operands(wg_group, bg_group, wg_exp, bg_exp):
    wr = jnp.zeros((D_MODEL, LANES), F32)
    wr = wr.at[:, 0:N_GROUPS].set(wg_group.astype(F32)).at[:, N_GROUPS:N_GROUPS + N_EXPERTS].set(wg_exp.astype(F32))
    br = jnp.zeros((1, LANES), F32)
    br = br.at[0, 0:N_GROUPS].set(bg_group.astype(F32)).at[0, N_GROUPS:N_GROUPS + N_EXPERTS].set(bg_exp.astype(F32))
    tri = jnp.asarray(np.tril(np.ones((TR, TR), np.float32), -1), BF16)
    return wr.astype(BF16), br, tri


_ROUTE_OUT_SPECS = [pl.BlockSpec((1, 8, TR), lambda i: (i, 0, 0)),
                    pl.BlockSpec((TR, 8), lambda i: (i, 0)),
                    pl.BlockSpec((1, 1, LANES), lambda i: (i, 0, 0))]
_ROUTE_OUT_SHAPES = [jax.ShapeDtypeStruct((N_TILES, 8, TR), I32), jax.ShapeDtypeStruct((NT, 8), F32),
                     jax.ShapeDtypeStruct((N_TILES, 1, LANES), F32)]


def _const_spec(a):
    return pl.BlockSpec(a.shape, lambda i, n=a.ndim: (0,) * n)


def _outproj_kernel(xp_ref, xs_ref, oa_ref, oas_ref, ob_ref, obs_ref, w_ref, g_ref, wr_ref, br_ref, tri_ref,
                    x1_ref, meta_ref, gate_ref, cnt_out_ref, cnt_ref):
    i = pl.program_id(0)
    x = _token_tile(i, xp_ref, xs_ref)
    mixed = jnp.concatenate([_token_tile(i, oa_ref, oas_ref), _token_tile(i, ob_ref, obs_ref)], -1)
    x1 = x + jnp.dot(mixed, w_ref[...], preferred_element_type=F32)
    x1_ref[...] = x1
    h2 = _rms(x1, g_ref[...])
    _route(h2, wr_ref, br_ref, tri_ref, cnt_ref, meta_ref, gate_ref, cnt_out_ref)


def _outproj_route(xp, xs_pad, oa, oa_s, ob, ob_s, w_out_bf16, g_ffn, wr, br, tri):
    row = pl.BlockSpec((TR, D_MODEL), lambda i: (i, 0))
    half = pl.BlockSpec((TR, A_WIDTH), lambda i: (jnp.minimum(i, SAMPLE_TILE - 1), 0))
    half_s = pl.BlockSpec((TR, A_WIDTH), lambda i: (0, 0))
    return pl.pallas_call(
        _outproj_kernel,
        grid=(N_TILES,),
        in_specs=[_PROMPT_ROWS, _SAMPLE_ROWS, half, half_s, half, half_s,
                  _const_spec(w_out_bf16), _const_spec(g_ffn),
                  _const_spec(wr), _const_spec(br), _const_spec(tri)],
        out_specs=[row] + _ROUTE_OUT_SPECS,
        out_shape=[jax.ShapeDtypeStruct((NT, D_MODEL), F32)] + _ROUTE_OUT_SHAPES,
        scratch_shapes=[pltpu.VMEM((1, LANES), F32)],
        compiler_params=_cparams(("arbitrary",)),
        name="outproj_route",
    )(xp, xs_pad, oa, oa_s, ob, ob_s, w_out_bf16, g_ffn, wr, br, tri)


def _plan(meta, tile_counts):
    tcnt = tile_counts[:, 0, :N_EXPERTS].astype(I32)
    cnt = jnp.sum(tcnt, 0)
    nblk = (cnt + TM - 1) // TM
    blk_end = jnp.cumsum(nblk)
    pstart = (blk_end - nblk) * TM
    n_used = blk_end[-1]
    blk = jnp.minimum(jnp.arange(N_SLOTS_BLK, dtype=I32), n_used - 1)
    blk_exp = jnp.sum((blk[:, None] >= blk_end[None, :]).astype(I32), -1)
    blk_exp = jnp.minimum(blk_exp, N_EXPERTS - 1)
    eid = meta[:, 0:2, :]
    experts = jnp.arange(N_EXPERTS, dtype=I32)[:, None, None, None]
    start_of = jnp.sum(jnp.where(eid[None] == experts, pstart[:, None, None, None], 0), 0)
    tok = (jnp.arange(N_TILES, dtype=I32) * TR)[:, None, None] + jnp.arange(TR, dtype=I32)
    dest = jnp.where(tok < NV, start_of + meta[:, 2:4, :], 0).astype(I32).reshape(-1)
    pad_lo = pstart + cnt
    pad_hi = pstart + nblk * TM
    return dest, blk_exp, n_used.reshape(1).astype(I32), pad_lo.astype(I32), pad_hi.astype(I32)


def _row(ref, r):
    return ref.at[pl.ds(pl.multiple_of(r * ROW_VREGS, ROW_VREGS), ROW_VREGS), :]


def _dest_index(tile, kk, t):
    return (2 * tile + kk) * TR + t


def _dispatch_kernel(dest_ref, plo_ref, phi_ref, nused_ref, x_ref, g_ref, xs_ref, stage, zero_ref, sems, zsem):
    i = pl.program_id(0)
    last = pl.num_programs(0) - 1
    slot = i % 2
    _store_rows_as_tiles(stage.at[slot], _rms(x_ref[...], g_ref[...]))

    def wait_tile(n_tok, s):
        for _ in range(2):
            pltpu.make_async_copy(stage.at[s, pl.ds(0, n_tok * ROW_VREGS), :],
                                  xs_ref.at[pl.ds(0, n_tok * ROW_VREGS), :], sems.at[s]).wait()

    def scatter(n_tok):
        def issue(t, carry):
            src = stage.at[slot, pl.ds(pl.multiple_of(t * ROW_VREGS, ROW_VREGS), ROW_VREGS), :]
            for kk in range(2):
                pltpu.make_async_copy(src, _row(xs_ref, dest_ref[_dest_index(i, kk, t)]),
                                      sems.at[slot]).start(priority=kk)
            return carry
        lax.fori_loop(0, n_tok, issue, 0)

    @pl.when(i < last)
    def _():
        scatter(TR)

    @pl.when(i > 0)
    def _():
        wait_tile(TR, 1 - slot)

    @pl.when(i == last)
    def _():
        n_last = NV - (N_TILES - 1) * TR
        scatter(n_last)
        wait_tile(n_last, slot)
        zero_ref[...] = jnp.zeros_like(zero_ref)

        zero_row = zero_ref.at[pl.ds(0, ROW_VREGS), :]

        def per_expert(e, carry):
            def fill(p, c2):
                pltpu.make_async_copy(zero_row, _row(xs_ref, p), zsem).start()
                return c2
            lax.fori_loop(plo_ref[e], phi_ref[e], fill, 0)

            def fill_wait(p, c2):
                pltpu.make_async_copy(zero_row, _row(xs_ref, 0), zsem).wait()
                return c2
            lax.fori_loop(plo_ref[e], phi_ref[e], fill_wait, 0)
            return carry
        lax.fori_loop(0, N_EXPERTS, per_expert, 0)

        def block_of(nb):
            return xs_ref.at[pl.ds(pl.multiple_of(nb * (TM * ROW_VREGS), TM * ROW_VREGS), TM * ROW_VREGS), :]

        def fill_blk(nb, carry):
            pltpu.make_async_copy(zero_ref, block_of(nb), zsem).start()
            return carry
        lax.fori_loop(nused_ref[0], N_SLOTS_BLK, fill_blk, 0)

        def fill_blk_wait(nb, carry):
            pltpu.make_async_copy(zero_ref, block_of(0), zsem).wait()
            return carry
        lax.fori_loop(nused_ref[0], N_SLOTS_BLK, fill_blk_wait, 0)


def _dispatch(x, g_ffn, dest, pad_lo, pad_hi, n_used):
    return pl.pallas_call(
        _dispatch_kernel,
        grid_spec=pltpu.PrefetchScalarGridSpec(
            num_scalar_prefetch=4, grid=(N_TILES,),
            in_specs=[pl.BlockSpec((TR, D_MODEL), lambda i, *_: (i, 0)),
                      pl.BlockSpec((1, D_MODEL), lambda i, *_: (0, 0))],
            out_specs=pl.BlockSpec(memory_space=pl.ANY),
            scratch_shapes=[pltpu.VMEM((2, TR * ROW_VREGS, LANES), F32),
                            pltpu.VMEM((TM * ROW_VREGS, LANES), F32),
                            pltpu.SemaphoreType.DMA((2,)), pltpu.SemaphoreType.DMA(())]),
        out_shape=jax.ShapeDtypeStruct((N_SLOTS * ROW_VREGS, LANES), F32),
        compiler_params=_cparams(("arbitrary",)),
        name="dispatch",
    )(dest, pad_lo, pad_hi, n_used, x, g_ffn)


class _ExpertRows:
    scratch = [pltpu.VMEM((2, 2 * TR * ROW_VREGS, LANES), F32), pltpu.SemaphoreType.DMA((2,))]

    def __init__(self, dest_ref, ys_ref, buf, sems):
        self.dest_ref, self.ys_ref, self.buf, self.sems = dest_ref, ys_ref, buf, sems

    def _copy(self, src_row, slot, dst_row):
        dst = self.buf.at[slot, pl.ds(pl.multiple_of(dst_row * ROW_VREGS, ROW_VREGS), ROW_VREGS), :]
        return pltpu.make_async_copy(_row(self.ys_ref, src_row), dst, self.sems.at[slot])

    def start(self, tile):
        slot = tile % 2

        def issue(g, carry):
            for un in range(ROW_DMA_UNROLL):
                t = g * ROW_DMA_UNROLL + un
                for kk in range(2):
                    self._copy(self.dest_ref[_dest_index(tile, kk, t)], slot, kk * TR + t).start(priority=kk)
            return carry
        lax.fori_loop(0, TR // ROW_DMA_UNROLL, issue, 0)

    def wait(self, tile):
        slot = tile % 2
        pltpu.make_async_copy(self.ys_ref.at[pl.ds(0, 2 * TR * ROW_VREGS), :], self.buf.at[slot],
                              self.sems.at[slot]).wait()

    def combine(self, tile, x, gate_ref):
        rows = self.buf.at[tile % 2]
        g = gate_ref[...]
        for kk in range(2):
            y = jnp.concatenate([rows[pl.ds(kk * TR * ROW_VREGS + j, TR, stride=ROW_VREGS), :]
                                 for j in range(ROW_VREGS)], -1)
            x = x + g[:, kk:kk + 1] * y
        return x

    def fetch_combine(self, x, gate_ref):
        i = pl.program_id(0)

        @pl.when(i == 0)
        def _():
            self.start(i)

        @pl.when(i + 1 < pl.num_programs(0))
        def _():
            self.start(i + 1)

        self.wait(i)
        return self.combine(i, x, gate_ref)


def _ffn_kernel(be_ref, nused_ref, xs_ref, w1_ref, w3_ref, w2_ref, y_ref, w1b, w3b, w2b):
    nb = pl.program_id(0)

    @pl.when(nb < nused_ref[0])
    def _():
        prev = be_ref[jnp.maximum(nb - 1, 0)]
        fresh = jnp.logical_or(nb == 0, be_ref[nb] != prev)

        @pl.when(fresh)
        def _():
            w1b[...] = w1_ref[...].astype(BF16)
            w3b[...] = w3_ref[...].astype(BF16)
            w2b[...] = w2_ref[...].astype(BF16)

        x = _load_rows_from_tiles(xs_ref, TM).astype(BF16)
        a = jnp.dot(x, w1b[...], preferred_element_type=F32)
        b = jnp.dot(x, w3b[...], preferred_element_type=F32)
        mid = (_silu(a) * b).astype(BF16)
        y = jnp.dot(mid, w2b[...], preferred_element_type=F32)
        _store_rows_as_tiles(y_ref, y)

    @pl.when(nb >= nused_ref[0])
    def _():
        y_ref[...] = jnp.zeros_like(y_ref)


def _experts(xs, blk_exp, n_used, layer, w1, w3, w2):
    rows = pl.BlockSpec((TM * ROW_VREGS, LANES), lambda nb, be, nu: (nb, 0))
    rows_in = rows

    def wspec(a, b):
        return pl.BlockSpec((None, None, a, b), lambda nb, be, nu: (layer, be[nb], 0, 0))

    return pl.pallas_call(
        _ffn_kernel,
        grid_spec=pltpu.PrefetchScalarGridSpec(
            num_scalar_prefetch=2, grid=(N_SLOTS_BLK,),
            in_specs=[rows_in, wspec(D_MODEL, D_EXPERT), wspec(D_MODEL, D_EXPERT), wspec(D_EXPERT, D_MODEL)],
            out_specs=rows,
            scratch_shapes=[pltpu.VMEM((D_MODEL, D_EXPERT), BF16), pltpu.VMEM((D_MODEL, D_EXPERT), BF16),
                            pltpu.VMEM((D_EXPERT, D_MODEL), BF16)]),
        out_shape=jax.ShapeDtypeStruct((N_SLOTS * ROW_VREGS, LANES), F32),
        compiler_params=_cparams(("arbitrary",)),
        name="experts",
    )(blk_exp, n_used, xs, w1, w3, w2)


def _moe(x, g_ffn, meta, counts, layer, w1, w3, w2):
    dest, blk_exp, n_used, pad_lo, pad_hi = _plan(meta, counts)
    xs = _dispatch(x, g_ffn, dest, pad_lo, pad_hi, n_used)
    return _experts(xs, blk_exp, n_used, layer, w1, w3, w2), dest


def _pool_kernel(dest_ref, x1_ref, ys_ref, gate0_ref, gmix_ref, sp_ref, pw_ref, ps_ref, gffn_ref,
                 wr_ref, br_ref, tri_ref,
                 x3_ref, hkeep_ref, hs_ref, meta_ref, gate_ref, cnt_out_ref,
                 cnt_ref, ext_ref, e1_ref, e2_ref, e3_ref, e4_ref, mixed_ref, ybuf, ysems):
    i = pl.program_id(0)
    x2 = _ExpertRows(dest_ref, ys_ref, ybuf, ysems).fetch_combine(x1_ref[...], gate0_ref)
    h = _rms(x2, gmix_ref[...])
    H = POOL_HALO
    lvl_refs = (e1_ref, e2_ref, e3_ref, e4_ref)

    @pl.when(i < SAMPLE_TILE)
    def _():
        hkeep_ref[...] = h[TR - 16:TR]
        @pl.when(i % TILES_PER_SEQ == 0)
        def _():
            ext_ref[0:H, :] = jnp.zeros((H, D_MODEL), F32)
        ext_ref[H:H + TR, :] = h
        src = ext_ref
        for lv in range(4):
            sh = 1 << lv
            lo = 8 * (lv + 1)
            c0 = POOL_GROUP * lv
            dst = lvl_refs[lv]
            dst[lo:H + TR, c0:] = src[lo:H + TR, c0:] + src[lo - sh:H + TR - sh, c0:]
            src = dst
        pos = (i % TILES_PER_SEQ) * TR + lax.broadcasted_iota(I32, (TR, 1), 0)
        for gi, w in enumerate(POOL_WINDOWS):
            cs = slice(gi * POOL_GROUP, (gi + 1) * POOL_GROUP)
            inv = 1.0 / jnp.minimum(w, pos + 1).astype(F32)
            mixed_ref[:, cs] = lvl_refs[gi][H:H + TR, cs] * inv - h[:, cs]
        ext_ref[0:H, :] = h[TR - H:TR]

    @pl.when(i == SAMPLE_TILE)
    def _():
        hs = h[0:DEC_BATCH]
        hs_ref[...] = hs
        mixed_ref[...] = jnp.zeros_like(mixed_ref)
        for gi, w in enumerate(POOL_WINDOWS):
            cs = slice(gi * POOL_GROUP, (gi + 1) * POOL_GROUP)
            win = hs[:, cs]
            for dlt in range(1, w):
                win = win + sp_ref[POOL_KEEP - dlt][:, cs]
            mixed_ref[0:DEC_BATCH, cs] = win * (1.0 / w) - hs[:, cs]

    mixed = mixed_ref[...].astype(BF16)
    outs = [jnp.dot(mixed[:, gi * POOL_GROUP:(gi + 1) * POOL_GROUP], pw_ref[gi], preferred_element_type=F32)
            for gi in range(len(POOL_WINDOWS))]
    x3 = x2 + jnp.concatenate(outs, -1) * ps_ref[...]
    x3_ref[...] = x3
    h2 = _rms(x3, gffn_ref[...])
    _route(h2, wr_ref, br_ref, tri_ref, cnt_ref, meta_ref, gate_ref, cnt_out_ref)


def _const_spec_p(a):
    return pl.BlockSpec(a.shape, lambda i, *_, n=a.ndim: (0,) * n)


def _pool_route(dest, x1, ys, gates0, g_mix, sp_t, pw_bf16, p_scale, g_ffn, wr, br, tri):
    def rows(shape):
        return pl.BlockSpec(shape, lambda i, *_: (i, 0))

    ext = pltpu.VMEM((POOL_HALO + TR, D_MODEL), F32)
    consts = [g_mix, sp_t, pw_bf16, p_scale, g_ffn, wr, br, tri]
    return pl.pallas_call(
        _pool_kernel,
        grid_spec=pltpu.PrefetchScalarGridSpec(
            num_scalar_prefetch=1, grid=(N_TILES,),
            in_specs=[rows((TR, D_MODEL)), pl.BlockSpec(memory_space=pl.ANY), rows((TR, 8))]
                     + [_const_spec_p(a) for a in consts],
            out_specs=[rows((TR, D_MODEL)),
                       pl.BlockSpec((None, 16, D_MODEL),
                                    lambda i, *_: (jnp.minimum(i // TILES_PER_SEQ, BATCH - 1), 0, 0)),
                       pl.BlockSpec((DEC_BATCH, D_MODEL), lambda i, *_: (0, 0)),
                       pl.BlockSpec((1, 8, TR), lambda i, *_: (i, 0, 0)), rows((TR, 8)),
                       pl.BlockSpec((1, 1, LANES), lambda i, *_: (i, 0, 0))],
            scratch_shapes=[pltpu.VMEM((1, LANES), F32), ext, ext, ext, ext, ext,
                            pltpu.VMEM((TR, D_MODEL), F32)] + _ExpertRows.scratch),
        out_shape=[jax.ShapeDtypeStruct((NT, D_MODEL), F32),
                   jax.ShapeDtypeStruct((BATCH, 16, D_MODEL), F32),
                   jax.ShapeDtypeStruct((DEC_BATCH, D_MODEL), F32)] + _ROUTE_OUT_SHAPES,
        compiler_params=_cparams(("arbitrary",)),
        name="pool_route",
    )(dest, x1, ys, gates0, *consts)


def _final_kernel(dest_ref, x_ref, ys_ref, gate_ref, g_ref, yp_ref, ysm_ref, ybuf, ysems):
    i = pl.program_id(0)
    y = _rms(_ExpertRows(dest_ref, ys_ref, ybuf, ysems).fetch_combine(x_ref[...], gate_ref), g_ref[...])

    @pl.when(i < SAMPLE_TILE)
    def _():
        yp_ref[...] = y

    @pl.when(i == SAMPLE_TILE)
    def _():
        ysm_ref[...] = y[0:DEC_BATCH]


def _final(dest, x3, ys, gates, g_final):
    return pl.pallas_call(
        _final_kernel,
        grid_spec=pltpu.PrefetchScalarGridSpec(
            num_scalar_prefetch=1, grid=(N_TILES,),
            in_specs=[pl.BlockSpec((TR, D_MODEL), lambda i, *_: (i, 0)), pl.BlockSpec(memory_space=pl.ANY),
                      pl.BlockSpec((TR, 8), lambda i, *_: (i, 0)), _const_spec_p(g_final)],
            out_specs=[pl.BlockSpec((TR, D_MODEL), lambda i, *_: (jnp.minimum(i, SAMPLE_TILE - 1), 0)),
                       pl.BlockSpec((DEC_BATCH, D_MODEL), lambda i, *_: (0, 0))],
            scratch_shapes=_ExpertRows.scratch),
        out_shape=[jax.ShapeDtypeStruct((NP, D_MODEL), F32), jax.ShapeDtypeStruct((DEC_BATCH, D_MODEL), F32)],
        compiler_params=_cparams(("arbitrary",)),
        name="final_norm",
    )(dest, x3, ys, gates, g_final)


def kernel(x_prompt, x_sample, cache_win_k, cache_win_v, state_hgrn, state_pool, rel_bias, norm_mix, norm_ffn,
           norm_final, w_in, w_out, hgrn_lb, hgrn_gnorm, pool_w, pool_scale, moe_wg_group, moe_bg_group,
           moe_wg_exp, moe_bg_exp, moe_w1, moe_w3, moe_w2):
    xp = x_prompt.reshape(NP, D_MODEL)
    xs_pad = jnp.pad(x_sample.reshape(DEC_BATCH, D_MODEL), ((0, TR - DEC_BATCH), (0, 0)))
    lb = jnp.cumsum(jax.nn.softmax(hgrn_lb.astype(F32), axis=0), axis=0)[0:1]
    gnorm = hgrn_gnorm[0:1].astype(F32)

    q, k, v, hq, hf, hi, hg, k_win, v_win = _inproj(xp, xs_pad, norm_mix[0:1], w_in[0].astype(BF16))
    oa = _attention_prompt(q, k, v, _attn_bias_tables(rel_bias))
    ob, s_prompt = _hgrn_prompt(hq, hf, hi, hg, lb, gnorm)
    oa_s, ob_s, s_sample = _sample_mixers(
        q, k, v, hq, hf, hi, hg,
        cache_win_k[0], cache_win_v[0], state_hgrn[0], _sample_bias_tables(rel_bias), lb, gnorm)
    wr, br, tri = _router_operands(moe_wg_group[0], moe_bg_group[0], moe_wg_exp[0], moe_bg_exp[0])
    x1, meta, gates0, counts = _outproj_route(xp, xs_pad, oa, oa_s, ob, ob_s, w_out[0].astype(BF16),
                                              norm_ffn[0:1], wr, br, tri)
    ys, dest = _moe(x1, norm_ffn[0:1], meta, counts, 0, moe_w1, moe_w3, moe_w2)

    wr, br, tri = _router_operands(moe_wg_group[1], moe_bg_group[1], moe_wg_exp[1], moe_bg_exp[1])
    sp_t = jnp.transpose(state_pool[0], (1, 0, 2))
    x3, hkeep, hsample, meta, gates1, counts = _pool_route(
        dest, x1, ys, gates0, norm_mix[1:2], sp_t, pool_w[0].astype(BF16), pool_scale[0:1], norm_ffn[1:2],
        wr, br, tri)
    ys, dest = _moe(x3, norm_ffn[1:2], meta, counts, 1, moe_w1, moe_w3, moe_w2)
    y_prompt, y_sample = _final(dest, x3, ys, gates1, norm_final.reshape(1, D_MODEL))

    k_s = k[NP:NV].reshape(1, DEC_BATCH, 1, N_HEADS, D_HEAD)
    v_s = v[NP:NV].reshape(1, DEC_BATCH, 1, N_HEADS, D_HEAD)
    return (y_prompt.reshape(BATCH, SEQ, D_MODEL), y_sample.reshape(DEC_BATCH, 1, D_MODEL),
            k_win.reshape(1, BATCH, W_MAX, N_HEADS, D_HEAD), v_win.reshape(1, BATCH, W_MAX, N_HEADS, D_HEAD),
            s_prompt[None], hkeep[:, 16 - POOL_KEEP:][None],
            k_s, v_s, s_sample[None], hsample.reshape(1, DEC_BATCH, 1, D_MODEL))
```

```python
import functools

import numpy as np
import jax
import jax.numpy as jnp
from jax import lax
from jax.experimental import pallas as pl
from jax.experimental.pallas import tpu as pltpu

F32 = jnp.float32
BF16 = jnp.bfloat16
I32 = jnp.int32

D_MODEL = 1024
BATCH = 4
SEQ = 4096
DEC_BATCH = 32
PAST_LEN = 8192
W_MAX = 2048
N_HEADS = 4
D_HEAD = 128
A_WIDTH = N_HEADS * D_HEAD
N_PROJ = 7 * A_WIDTH
DILATED = ((128, 1), (512, 4), (2048, 16))
NUM_BUCKETS = 32
MAX_DISTANCE = 2048
POOL_WINDOWS = (2, 4, 8, 16)
POOL_GROUP = 256
POOL_KEEP = 15
N_GROUPS = 4
EXPERTS_PER_GROUP = 8
N_EXPERTS = 32
D_EXPERT = 512
EPS = 1e-6
NEG = -1e30

LANES = 128
SUBLANES = 8
ROW_VREGS = D_MODEL // LANES
TR = 256
NP = BATCH * SEQ
NV = NP + DEC_BATCH
NT = NP + TR
N_TILES = NT // TR
SAMPLE_TILE = NP // TR
TILES_PER_SEQ = SEQ // TR
TM = 256
N_SLOTS_BLK = (2 * NV + N_EXPERTS * (TM - 1) + TM - 1) // TM
N_SLOTS = N_SLOTS_BLK * TM
QB = 128
ATTN_UNROLL = 32
ATTN_PERIOD = 3 * QB
ROW_DMA_UNROLL = 8
CH = 128
HGRN_HEADS_PER_STEP = 2
HGRN_CHUNKS_PER_ITER = 8
N_LEVELS = 7
POOL_HALO = 32
VMEM_LIMIT = 56 * 1024 * 1024


def _cparams(sem=None, vmem=VMEM_LIMIT):
    kw = dict(vmem_limit_bytes=vmem)
    if sem is not None:
        kw["dimension_semantics"] = sem
    return pltpu.CompilerParams(**kw)


def _rms(x, g):
    return x * lax.rsqrt(jnp.mean(x * x, -1, keepdims=True) + EPS) * g


def _sigmoid(x):
    return 1.0 / (1.0 + jnp.exp(-x))


def _silu(x):
    return x * _sigmoid(x)


def _token_tile(i, xp_ref, xs_ref):
    return jnp.where(i == SAMPLE_TILE, xs_ref[...], xp_ref[...])


_PROMPT_ROWS = pl.BlockSpec((TR, D_MODEL), lambda i: (jnp.minimum(i, SAMPLE_TILE - 1), 0))
_SAMPLE_ROWS = pl.BlockSpec((TR, D_MODEL), lambda i: (0, 0))
WIN_TILES = W_MAX // TR


def _window_block(i):
    seq = jnp.minimum(i // TILES_PER_SEQ, BATCH - 1)
    j = jnp.clip(i % TILES_PER_SEQ - (TILES_PER_SEQ - WIN_TILES), 0, WIN_TILES - 1)
    return jnp.where(i >= SAMPLE_TILE, BATCH * WIN_TILES - 1, seq * WIN_TILES + j)


def _inproj_kernel(xp_ref, xs_ref, g_ref, w_ref, *out_refs):
    i = pl.program_id(0)
    h = _rms(_token_tile(i, xp_ref, xs_ref), g_ref[...])
    p = jnp.dot(h.astype(BF16), w_ref[...], preferred_element_type=F32)
    for n, o_ref in enumerate(out_refs[:7]):
        o_ref[...] = p[:, n * A_WIDTH:(n + 1) * A_WIDTH]

    @pl.when(jnp.logical_and(i < SAMPLE_TILE, i % TILES_PER_SEQ >= TILES_PER_SEQ - WIN_TILES))
    def _():
        for n, o_ref in ((1, out_refs[7]), (2, out_refs[8])):
            for h_i in range(N_HEADS):
                c0 = n * A_WIDTH + h_i * D_HEAD
                o_ref[pl.ds(h_i, TR, stride=N_HEADS), :] = p[:, c0:c0 + D_HEAD]


def _inproj(xp, xs_pad, g, w_bf16):
    out_sds = ([jax.ShapeDtypeStruct((NT, A_WIDTH), F32)] * 7
               + [jax.ShapeDtypeStruct((BATCH * W_MAX * N_HEADS, D_HEAD), F32)] * 2)
    win = pl.BlockSpec((TR * N_HEADS, D_HEAD), lambda i: (_window_block(i), 0))
    return pl.pallas_call(
        _inproj_kernel,
        grid=(N_TILES,),
        in_specs=[_PROMPT_ROWS, _SAMPLE_ROWS,
                  pl.BlockSpec((1, D_MODEL), lambda i: (0, 0)),
                  pl.BlockSpec((D_MODEL, N_PROJ), lambda i: (0, 0))],
        out_specs=[pl.BlockSpec((TR, A_WIDTH), lambda i: (i, 0))] * 7 + [win, win],
        out_shape=out_sds,
        compiler_params=_cparams(("arbitrary",)),
        name="inproj",
    )(xp, xs_pad, g, w_bf16)


def _t5_bucket(dist):
    max_exact = NUM_BUCKETS // 2
    d = np.asarray(dist)
    large = max_exact + np.floor(np.log(np.maximum(d, 1) / max_exact)
                                 / np.log(MAX_DISTANCE / max_exact) * (NUM_BUCKETS - max_exact)).astype(np.int32)
    large = np.minimum(large, NUM_BUCKETS - 1)
    return np.where(d < max_exact, d, large).astype(np.int32)


def _attn_bias_tables(rel_bias):
    period = ATTN_PERIOD
    m = np.arange(period)
    u = np.where(m < 2 * QB, m, m - period)
    pick = np.zeros((len(DILATED), 2, period, NUM_BUCKETS), np.float32)
    mask = np.zeros((len(DILATED), 2, period, 1), np.float32)
    for bi, (w, d) in enumerate(DILATED):
        nk = w // d
        for vi, off in enumerate((0, QB)):
            j = off - u
            ok = (j >= 0) & (j <= nk)
            pick[bi, vi, m[ok], _t5_bucket(d * j[ok])] = 1.0
            mask[bi, vi, ~ok, 0] = NEG
    vec = jnp.einsum("bvmk,kh->hbvm", pick, rel_bias.astype(F32), precision=lax.Precision.HIGHEST)
    vec = vec + jnp.transpose(mask, (3, 0, 1, 2))
    return vec[:, :, :, None, :]


def _attn_kernel(q_ref, k_ref, v_ref, vec_ref, o_ref,
                 q4, k4, v4, qd, kd, vd, ud, md, sd, u_acc, m_acc, s_acc, bias_ref):
    scale = D_HEAD ** -0.5
    c4 = SEQ // 4
    c16 = SEQ // 16
    for bi in range(len(DILATED)):
        for vi in range(2):
            rows = jnp.broadcast_to(vec_ref[bi, vi], (QB, ATTN_PERIOD))
            bias_ref[bi, vi] = pltpu.roll(rows, 0, 1, stride=1, stride_axis=0)[:, :2 * QB]

    def block_stats(bi, t, nb):
        has_prev = jnp.minimum(t % nb, 1)
        q0 = pl.multiple_of(t * QB, QB)
        k0 = pl.multiple_of((t - has_prev) * QB, QB)
        qb = qd[pl.ds(q0, QB), :]
        kb = kd[pl.ds(k0, 2 * QB), :]
        vb = vd[pl.ds(k0, 2 * QB), :]
        s = lax.dot_general(qb, kb, (((1,), (1,)), ((), ())), preferred_element_type=F32)
        s = s + bias_ref[bi, has_prev]
        mb = jnp.max(s, -1, keepdims=True)
        p = jnp.exp(s - mb)
        sb = jnp.sum(p, -1, keepdims=True)
        u = jnp.dot(p.astype(BF16), vb, preferred_element_type=F32)
        return q0, mb, sb, u

    def run_blocks(bi, nb, consume):
        def body(g, carry):
            for un in range(ATTN_UNROLL):
                q0, mb, sb, u = block_stats(bi, g * ATTN_UNROLL + un, nb)
                consume(pl.ds(q0, QB), mb, sb, u)
            return carry
        lax.fori_loop(0, SEQ // QB // ATTN_UNROLL, body, 0)

    def merged(rows, m_ref, s_ref, u_ref, mb, sb, u):
        m_old = m_ref[rows, :]
        m_new = jnp.maximum(m_old, mb)
        a = jnp.exp(m_old - m_new)
        b = jnp.exp(mb - m_new)
        return m_new, a * s_ref[rows, :] + b * sb, a * u_ref[rows, :] + b * u

    for r in range(4):
        src, dst = pl.ds(r, c4, stride=4), pl.ds(r * c4, c4)
        q4[dst, :] = q_ref[src, :] * scale
        k4[dst, :] = k_ref[src, :]
        v4[dst, :] = v_ref[src, :]

    for r in range(4):
        for j in range(4):
            src, dst = pl.ds(r * c4 + j, c16, stride=4), pl.ds((r + 4 * j) * c16, c16)
            qd[dst, :] = q4[src, :].astype(BF16)
            kd[dst, :] = k4[src, :].astype(BF16)
            vd[dst, :] = v4[src, :].astype(BF16)

    def keep16(rows, mb, sb, u):
        ud[rows, :] = u
        md[rows, :] = jnp.broadcast_to(mb, (QB, D_HEAD))
        sd[rows, :] = jnp.broadcast_to(sb, (QB, D_HEAD))
    run_blocks(2, c16 // QB, keep16)
    for r in range(4):
        for j in range(4):
            src, dst = pl.ds((r + 4 * j) * c16, c16), pl.ds(r * c4 + j, c16, stride=4)
            u_acc[dst, :] = ud[src, :]
            m_acc[dst, :] = md[src, :]
            s_acc[dst, :] = sd[src, :]

    qd[...] = q4[...].astype(BF16)
    kd[...] = k4[...].astype(BF16)
    vd[...] = v4[...].astype(BF16)

    def merge4(rows, mb, sb, u):
        m_acc[rows, :], s_acc[rows, :], u_acc[rows, :] = merged(rows, m_acc, s_acc, u_acc, mb, sb, u)
    run_blocks(1, c4 // QB, merge4)

    for r in range(4):
        src, dst = pl.ds(r * c4, c4), pl.ds(r, c4, stride=4)
        ud[dst, :] = u_acc[src, :]
        md[dst, :] = m_acc[src, :]
        sd[dst, :] = s_acc[src, :]
    qd[...] = (q_ref[...] * scale).astype(BF16)
    kd[...] = k_ref[...].astype(BF16)
    vd[...] = v_ref[...].astype(BF16)

    def finish(rows, mb, sb, u):
        _, den, num = merged(rows, md, sd, ud, mb, sb, u)
        o_ref[rows, :] = (num / den).astype(o_ref.dtype)
    run_blocks(0, SEQ // QB, finish)


def _attention_prompt(q, k, v, bias_tabs):
    blk = pl.BlockSpec((SEQ, D_HEAD), lambda b, h: (b, h))
    return pl.pallas_call(
        _attn_kernel,
        grid=(BATCH, N_HEADS),
        in_specs=[blk, blk, blk,
                  pl.BlockSpec((None, 3, 2, 1, ATTN_PERIOD), lambda b, h: (h, 0, 0, 0, 0))],
        out_specs=blk,
        out_shape=jax.ShapeDtypeStruct((NP, A_WIDTH), BF16),
        scratch_shapes=[pltpu.VMEM((SEQ, D_HEAD), F32)] * 3
                       + [pltpu.VMEM((SEQ, D_HEAD), BF16)] * 3
                       + [pltpu.VMEM((SEQ, D_HEAD), F32)] * 6
                       + [pltpu.VMEM((len(DILATED), 2, QB, 2 * QB), F32)],
        compiler_params=_cparams(("arbitrary", "arbitrary")),
        name="attn_prompt",
    )(q, k, v, bias_tabs)


def _hgrn_tables():
    t = np.arange(CH)
    u = np.arange(CH)
    sums_q = np.zeros((1 + N_LEVELS, CH, CH), np.float32)
    sums_k = np.zeros((2 + N_LEVELS, CH, CH), np.float32)
    sums_q[0] = (u[None, :] <= t[:, None])
    sums_k[0] = (u[None, :] > t[:, None])
    sums_k[1 + N_LEVELS] = 1.0
    pair = np.zeros((N_LEVELS, CH, CH), np.float32)
    for l in range(N_LEVELS):
        h = CH >> (l + 1)
        is_q = (t // h) % 2 == 1
        half_start = (t // h) * h
        half_end = half_start + h
        sel_q = (u[None, :] >= half_start[:, None]) & (u[None, :] <= t[:, None])
        sel_k = (u[None, :] > t[:, None]) & (u[None, :] < half_end[:, None])
        sums_q[1 + l] = sel_q & is_q[:, None]
        sums_k[1 + l] = sel_k & ~is_q[:, None]
        same = (t[:, None] // (2 * h)) == (t[None, :] // (2 * h))
        pair[l] = same & is_q[:, None] & (~is_q)[None, :]
    sums_kt = np.transpose(sums_k, (2, 0, 1)).reshape(CH, (2 + N_LEVELS) * CH)
    sums_q = sums_q.reshape((1 + N_LEVELS) * CH, CH)
    return (jnp.asarray(np.concatenate([sums_q, sums_q], 1), BF16),
            jnp.asarray(np.concatenate([sums_kt, sums_kt], 0), BF16),
            jnp.asarray(pair), jnp.asarray(np.eye(CH, dtype=np.float32)))


def _split_bf16(x):
    hi = x.astype(BF16)
    return hi, (x - hi.astype(F32)).astype(BF16)


def _hgrn_kernel(q_ref, f_ref, i_ref, g_ref, lb_ref, gn_ref, sq_ref, skt_ref, pair_ref, eye_ref, o_ref, s_ref):
    gn = gn_ref[...]

    def chunk(c, st, hh):
        rows = pl.ds(pl.multiple_of(c * CH, CH), CH)
        cols = slice(hh * D_HEAD, (hh + 1) * D_HEAD)
        lb = lb_ref[:, cols]
        q = _silu(q_ref[rows, cols])
        f = lb + (1.0 - lb) * _sigmoid(f_ref[rows, cols])
        lf = jnp.log(f)
        k = 1.0 - f
        v16 = i_ref[rows, cols].astype(BF16)
        kt = k.T
        exq = jnp.exp(jnp.dot(sq_ref[...], jnp.concatenate(_split_bf16(lf), 0),
                              preferred_element_type=F32))
        exk = jnp.exp(jnp.dot(jnp.concatenate(_split_bf16(lf.T), 1), skt_ref[...],
                              preferred_element_type=F32))
        inter = jnp.dot((q * exq[0:CH]).astype(BF16), st.astype(BF16), preferred_element_type=F32)
        sc = eye_ref[...] * jnp.sum(q * k, -1, keepdims=True)
        for l in range(N_LEVELS):
            ql = (q * exq[(1 + l) * CH:(2 + l) * CH]).astype(BF16)
            klt = (kt * exk[:, (1 + l) * CH:(2 + l) * CH]).astype(BF16)
            sc = sc + pair_ref[l] * jnp.dot(ql, klt, preferred_element_type=F32)
        o = inter + jnp.dot(sc.astype(BF16), v16, preferred_element_type=F32)
        st_new = (st * exk[:, (1 + N_LEVELS) * CH:]
                  + jnp.dot((kt * exk[:, 0:CH]).astype(BF16), v16, preferred_element_type=F32))
        o = _rms(o, gn) * _silu(g_ref[rows, cols])
        o_ref[rows, cols] = o.astype(o_ref.dtype)
        return st_new

    def step(c2, states):
        for un in range(HGRN_CHUNKS_PER_ITER):
            states = tuple(chunk(c2 * HGRN_CHUNKS_PER_ITER + un, st, hh) for hh, st in enumerate(states))
        return states

    zero = jnp.zeros((D_HEAD, D_HEAD), F32)
    states = lax.fori_loop(0, SEQ // CH // HGRN_CHUNKS_PER_ITER, step, (zero,) * HGRN_HEADS_PER_STEP)
    for hh, st in enumerate(states):
        s_ref[hh] = st


def _hgrn_prompt(hq, hf, hi, hg, lb, gnorm):
    tables = _hgrn_tables()
    width = HGRN_HEADS_PER_STEP * D_HEAD
    blk = pl.BlockSpec((SEQ, width), lambda b, h: (b, h))

    def full(a):
        return pl.BlockSpec(a.shape, lambda b, h, n=a.ndim: (0,) * n)

    return pl.pallas_call(
        _hgrn_kernel,
        grid=(BATCH, N_HEADS // HGRN_HEADS_PER_STEP),
        in_specs=[blk, blk, blk, blk,
                  pl.BlockSpec((1, width), lambda b, h: (0, h)),
                  pl.BlockSpec((1, D_HEAD), lambda b, h: (0, 0))] + [full(a) for a in tables],
        out_specs=[blk, pl.BlockSpec((None, HGRN_HEADS_PER_STEP, D_HEAD, D_HEAD), lambda b, h: (b, h, 0, 0))],
        out_shape=[jax.ShapeDtypeStruct((NP, A_WIDTH), BF16),
                   jax.ShapeDtypeStruct((BATCH, N_HEADS, D_HEAD, D_HEAD), F32)],
        compiler_params=_cparams(("arbitrary", "arbitrary")),
        name="hgrn_prompt",
    )(hq, hf, hi, hg, lb, gnorm, *tables)


def _sample_bias_tables(rel_bias):
    j = np.concatenate([QB - np.arange(QB), [0]])
    pick = np.zeros((len(DILATED), QB + 1, NUM_BUCKETS), np.float32)
    for bi, (w, d) in enumerate(DILATED):
        pick[bi, np.arange(QB + 1), _t5_bucket(d * j)] = 1.0
    return jnp.einsum("bjk,kh->bhj", pick, rel_bias.astype(F32), precision=lax.Precision.HIGHEST)[..., None]


def _bf16_round(x):
    return x.astype(BF16).astype(F32)


def _col(row, eye):
    return jnp.sum(eye * row, -1, keepdims=True)


NEAR_POS = 4 * QB
FAR_STEPS = (W_MAX - NEAR_POS) // 16


def _cached_rows(near_ref, far_ref, d, h):
    def near(first_pos, n, step):
        return near_ref[pl.ds((first_pos - (W_MAX - NEAR_POS)) * N_HEADS + h, n, stride=step * N_HEADS), :]

    if QB * d <= NEAR_POS:
        return near(W_MAX - QB * d, QB, d)
    return jnp.concatenate([far_ref[:, h, :], near(W_MAX - NEAR_POS, QB - FAR_STEPS, d)], 0)


def _sample_kernel(q_ref, k_ref, v_ref, hq_ref, hf_ref, hi_ref, hg_ref, kn_ref, kf_ref, vn_ref, vf_ref, s0_ref,
                   bias_ref, lb_ref, gn_ref, eye_ref, oa_ref, ob_ref, s_ref, oa_acc, ob_acc):
    b = pl.program_id(0)
    row = pl.ds(b, 1)
    scale = D_HEAD ** -0.5
    q = _bf16_round(q_ref[row, :] * scale)
    kn = _bf16_round(k_ref[row, :])
    vn = _bf16_round(v_ref[row, :])
    eye = eye_ref[...]

    stats = []
    for bi, (w, d) in enumerate(DILATED):
        per_head = []
        for h in range(N_HEADS):
            hs = slice(h * D_HEAD, (h + 1) * D_HEAD)
            kb = _bf16_round(_cached_rows(kn_ref, kf_ref, d, h))
            vb = _bf16_round(_cached_rows(vn_ref, vf_ref, d, h))
            s = jnp.sum(kb * q[:, hs], -1, keepdims=True) + bias_ref[bi, h, 0:QB]
            s0 = jnp.sum(q[:, hs] * kn[:, hs], -1, keepdims=True) + bias_ref[bi, h, QB:QB + 1]
            m = jnp.maximum(jnp.max(s, 0, keepdims=True), s0)
            p = jnp.exp(s - m)
            p0 = jnp.exp(s0 - m)
            ssum = jnp.sum(p, 0, keepdims=True) + p0
            u = jnp.sum(_bf16_round(p) * vb, 0, keepdims=True) + _bf16_round(p0) * vn[:, hs]
            per_head.append((m, ssum, u))
        stats.append(per_head)
    outs = []
    for h in range(N_HEADS):
        m_all = functools.reduce(jnp.maximum, [stats[bi][h][0] for bi in range(3)])
        num = 0.0
        den = 0.0
        for bi in range(3):
            m, ssum, u = stats[bi][h]
            c = jnp.exp(m - m_all)
            num = num + c * u
            den = den + c * ssum
        outs.append(num / den)
    oa_acc[row, :] = jnp.concatenate(outs, -1)

    qh = _silu(hq_ref[row, :])
    lb = lb_ref[...]
    f = lb + (1.0 - lb) * _sigmoid(hf_ref[row, :])
    vi = hi_ref[row, :]
    gate = _silu(hg_ref[row, :])
    gn = gn_ref[...]
    obs = []
    for h in range(N_HEADS):
        hs = slice(h * D_HEAD, (h + 1) * D_HEAD)
        f_col = _col(f[:, hs], eye)
        q_col = _col(qh[:, hs], eye)
        s_old = s0_ref[h]
        s_ref[h] = f_col * s_old + (1.0 - f_col) * vi[:, hs]
        inter = jnp.sum(_bf16_round(q_col * f_col) * _bf16_round(s_old), 0, keepdims=True)
        qk = jnp.sum(qh[:, hs] * (1.0 - f[:, hs]), -1, keepdims=True)
        o = inter + qk * vi[:, hs]
        obs.append(_rms(o, gn) * gate[:, hs])
    ob_acc[row, :] = jnp.concatenate(obs, -1)

    @pl.when(b == DEC_BATCH - 1)
    def _():
        pad = jnp.zeros((TR - DEC_BATCH, A_WIDTH), F32)
        oa_ref[...] = jnp.concatenate([oa_acc[...], pad], 0).astype(oa_ref.dtype)
        ob_ref[...] = jnp.concatenate([ob_acc[...], pad], 0).astype(ob_ref.dtype)


def _sample_mixers(q, k, v, hq, hf, hi, hg, cache_k, cache_v, state, bias_s, lb, gnorm):
    tile = pl.BlockSpec((TR, A_WIDTH), lambda b: (SAMPLE_TILE, 0))
    out_tile = pl.BlockSpec((TR, A_WIDTH), lambda b: (0, 0))
    near = pl.BlockSpec((None, NEAR_POS * N_HEADS, D_HEAD), lambda b: (b, W_MAX // NEAR_POS - 1, 0))
    far = pl.BlockSpec((None, FAR_STEPS, SUBLANES, D_HEAD), lambda b: (b, 0, 0, 0))
    rows = (DEC_BATCH, W_MAX * N_HEADS, D_HEAD)
    groups = (DEC_BATCH, W_MAX // 16, 16 * N_HEADS, D_HEAD)
    eye = jnp.eye(D_HEAD, dtype=F32)

    return pl.pallas_call(
        _sample_kernel,
        grid=(DEC_BATCH,),
        in_specs=[tile, tile, tile, tile, tile, tile, tile, near, far, near, far,
                  pl.BlockSpec((None, N_HEADS, D_HEAD, D_HEAD), lambda b: (b, 0, 0, 0)),
                  pl.BlockSpec(bias_s.shape, lambda b: (0, 0, 0, 0)),
                  pl.BlockSpec((1, A_WIDTH), lambda b: (0, 0)),
                  pl.BlockSpec((1, D_HEAD), lambda b: (0, 0)),
                  pl.BlockSpec((D_HEAD, D_HEAD), lambda b: (0, 0))],
        out_specs=[out_tile, out_tile,
                   pl.BlockSpec((None, N_HEADS, D_HEAD, D_HEAD), lambda b: (b, 0, 0, 0))],
        out_shape=[jax.ShapeDtypeStruct((TR, A_WIDTH), BF16), jax.ShapeDtypeStruct((TR, A_WIDTH), BF16),
                   jax.ShapeDtypeStruct((DEC_BATCH, N_HEADS, D_HEAD, D_HEAD), F32)],
        scratch_shapes=[pltpu.VMEM((DEC_BATCH, A_WIDTH), F32)] * 2,
        compiler_params=_cparams(("arbitrary",)),
        name="sample_mixers",
    )(q, k, v, hq, hf, hi, hg, cache_k.reshape(rows), cache_k.reshape(groups),
      cache_v.reshape(rows), cache_v.reshape(groups), state, bias_s, lb, gnorm, eye)


def _store_rows_as_tiles(ref, val):
    n = val.shape[0]
    for j in range(ROW_VREGS):
        ref[pl.ds(j, n, stride=ROW_VREGS), :] = val[:, j * LANES:(j + 1) * LANES]


def _load_rows_from_tiles(ref, n):
    return jnp.concatenate([ref[pl.ds(j, n, stride=ROW_VREGS), :] for j in range(ROW_VREGS)], -1)


def _route(h2, wr_ref, br_ref, tri_ref, cnt_ref, meta_ref, gate_ref, cnt_out_ref):
    i = pl.program_id(0)

    @pl.when(i == 0)
    def _():
        cnt_ref[...] = jnp.zeros_like(cnt_ref)

    logits = jnp.dot(h2.astype(BF16), wr_ref[...], preferred_element_type=F32) + br_ref[...]
    lane = lax.broadcasted_iota(I32, (TR, LANES), 1).astype(F32)
    big = float(1 << 20)
    is_g = lane < N_GROUPS
    gl = jnp.where(is_g, logits, NEG)
    gmax = jnp.max(gl, -1, keepdims=True)
    gsel = jnp.min(jnp.where(gl == gmax, lane, big), -1, keepdims=True)
    pg = 1.0 / jnp.sum(jnp.where(is_g, jnp.exp(gl - gmax), 0.0), -1, keepdims=True)
    lo = N_GROUPS + EXPERTS_PER_GROUP * gsel
    in_grp = jnp.logical_and(lane >= lo, lane < lo + EXPERTS_PER_GROUP)
    el = jnp.where(in_grp, logits, NEG)
    m1 = jnp.max(el, -1, keepdims=True)
    i1 = jnp.min(jnp.where(el == m1, lane, big), -1, keepdims=True)
    el2 = jnp.where(lane == i1, NEG, el)
    m2 = jnp.max(el2, -1, keepdims=True)
    i2 = jnp.min(jnp.where(el2 == m2, lane, big), -1, keepdims=True)
    r = jnp.exp(m2 - m1)
    g1 = pg / (1.0 + r)
    g2 = pg * r / (1.0 + r)
    e1 = i1 - N_GROUPS
    e2 = i2 - N_GROUPS

    tok = i * TR + lax.broadcasted_iota(I32, (TR, 1), 0)
    valid = tok < NV
    oh1 = jnp.logical_and(lane == e1, valid)
    oh2 = jnp.logical_and(lane == e2, valid)
    oh = jnp.where(jnp.logical_or(oh1, oh2), 1.0, 0.0)
    before = jnp.dot(tri_ref[...], oh.astype(BF16), preferred_element_type=F32) + cnt_ref[...]
    rank1 = jnp.sum(jnp.where(oh1, before, 0.0), -1, keepdims=True)
    rank2 = jnp.sum(jnp.where(oh2, before, 0.0), -1, keepdims=True)
    tile_cnt = jnp.sum(oh, 0, keepdims=True)
    cnt_ref[...] = cnt_ref[...] + tile_cnt

    eye = jnp.where(lax.broadcasted_iota(I32, (TR, TR), 0) == lax.broadcasted_iota(I32, (TR, TR), 1), 1.0, 0.0)
    rows = [jnp.sum(eye * col, 0, keepdims=True) for col in (e1, e2, rank1, rank2)]
    meta_ref[0] = jnp.concatenate(rows + [jnp.zeros((4, TR), F32)], 0).astype(I32)
    gates = jnp.where(lane == 0, g1, jnp.where(lane == 1, g2, 0.0))
    gates = jnp.where(valid, gates, 0.0)
    gate_ref[...] = gates[:, 0:8]
    cnt_out_ref[0] = tile_cnt


def _router_operands(wg_group, bg_group, wg_exp, bg_exp):
    wr = jnp.zeros((D_MODEL, LANES), F32)
    wr = wr.at[:, 0:N_GROUPS].set(wg_group.astype(F32)).at[:, N_GROUPS:N_GROUPS + N_EXPERTS].set(wg_exp.astype(F32))
    br = jnp.zeros((1, LANES), F32)
    br = br.at[0, 0:N_GROUPS].set(bg_group.astype(F32)).at[0, N_GROUPS:N_GROUPS + N_EXPERTS].set(bg_exp.astype(F32))
    tri = jnp.asarray(np.tril(np.ones((TR, TR), np.float32), -1), BF16)
    return wr.astype(BF16), br, tri


_ROUTE_OUT_SPECS = [pl.BlockSpec((1, 8, TR), lambda i: (i, 0, 0)),
                    pl.BlockSpec((TR, 8), lambda i: (i, 0)),
                    pl.BlockSpec((1, 1, LANES), lambda i: (i, 0, 0))]
_ROUTE_OUT_SHAPES = [jax.ShapeDtypeStruct((N_TILES, 8, TR), I32), jax.ShapeDtypeStruct((NT, 8), F32),
                     jax.ShapeDtypeStruct((N_TILES, 1, LANES), F32)]


def _const_spec(a):
    return pl.BlockSpec(a.shape, lambda i, n=a.ndim: (0,) * n)


def _outproj_kernel(xp_ref, xs_ref, oa_ref, oas_ref, ob_ref, obs_ref, w_ref, g_ref, wr_ref, br_ref, tri_ref,
                    x1_ref, meta_ref, gate_ref, cnt_out_ref, cnt_ref):
    i = pl.program_id(0)
    x = _token_tile(i, xp_ref, xs_ref)
    mixed = jnp.concatenate([_token_tile(i, oa_ref, oas_ref), _token_tile(i, ob_ref, obs_ref)], -1)
    x1 = x + jnp.dot(mixed, w_ref[...], preferred_element_type=F32)
    x1_ref[...] = x1
    h2 = _rms(x1, g_ref[...])
    _route(h2, wr_ref, br_ref, tri_ref, cnt_ref, meta_ref, gate_ref, cnt_out_ref)


def _outproj_route(xp, xs_pad, oa, oa_s, ob, ob_s, w_out_bf16, g_ffn, wr, br, tri):
    row = pl.BlockSpec((TR, D_MODEL), lambda i: (i, 0))
    half = pl.BlockSpec((TR, A_WIDTH), lambda i: (jnp.minimum(i, SAMPLE_TILE - 1), 0))
    half_s = pl.BlockSpec((TR, A_WIDTH), lambda i: (0, 0))
    return pl.pallas_call(
        _outproj_kernel,
        grid=(N_TILES,),
        in_specs=[_PROMPT_ROWS, _SAMPLE_ROWS, half, half_s, half, half_s,
                  _const_spec(w_out_bf16), _const_spec(g_ffn),
                  _const_spec(wr), _const_spec(br), _const_spec(tri)],
        out_specs=[row] + _ROUTE_OUT_SPECS,
        out_shape=[jax.ShapeDtypeStruct((NT, D_MODEL), F32)] + _ROUTE_OUT_SHAPES,
        scratch_shapes=[pltpu.VMEM((1, LANES), F32)],
        compiler_params=_cparams(("arbitrary",)),
        name="outproj_route",
    )(xp, xs_pad, oa, oa_s, ob, ob_s, w_out_bf16, g_ffn, wr, br, tri)


def _plan(meta, tile_counts):
    tcnt = tile_counts[:, 0, :N_EXPERTS].astype(I32)
    cnt = jnp.sum(tcnt, 0)
    nblk = (cnt + TM - 1) // TM
    blk_end = jnp.cumsum(nblk)
    pstart = (blk_end - nblk) * TM
    n_used = blk_end[-1]
    blk = jnp.minimum(jnp.arange(N_SLOTS_BLK, dtype=I32), n_used - 1)
    blk_exp = jnp.sum((blk[:, None] >= blk_end[None, :]).astype(I32), -1)
    blk_exp = jnp.minimum(blk_exp, N_EXPERTS - 1)
    eid = meta[:, 0:2, :]
    experts = jnp.arange(N_EXPERTS, dtype=I32)[:, None, None, None]
    start_of = jnp.sum(jnp.where(eid[None] == experts, pstart[:, None, None, None], 0), 0)
    tok = (jnp.arange(N_TILES, dtype=I32) * TR)[:, None, None] + jnp.arange(TR, dtype=I32)
    dest = jnp.where(tok < NV, start_of + meta[:, 2:4, :], 0).astype(I32).reshape(-1)
    pad_lo = pstart + cnt
    pad_hi = pstart + nblk * TM
    return dest, blk_exp, n_used.reshape(1).astype(I32), pad_lo.astype(I32), pad_hi.astype(I32)


def _row(ref, r):
    return ref.at[pl.ds(pl.multiple_of(r * ROW_VREGS, ROW_VREGS), ROW_VREGS), :]


def _dest_index(tile, kk, t):
    return (2 * tile + kk) * TR + t


def _dispatch_kernel(dest_ref, plo_ref, phi_ref, nused_ref, x_ref, g_ref, xs_ref, stage, zero_ref, sems, zsem):
    i = pl.program_id(0)
    last = pl.num_programs(0) - 1
    slot = i % 2
    _store_rows_as_tiles(stage.at[slot], _rms(x_ref[...], g_ref[...]))

    def wait_tile(n_tok, s):
        for _ in range(2):
            pltpu.make_async_copy(stage.at[s, pl.ds(0, n_tok * ROW_VREGS), :],
                                  xs_ref.at[pl.ds(0, n_tok * ROW_VREGS), :], sems.at[s]).wait()

    def scatter(n_tok):
        def issue(t, carry):
            src = stage.at[slot, pl.ds(pl.multiple_of(t * ROW_VREGS, ROW_VREGS), ROW_VREGS), :]
            for kk in range(2):
                pltpu.make_async_copy(src, _row(xs_ref, dest_ref[_dest_index(i, kk, t)]),
                                      sems.at[slot]).start(priority=kk)
            return carry
        lax.fori_loop(0, n_tok, issue, 0)

    @pl.when(i < last)
    def _():
        scatter(TR)

    @pl.when(i > 0)
    def _():
        wait_tile(TR, 1 - slot)

    @pl.when(i == last)
    def _():
        n_last = NV - (N_TILES - 1) * TR
        scatter(n_last)
        wait_tile(n_last, slot)
        zero_ref[...] = jnp.zeros_like(zero_ref)

        zero_row = zero_ref.at[pl.ds(0, ROW_VREGS), :]

        def per_expert(e, carry):
            def fill(p, c2):
                pltpu.make_async_copy(zero_row, _row(xs_ref, p), zsem).start()
                return c2
            lax.fori_loop(plo_ref[e], phi_ref[e], fill, 0)

            def fill_wait(p, c2):
                pltpu.make_async_copy(zero_row, _row(xs_ref, 0), zsem).wait()
                return c2
            lax.fori_loop(plo_ref[e], phi_ref[e], fill_wait, 0)
            return carry
        lax.fori_loop(0, N_EXPERTS, per_expert, 0)

        def block_of(nb):
            return xs_ref.at[pl.ds(pl.multiple_of(nb * (TM * ROW_VREGS), TM * ROW_VREGS), TM * ROW_VREGS), :]

        def fill_blk(nb, carry):
            pltpu.make_async_copy(zero_ref, block_of(nb), zsem).start()
            return carry
        lax.fori_loop(nused_ref[0], N_SLOTS_BLK, fill_blk, 0)

        def fill_blk_wait(nb, carry):
            pltpu.make_async_copy(zero_ref, block_of(0), zsem).wait()
            return carry
        lax.fori_loop(nused_ref[0], N_SLOTS_BLK, fill_blk_wait, 0)


def _dispatch(x, g_ffn, dest, pad_lo, pad_hi, n_used):
    return pl.pallas_call(
        _dispatch_kernel,
        grid_spec=pltpu.PrefetchScalarGridSpec(
            num_scalar_prefetch=4, grid=(N_TILES,),
            in_specs=[pl.BlockSpec((TR, D_MODEL), lambda i, *_: (i, 0)),
                      pl.BlockSpec((1, D_MODEL), lambda i, *_: (0, 0))],
            out_specs=pl.BlockSpec(memory_space=pl.ANY),
            scratch_shapes=[pltpu.VMEM((2, TR * ROW_VREGS, LANES), F32),
                            pltpu.VMEM((TM * ROW_VREGS, LANES), F32),
                            pltpu.SemaphoreType.DMA((2,)), pltpu.SemaphoreType.DMA(())]),
        out_shape=jax.ShapeDtypeStruct((N_SLOTS * ROW_VREGS, LANES), F32),
        compiler_params=_cparams(("arbitrary",)),
        name="dispatch",
    )(dest, pad_lo, pad_hi, n_used, x, g_ffn)


class _ExpertRows:
    scratch = [pltpu.VMEM((2, 2 * TR * ROW_VREGS, LANES), F32), pltpu.SemaphoreType.DMA((2,))]

    def __init__(self, dest_ref, ys_ref, buf, sems):
        self.dest_ref, self.ys_ref, self.buf, self.sems = dest_ref, ys_ref, buf, sems

    def _copy(self, src_row, slot, dst_row):
        dst = self.buf.at[slot, pl.ds(pl.multiple_of(dst_row * ROW_VREGS, ROW_VREGS), ROW_VREGS), :]
        return pltpu.make_async_copy(_row(self.ys_ref, src_row), dst, self.sems.at[slot])

    def start(self, tile):
        slot = tile % 2

        def issue(g, carry):
            for un in range(ROW_DMA_UNROLL):
                t = g * ROW_DMA_UNROLL + un
                for kk in range(2):
                    self._copy(self.dest_ref[_dest_index(tile, kk, t)], slot, kk * TR + t).start(priority=kk)
            return carry
        lax.fori_loop(0, TR // ROW_DMA_UNROLL, issue, 0)

    def wait(self, tile):
        slot = tile % 2
        pltpu.make_async_copy(self.ys_ref.at[pl.ds(0, 2 * TR * ROW_VREGS), :], self.buf.at[slot],
                              self.sems.at[slot]).wait()

    def combine(self, tile, x, gate_ref):
        rows = self.buf.at[tile % 2]
        g = gate_ref[...]
        for kk in range(2):
            y = jnp.concatenate([rows[pl.ds(kk * TR * ROW_VREGS + j, TR, stride=ROW_VREGS), :]
                                 for j in range(ROW_VREGS)], -1)
            x = x + g[:, kk:kk + 1] * y
        return x

    def fetch_combine(self, x, gate_ref):
        i = pl.program_id(0)

        @pl.when(i == 0)
        def _():
            self.start(i)

        @pl.when(i + 1 < pl.num_programs(0))
        def _():
            self.start(i + 1)

        self.wait(i)
        return self.combine(i, x, gate_ref)


def _ffn_kernel(be_ref, nused_ref, xs_ref, w1_ref, w3_ref, w2_ref, y_ref, w1b, w3b, w2b):
    nb = pl.program_id(0)

    @pl.when(nb < nused_ref[0])
    def _():
        prev = be_ref[jnp.maximum(nb - 1, 0)]
        fresh = jnp.logical_or(nb == 0, be_ref[nb] != prev)

        @pl.when(fresh)
        def _():
            w1b[...] = w1_ref[...].astype(BF16)
            w3b[...] = w3_ref[...].astype(BF16)
            w2b[...] = w2_ref[...].astype(BF16)

        x = _load_rows_from_tiles(xs_ref, TM).astype(BF16)
        a = jnp.dot(x, w1b[...], preferred_element_type=F32)
        b = jnp.dot(x, w3b[...], preferred_element_type=F32)
        mid = (_silu(a) * b).astype(BF16)
        y = jnp.dot(mid, w2b[...], preferred_element_type=F32)
        _store_rows_as_tiles(y_ref, y)

    @pl.when(nb >= nused_ref[0])
    def _():
        y_ref[...] = jnp.zeros_like(y_ref)


def _experts(xs, blk_exp, n_used, layer, w1, w3, w2):
    rows = pl.BlockSpec((TM * ROW_VREGS, LANES), lambda nb, be, nu: (nb, 0))
    rows_in = rows

    def wspec(a, b):
        return pl.BlockSpec((None, None, a, b), lambda nb, be, nu: (layer, be[nb], 0, 0))

    return pl.pallas_call(
        _ffn_kernel,
        grid_spec=pltpu.PrefetchScalarGridSpec(
            num_scalar_prefetch=2, grid=(N_SLOTS_BLK,),
            in_specs=[rows_in, wspec(D_MODEL, D_EXPERT), wspec(D_MODEL, D_EXPERT), wspec(D_EXPERT, D_MODEL)],
            out_specs=rows,
            scratch_shapes=[pltpu.VMEM((D_MODEL, D_EXPERT), BF16), pltpu.VMEM((D_MODEL, D_EXPERT), BF16),
                            pltpu.VMEM((D_EXPERT, D_MODEL), BF16)]),
        out_shape=jax.ShapeDtypeStruct((N_SLOTS * ROW_VREGS, LANES), F32),
        compiler_params=_cparams(("arbitrary",)),
        name="experts",
    )(blk_exp, n_used, xs, w1, w3, w2)


def _moe(x, g_ffn, meta, counts, layer, w1, w3, w2):
    dest, blk_exp, n_used, pad_lo, pad_hi = _plan(meta, counts)
    xs = _dispatch(x, g_ffn, dest, pad_lo, pad_hi, n_used)
    return _experts(xs, blk_exp, n_used, layer, w1, w3, w2), dest


def _pool_kernel(dest_ref, x1_ref, ys_ref, gate0_ref, gmix_ref, sp_ref, pw_ref, ps_ref, gffn_ref,
                 wr_ref, br_ref, tri_ref,
                 x3_ref, hkeep_ref, hs_ref, meta_ref, gate_ref, cnt_out_ref,
                 cnt_ref, ext_ref, e1_ref, e2_ref, e3_ref, e4_ref, mixed_ref, ybuf, ysems):
    i = pl.program_id(0)
    x2 = _ExpertRows(dest_ref, ys_ref, ybuf, ysems).fetch_combine(x1_ref[...], gate0_ref)
    h = _rms(x2, gmix_ref[...])
    H = POOL_HALO
    lvl_refs = (e1_ref, e2_ref, e3_ref, e4_ref)

    @pl.when(i < SAMPLE_TILE)
    def _():
        hkeep_ref[...] = h[TR - 16:TR]
        @pl.when(i % TILES_PER_SEQ == 0)
        def _():
            ext_ref[0:H, :] = jnp.zeros((H, D_MODEL), F32)
        ext_ref[H:H + TR, :] = h
        src = ext_ref
        for lv in range(4):
            sh = 1 << lv
            lo = 8 * (lv + 1)
            c0 = POOL_GROUP * lv
            dst = lvl_refs[lv]
            dst[lo:H + TR, c0:] = src[lo:H + TR, c0:] + src[lo - sh:H + TR - sh, c0:]
            src = dst
        pos = (i % TILES_PER_SEQ) * TR + lax.broadcasted_iota(I32, (TR, 1), 0)
        for gi, w in enumerate(POOL_WINDOWS):
            cs = slice(gi * POOL_GROUP, (gi + 1) * POOL_GROUP)
            inv = 1.0 / jnp.minimum(w, pos + 1).astype(F32)
            mixed_ref[:, cs] = lvl_refs[gi][H:H + TR, cs] * inv - h[:, cs]
        ext_ref[0:H, :] = h[TR - H:TR]

    @pl.when(i == SAMPLE_TILE)
    def _():
        hs = h[0:DEC_BATCH]
        hs_ref[...] = hs
        mixed_ref[...] = jnp.zeros_like(mixed_ref)
        for gi, w in enumerate(POOL_WINDOWS):
            cs = slice(gi * POOL_GROUP, (gi + 1) * POOL_GROUP)
            win = hs[:, cs]
            for dlt in range(1, w):
                win = win + sp_ref[POOL_KEEP - dlt][:, cs]
            mixed_ref[0:DEC_BATCH, cs] = win * (1.0 / w) - hs[:, cs]

    mixed = mixed_ref[...].astype(BF16)
    outs = [jnp.dot(mixed[:, gi * POOL_GROUP:(gi + 1) * POOL_GROUP], pw_ref[gi], preferred_element_type=F32)
            for gi in range(len(POOL_WINDOWS))]
    x3 = x2 + jnp.concatenate(outs, -1) * ps_ref[...]
    x3_ref[...] = x3
    h2 = _rms(x3, gffn_ref[...])
    _route(h2, wr_ref, br_ref, tri_ref, cnt_ref, meta_ref, gate_ref, cnt_out_ref)


def _const_spec_p(a):
    return pl.BlockSpec(a.shape, lambda i, *_, n=a.ndim: (0,) * n)


def _pool_route(dest, x1, ys, gates0, g_mix, sp_t, pw_bf16, p_scale, g_ffn, wr, br, tri):
    def rows(shape):
        return pl.BlockSpec(shape, lambda i, *_: (i, 0))

    ext = pltpu.VMEM((POOL_HALO + TR, D_MODEL), F32)
    consts = [g_mix, sp_t, pw_bf16, p_scale, g_ffn, wr, br, tri]
    return pl.pallas_call(
        _pool_kernel,
        grid_spec=pltpu.PrefetchScalarGridSpec(
            num_scalar_prefetch=1, grid=(N_TILES,),
            in_specs=[rows((TR, D_MODEL)), pl.BlockSpec(memory_space=pl.ANY), rows((TR, 8))]
                     + [_const_spec_p(a) for a in consts],
            out_specs=[rows((TR, D_MODEL)),
                       pl.BlockSpec((None, 16, D_MODEL),
                                    lambda i, *_: (jnp.minimum(i // TILES_PER_SEQ, BATCH - 1), 0, 0)),
                       pl.BlockSpec((DEC_BATCH, D_MODEL), lambda i, *_: (0, 0)),
                       pl.BlockSpec((1, 8, TR), lambda i, *_: (i, 0, 0)), rows((TR, 8)),
                       pl.BlockSpec((1, 1, LANES), lambda i, *_: (i, 0, 0))],
            scratch_shapes=[pltpu.VMEM((1, LANES), F32), ext, ext, ext, ext, ext,
                            pltpu.VMEM((TR, D_MODEL), F32)] + _ExpertRows.scratch),
        out_shape=[jax.ShapeDtypeStruct((NT, D_MODEL), F32),
                   jax.ShapeDtypeStruct((BATCH, 16, D_MODEL), F32),
                   jax.ShapeDtypeStruct((DEC_BATCH, D_MODEL), F32)] + _ROUTE_OUT_SHAPES,
        compiler_params=_cparams(("arbitrary",)),
        name="pool_route",
    )(dest, x1, ys, gates0, *consts)


def _final_kernel(dest_ref, x_ref, ys_ref, gate_ref, g_ref, yp_ref, ysm_ref, ybuf, ysems):
    i = pl.program_id(0)
    y = _rms(_ExpertRows(dest_ref, ys_ref, ybuf, ysems).fetch_combine(x_ref[...], gate_ref), g_ref[...])

    @pl.when(i < SAMPLE_TILE)
    def _():
        yp_ref[...] = y

    @pl.when(i == SAMPLE_TILE)
    def _():
        ysm_ref[...] = y[0:DEC_BATCH]


def _final(dest, x3, ys, gates, g_final):
    return pl.pallas_call(
        _final_kernel,
        grid_spec=pltpu.PrefetchScalarGridSpec(
            num_scalar_prefetch=1, grid=(N_TILES,),
            in_specs=[pl.BlockSpec((TR, D_MODEL), lambda i, *_: (i, 0)), pl.BlockSpec(memory_space=pl.ANY),
                      pl.BlockSpec((TR, 8), lambda i, *_: (i, 0)), _const_spec_p(g_final)],
            out_specs=[pl.BlockSpec((TR, D_MODEL), lambda i, *_: (jnp.minimum(i, SAMPLE_TILE - 1), 0)),
                       pl.BlockSpec((DEC_BATCH, D_MODEL), lambda i, *_: (0, 0))],
            scratch_shapes=_ExpertRows.scratch),
        out_shape=[jax.ShapeDtypeStruct((NP, D_MODEL), F32), jax.ShapeDtypeStruct((DEC_BATCH, D_MODEL), F32)],
        compiler_params=_cparams(("arbitrary",)),
        name="final_norm",
    )(dest, x3, ys, gates, g_final)


def kernel(x_prompt, x_sample, cache_win_k, cache_win_v, state_hgrn, state_pool, rel_bias, norm_mix, norm_ffn,
           norm_final, w_in, w_out, hgrn_lb, hgrn_gnorm, pool_w, pool_scale, moe_wg_group, moe_bg_group,
           moe_wg_exp, moe_bg_exp, moe_w1, moe_w3, moe_w2):
    xp = x_prompt.reshape(NP, D_MODEL)
    xs_pad = jnp.pad(x_sample.reshape(DEC_BATCH, D_MODEL), ((0, TR - DEC_BATCH), (0, 0)))
    lb = jnp.cumsum(jax.nn.softmax(hgrn_lb.astype(F32), axis=0), axis=0)[0:1]
    gnorm = hgrn_gnorm[0:1].astype(F32)

    q, k, v, hq, hf, hi, hg, k_win, v_win = _inproj(xp, xs_pad, norm_mix[0:1], w_in[0].astype(BF16))
    oa = _attention_prompt(q, k, v, _attn_bias_tables(rel_bias))
    ob, s_prompt = _hgrn_prompt(hq, hf, hi, hg, lb, gnorm)
    oa_s, ob_s, s_sample = _sample_mixers(
        q, k, v, hq, hf, hi, hg,
        cache_win_k[0], cache_win_v[0], state_hgrn[0], _sample_bias_tables(rel_bias), lb, gnorm)
    wr, br, tri = _router_operands(moe_wg_group[0], moe_bg_group[0], moe_wg_exp[0], moe_bg_exp[0])
    x1, meta, gates0, counts = _outproj_route(xp, xs_pad, oa, oa_s, ob, ob_s, w_out[0].astype(BF16),
                                              norm_ffn[0:1], wr, br, tri)
    ys, dest = _moe(x1, norm_ffn[0:1], meta, counts, 0, moe_w1, moe_w3, moe_w2)

    wr, br, tri = _router_operands(moe_wg_group[1], moe_bg_group[1], moe_wg_exp[1], moe_bg_exp[1])
    sp_t = jnp.transpose(state_pool[0], (1, 0, 2))
    x3, hkeep, hsample, meta, gates1, counts = _pool_route(
        dest, x1, ys, gates0, norm_mix[1:2], sp_t, pool_w[0].astype(BF16), pool_scale[0:1], norm_ffn[1:2],
        wr, br, tri)
    ys, dest = _moe(x3, norm_ffn[1:2], meta, counts, 1, moe_w1, moe_w3, moe_w2)
    y_prompt, y_sample = _final(dest, x3, ys, gates1, norm_final.reshape(1, D_MODEL))

    k_s = k[NP:NV].reshape(1, DEC_BATCH, 1, N_HEADS, D_HEAD)
    v_s = v[NP:NV].reshape(1, DEC_BATCH, 1, N_HEADS, D_HEAD)
    return (y_prompt.reshape(BATCH, SEQ, D_MODEL), y_sample.reshape(DEC_BATCH, 1, D_MODEL),
            k_win.reshape(1, BATCH, W_MAX, N_HEADS, D_HEAD), v_win.reshape(1, BATCH, W_MAX, N_HEADS, D_HEAD),
            s_prompt[None], hkeep[:, 16 - POOL_KEEP:][None],
            k_s, v_s, s_sample[None], hsample.reshape(1, DEC_BATCH, 1, D_MODEL))
```

```python
import functools

import numpy as np
import jax
import jax.numpy as jnp
from jax import lax
from jax.experimental import pallas as pl
from jax.experimental.pallas import tpu as pltpu

F32 = jnp.float32
BF16 = jnp.bfloat16
I32 = jnp.int32

D_MODEL = 1024
BATCH = 4
SEQ = 4096
DEC_BATCH = 32
PAST_LEN = 8192
W_MAX = 2048
N_HEADS = 4
D_HEAD = 128
A_WIDTH = N_HEADS * D_HEAD
N_PROJ = 7 * A_WIDTH
DILATED = ((128, 1), (512, 4), (2048, 16))
NUM_BUCKETS = 32
MAX_DISTANCE = 2048
POOL_WINDOWS = (2, 4, 8, 16)
POOL_GROUP = 256
POOL_KEEP = 15
N_GROUPS = 4
EXPERTS_PER_GROUP = 8
N_EXPERTS = 32
D_EXPERT = 512
EPS = 1e-6
NEG = -1e30

LANES = 128
SUBLANES = 8
ROW_VREGS = D_MODEL // LANES
TR = 256
NP = BATCH * SEQ
NV = NP + DEC_BATCH
NT = NP + TR
N_TILES = NT // TR
SAMPLE_TILE = NP // TR
TILES_PER_SEQ = SEQ // TR
TM = 256
N_SLOTS_BLK = (2 * NV + N_EXPERTS * (TM - 1) + TM - 1) // TM
N_SLOTS = N_SLOTS_BLK * TM
QB = 128
ATTN_UNROLL = 32
ATTN_PERIOD = 3 * QB
ROW_DMA_UNROLL = 16
CH = 128
HGRN_HEADS_PER_STEP = 2
HGRN_CHUNKS_PER_ITER = 8
N_LEVELS = 7
POOL_HALO = 32
VMEM_LIMIT = 56 * 1024 * 1024


def _cparams(sem=None, vmem=VMEM_LIMIT):
    kw = dict(vmem_limit_bytes=vmem)
    if sem is not None:
        kw["dimension_semantics"] = sem
    return pltpu.CompilerParams(**kw)


def _rms(x, g):
    return x * lax.rsqrt(jnp.mean(x * x, -1, keepdims=True) + EPS) * g


def _sigmoid(x):
    return 1.0 / (1.0 + jnp.exp(-x))


def _silu(x):
    return x * _sigmoid(x)


def _token_tile(i, xp_ref, xs_ref):
    return jnp.where(i == SAMPLE_TILE, xs_ref[...], xp_ref[...])


_PROMPT_ROWS = pl.BlockSpec((TR, D_MODEL), lambda i: (jnp.minimum(i, SAMPLE_TILE - 1), 0))
_SAMPLE_ROWS = pl.BlockSpec((TR, D_MODEL), lambda i: (0, 0))
WIN_TILES = W_MAX // TR


def _window_block(i):
    seq = jnp.minimum(i // TILES_PER_SEQ, BATCH - 1)
    j = jnp.clip(i % TILES_PER_SEQ - (TILES_PER_SEQ - WIN_TILES), 0, WIN_TILES - 1)
    return jnp.where(i >= SAMPLE_TILE, BATCH * WIN_TILES - 1, seq * WIN_TILES + j)


def _inproj_kernel(xp_ref, xs_ref, g_ref, w_ref, *out_refs):
    i = pl.program_id(0)
    h = _rms(_token_tile(i, xp_ref, xs_ref), g_ref[...])
    p = jnp.dot(h.astype(BF16), w_ref[...], preferred_element_type=F32)
    for n, o_ref in enumerate(out_refs[:7]):
        o_ref[...] = p[:, n * A_WIDTH:(n + 1) * A_WIDTH]

    @pl.when(jnp.logical_and(i < SAMPLE_TILE, i % TILES_PER_SEQ >= TILES_PER_SEQ - WIN_TILES))
    def _():
        for n, o_ref in ((1, out_refs[7]), (2, out_refs[8])):
            for h_i in range(N_HEADS):
                c0 = n * A_WIDTH + h_i * D_HEAD
                o_ref[pl.ds(h_i, TR, stride=N_HEADS), :] = p[:, c0:c0 + D_HEAD]


def _inproj(xp, xs_pad, g, w_bf16):
    out_sds = ([jax.ShapeDtypeStruct((NT, A_WIDTH), F32)] * 7
               + [jax.ShapeDtypeStruct((BATCH * W_MAX * N_HEADS, D_HEAD), F32)] * 2)
    win = pl.BlockSpec((TR * N_HEADS, D_HEAD), lambda i: (_window_block(i), 0))
    return pl.pallas_call(
        _inproj_kernel,
        grid=(N_TILES,),
        in_specs=[_PROMPT_ROWS, _SAMPLE_ROWS,
                  pl.BlockSpec((1, D_MODEL), lambda i: (0, 0)),
                  pl.BlockSpec((D_MODEL, N_PROJ), lambda i: (0, 0))],
        out_specs=[pl.BlockSpec((TR, A_WIDTH), lambda i: (i, 0))] * 7 + [win, win],
        out_shape=out_sds,
        compiler_params=_cparams(("arbitrary",)),
        name="inproj",
    )(xp, xs_pad, g, w_bf16)


def _t5_bucket(dist):
    max_exact = NUM_BUCKETS // 2
    d = np.asarray(dist)
    large = max_exact + np.floor(np.log(np.maximum(d, 1) / max_exact)
                                 / np.log(MAX_DISTANCE / max_exact) * (NUM_BUCKETS - max_exact)).astype(np.int32)
    large = np.minimum(large, NUM_BUCKETS - 1)
    return np.where(d < max_exact, d, large).astype(np.int32)


def _attn_bias_tables(rel_bias):
    period = ATTN_PERIOD
    m = np.arange(period)
    u = np.where(m < 2 * QB, m, m - period)
    pick = np.zeros((len(DILATED), 2, period, NUM_BUCKETS), np.float32)
    mask = np.zeros((len(DILATED), 2, period, 1), np.float32)
    for bi, (w, d) in enumerate(DILATED):
        nk = w // d
        for vi, off in enumerate((0, QB)):
            j = off - u
            ok = (j >= 0) & (j <= nk)
            pick[bi, vi, m[ok], _t5_bucket(d * j[ok])] = 1.0
            mask[bi, vi, ~ok, 0] = NEG
    vec = jnp.einsum("bvmk,kh->hbvm", pick, rel_bias.astype(F32), precision=lax.Precision.HIGHEST)
    vec = vec + jnp.transpose(mask, (3, 0, 1, 2))
    return vec[:, :, :, None, :]


def _attn_kernel(q_ref, k_ref, v_ref, vec_ref, o_ref,
                 q4, k4, v4, qd, kd, vd, ud, md, sd, u_acc, m_acc, s_acc, bias_ref):
    scale = D_HEAD ** -0.5
    c4 = SEQ // 4
    c16 = SEQ // 16
    for bi in range(len(DILATED)):
        for vi in range(2):
            rows = jnp.broadcast_to(vec_ref[bi, vi], (QB, ATTN_PERIOD))
            bias_ref[bi, vi] = pltpu.roll(rows, 0, 1, stride=1, stride_axis=0)[:, :2 * QB]

    def block_stats(bi, t, nb):
        has_prev = jnp.minimum(t % nb, 1)
        q0 = pl.multiple_of(t * QB, QB)
        k0 = pl.multiple_of((t - has_prev) * QB, QB)
        qb = qd[pl.ds(q0, QB), :]
        kb = kd[pl.ds(k0, 2 * QB), :]
        vb = vd[pl.ds(k0, 2 * QB), :]
        s = lax.dot_general(qb, kb, (((1,), (1,)), ((), ())), preferred_element_type=F32)
        s = s + bias_ref[bi, has_prev]
        mb = jnp.max(s, -1, keepdims=True)
        p = jnp.exp(s - mb)
        sb = jnp.sum(p, -1, keepdims=True)
        u = jnp.dot(p.astype(BF16), vb, preferred_element_type=F32)
        return q0, mb, sb, u

    def run_blocks(bi, nb, consume):
        def body(g, carry):
            for un in range(ATTN_UNROLL):
                q0, mb, sb, u = block_stats(bi, g * ATTN_UNROLL + un, nb)
                consume(pl.ds(q0, QB), mb, sb, u)
            return carry
        lax.fori_loop(0, SEQ // QB // ATTN_UNROLL, body, 0)

    def merged(rows, m_ref, s_ref, u_ref, mb, sb, u):
        m_old = m_ref[rows, :]
        m_new = jnp.maximum(m_old, mb)
        a = jnp.exp(m_old - m_new)
        b = jnp.exp(mb - m_new)
        return m_new, a * s_ref[rows, :] + b * sb, a * u_ref[rows, :] + b * u

    for r in range(4):
        src, dst = pl.ds(r, c4, stride=4), pl.ds(r * c4, c4)
        q4[dst, :] = q_ref[src, :] * scale
        k4[dst, :] = k_ref[src, :]
        v4[dst, :] = v_ref[src, :]

    for r in range(4):
        for j in range(4):
            src, dst = pl.ds(r * c4 + j, c16, stride=4), pl.ds((r + 4 * j) * c16, c16)
            qd[dst, :] = q4[src, :].astype(BF16)
            kd[dst, :] = k4[src, :].astype(BF16)
            vd[dst, :] = v4[src, :].astype(BF16)

    def keep16(rows, mb, sb, u):
        ud[rows, :] = u
        md[rows, :] = jnp.broadcast_to(mb, (QB, D_HEAD))
        sd[rows, :] = jnp.broadcast_to(sb, (QB, D_HEAD))
    run_blocks(2, c16 // QB, keep16)
    for r in range(4):
        for j in range(4):
            src, dst = pl.ds((r + 4 * j) * c16, c16), pl.ds(r * c4 + j, c16, stride=4)
            u_acc[dst, :] = ud[src, :]
            m_acc[dst, :] = md[src, :]
            s_acc[dst, :] = sd[src, :]

    qd[...] = q4[...].astype(BF16)
    kd[...] = k4[...].astype(BF16)
    vd[...] = v4[...].astype(BF16)

    def merge4(rows, mb, sb, u):
        m_acc[rows, :], s_acc[rows, :], u_acc[rows, :] = merged(rows, m_acc, s_acc, u_acc, mb, sb, u)
    run_blocks(1, c4 // QB, merge4)

    for r in range(4):
        src, dst = pl.ds(r * c4, c4), pl.ds(r, c4, stride=4)
        ud[dst, :] = u_acc[src, :]
        md[dst, :] = m_acc[src, :]
        sd[dst, :] = s_acc[src, :]
    qd[...] = (q_ref[...] * scale).astype(BF16)
    kd[...] = k_ref[...].astype(BF16)
    vd[...] = v_ref[...].astype(BF16)

    def finish(rows, mb, sb, u):
        _, den, num = merged(rows, md, sd, ud, mb, sb, u)
        o_ref[rows, :] = (num / den).astype(o_ref.dtype)
    run_blocks(0, SEQ // QB, finish)


def _attention_prompt(q, k, v, bias_tabs):
    blk = pl.BlockSpec((SEQ, D_HEAD), lambda b, h: (b, h))
    return pl.pallas_call(
        _attn_kernel,
        grid=(BATCH, N_HEADS),
        in_specs=[blk, blk, blk,
                  pl.BlockSpec((None, 3, 2, 1, ATTN_PERIOD), lambda b, h: (h, 0, 0, 0, 0))],
        out_specs=blk,
        out_shape=jax.ShapeDtypeStruct((NP, A_WIDTH), BF16),
        scratch_shapes=[pltpu.VMEM((SEQ, D_HEAD), F32)] * 3
                       + [pltpu.VMEM((SEQ, D_HEAD), BF16)] * 3
                       + [pltpu.VMEM((SEQ, D_HEAD), F32)] * 6
                       + [pltpu.VMEM((len(DILATED), 2, QB, 2 * QB), F32)],
        compiler_params=_cparams(("arbitrary", "arbitrary")),
        name="attn_prompt",
    )(q, k, v, bias_tabs)


def _hgrn_tables():
    t = np.arange(CH)
    u = np.arange(CH)
    sums_q = np.zeros((1 + N_LEVELS, CH, CH), np.float32)
    sums_k = np.zeros((2 + N_LEVELS, CH, CH), np.float32)
    sums_q[0] = (u[None, :] <= t[:, None])
    sums_k[0] = (u[None, :] > t[:, None])
    sums_k[1 + N_LEVELS] = 1.0
    pair = np.zeros((N_LEVELS, CH, CH), np.float32)
    for l in range(N_LEVELS):
        h = CH >> (l + 1)
        is_q = (t // h) % 2 == 1
        half_start = (t // h) * h
        half_end = half_start + h
        sel_q = (u[None, :] >= half_start[:, None]) & (u[None, :] <= t[:, None])
        sel_k = (u[None, :] > t[:, None]) & (u[None, :] < half_end[:, None])
        sums_q[1 + l] = sel_q & is_q[:, None]
        sums_k[1 + l] = sel_k & ~is_q[:, None]
        same = (t[:, None] // (2 * h)) == (t[None, :] // (2 * h))
        pair[l] = same & is_q[:, None] & (~is_q)[None, :]
    sums_kt = np.transpose(sums_k, (2, 0, 1)).reshape(CH, (2 + N_LEVELS) * CH)
    sums_q = sums_q.reshape((1 + N_LEVELS) * CH, CH)
    return (jnp.asarray(np.concatenate([sums_q, sums_q], 1), BF16),
            jnp.asarray(np.concatenate([sums_kt, sums_kt], 0), BF16),
            jnp.asarray(pair), jnp.asarray(np.eye(CH, dtype=np.float32)))


def _split_bf16(x):
    hi = x.astype(BF16)
    return hi, (x - hi.astype(F32)).astype(BF16)


def _hgrn_kernel(q_ref, f_ref, i_ref, g_ref, lb_ref, gn_ref, sq_ref, skt_ref, pair_ref, eye_ref, o_ref, s_ref):
    gn = gn_ref[...]

    def chunk(c, st, hh):
        rows = pl.ds(pl.multiple_of(c * CH, CH), CH)
        cols = slice(hh * D_HEAD, (hh + 1) * D_HEAD)
        lb = lb_ref[:, cols]
        q = _silu(q_ref[rows, cols])
        f = lb + (1.0 - lb) * _sigmoid(f_ref[rows, cols])
        lf = jnp.log(f)
        k = 1.0 - f
        v16 = i_ref[rows, cols].astype(BF16)
        kt = k.T
        exq = jnp.exp(jnp.dot(sq_ref[...], jnp.concatenate(_split_bf16(lf), 0),
                              preferred_element_type=F32))
        exk = jnp.exp(jnp.dot(jnp.concatenate(_split_bf16(lf.T), 1), skt_ref[...],
                              preferred_element_type=F32))
        inter = jnp.dot((q * exq[0:CH]).astype(BF16), st.astype(BF16), preferred_element_type=F32)
        sc = eye_ref[...] * jnp.sum(q * k, -1, keepdims=True)
        for l in range(N_LEVELS):
            ql = (q * exq[(1 + l) * CH:(2 + l) * CH]).astype(BF16)
            klt = (kt * exk[:, (1 + l) * CH:(2 + l) * CH]).astype(BF16)
            sc = sc + pair_ref[l] * jnp.dot(ql, klt, preferred_element_type=F32)
        o = inter + jnp.dot(sc.astype(BF16), v16, preferred_element_type=F32)
        st_new = (st * exk[:, (1 + N_LEVELS) * CH:]
                  + jnp.dot((kt * exk[:, 0:CH]).astype(BF16), v16, preferred_element_type=F32))
        o = _rms(o, gn) * _silu(g_ref[rows, cols])
        o_ref[rows, cols] = o.astype(o_ref.dtype)
        return st_new

    def step(c2, states):
        for un in range(HGRN_CHUNKS_PER_ITER):
            states = tuple(chunk(c2 * HGRN_CHUNKS_PER_ITER + un, st, hh) for hh, st in enumerate(states))
        return states

    zero = jnp.zeros((D_HEAD, D_HEAD), F32)
    states = lax.fori_loop(0, SEQ // CH // HGRN_CHUNKS_PER_ITER, step, (zero,) * HGRN_HEADS_PER_STEP)
    for hh, st in enumerate(states):
        s_ref[hh] = st


def _hgrn_prompt(hq, hf, hi, hg, lb, gnorm):
    tables = _hgrn_tables()
    width = HGRN_HEADS_PER_STEP * D_HEAD
    blk = pl.BlockSpec((SEQ, width), lambda b, h: (b, h))

    def full(a):
        return pl.BlockSpec(a.shape, lambda b, h, n=a.ndim: (0,) * n)

    return pl.pallas_call(
        _hgrn_kernel,
        grid=(BATCH, N_HEADS // HGRN_HEADS_PER_STEP),
        in_specs=[blk, blk, blk, blk,
                  pl.BlockSpec((1, width), lambda b, h: (0, h)),
                  pl.BlockSpec((1, D_HEAD), lambda b, h: (0, 0))] + [full(a) for a in tables],
        out_specs=[blk, pl.BlockSpec((None, HGRN_HEADS_PER_STEP, D_HEAD, D_HEAD), lambda b, h: (b, h, 0, 0))],
        out_shape=[jax.ShapeDtypeStruct((NP, A_WIDTH), BF16),
                   jax.ShapeDtypeStruct((BATCH, N_HEADS, D_HEAD, D_HEAD), F32)],
        compiler_params=_cparams(("arbitrary", "arbitrary")),
        name="hgrn_prompt",
    )(hq, hf, hi, hg, lb, gnorm, *tables)


def _sample_bias_tables(rel_bias):
    j = np.concatenate([QB - np.arange(QB), [0]])
    pick = np.zeros((len(DILATED), QB + 1, NUM_BUCKETS), np.float32)
    for bi, (w, d) in enumerate(DILATED):
        pick[bi, np.arange(QB + 1), _t5_bucket(d * j)] = 1.0
    return jnp.einsum("bjk,kh->bhj", pick, rel_bias.astype(F32), precision=lax.Precision.HIGHEST)[..., None]


def _bf16_round(x):
    return x.astype(BF16).astype(F32)


def _col(row, eye):
    return jnp.sum(eye * row, -1, keepdims=True)


NEAR_POS = 4 * QB
FAR_STEPS = (W_MAX - NEAR_POS) // 16


def _cached_rows(near_ref, far_ref, d, h):
    def near(first_pos, n, step):
        return near_ref[pl.ds((first_pos - (W_MAX - NEAR_POS)) * N_HEADS + h, n, stride=step * N_HEADS), :]

    if QB * d <= NEAR_POS:
        return near(W_MAX - QB * d, QB, d)
    return jnp.concatenate([far_ref[:, h, :], near(W_MAX - NEAR_POS, QB - FAR_STEPS, d)], 0)


def _sample_kernel(q_ref, k_ref, v_ref, hq_ref, hf_ref, hi_ref, hg_ref, kn_ref, kf_ref, vn_ref, vf_ref, s0_ref,
                   bias_ref, lb_ref, gn_ref, eye_ref, oa_ref, ob_ref, s_ref, oa_acc, ob_acc):
    b = pl.program_id(0)
    row = pl.ds(b, 1)
    scale = D_HEAD ** -0.5
    q = _bf16_round(q_ref[row, :] * scale)
    kn = _bf16_round(k_ref[row, :])
    vn = _bf16_round(v_ref[row, :])
    eye = eye_ref[...]

    stats = []
    for bi, (w, d) in enumerate(DILATED):
        per_head = []
        for h in range(N_HEADS):
            hs = slice(h * D_HEAD, (h + 1) * D_HEAD)
            kb = _bf16_round(_cached_rows(kn_ref, kf_ref, d, h))
            vb = _bf16_round(_cached_rows(vn_ref, vf_ref, d, h))
            s = jnp.sum(kb * q[:, hs], -1, keepdims=True) + bias_ref[bi, h, 0:QB]
            s0 = jnp.sum(q[:, hs] * kn[:, hs], -1, keepdims=True) + bias_ref[bi, h, QB:QB + 1]
            m = jnp.maximum(jnp.max(s, 0, keepdims=True), s0)
            p = jnp.exp(s - m)
            p0 = jnp.exp(s0 - m)
            ssum = jnp.sum(p, 0, keepdims=True) + p0
            u = jnp.sum(_bf16_round(p) * vb, 0, keepdims=True) + _bf16_round(p0) * vn[:, hs]
            per_head.append((m, ssum, u))
        stats.append(per_head)
    outs = []
    for h in range(N_HEADS):
        m_all = functools.reduce(jnp.maximum, [stats[bi][h][0] for bi in range(3)])
        num = 0.0
        den = 0.0
        for bi in range(3):
            m, ssum, u = stats[bi][h]
            c = jnp.exp(m - m_all)
            num = num + c * u
            den = den + c * ssum
        outs.append(num / den)
    oa_acc[row, :] = jnp.concatenate(outs, -1)

    qh = _silu(hq_ref[row, :])
    lb = lb_ref[...]
    f = lb + (1.0 - lb) * _sigmoid(hf_ref[row, :])
    vi = hi_ref[row, :]
    gate = _silu(hg_ref[row, :])
    gn = gn_ref[...]
    obs = []
    for h in range(N_HEADS):
        hs = slice(h * D_HEAD, (h + 1) * D_HEAD)
        f_col = _col(f[:, hs], eye)
        q_col = _col(qh[:, hs], eye)
        s_old = s0_ref[h]
        s_ref[h] = f_col * s_old + (1.0 - f_col) * vi[:, hs]
        inter = jnp.sum(_bf16_round(q_col * f_col) * _bf16_round(s_old), 0, keepdims=True)
        qk = jnp.sum(qh[:, hs] * (1.0 - f[:, hs]), -1, keepdims=True)
        o = inter + qk * vi[:, hs]
        obs.append(_rms(o, gn) * gate[:, hs])
    ob_acc[row, :] = jnp.concatenate(obs, -1)

    @pl.when(b == DEC_BATCH - 1)
    def _():
        pad = jnp.zeros((TR - DEC_BATCH, A_WIDTH), F32)
        oa_ref[...] = jnp.concatenate([oa_acc[...], pad], 0).astype(oa_ref.dtype)
        ob_ref[...] = jnp.concatenate([ob_acc[...], pad], 0).astype(ob_ref.dtype)


def _sample_mixers(q, k, v, hq, hf, hi, hg, cache_k, cache_v, state, bias_s, lb, gnorm):
    tile = pl.BlockSpec((TR, A_WIDTH), lambda b: (SAMPLE_TILE, 0))
    out_tile = pl.BlockSpec((TR, A_WIDTH), lambda b: (0, 0))
    near = pl.BlockSpec((None, NEAR_POS * N_HEADS, D_HEAD), lambda b: (b, W_MAX // NEAR_POS - 1, 0))
    far = pl.BlockSpec((None, FAR_STEPS, SUBLANES, D_HEAD), lambda b: (b, 0, 0, 0))
    rows = (DEC_BATCH, W_MAX * N_HEADS, D_HEAD)
    groups = (DEC_BATCH, W_MAX // 16, 16 * N_HEADS, D_HEAD)
    eye = jnp.eye(D_HEAD, dtype=F32)

    return pl.pallas_call(
        _sample_kernel,
        grid=(DEC_BATCH,),
        in_specs=[tile, tile, tile, tile, tile, tile, tile, near, far, near, far,
                  pl.BlockSpec((None, N_HEADS, D_HEAD, D_HEAD), lambda b: (b, 0, 0, 0)),
                  pl.BlockSpec(bias_s.shape, lambda b: (0, 0, 0, 0)),
                  pl.BlockSpec((1, A_WIDTH), lambda b: (0, 0)),
                  pl.BlockSpec((1, D_HEAD), lambda b: (0, 0)),
                  pl.BlockSpec((D_HEAD, D_HEAD), lambda b: (0, 0))],
        out_specs=[out_tile, out_tile,
                   pl.BlockSpec((None, N_HEADS, D_HEAD, D_HEAD), lambda b: (b, 0, 0, 0))],
        out_shape=[jax.ShapeDtypeStruct((TR, A_WIDTH), BF16), jax.ShapeDtypeStruct((TR, A_WIDTH), BF16),
                   jax.ShapeDtypeStruct((DEC_BATCH, N_HEADS, D_HEAD, D_HEAD), F32)],
        scratch_shapes=[pltpu.VMEM((DEC_BATCH, A_WIDTH), F32)] * 2,
        compiler_params=_cparams(("arbitrary",)),
        name="sample_mixers",
    )(q, k, v, hq, hf, hi, hg, cache_k.reshape(rows), cache_k.reshape(groups),
      cache_v.reshape(rows), cache_v.reshape(groups), state, bias_s, lb, gnorm, eye)


def _store_rows_as_tiles(ref, val):
    n = val.shape[0]
    for j in range(ROW_VREGS):
        ref[pl.ds(j, n, stride=ROW_VREGS), :] = val[:, j * LANES:(j + 1) * LANES]


def _load_rows_from_tiles(ref, n):
    return jnp.concatenate([ref[pl.ds(j, n, stride=ROW_VREGS), :] for j in range(ROW_VREGS)], -1)


def _route(h2, wr_ref, br_ref, tri_ref, cnt_ref, meta_ref, gate_ref, cnt_out_ref):
    i = pl.program_id(0)

    @pl.when(i == 0)
    def _():
        cnt_ref[...] = jnp.zeros_like(cnt_ref)

    logits = jnp.dot(h2.astype(BF16), wr_ref[...], preferred_element_type=F32) + br_ref[...]
    lane = lax.broadcasted_iota(I32, (TR, LANES), 1).astype(F32)
    big = float(1 << 20)
    is_g = lane < N_GROUPS
    gl = jnp.where(is_g, logits, NEG)
    gmax = jnp.max(gl, -1, keepdims=True)
    gsel = jnp.min(jnp.where(gl == gmax, lane, big), -1, keepdims=True)
    pg = 1.0 / jnp.sum(jnp.where(is_g, jnp.exp(gl - gmax), 0.0), -1, keepdims=True)
    lo = N_GROUPS + EXPERTS_PER_GROUP * gsel
    in_grp = jnp.logical_and(lane >= lo, lane < lo + EXPERTS_PER_GROUP)
    el = jnp.where(in_grp, logits, NEG)
    m1 = jnp.max(el, -1, keepdims=True)
    i1 = jnp.min(jnp.where(el == m1, lane, big), -1, keepdims=True)
    el2 = jnp.where(lane == i1, NEG, el)
    m2 = jnp.max(el2, -1, keepdims=True)
    i2 = jnp.min(jnp.where(el2 == m2, lane, big), -1, keepdims=True)
    r = jnp.exp(m2 - m1)
    g1 = pg / (1.0 + r)
    g2 = pg * r / (1.0 + r)
    e1 = i1 - N_GROUPS
    e2 = i2 - N_GROUPS

    tok = i * TR + lax.broadcasted_iota(I32, (TR, 1), 0)
    valid = tok < NV
    oh1 = jnp.logical_and(lane == e1, valid)
    oh2 = jnp.logical_and(lane == e2, valid)
    oh = jnp.where(jnp.logical_or(oh1, oh2), 1.0, 0.0)
    before = jnp.dot(tri_ref[...], oh.astype(BF16), preferred_element_type=F32) + cnt_ref[...]
    rank1 = jnp.sum(jnp.where(oh1, before, 0.0), -1, keepdims=True)
    rank2 = jnp.sum(jnp.where(oh2, before, 0.0), -1, keepdims=True)
    tile_cnt = jnp.sum(oh, 0, keepdims=True)
    cnt_ref[...] = cnt_ref[...] + tile_cnt

    eye = jnp.where(lax.broadcasted_iota(I32, (TR, TR), 0) == lax.broadcasted_iota(I32, (TR, TR), 1), 1.0, 0.0)
    rows = [jnp.sum(eye * col, 0, keepdims=True) for col in (e1, e2, rank1, rank2)]
    meta_ref[0] = jnp.concatenate(rows + [jnp.zeros((4, TR), F32)], 0).astype(I32)
    gates = jnp.where(lane == 0, g1, jnp.where(lane == 1, g2, 0.0))
    gates = jnp.where(valid, gates, 0.0)
    gate_ref[...] = gates[:, 0:8]
    cnt_out_ref[0] = tile_cnt


def _router_operands(wg_group, bg_group, wg_exp, bg_exp):
    wr = jnp.zeros((D_MODEL, LANES), F32)
    wr = wr.at[:, 0:N_GROUPS].set(wg_group.astype(F32)).at[:, N_GROUPS:N_GROUPS + N_EXPERTS].set(wg_exp.astype(F32))
    br = jnp.zeros((1, LANES), F32)
    br = br.at[0, 0:N_GROUPS].set(bg_group.astype(F32)).at[0, N_GROUPS:N_GROUPS + N_EXPERTS].set(bg_exp.astype(F32))
    tri = jnp.asarray(np.tril(np.ones((TR, TR), np.float32), -1), BF16)
    return wr.astype(BF16), br, tri


_ROUTE_OUT_SPECS = [pl.BlockSpec((1, 8, TR), lambda i: (i, 0, 0)),
                    pl.BlockSpec((TR, 8), lambda i: (i, 0)),
                    pl.BlockSpec((1, 1, LANES), lambda i: (i, 0, 0))]
_ROUTE_OUT_SHAPES = [jax.ShapeDtypeStruct((N_TILES, 8, TR), I32), jax.ShapeDtypeStruct((NT, 8), F32),
                     jax.ShapeDtypeStruct((N_TILES, 1, LANES), F32)]


def _const_spec(a):
    return pl.BlockSpec(a.shape, lambda i, n=a.ndim: (0,) * n)


def _outproj_kernel(xp_ref, xs_ref, oa_ref, oas_ref, ob_ref, obs_ref, w_ref, g_ref, wr_ref, br_ref, tri_ref,
                    x1_ref, meta_ref, gate_ref, cnt_out_ref, cnt_ref):
    i = pl.program_id(0)
    x = _token_tile(i, xp_ref, xs_ref)
    mixed = jnp.concatenate([_token_tile(i, oa_ref, oas_ref), _token_tile(i, ob_ref, obs_ref)], -1)
    x1 = x + jnp.dot(mixed, w_ref[...], preferred_element_type=F32)
    x1_ref[...] = x1
    h2 = _rms(x1, g_ref[...])
    _route(h2, wr_ref, br_ref, tri_ref, cnt_ref, meta_ref, gate_ref, cnt_out_ref)


def _outproj_route(xp, xs_pad, oa, oa_s, ob, ob_s, w_out_bf16, g_ffn, wr, br, tri):
    row = pl.BlockSpec((TR, D_MODEL), lambda i: (i, 0))
    half = pl.BlockSpec((TR, A_WIDTH), lambda i: (jnp.minimum(i, SAMPLE_TILE - 1), 0))
    half_s = pl.BlockSpec((TR, A_WIDTH), lambda i: (0, 0))
    return pl.pallas_call(
        _outproj_kernel,
        grid=(N_TILES,),
        in_specs=[_PROMPT_ROWS, _SAMPLE_ROWS, half, half_s, half, half_s,
                  _const_spec(w_out_bf16), _const_spec(g_ffn),
                  _const_spec(wr), _const_spec(br), _const_spec(tri)],
        out_specs=[row] + _ROUTE_OUT_SPECS,
        out_shape=[jax.ShapeDtypeStruct((NT, D_MODEL), F32)] + _ROUTE_OUT_SHAPES,
        scratch_shapes=[pltpu.VMEM((1, LANES), F32)],
        compiler_params=_cparams(("arbitrary",)),
        name="outproj_route",
    )(xp, xs_pad, oa, oa_s, ob, ob_s, w_out_bf16, g_ffn, wr, br, tri)


def _plan(meta, tile_counts):
    tcnt = tile_counts[:, 0, :N_EXPERTS].astype(I32)
    cnt = jnp.sum(tcnt, 0)
    nblk = (cnt + TM - 1) // TM
    blk_end = jnp.cumsum(nblk)
    pstart = (blk_end - nblk) * TM
    n_used = blk_end[-1]
    blk = jnp.minimum(jnp.arange(N_SLOTS_BLK, dtype=I32), n_used - 1)
    blk_exp = jnp.sum((blk[:, None] >= blk_end[None, :]).astype(I32), -1)
    blk_exp = jnp.minimum(blk_exp, N_EXPERTS - 1)
    eid = meta[:, 0:2, :]
    experts = jnp.arange(N_EXPERTS, dtype=I32)[:, None, None, None]
    start_of = jnp.sum(jnp.where(eid[None] == experts, pstart[:, None, None, None], 0), 0)
    tok = (jnp.arange(N_TILES, dtype=I32) * TR)[:, None, None] + jnp.arange(TR, dtype=I32)
    dest = jnp.where(tok < NV, start_of + meta[:, 2:4, :], 0).astype(I32).reshape(-1)
    pad_lo = pstart + cnt
    pad_hi = pstart + nblk * TM
    return dest, blk_exp, n_used.reshape(1).astype(I32), pad_lo.astype(I32), pad_hi.astype(I32)


def _row(ref, r):
    return ref.at[pl.ds(pl.multiple_of(r * ROW_VREGS, ROW_VREGS), ROW_VREGS), :]


def _dest_index(tile, kk, t):
    return (2 * tile + kk) * TR + t


def _dispatch_kernel(dest_ref, plo_ref, phi_ref, nused_ref, x_ref, g_ref, xs_ref, stage, zero_ref, sems, zsem):
    i = pl.program_id(0)
    last = pl.num_programs(0) - 1
    slot = i % 2
    _store_rows_as_tiles(stage.at[slot], _rms(x_ref[...], g_ref[...]))

    def wait_tile(n_tok, s):
        for _ in range(2):
            pltpu.make_async_copy(stage.at[s, pl.ds(0, n_tok * ROW_VREGS), :],
                                  xs_ref.at[pl.ds(0, n_tok * ROW_VREGS), :], sems.at[s]).wait()

    def scatter(n_tok):
        def issue(t, carry):
            src = stage.at[slot, pl.ds(pl.multiple_of(t * ROW_VREGS, ROW_VREGS), ROW_VREGS), :]
            for kk in range(2):
                pltpu.make_async_copy(src, _row(xs_ref, dest_ref[_dest_index(i, kk, t)]),
                                      sems.at[slot]).start(priority=kk)
            return carry
        lax.fori_loop(0, n_tok, issue, 0)

    @pl.when(i < last)
    def _():
        scatter(TR)

    @pl.when(i > 0)
    def _():
        wait_tile(TR, 1 - slot)

    @pl.when(i == last)
    def _():
        n_last = NV - (N_TILES - 1) * TR
        scatter(n_last)
        wait_tile(n_last, slot)
        zero_ref[...] = jnp.zeros_like(zero_ref)

        zero_row = zero_ref.at[pl.ds(0, ROW_VREGS), :]

        def per_expert(e, carry):
            def fill(p, c2):
                pltpu.make_async_copy(zero_row, _row(xs_ref, p), zsem).start()
                return c2
            lax.fori_loop(plo_ref[e], phi_ref[e], fill, 0)

            def fill_wait(p, c2):
                pltpu.make_async_copy(zero_row, _row(xs_ref, 0), zsem).wait()
                return c2
            lax.fori_loop(plo_ref[e], phi_ref[e], fill_wait, 0)
            return carry
        lax.fori_loop(0, N_EXPERTS, per_expert, 0)

        def block_of(nb):
            return xs_ref.at[pl.ds(pl.multiple_of(nb * (TM * ROW_VREGS), TM * ROW_VREGS), TM * ROW_VREGS), :]

        def fill_blk(nb, carry):
            pltpu.make_async_copy(zero_ref, block_of(nb), zsem).start()
            return carry
        lax.fori_loop(nused_ref[0], N_SLOTS_BLK, fill_blk, 0)

        def fill_blk_wait(nb, carry):
            pltpu.make_async_copy(zero_ref, block_of(0), zsem).wait()
            return carry
        lax.fori_loop(nused_ref[0], N_SLOTS_BLK, fill_blk_wait, 0)


def _dispatch(x, g_ffn, dest, pad_lo, pad_hi, n_used):
    return pl.pallas_call(
        _dispatch_kernel,
        grid_spec=pltpu.PrefetchScalarGridSpec(
            num_scalar_prefetch=4, grid=(N_TILES,),
            in_specs=[pl.BlockSpec((TR, D_MODEL), lambda i, *_: (i, 0)),
                      pl.BlockSpec((1, D_MODEL), lambda i, *_: (0, 0))],
            out_specs=pl.BlockSpec(memory_space=pl.ANY),
            scratch_shapes=[pltpu.VMEM((2, TR * ROW_VREGS, LANES), F32),
                            pltpu.VMEM((TM * ROW_VREGS, LANES), F32),
                            pltpu.SemaphoreType.DMA((2,)), pltpu.SemaphoreType.DMA(())]),
        out_shape=jax.ShapeDtypeStruct((N_SLOTS * ROW_VREGS, LANES), F32),
        compiler_params=_cparams(("arbitrary",)),
        name="dispatch",
    )(dest, pad_lo, pad_hi, n_used, x, g_ffn)


class _ExpertRows:
    scratch = [pltpu.VMEM((2, 2 * TR * ROW_VREGS, LANES), F32), pltpu.SemaphoreType.DMA((2,))]

    def __init__(self, dest_ref, ys_ref, buf, sems):
        self.dest_ref, self.ys_ref, self.buf, self.sems = dest_ref, ys_ref, buf, sems

    def _copy(self, src_row, slot, dst_row):
        dst = self.buf.at[slot, pl.ds(pl.multiple_of(dst_row * ROW_VREGS, ROW_VREGS), ROW_VREGS), :]
        return pltpu.make_async_copy(_row(self.ys_ref, src_row), dst, self.sems.at[slot])

    def start(self, tile):
        slot = tile % 2

        def issue(g, carry):
            for un in range(ROW_DMA_UNROLL):
                t = g * ROW_DMA_UNROLL + un
                for kk in range(2):
                    self._copy(self.dest_ref[_dest_index(tile, kk, t)], slot, kk * TR + t).start(priority=kk)
            return carry
        lax.fori_loop(0, TR // ROW_DMA_UNROLL, issue, 0)

    def wait(self, tile):
        slot = tile % 2
        pltpu.make_async_copy(self.ys_ref.at[pl.ds(0, 2 * TR * ROW_VREGS), :], self.buf.at[slot],
                              self.sems.at[slot]).wait()

    def combine(self, tile, x, gate_ref):
        rows = self.buf.at[tile % 2]
        g = gate_ref[...]
        for kk in range(2):
            y = jnp.concatenate([rows[pl.ds(kk * TR * ROW_VREGS + j, TR, stride=ROW_VREGS), :]
                                 for j in range(ROW_VREGS)], -1)
            x = x + g[:, kk:kk + 1] * y
        return x

    def fetch_combine(self, x, gate_ref):
        i = pl.program_id(0)

        @pl.when(i == 0)
        def _():
            self.start(i)

        @pl.when(i + 1 < pl.num_programs(0))
        def _():
            self.start(i + 1)

        self.wait(i)
        return self.combine(i, x, gate_ref)


def _ffn_kernel(be_ref, nused_ref, xs_ref, w1_ref, w3_ref, w2_ref, y_ref, w1b, w3b, w2b):
    nb = pl.program_id(0)

    @pl.when(nb < nused_ref[0])
    def _():
        prev = be_ref[jnp.maximum(nb - 1, 0)]
        fresh = jnp.logical_or(nb == 0, be_ref[nb] != prev)

        @pl.when(fresh)
        def _():
            w1b[...] = w1_ref[...].astype(BF16)
            w3b[...] = w3_ref[...].astype(BF16)
            w2b[...] = w2_ref[...].astype(BF16)

        x = _load_rows_from_tiles(xs_ref, TM).astype(BF16)
        a = jnp.dot(x, w1b[...], preferred_element_type=F32)
        b = jnp.dot(x, w3b[...], preferred_element_type=F32)
        mid = (_silu(a) * b).astype(BF16)
        y = jnp.dot(mid, w2b[...], preferred_element_type=F32)
        _store_rows_as_tiles(y_ref, y)

    @pl.when(nb >= nused_ref[0])
    def _():
        y_ref[...] = jnp.zeros_like(y_ref)


def _experts(xs, blk_exp, n_used, layer, w1, w3, w2):
    rows = pl.BlockSpec((TM * ROW_VREGS, LANES), lambda nb, be, nu: (nb, 0))
    rows_in = rows

    def wspec(a, b):
        return pl.BlockSpec((None, None, a, b), lambda nb, be, nu: (layer, be[nb], 0, 0))

    return pl.pallas_call(
        _ffn_kernel,
        grid_spec=pltpu.PrefetchScalarGridSpec(
            num_scalar_prefetch=2, grid=(N_SLOTS_BLK,),
            in_specs=[rows_in, wspec(D_MODEL, D_EXPERT), wspec(D_MODEL, D_EXPERT), wspec(D_EXPERT, D_MODEL)],
            out_specs=rows,
            scratch_shapes=[pltpu.VMEM((D_MODEL, D_EXPERT), BF16), pltpu.VMEM((D_MODEL, D_EXPERT), BF16),
                            pltpu.VMEM((D_EXPERT, D_MODEL), BF16)]),
        out_shape=jax.ShapeDtypeStruct((N_SLOTS * ROW_VREGS, LANES), F32),
        compiler_params=_cparams(("arbitrary",)),
        name="experts",
    )(blk_exp, n_used, xs, w1, w3, w2)


def _moe(x, g_ffn, meta, counts, layer, w1, w3, w2):
    dest, blk_exp, n_used, pad_lo, pad_hi = _plan(meta, counts)
    xs = _dispatch(x, g_ffn, dest, pad_lo, pad_hi, n_used)
    return _experts(xs, blk_exp, n_used, layer, w1, w3, w2), dest


def _pool_kernel(dest_ref, x1_ref, ys_ref, gate0_ref, gmix_ref, sp_ref, pw_ref, ps_ref, gffn_ref,
                 wr_ref, br_ref, tri_ref,
                 x3_ref, hkeep_ref, hs_ref, meta_ref, gate_ref, cnt_out_ref,
                 cnt_ref, ext_ref, e1_ref, e2_ref, e3_ref, e4_ref, mixed_ref, ybuf, ysems):
    i = pl.program_id(0)
    x2 = _ExpertRows(dest_ref, ys_ref, ybuf, ysems).fetch_combine(x1_ref[...], gate0_ref)
    h = _rms(x2, gmix_ref[...])
    H = POOL_HALO
    lvl_refs = (e1_ref, e2_ref, e3_ref, e4_ref)

    @pl.when(i < SAMPLE_TILE)
    def _():
        hkeep_ref[...] = h[TR - 16:TR]
        @pl.when(i % TILES_PER_SEQ == 0)
        def _():
            ext_ref[0:H, :] = jnp.zeros((H, D_MODEL), F32)
        ext_ref[H:H + TR, :] = h
        src = ext_ref
        for lv in range(4):
            sh = 1 << lv
            lo = 8 * (lv + 1)
            c0 = POOL_GROUP * lv
            dst = lvl_refs[lv]
            dst[lo:H + TR, c0:] = src[lo:H + TR, c0:] + src[lo - sh:H + TR - sh, c0:]
            src = dst
        pos = (i % TILES_PER_SEQ) * TR + lax.broadcasted_iota(I32, (TR, 1), 0)
        for gi, w in enumerate(POOL_WINDOWS):
            cs = slice(gi * POOL_GROUP, (gi + 1) * POOL_GROUP)
            inv = 1.0 / jnp.minimum(w, pos + 1).astype(F32)
            mixed_ref[:, cs] = lvl_refs[gi][H:H + TR, cs] * inv - h[:, cs]
        ext_ref[0:H, :] = h[TR - H:TR]

    @pl.when(i == SAMPLE_TILE)
    def _():
        hs = h[0:DEC_BATCH]
        hs_ref[...] = hs
        mixed_ref[...] = jnp.zeros_like(mixed_ref)
        for gi, w in enumerate(POOL_WINDOWS):
            cs = slice(gi * POOL_GROUP, (gi + 1) * POOL_GROUP)
            win = hs[:, cs]
            for dlt in range(1, w):
                win = win + sp_ref[POOL_KEEP - dlt][:, cs]
            mixed_ref[0:DEC_BATCH, cs] = win * (1.0 / w) - hs[:, cs]

    mixed = mixed_ref[...].astype(BF16)
    outs = [jnp.dot(mixed[:, gi * POOL_GROUP:(gi + 1) * POOL_GROUP], pw_ref[gi], preferred_element_type=F32)
            for gi in range(len(POOL_WINDOWS))]
    x3 = x2 + jnp.concatenate(outs, -1) * ps_ref[...]
    x3_ref[...] = x3
    h2 = _rms(x3, gffn_ref[...])
    _route(h2, wr_ref, br_ref, tri_ref, cnt_ref, meta_ref, gate_ref, cnt_out_ref)


def _const_spec_p(a):
    return pl.BlockSpec(a.shape, lambda i, *_, n=a.ndim: (0,) * n)


def _pool_route(dest, x1, ys, gates0, g_mix, sp_t, pw_bf16, p_scale, g_ffn, wr, br, tri):
    def rows(shape):
        return pl.BlockSpec(shape, lambda i, *_: (i, 0))

    ext = pltpu.VMEM((POOL_HALO + TR, D_MODEL), F32)
    consts = [g_mix, sp_t, pw_bf16, p_scale, g_ffn, wr, br, tri]
    return pl.pallas_call(
        _pool_kernel,
        grid_spec=pltpu.PrefetchScalarGridSpec(
            num_scalar_prefetch=1, grid=(N_TILES,),
            in_specs=[rows((TR, D_MODEL)), pl.BlockSpec(memory_space=pl.ANY), rows((TR, 8))]
                     + [_const_spec_p(a) for a in consts],
            out_specs=[rows((TR, D_MODEL)),
                       pl.BlockSpec((None, 16, D_MODEL),
                                    lambda i, *_: (jnp.minimum(i // TILES_PER_SEQ, BATCH - 1), 0, 0)),
                       pl.BlockSpec((DEC_BATCH, D_MODEL), lambda i, *_: (0, 0)),
                       pl.BlockSpec((1, 8, TR), lambda i, *_: (i, 0, 0)), rows((TR, 8)),
                       pl.BlockSpec((1, 1, LANES), lambda i, *_: (i, 0, 0))],
            scratch_shapes=[pltpu.VMEM((1, LANES), F32), ext, ext, ext, ext, ext,
                            pltpu.VMEM((TR, D_MODEL), F32)] + _ExpertRows.scratch),
        out_shape=[jax.ShapeDtypeStruct((NT, D_MODEL), F32),
                   jax.ShapeDtypeStruct((BATCH, 16, D_MODEL), F32),
                   jax.ShapeDtypeStruct((DEC_BATCH, D_MODEL), F32)] + _ROUTE_OUT_SHAPES,
        compiler_params=_cparams(("arbitrary",)),
        name="pool_route",
    )(dest, x1, ys, gates0, *consts)


def _final_kernel(dest_ref, x_ref, ys_ref, gate_ref, g_ref, yp_ref, ysm_ref, ybuf, ysems):
    i = pl.program_id(0)
    y = _rms(_ExpertRows(dest_ref, ys_ref, ybuf, ysems).fetch_combine(x_ref[...], gate_ref), g_ref[...])

    @pl.when(i < SAMPLE_TILE)
    def _():
        yp_ref[...] = y

    @pl.when(i == SAMPLE_TILE)
    def _():
        ysm_ref[...] = y[0:DEC_BATCH]


def _final(dest, x3, ys, gates, g_final):
    return pl.pallas_call(
        _final_kernel,
        grid_spec=pltpu.PrefetchScalarGridSpec(
            num_scalar_prefetch=1, grid=(N_TILES,),
            in_specs=[pl.BlockSpec((TR, D_MODEL), lambda i, *_: (i, 0)), pl.BlockSpec(memory_space=pl.ANY),
                      pl.BlockSpec((TR, 8), lambda i, *_: (i, 0)), _const_spec_p(g_final)],
            out_specs=[pl.BlockSpec((TR, D_MODEL), lambda i, *_: (jnp.minimum(i, SAMPLE_TILE - 1), 0)),
                       pl.BlockSpec((DEC_BATCH, D_MODEL), lambda i, *_: (0, 0))],
            scratch_shapes=_ExpertRows.scratch),
        out_shape=[jax.ShapeDtypeStruct((NP, D_MODEL), F32), jax.ShapeDtypeStruct((DEC_BATCH, D_MODEL), F32)],
        compiler_params=_cparams(("arbitrary",)),
        name="final_norm",
    )(dest, x3, ys, gates, g_final)


def kernel(x_prompt, x_sample, cache_win_k, cache_win_v, state_hgrn, state_pool, rel_bias, norm_mix, norm_ffn,
           norm_final, w_in, w_out, hgrn_lb, hgrn_gnorm, pool_w, pool_scale, moe_wg_group, moe_bg_group,
           moe_wg_exp, moe_bg_exp, moe_w1, moe_w3, moe_w2):
    xp = x_prompt.reshape(NP, D_MODEL)
    xs_pad = jnp.pad(x_sample.reshape(DEC_BATCH, D_MODEL), ((0, TR - DEC_BATCH), (0, 0)))
    lb = jnp.cumsum(jax.nn.softmax(hgrn_lb.astype(F32), axis=0), axis=0)[0:1]
    gnorm = hgrn_gnorm[0:1].astype(F32)

    q, k, v, hq, hf, hi, hg, k_win, v_win = _inproj(xp, xs_pad, norm_mix[0:1], w_in[0].astype(BF16))
    oa = _attention_prompt(q, k, v, _attn_bias_tables(rel_bias))
    ob, s_prompt = _hgrn_prompt(hq, hf, hi, hg, lb, gnorm)
    oa_s, ob_s, s_sample = _sample_mixers(
        q, k, v, hq, hf, hi, hg,
        cache_win_k[0], cache_win_v[0], state_hgrn[0], _sample_bias_tables(rel_bias), lb, gnorm)
    wr, br, tri = _router_operands(moe_wg_group[0], moe_bg_group[0], moe_wg_exp[0], moe_bg_exp[0])
    x1, meta, gates0, counts = _outproj_route(xp, xs_pad, oa, oa_s, ob, ob_s, w_out[0].astype(BF16),
                                              norm_ffn[0:1], wr, br, tri)
    ys, dest = _moe(x1, norm_ffn[0:1], meta, counts, 0, moe_w1, moe_w3, moe_w2)

    wr, br, tri = _router_operands(moe_wg_group[1], moe_bg_group[1], moe_wg_exp[1], moe_bg_exp[1])
    sp_t = jnp.transpose(state_pool[0], (1, 0, 2))
    x3, hkeep, hsample, meta, gates1, counts = _pool_route(
        dest, x1, ys, gates0, norm_mix[1:2], sp_t, pool_w[0].astype(BF16), pool_scale[0:1], norm_ffn[1:2],
        wr, br, tri)
    ys, dest = _moe(x3, norm_ffn[1:2], meta, counts, 1, moe_w1, moe_w3, moe_w2)
    y_prompt, y_sample = _final(dest, x3, ys, gates1, norm_final.reshape(1, D_MODEL))

    k_s = k[NP:NV].reshape(1, DEC_BATCH, 1, N_HEADS, D_HEAD)
    v_s = v[NP:NV].reshape(1, DEC_BATCH, 1, N_HEADS, D_HEAD)
    return (y_prompt.reshape(BATCH, SEQ, D_MODEL), y_sample.reshape(DEC_BATCH, 1, D_MODEL),
            k_win.reshape(1, BATCH, W_MAX, N_HEADS, D_HEAD), v_win.reshape(1, BATCH, W_MAX, N_HEADS, D_HEAD),
            s_prompt[None], hkeep[:, 16 - POOL_KEEP:][None],
            k_s, v_s, s_sample[None], hsample.reshape(1, DEC_BATCH, 1, D_MODEL))
```
